```python
import jax, jax.numpy as jnp
from jax import lax
import numpy as np

D_MODEL = 4096
BATCH = 8
SEQ = 4096
DEPTH = 1

CHUNK = 64
LEFT_CHUNKS = 8
BAND = LEFT_CHUNKS + 1
HEAD_DIM = 128
N_HEADS_A = 16
N_HEADS_B = 16
WIDTH_A = N_HEADS_A * HEAD_DIM
WIDTH_B = N_HEADS_B * HEAD_DIM
MAX_REL = 256
N_REL = 2 * MAX_REL + 1
Q_BLOCK = 128
N_BRANCHES = 2
D_FF = -(-(8 * D_MODEL) // (3 * 256)) * 256
D_IN = 3 * WIDTH_A + 3 * WIDTH_B + N_HEADS_B + N_BRANCHES * D_MODEL
RMS_EPS = 1e-6
NEG_INF = -1e30

kernel_name = "hybrid_chunked_relpos_fox_gated_block"


def rms_norm(x, g):
    xf = x.astype(jnp.float32)
    y = xf * lax.rsqrt(jnp.mean(xf * xf, axis=-1, keepdims=True) + RMS_EPS)
    return (y * g.astype(jnp.float32)).astype(x.dtype)


def chunked_relpos_attention(q, k, v, rel_bias):
    b, s, h, dh = q.shape
    n_chunks = s // CHUNK
    q = q.reshape(b, n_chunks, CHUNK, h, dh)
    pad = ((0, 0), (LEFT_CHUNKS, 0), (0, 0), (0, 0), (0, 0))
    kp = jnp.pad(k.reshape(b, n_chunks, CHUNK, h, dh), pad)
    vp = jnp.pad(v.reshape(b, n_chunks, CHUNK, h, dh), pad)
    band_idx = jnp.arange(n_chunks)[:, None] + jnp.arange(BAND)[None, :]
    k_band = kp[:, band_idx].reshape(b, n_chunks, BAND * CHUNK, h, dh)
    v_band = vp[:, band_idx].reshape(b, n_chunks, BAND * CHUNK, h, dh)
    valid = (band_idx - LEFT_CHUNKS) >= 0
    valid = jnp.repeat(valid, CHUNK, axis=1)
    a_pos = jnp.arange(CHUNK)[:, None, None]
    slot = jnp.arange(BAND)[None, :, None]
    b_pos = jnp.arange(CHUNK)[None, None, :]
    dist = ((LEFT_CHUNKS - slot) * CHUNK + a_pos - b_pos).reshape(CHUNK, BAND * CHUNK)
    rel_idx = jnp.clip(dist, -MAX_REL, MAX_REL) + MAX_REL
    bias = rel_bias.astype(jnp.float32)[:, rel_idx]
    scale = HEAD_DIM ** -0.5
    logits = jnp.einsum('bcqhd,bckhd->bhcqk', q, k_band,
                        preferred_element_type=jnp.float32) * scale
    logits = logits + bias[None, :, None, :, :]
    logits = jnp.where(valid[None, None, :, None, :], logits, NEG_INF)
    p = jax.nn.softmax(logits, axis=-1).astype(v.dtype)
    out = jnp.einsum('bhcqk,bckhd->bcqhd', p, v_band)
    return out.reshape(b, s, h * dh)


def forgetting_attention(q, k, v, log_f):
    b, s, h, dh = q.shape
    n_blocks = s // Q_BLOCK
    cum = jnp.cumsum(log_f, axis=1)
    cum_k = cum.transpose(0, 2, 1)
    q_blocks = q.reshape(b, n_blocks, Q_BLOCK, h, dh).transpose(1, 0, 2, 3, 4)
    c_blocks = cum.reshape(b, n_blocks, Q_BLOCK, h).transpose(1, 0, 3, 2)
    starts = jnp.arange(n_blocks) * Q_BLOCK
    k_pos = jnp.arange(s)
    scale = HEAD_DIM ** -0.5

    def one_block(args):
        q_i, c_i, start = args
        logits = jnp.einsum('bqhd,bkhd->bhqk', q_i, k,
                            preferred_element_type=jnp.float32) * scale
        logits = logits + c_i[..., :, None] - cum_k[:, :, None, :]
        q_pos = start + jnp.arange(Q_BLOCK)
        causal = k_pos[None, :] <= q_pos[:, None]
        logits = jnp.where(causal[None, None], logits, NEG_INF)
        p = jax.nn.softmax(logits, axis=-1).astype(v.dtype)
        return jnp.einsum('bhqk,bkhd->bqhd', p, v)

    out = lax.map(one_block, (q_blocks, c_blocks, starts))
    return out.transpose(1, 0, 2, 3, 4).reshape(b, s, h * dh)


def _fwd_setup_inputs(seed: int = 0) -> dict:
    key = jax.random.key(seed)
    ks = jax.random.split(key, 16)
    L = DEPTH

    def dense(k, shape, fan_in):
        return jax.random.normal(k, shape, jnp.float32) * fan_in ** -0.5

    def normal(k, shape):
        return jax.random.normal(k, shape, jnp.float32)

    return {
        "x": normal(ks[0], (BATCH, SEQ, D_MODEL)),
        "g_mix": 1.0 + 0.02 * normal(ks[1], (L, D_MODEL)),
        "w_in": dense(ks[2], (L, D_MODEL, D_IN), D_MODEL),
        "b_f": 2.0 + 0.5 * normal(ks[3], (L, N_HEADS_B)),
        "b_gate": 0.01 * normal(ks[4], (L, N_BRANCHES * D_MODEL)),
        "rel_bias": 0.1 * normal(ks[5], (L, N_HEADS_A, N_REL)),
        "w_branch_a": dense(ks[6], (L, WIDTH_A, D_MODEL), WIDTH_A),
        "w_branch_b": dense(ks[7], (L, WIDTH_B, D_MODEL), WIDTH_B),
        "w_out": dense(ks[8], (L, D_MODEL, D_MODEL), D_MODEL),
        "g_ffn": 1.0 + 0.02 * normal(ks[9], (L, D_MODEL)),
        "w_gate_ffn": dense(ks[10], (L, D_MODEL, D_FF), D_MODEL),
        "w_up_ffn": dense(ks[11], (L, D_MODEL, D_FF), D_MODEL),
        "w_down_ffn": dense(ks[12], (L, D_FF, D_MODEL), D_FF),
        "g_final": 1.0 + 0.02 * normal(ks[13], (D_MODEL,)),
    }


def _fwd_reference(x, g_mix, w_in, b_f, b_gate, rel_bias, w_branch_a, w_branch_b, w_out,
              g_ffn, w_gate_ffn, w_up_ffn, w_down_ffn, g_final):
    b, s, _ = x.shape
    sizes = [WIDTH_A, WIDTH_A, WIDTH_A, WIDTH_B, WIDTH_B, WIDTH_B, N_HEADS_B, D_MODEL, D_MODEL]
    split_at = [int(v) for v in np.cumsum(sizes)[:-1]]
    for l in range(DEPTH):
        h = rms_norm(x, g_mix[l])
        proj = jnp.einsum('bsd,de->bse', h, w_in[l])
        qa, ka, va, qb, kb, vb, f_logit, gate_a, gate_b = jnp.split(proj, split_at, axis=-1)
        heads_a = lambda t: t.reshape(b, s, N_HEADS_A, HEAD_DIM)
        heads_b = lambda t: t.reshape(b, s, N_HEADS_B, HEAD_DIM)
        o_a = chunked_relpos_attention(heads_a(qa), heads_a(ka), heads_a(va), rel_bias[l])
        log_f = jax.nn.log_sigmoid((f_logit + b_f[l]).astype(jnp.float32))
        o_b = forgetting_attention(heads_b(qb), heads_b(kb), heads_b(vb), log_f)
        u_a = jnp.einsum('bse,ed->bsd', o_a, w_branch_a[l])
        u_b = jnp.einsum('bse,ed->bsd', o_b, w_branch_b[l])
        merged = (jax.nn.sigmoid(gate_a + b_gate[l, :D_MODEL]) * u_a
                  + jax.nn.sigmoid(gate_b + b_gate[l, D_MODEL:]) * u_b)
        x = x + jnp.einsum('bsd,de->bse', merged, w_out[l])
        h2 = rms_norm(x, g_ffn[l])
        hidden = jax.nn.silu(jnp.einsum('bsd,df->bsf', h2, w_gate_ffn[l])) * \
            jnp.einsum('bsd,df->bsf', h2, w_up_ffn[l])
        x = x + jnp.einsum('bsf,fd->bsd', hidden, w_down_ffn[l])
    return rms_norm(x, g_final)


import jax as _jax
import jax.numpy as _jnp

TWIN_FORMAT = 'train_step'
FWD_PARAMS = ['x', 'g_mix', 'w_in', 'b_f', 'b_gate', 'rel_bias', 'w_branch_a', 'w_branch_b', 'w_out', 'g_ffn', 'w_gate_ffn', 'w_up_ffn', 'w_down_ffn', 'g_final']
TWIN_WEIGHTS = ['g_mix', 'w_in', 'b_f', 'b_gate', 'rel_bias', 'w_branch_a', 'w_branch_b', 'w_out', 'g_ffn', 'w_gate_ffn', 'w_up_ffn', 'w_down_ffn', 'g_final']
TWIN_DIFF_INPUT = 'x'
TWIN_INPUTS = ['x', 'g_mix', 'w_in', 'b_f', 'b_gate', 'rel_bias', 'w_branch_a', 'w_branch_b', 'w_out', 'g_ffn', 'w_gate_ffn', 'w_up_ffn', 'w_down_ffn', 'g_final', 'loss_target', 'm_g_mix', 'm_w_in', 'm_b_f', 'm_b_gate', 'm_rel_bias', 'm_w_branch_a', 'm_w_branch_b', 'm_w_out', 'm_g_ffn', 'm_w_gate_ffn', 'm_w_up_ffn', 'm_w_down_ffn', 'm_g_final', 'v_g_mix', 'v_w_in', 'v_b_f', 'v_b_gate', 'v_rel_bias', 'v_w_branch_a', 'v_w_branch_b', 'v_w_out', 'v_g_ffn', 'v_w_gate_ffn', 'v_w_up_ffn', 'v_w_down_ffn', 'v_g_final']
TWIN_OUTPUTS = ['loss', 'grad_x', 'grad_g_mix', 'grad_w_in', 'grad_b_f', 'grad_b_gate', 'grad_rel_bias', 'grad_w_branch_a', 'grad_w_branch_b', 'grad_w_out', 'grad_g_ffn', 'grad_w_gate_ffn', 'grad_w_up_ffn', 'grad_w_down_ffn', 'grad_g_final', 'delta_g_mix', 'delta_w_in', 'delta_b_f', 'delta_b_gate', 'delta_rel_bias', 'delta_w_branch_a', 'delta_w_branch_b', 'delta_w_out', 'delta_g_ffn', 'delta_w_gate_ffn', 'delta_w_up_ffn', 'delta_w_down_ffn', 'delta_g_final', 'new_m_g_mix', 'new_m_w_in', 'new_m_b_f', 'new_m_b_gate', 'new_m_rel_bias', 'new_m_w_branch_a', 'new_m_w_branch_b', 'new_m_w_out', 'new_m_g_ffn', 'new_m_w_gate_ffn', 'new_m_w_up_ffn', 'new_m_w_down_ffn', 'new_m_g_final', 'new_v_g_mix', 'new_v_w_in', 'new_v_b_f', 'new_v_b_gate', 'new_v_rel_bias', 'new_v_w_branch_a', 'new_v_w_branch_b', 'new_v_w_out', 'new_v_g_ffn', 'new_v_w_gate_ffn', 'new_v_w_up_ffn', 'new_v_w_down_ffn', 'new_v_g_final']
TWIN_LEAF_KINDS = {'loss': 'loss', 'grad_x': 'grad_x', 'grad_g_mix': 'grad_w', 'grad_w_in': 'grad_w', 'grad_b_f': 'grad_w', 'grad_b_gate': 'grad_w', 'grad_rel_bias': 'grad_w', 'grad_w_branch_a': 'grad_w', 'grad_w_branch_b': 'grad_w', 'grad_w_out': 'grad_w', 'grad_g_ffn': 'grad_w', 'grad_w_gate_ffn': 'grad_w', 'grad_w_up_ffn': 'grad_w', 'grad_w_down_ffn': 'grad_w', 'grad_g_final': 'grad_w', 'delta_g_mix': 'delta_w', 'delta_w_in': 'delta_w', 'delta_b_f': 'delta_w', 'delta_b_gate': 'delta_w', 'delta_rel_bias': 'delta_w', 'delta_w_branch_a': 'delta_w', 'delta_w_branch_b': 'delta_w', 'delta_w_out': 'delta_w', 'delta_g_ffn': 'delta_w', 'delta_w_gate_ffn': 'delta_w', 'delta_w_up_ffn': 'delta_w', 'delta_w_down_ffn': 'delta_w', 'delta_g_final': 'delta_w', 'new_m_g_mix': 'new_m', 'new_m_w_in': 'new_m', 'new_m_b_f': 'new_m', 'new_m_b_gate': 'new_m', 'new_m_rel_bias': 'new_m', 'new_m_w_branch_a': 'new_m', 'new_m_w_branch_b': 'new_m', 'new_m_w_out': 'new_m', 'new_m_g_ffn': 'new_m', 'new_m_w_gate_ffn': 'new_m', 'new_m_w_up_ffn': 'new_m', 'new_m_w_down_ffn': 'new_m', 'new_m_g_final': 'new_m', 'new_v_g_mix': 'new_v', 'new_v_w_in': 'new_v', 'new_v_b_f': 'new_v', 'new_v_b_gate': 'new_v', 'new_v_rel_bias': 'new_v', 'new_v_w_branch_a': 'new_v', 'new_v_w_branch_b': 'new_v', 'new_v_w_out': 'new_v', 'new_v_g_ffn': 'new_v', 'new_v_w_gate_ffn': 'new_v', 'new_v_w_up_ffn': 'new_v', 'new_v_w_down_ffn': 'new_v', 'new_v_g_final': 'new_v'}


def _forward(args):
    return _fwd_reference(*[args[k] for k in FWD_PARAMS])


def _output_shape():
    out = _jax.eval_shape(lambda: _forward(_fwd_setup_inputs(0)))
    return out.shape, out.dtype

N_MICROBATCH = 1
ADAM_LR = 0.001
ADAM_B1 = 0.9
ADAM_B2 = 0.999
ADAM_EPS = 1e-08
ADAM_WD = 0.01
ADAM_STEP = 10
PER_EXAMPLE_BATCH_AXIS = {'x': 0, 'loss_target': 0}
SHARED_INPUTS = []
_WEIGHT_DTYPES = {'g_mix': _jnp.float32, 'w_in': _jnp.float32, 'b_f': _jnp.float32, 'b_gate': _jnp.float32, 'rel_bias': _jnp.float32, 'w_branch_a': _jnp.float32, 'w_branch_b': _jnp.float32, 'w_out': _jnp.float32, 'g_ffn': _jnp.float32, 'w_gate_ffn': _jnp.float32, 'w_up_ffn': _jnp.float32, 'w_down_ffn': _jnp.float32, 'g_final': _jnp.float32}
MOMENT_SCALE = {'g_mix': 1.796372e-02, 'w_in': 8.030130e-03, 'b_f': 9.551153e-02, 'b_gate': 3.132667e-03, 'rel_bias': 2.540606e-03, 'w_branch_a': 3.822850e-03, 'w_branch_b': 1.108564e-02, 'w_out': 1.172271e-02, 'g_ffn': 3.254216e-02, 'w_gate_ffn': 1.407063e-02, 'w_up_ffn': 1.361213e-02, 'w_down_ffn': 2.235117e-02, 'g_final': 7.987207e+00}


def _to_microbatches(a, axis):
    t = _jnp.moveaxis(a, axis, 0)
    t = t.reshape((N_MICROBATCH, t.shape[0] // N_MICROBATCH) + t.shape[1:])
    return _jnp.moveaxis(t, 1, axis + 1)


def setup_inputs(seed: int = 0) -> dict:
    inp = _fwd_setup_inputs(seed)
    key = _jax.random.fold_in(_jax.random.key(seed), 7919)
    shape, _ = _output_shape()
    out = dict(inp)
    out["loss_target"] = _jax.random.normal(_jax.random.fold_in(key, 0), shape, _jnp.float32)
    for i, name in enumerate(TWIN_WEIGHTS):
        w = inp[name].astype(_jnp.float32)
        if MOMENT_SCALE is None:
            s = _jnp.sqrt(_jnp.mean(_jnp.square(w)) + 1e-30)
        else:
            s = MOMENT_SCALE[name]
        km, kv = _jax.random.split(_jax.random.fold_in(key, i + 1))
        out[name] = w
        out["m_" + name] = s * _jax.random.normal(km, w.shape, _jnp.float32)
        out["v_" + name] = (s * s) * _jax.random.uniform(kv, w.shape, _jnp.float32, 0.5, 1.5)
    if N_MICROBATCH > 1:
        for name, axis in PER_EXAMPLE_BATCH_AXIS.items():
            out[name] = _to_microbatches(out[name], axis)
    return {'x': out['x'], 'g_mix': out['g_mix'], 'w_in': out['w_in'], 'b_f': out['b_f'], 'b_gate': out['b_gate'], 'rel_bias': out['rel_bias'], 'w_branch_a': out['w_branch_a'], 'w_branch_b': out['w_branch_b'], 'w_out': out['w_out'], 'g_ffn': out['g_ffn'], 'w_gate_ffn': out['w_gate_ffn'], 'w_up_ffn': out['w_up_ffn'], 'w_down_ffn': out['w_down_ffn'], 'g_final': out['g_final'], 'loss_target': out['loss_target'], 'm_g_mix': out['m_g_mix'], 'm_w_in': out['m_w_in'], 'm_b_f': out['m_b_f'], 'm_b_gate': out['m_b_gate'], 'm_rel_bias': out['m_rel_bias'], 'm_w_branch_a': out['m_w_branch_a'], 'm_w_branch_b': out['m_w_branch_b'], 'm_w_out': out['m_w_out'], 'm_g_ffn': out['m_g_ffn'], 'm_w_gate_ffn': out['m_w_gate_ffn'], 'm_w_up_ffn': out['m_w_up_ffn'], 'm_w_down_ffn': out['m_w_down_ffn'], 'm_g_final': out['m_g_final'], 'v_g_mix': out['v_g_mix'], 'v_w_in': out['v_w_in'], 'v_b_f': out['v_b_f'], 'v_b_gate': out['v_b_gate'], 'v_rel_bias': out['v_rel_bias'], 'v_w_branch_a': out['v_w_branch_a'], 'v_w_branch_b': out['v_w_branch_b'], 'v_w_out': out['v_w_out'], 'v_g_ffn': out['v_g_ffn'], 'v_w_gate_ffn': out['v_w_gate_ffn'], 'v_w_up_ffn': out['v_w_up_ffn'], 'v_w_down_ffn': out['v_w_down_ffn'], 'v_g_final': out['v_g_final']}


def _loss(weights, diff, rest, loss_target):
    with _jax.named_scope("forward"):
        args = {**rest, TWIN_DIFF_INPUT: diff, **{k: w.astype(_WEIGHT_DTYPES[k]) for k, w in weights.items()}}
        y = _forward(args)
    with _jax.named_scope("loss_head"):
        err = _jnp.square(y.astype(_jnp.float32) - loss_target)
        return 0.5 * _jnp.sum(_jnp.mean(err, axis=-1)) if err.ndim else 0.5 * err


def _adamw(w, g, m, v):
    m = ADAM_B1 * m + (1.0 - ADAM_B1) * g
    v = ADAM_B2 * v + (1.0 - ADAM_B2) * _jnp.square(g)
    m_hat = m / (1.0 - ADAM_B1 ** ADAM_STEP)
    v_hat = v / (1.0 - ADAM_B2 ** ADAM_STEP)
    delta = -ADAM_LR * (m_hat / (_jnp.sqrt(v_hat) + ADAM_EPS) + ADAM_WD * w)
    return delta, m, v


def reference(x, g_mix, w_in, b_f, b_gate, rel_bias, w_branch_a, w_branch_b, w_out, g_ffn, w_gate_ffn, w_up_ffn, w_down_ffn, g_final, loss_target, m_g_mix, m_w_in, m_b_f, m_b_gate, m_rel_bias, m_w_branch_a, m_w_branch_b, m_w_out, m_g_ffn, m_w_gate_ffn, m_w_up_ffn, m_w_down_ffn, m_g_final, v_g_mix, v_w_in, v_b_f, v_b_gate, v_rel_bias, v_w_branch_a, v_w_branch_b, v_w_out, v_g_ffn, v_w_gate_ffn, v_w_up_ffn, v_w_down_ffn, v_g_final):
    given = dict(x=x, g_mix=g_mix, w_in=w_in, b_f=b_f, b_gate=b_gate, rel_bias=rel_bias, w_branch_a=w_branch_a, w_branch_b=w_branch_b, w_out=w_out, g_ffn=g_ffn, w_gate_ffn=w_gate_ffn, w_up_ffn=w_up_ffn, w_down_ffn=w_down_ffn, g_final=g_final, loss_target=loss_target, m_g_mix=m_g_mix, m_w_in=m_w_in, m_b_f=m_b_f, m_b_gate=m_b_gate, m_rel_bias=m_rel_bias, m_w_branch_a=m_w_branch_a, m_w_branch_b=m_w_branch_b, m_w_out=m_w_out, m_g_ffn=m_g_ffn, m_w_gate_ffn=m_w_gate_ffn, m_w_up_ffn=m_w_up_ffn, m_w_down_ffn=m_w_down_ffn, m_g_final=m_g_final, v_g_mix=v_g_mix, v_w_in=v_w_in, v_b_f=v_b_f, v_b_gate=v_b_gate, v_rel_bias=v_rel_bias, v_w_branch_a=v_w_branch_a, v_w_branch_b=v_w_branch_b, v_w_out=v_w_out, v_g_ffn=v_g_ffn, v_w_gate_ffn=v_w_gate_ffn, v_w_up_ffn=v_w_up_ffn, v_w_down_ffn=v_w_down_ffn, v_g_final=v_g_final)
    weights = {n: given[n] for n in TWIN_WEIGHTS}
    shared = {n: given[n] for n in SHARED_INPUTS}
    per_example = {n: given[n] for n in ['x']}
    grad_fn = _jax.value_and_grad(_loss, argnums=(0, 1))

    def one_microbatch(ex, loss_target):
        ex = dict(ex)
        diff = ex.pop(TWIN_DIFF_INPUT)
        return grad_fn(weights, diff, {**shared, **ex}, loss_target)

    if N_MICROBATCH == 1:
        loss, (grad_w, grad_x) = one_microbatch(per_example, given["loss_target"])
    else:
        def body(carry, xs):
            loss_sum, grad_sum = carry
            l_k, (gw_k, gx_k) = one_microbatch(xs[0], xs[1])
            with _jax.named_scope("update"):
                return (loss_sum + l_k, _jax.tree.map(_jnp.add, grad_sum, gw_k)), gx_k

        init = (_jnp.zeros((), _jnp.float32), _jax.tree.map(_jnp.zeros_like, weights))
        (loss, grad_w), grad_x = _jax.lax.scan(body, init, (per_example, given["loss_target"]))
    with _jax.named_scope("update"):
        delta_w, new_m, new_v = {}, {}, {}
        for n in TWIN_WEIGHTS:
            delta_w[n], new_m[n], new_v[n] = _adamw(weights[n], grad_w[n], given["m_" + n], given["v_" + n])
    return (loss, grad_x, *[grad_w[n] for n in TWIN_WEIGHTS], *[delta_w[n] for n in TWIN_WEIGHTS],
            *[new_m[n] for n in TWIN_WEIGHTS], *[new_v[n] for n in TWIN_WEIGHTS])
```

```python
import functools

import jax
import jax.numpy as jnp
from jax import lax
from jax.experimental import pallas as pl
from jax.experimental.pallas import tpu as pltpu

F32 = jnp.float32
BF16 = jnp.bfloat16
MESH = pl.DeviceIdType.MESH

N_DEV = 8
LANE = 128
HEAD_DIM = 128
CHUNK = 64
LEFT_CHUNKS = 8
MAX_REL = 256
RMS_EPS = 1e-6
NEG_INF = -1e30
ATTN_SCALE = HEAD_DIM ** -0.5
VMEM_LIMIT_BYTES = 56 * 1024 * 1024

ADAM_LR = 0.001
ADAM_B1 = 0.9
ADAM_B2 = 0.999
ADAM_EPS = 1e-08
ADAM_WD = 0.01
ADAM_STEP = 10

NT_DIMS = (((1,), (1,)), ((), ()))
TN_DIMS = (((0,), (0,)), ((), ()))


def _cp(*sem):
    return pltpu.CompilerParams(dimension_semantics=sem if sem else None, vmem_limit_bytes=VMEM_LIMIT_BYTES)


def _pad_to(n, mult):
    return -(-n // mult) * mult


def _tile(dim, pref, align):
    t = (min(pref, dim) // align) * align
    while t >= align:
        if dim % t == 0:
            return t
        t -= align
    return dim


def _sigmoid(z):
    return 1.0 / (1.0 + jnp.exp(-z))


def _mm_nn(x, w, *, name, out_dtype, res=None, tm=1024, tn=2816, tk=512):
    M, K = x.shape
    NB, _, NW = w.shape
    tm, tk, tn = _tile(M, tm, 16), _tile(K, tk, LANE), _tile(NW, tn, LANE)
    nj, nk = NW // tn, K // tk

    def body(*refs):
        if res is None:
            x_ref, w_ref, o_ref, acc = refs
        else:
            x_ref, w_ref, r_ref, o_ref, acc = refs
        k = pl.program_id(2)
        p = jnp.dot(x_ref[...], w_ref[...], preferred_element_type=F32)

        @pl.when(k == 0)
        def _():
            acc[...] = p

        @pl.when(k > 0)
        def _():
            acc[...] += p

        @pl.when(k == nk - 1)
        def _():
            v = acc[...]
            if res is not None:
                v = v + r_ref[...]
            o_ref[...] = v.astype(o_ref.dtype)

    in_specs = [pl.BlockSpec((tm, tk), lambda j, i, k: (i, k)),
                pl.BlockSpec((None, tk, tn), lambda j, i, k: (j // nj, k, j % nj))]
    args = [x, w]
    if res is not None:
        in_specs.append(pl.BlockSpec((tm, tn), lambda j, i, k: (i, j)))
        args.append(res)
    return pl.pallas_call(
        body, name=name, grid=(NB * nj, M // tm, nk), in_specs=in_specs,
        out_specs=pl.BlockSpec((tm, tn), lambda j, i, k: (i, j)),
        out_shape=jax.ShapeDtypeStruct((M, NB * NW), out_dtype),
        scratch_shapes=[pltpu.VMEM((tm, tn), F32)],
        compiler_params=_cp("parallel", "parallel", "arbitrary"))(*args)


def _mm_nt(dy, w, *, name, out_dtype, res=None, tm=1024, tn=1024, tc=2816):
    M, N = dy.shape
    NB, Kw, NW = w.shape
    assert N == NB * NW
    tm, tn, tc = _tile(M, tm, 16), _tile(Kw, tn, LANE), _tile(NW, tc, LANE)
    nc = NW // tc
    nk = NB * nc

    def body(*refs):
        if res is None:
            a_ref, w_ref, o_ref, acc = refs
        else:
            a_ref, w_ref, r_ref, o_ref, acc = refs
        k = pl.program_id(2)
        p = lax.dot_general(a_ref[...], w_ref[...], NT_DIMS, preferred_element_type=F32)

        @pl.when(k == 0)
        def _():
            acc[...] = p

        @pl.when(k > 0)
        def _():
            acc[...] += p

        @pl.when(k == nk - 1)
        def _():
            v = acc[...]
            if res is not None:
                v = v + r_ref[...]
            o_ref[...] = v.astype(o_ref.dtype)

    in_specs = [pl.BlockSpec((tm, tc), lambda i, j, k: (i, k)),
                pl.BlockSpec((None, tn, tc), lambda i, j, k: (k // nc, j, k % nc))]
    args = [dy, w]
    if res is not None:
        in_specs.append(pl.BlockSpec((tm, tn), lambda i, j, k: (i, j)))
        args.append(res)
    return pl.pallas_call(
        body, name=name, grid=(M // tm, Kw // tn, nk), in_specs=in_specs,
        out_specs=pl.BlockSpec((tm, tn), lambda i, j, k: (i, j)),
        out_shape=jax.ShapeDtypeStruct((M, Kw), out_dtype),
        scratch_shapes=[pltpu.VMEM((tm, tn), F32)],
        compiler_params=_cp("parallel", "parallel", "arbitrary"))(*args)


def _mm_tn(x, dy, nb, *, name, tm=1024, tn=2816, tk=512):
    M, Kw = x.shape
    NW = dy.shape[1] // nb
    tm, tn, tk = _tile(Kw, tm, LANE), _tile(NW, tn, LANE), _tile(M, tk, LANE)
    nj, nk = NW // tn, M // tk

    def body(x_ref, d_ref, o_ref, acc):
        k = pl.program_id(2)
        p = lax.dot_general(x_ref[...], d_ref[...], TN_DIMS, preferred_element_type=F32)

        @pl.when(k == 0)
        def _():
            acc[...] = p

        @pl.when(k > 0)
        def _():
            acc[...] += p

        @pl.when(k == nk - 1)
        def _():
            o_ref[...] = acc[...].astype(o_ref.dtype)

    return pl.pallas_call(
        body, name=name, grid=(nb * nj, Kw // tm, nk),
        in_specs=[pl.BlockSpec((tk, tm), lambda j, i, k: (k, i)),
                  pl.BlockSpec((tk, tn), lambda j, i, k: (k, j))],
        out_specs=pl.BlockSpec((None, tm, tn), lambda j, i, k: (j // nj, i, j % nj)),
        out_shape=jax.ShapeDtypeStruct((nb, Kw, NW), BF16),
        scratch_shapes=[pltpu.VMEM((tm, tn), F32)],
        compiler_params=_cp("parallel", "parallel", "arbitrary"))(x, dy)


def _rms_fwd(x, g, *, name, tr=256):
    T, D = x.shape
    tr = _tile(T, tr, 16)

    def body(x_ref, g_ref, h_ref):
        xv = x_ref[...]
        r = lax.rsqrt(jnp.mean(xv * xv, axis=-1, keepdims=True) + RMS_EPS)
        h_ref[...] = ((xv * r) * g_ref[...]).astype(BF16)

    return pl.pallas_call(
        body, name=name, grid=(T // tr,),
        in_specs=[pl.BlockSpec((tr, D), lambda i: (i, 0)), pl.BlockSpec((1, D), lambda i: (0, 0))],
        out_specs=pl.BlockSpec((tr, D), lambda i: (i, 0)),
        out_shape=jax.ShapeDtypeStruct((T, D), BF16),
        compiler_params=_cp("parallel"))(x, g)


def _rms_bwd(dh, x, g, dres, *, name, tr=128):
    T, D = x.shape
    tr = _tile(T, tr, 16)

    def body(dh_ref, x_ref, g_ref, dr_ref, dx_ref, dxb_ref, dg_ref):
        i = pl.program_id(0)
        xv = x_ref[...]
        dhv = dh_ref[...]
        r = lax.rsqrt(jnp.mean(xv * xv, axis=-1, keepdims=True) + RMS_EPS)
        a = dhv * g_ref[...]
        dot = jnp.mean(a * xv, axis=-1, keepdims=True)
        dx = dr_ref[...] + r * a - xv * (r * r * r * dot)
        dx_ref[...] = dx
        dxb_ref[...] = dx.astype(BF16)
        part = jnp.sum(dhv * (xv * r), axis=0, keepdims=True)

        @pl.when(i == 0)
        def _():
            dg_ref[...] = part

        @pl.when(i > 0)
        def _():
            dg_ref[...] += part

    row = pl.BlockSpec((tr, D), lambda i: (i, 0))
    vec = pl.BlockSpec((1, D), lambda i: (0, 0))
    return pl.pallas_call(
        body, name=name, grid=(T // tr,),
        in_specs=[row, row, vec, row], out_specs=[row, row, vec],
        out_shape=[jax.ShapeDtypeStruct((T, D), F32), jax.ShapeDtypeStruct((T, D), BF16),
                   jax.ShapeDtypeStruct((1, D), F32)],
        compiler_params=_cp("arbitrary"))(dh, x, g, dres)


def _final_loss(x2, tgt, g, *, name, tr=128):
    T, D = x2.shape
    tr = _tile(T, tr, 16)

    def body(x_ref, t_ref, g_ref, dx_ref, dxb_ref, dg_ref, loss_ref):
        i = pl.program_id(0)
        xv = x_ref[...]
        gv = g_ref[...]
        r = lax.rsqrt(jnp.mean(xv * xv, axis=-1, keepdims=True) + RMS_EPS)
        xh = xv * r
        diff = xh * gv - t_ref[...]
        lpart = 0.5 * jnp.sum(jnp.mean(diff * diff, axis=-1, keepdims=True))
        dy = diff * (1.0 / D)
        a = dy * gv
        dot = jnp.mean(a * xv, axis=-1, keepdims=True)
        dx = r * a - xv * (r * r * r * dot)
        dx_ref[...] = dx
        dxb_ref[...] = dx.astype(BF16)
        part = jnp.sum(dy * xh, axis=0, keepdims=True)
        lblk = jnp.full((8, LANE), lpart, F32)

        @pl.when(i == 0)
        def _():
            dg_ref[...] = part
            loss_ref[...] = lblk

        @pl.when(i > 0)
        def _():
            dg_ref[...] += part
            loss_ref[...] += lblk

    row = pl.BlockSpec((tr, D), lambda i: (i, 0))
    vec = pl.BlockSpec((1, D), lambda i: (0, 0))
    return pl.pallas_call(
        body, name=name, grid=(T // tr,),
        in_specs=[row, row, vec],
        out_specs=[row, row, vec, pl.BlockSpec((8, LANE), lambda i: (0, 0))],
        out_shape=[jax.ShapeDtypeStruct((T, D), F32), jax.ShapeDtypeStruct((T, D), BF16),
                   jax.ShapeDtypeStruct((1, D), F32), jax.ShapeDtypeStruct((8, LANE), F32)],
        compiler_params=_cp("arbitrary"))(x2, tgt, g)


def _merge_fwd(gates, b_gate, ua, ub, *, name, tr=256):
    T, D = ua.shape
    tr = _tile(T, tr, 16)

    def body(ga_ref, gb_ref, ba_ref, bb_ref, ua_ref, ub_ref, o_ref):
        sa = _sigmoid(ga_ref[...].astype(F32) + ba_ref[...])
        sb = _sigmoid(gb_ref[...].astype(F32) + bb_ref[...])
        o_ref[...] = (sa * ua_ref[...].astype(F32) + sb * ub_ref[...].astype(F32)).astype(BF16)

    row = pl.BlockSpec((tr, D), lambda i: (i, 0))
    return pl.pallas_call(
        body, name=name, grid=(T // tr,),
        in_specs=[row, pl.BlockSpec((tr, D), lambda i: (i, 1)),
                  pl.BlockSpec((1, D), lambda i: (0, 0)), pl.BlockSpec((1, D), lambda i: (0, 1)), row, row],
        out_specs=row, out_shape=jax.ShapeDtypeStruct((T, D), BF16),
        compiler_params=_cp("parallel"))(gates, gates, b_gate, b_gate, ua, ub)


def _merge_bwd(dm, gates, b_gate, ua, ub, *, name, tr=256):
    T, D = ua.shape
    tr = _tile(T, tr, 16)

    def body(dm_ref, ga_ref, gb_ref, ba_ref, bb_ref, ua_ref, ub_ref,
             dua_ref, dub_ref, dga_ref, dgb_ref, dba_ref, dbb_ref):
        i = pl.program_id(0)
        dmv = dm_ref[...].astype(F32)
        sa = _sigmoid(ga_ref[...].astype(F32) + ba_ref[...])
        sb = _sigmoid(gb_ref[...].astype(F32) + bb_ref[...])
        dua_ref[...] = (dmv * sa).astype(BF16)
        dub_ref[...] = (dmv * sb).astype(BF16)
        dga = dmv * ua_ref[...].astype(F32) * (sa * (1.0 - sa))
        dgb = dmv * ub_ref[...].astype(F32) * (sb * (1.0 - sb))
        dga_ref[...] = dga.astype(BF16)
        dgb_ref[...] = dgb.astype(BF16)
        pa = jnp.sum(dga, axis=0, keepdims=True)
        pb = jnp.sum(dgb, axis=0, keepdims=True)

        @pl.when(i == 0)
        def _():
            dba_ref[...] = pa
            dbb_ref[...] = pb

        @pl.when(i > 0)
        def _():
            dba_ref[...] += pa
            dbb_ref[...] += pb

    row = pl.BlockSpec((tr, D), lambda i: (i, 0))
    vec = pl.BlockSpec((1, D), lambda i: (0, 0))
    act = jax.ShapeDtypeStruct((T, D), BF16)
    return pl.pallas_call(
        body, name=name, grid=(T // tr,),
        in_specs=[row, row, pl.BlockSpec((tr, D), lambda i: (i, 1)),
                  vec, pl.BlockSpec((1, D), lambda i: (0, 1)), row, row],
        out_specs=[row, row, row, row, vec, vec],
        out_shape=[act, act, act, act, jax.ShapeDtypeStruct((1, D), F32), jax.ShapeDtypeStruct((1, D), F32)],
        compiler_params=_cp("arbitrary"))(dm, gates, gates, b_gate, b_gate, ua, ub)


def _swiglu_fwd(gt, up, *, name, tr=256, tc=1408):
    T, F = gt.shape
    tr, tc = _tile(T, tr, 16), _tile(F, tc, LANE)

    def body(g_ref, u_ref, o_ref):
        g = g_ref[...].astype(F32)
        o_ref[...] = (g * _sigmoid(g) * u_ref[...].astype(F32)).astype(BF16)

    blk = pl.BlockSpec((tr, tc), lambda i, j: (i, j))
    return pl.pallas_call(
        body, name=name, grid=(T // tr, F // tc), in_specs=[blk, blk], out_specs=blk,
        out_shape=jax.ShapeDtypeStruct((T, F), BF16), compiler_params=_cp("parallel", "parallel"))(gt, up)


def _swiglu_bwd(dh, gt, up, *, name, tr=256, tc=1408):
    T, F = gt.shape
    tr, tc = _tile(T, tr, 16), _tile(F, tc, LANE)

    def body(d_ref, g_ref, u_ref, dg_ref, du_ref):
        d = d_ref[...].astype(F32)
        g = g_ref[...].astype(F32)
        s = _sigmoid(g)
        du_ref[...] = (d * (g * s)).astype(BF16)
        dg_ref[...] = (d * u_ref[...].astype(F32) * (s * (1.0 + g * (1.0 - s)))).astype(BF16)

    blk = pl.BlockSpec((tr, tc), lambda i, j: (i, j))
    act = jax.ShapeDtypeStruct((T, F), BF16)
    return pl.pallas_call(
        body, name=name, grid=(T // tr, F // tc), in_specs=[blk, blk, blk], out_specs=[blk, blk],
        out_shape=[act, act], compiler_params=_cp("parallel", "parallel"))(dh, gt, up)


def _bias_index(qb, kw):
    i = jnp.arange(qb)[:, None]
    j = jnp.arange(kw)[None, :]
    idx = jnp.clip((kw - qb) + i - j, -MAX_REL, MAX_REL) + MAX_REL
    band = (j // CHUNK >= i // CHUNK) & (j // CHUNK <= i // CHUNK + LEFT_CHUNKS)
    return idx, band


def _attn_a_fwd(q, kp, vp, btab, *, name):
    T, WA = q.shape
    HA = WA // HEAD_DIM
    _, QB, KW = btab.shape
    padk = KW - QB

    def body(q_ref, k_ref, v_ref, b_ref, o_ref, lse_ref):
        i = pl.program_id(1)
        start = pl.multiple_of(i * QB, QB)
        k = k_ref[pl.ds(start, KW), :]
        v = v_ref[pl.ds(start, KW), :]
        s = lax.dot_general(q_ref[...], k, NT_DIMS, preferred_element_type=F32) * ATTN_SCALE + b_ref[...]
        kpos = start - padk + lax.broadcasted_iota(jnp.int32, (QB, KW), 1)
        s = jnp.where(kpos >= 0, s, NEG_INF)
        m = jnp.max(s, axis=-1, keepdims=True)
        p = jnp.exp(s - m)
        l = jnp.sum(p, axis=-1, keepdims=True)
        o = jnp.dot(p.astype(BF16), v, preferred_element_type=F32) / l
        o_ref[...] = o.astype(BF16)
        lse_ref[...] = jnp.broadcast_to(m + jnp.log(l), (QB, HEAD_DIM))

    qspec = pl.BlockSpec((QB, HEAD_DIM), lambda h, i: (i, h))
    kspec = pl.BlockSpec((T + padk, HEAD_DIM), lambda h, i: (0, h))
    return pl.pallas_call(
        body, name=name, grid=(HA, T // QB),
        in_specs=[qspec, kspec, kspec, pl.BlockSpec((None, QB, KW), lambda h, i: (h, 0, 0))],
        out_specs=[qspec, qspec],
        out_shape=[jax.ShapeDtypeStruct((T, WA), BF16), jax.ShapeDtypeStruct((T, WA), F32)],
        compiler_params=_cp("parallel", "arbitrary"))(q, kp, vp, btab)


def _attn_a_bwd(q, kp, vp, btab, o, do, lse, *, name):
    T, WA = q.shape
    HA = WA // HEAD_DIM
    _, QB, KW = btab.shape
    padk = KW - QB

    def body(q_ref, k_ref, v_ref, b_ref, o_ref, do_ref, lse_ref, dq_ref, dk_ref, dv_ref, db_ref):
        i = pl.program_id(1)

        @pl.when(i == 0)
        def _():
            dk_ref[...] = jnp.zeros_like(dk_ref)
            dv_ref[...] = jnp.zeros_like(dv_ref)
            db_ref[...] = jnp.zeros_like(db_ref)

        start = pl.multiple_of(i * QB, QB)
        qv = q_ref[...]
        dov = do_ref[...]
        k = k_ref[pl.ds(start, KW), :]
        v = v_ref[pl.ds(start, KW), :]
        s = lax.dot_general(qv, k, NT_DIMS, preferred_element_type=F32) * ATTN_SCALE + b_ref[...]
        kpos = start - padk + lax.broadcasted_iota(jnp.int32, (QB, KW), 1)
        s = jnp.where(kpos >= 0, s, NEG_INF)
        p = jnp.exp(s - lse_ref[:, :1])
        dp = lax.dot_general(dov, v, NT_DIMS, preferred_element_type=F32)
        delta = jnp.sum(dov.astype(F32) * o_ref[...].astype(F32), axis=-1, keepdims=True)
        ds = p * (dp - delta)
        db_ref[...] += ds
        dsb = ds.astype(BF16)
        dq_ref[...] = (jnp.dot(dsb, k, preferred_element_type=F32) * ATTN_SCALE).astype(BF16)
        dk_ref[pl.ds(start, KW), :] += lax.dot_general(dsb, qv, TN_DIMS, preferred_element_type=F32) * ATTN_SCALE
        dv_ref[pl.ds(start, KW), :] += lax.dot_general(p.astype(BF16), dov, TN_DIMS, preferred_element_type=F32)

    qspec = pl.BlockSpec((QB, HEAD_DIM), lambda h, i: (i, h))
    kspec = pl.BlockSpec((T + padk, HEAD_DIM), lambda h, i: (0, h))
    bspec = pl.BlockSpec((None, QB, KW), lambda h, i: (h, 0, 0))
    return pl.pallas_call(
        body, name=name, grid=(HA, T // QB),
        in_specs=[qspec, kspec, kspec, bspec, qspec, qspec, qspec],
        out_specs=[qspec, kspec, kspec, bspec],
        out_shape=[jax.ShapeDtypeStruct((T, WA), BF16), jax.ShapeDtypeStruct((T + padk, WA), F32),
                   jax.ShapeDtypeStruct((T + padk, WA), F32), jax.ShapeDtypeStruct((HA, QB, KW), F32)],
        compiler_params=_cp("parallel", "arbitrary"))(q, kp, vp, btab, o, do, lse)


def _relbias_fold(skew, dtab, *, name):
    HA, qb, kw = dtab.shape
    W = kw + 1

    def body(s_ref, t_ref, col_ref, tail_ref):
        row = lax.broadcasted_iota(jnp.int32, (qb, W), 0)
        col = lax.broadcasted_iota(jnp.int32, (qb, W), 1)
        col_ref[...] = jnp.sum(jnp.where(row + col < kw, s_ref[...], 0.0), axis=0, keepdims=True)
        i = lax.broadcasted_iota(jnp.int32, (qb, kw), 0)
        j = lax.broadcasted_iota(jnp.int32, (qb, kw), 1)
        tail = jnp.sum(jnp.where((kw - qb) + i - j >= MAX_REL, t_ref[...], 0.0))
        tail_ref[...] = jnp.full((1, LANE), tail, F32)

    return pl.pallas_call(
        body, name=name, grid=(HA,),
        in_specs=[pl.BlockSpec((None, qb, W), lambda h: (h, 0, 0)), pl.BlockSpec((None, qb, kw), lambda h: (h, 0, 0))],
        out_specs=[pl.BlockSpec((None, 1, W), lambda h: (h, 0, 0)), pl.BlockSpec((None, 1, LANE), lambda h: (h, 0, 0))],
        out_shape=[jax.ShapeDtypeStruct((HA, 1, W), F32), jax.ShapeDtypeStruct((HA, 1, LANE), F32)],
        compiler_params=_cp("parallel"))(skew, dtab)


CUM_BLOCK = 128


def _fgate_fwd(f, bf, *, name):
    T = f.shape[0]
    nb = T // CUM_BLOCK

    def body(f_ref, b_ref, cum_ref):
        r = lax.broadcasted_iota(jnp.int32, (CUM_BLOCK, CUM_BLOCK), 0)
        c = lax.broadcasted_iota(jnp.int32, (CUM_BLOCK, CUM_BLOCK), 1)
        tri = (c <= r).astype(F32)

        def step(b, carry):
            r0 = pl.multiple_of(b * CUM_BLOCK, CUM_BLOCK)
            z = f_ref[pl.ds(r0, CUM_BLOCK), :] + b_ref[...]
            lf = jnp.minimum(z, 0.0) - jnp.log(1.0 + jnp.exp(-jnp.abs(z)))
            cs = jnp.dot(tri, lf, precision=lax.Precision.HIGHEST, preferred_element_type=F32) + carry
            cum_ref[pl.ds(r0, CUM_BLOCK), :] = cs
            return carry + jnp.sum(lf, axis=0, keepdims=True)

        lax.fori_loop(0, nb, step, jnp.zeros((1, LANE), F32))

    return pl.pallas_call(
        body, name=name, out_shape=jax.ShapeDtypeStruct((T, LANE), F32),
        compiler_params=_cp())(f, bf)


def _fgate_bwd(dcq, dck, f, bf, *, name):
    T = f.shape[0]
    nb = T // CUM_BLOCK

    def body(dq_ref, dk_ref, f_ref, b_ref, df_ref, dbf_ref):
        r = lax.broadcasted_iota(jnp.int32, (CUM_BLOCK, CUM_BLOCK), 0)
        c = lax.broadcasted_iota(jnp.int32, (CUM_BLOCK, CUM_BLOCK), 1)
        tri = (c >= r).astype(F32)

        def step(n, carry):
            tail, dbf = carry
            r0 = pl.multiple_of((nb - 1 - n) * CUM_BLOCK, CUM_BLOCK)
            dc = dq_ref[pl.ds(r0, CUM_BLOCK), :] + dk_ref[pl.ds(r0, CUM_BLOCK), :]
            ss = jnp.dot(tri, dc, precision=lax.Precision.HIGHEST, preferred_element_type=F32) + tail
            z = f_ref[pl.ds(r0, CUM_BLOCK), :] + b_ref[...]
            df = ss * _sigmoid(-z)
            df_ref[pl.ds(r0, CUM_BLOCK), :] = df
            return tail + jnp.sum(dc, axis=0, keepdims=True), dbf + jnp.sum(df, axis=0, keepdims=True)

        zero = jnp.zeros((1, LANE), F32)
        _, dbf = lax.fori_loop(0, nb, step, (zero, zero))
        dbf_ref[...] = dbf

    return pl.pallas_call(
        body, name=name,
        out_shape=[jax.ShapeDtypeStruct((T, LANE), F32), jax.ShapeDtypeStruct((1, LANE), F32)],
        compiler_params=_cp())(dcq, dck, f, bf)


def _fox_logits(qv, kk, cq, ck, k0, qpos, koff):
    s = lax.dot_general(qv, kk, NT_DIMS, preferred_element_type=F32) * ATTN_SCALE + cq - ck
    return jnp.where(k0 + koff <= qpos, s, NEG_INF)


def _fox_fwd(q, k, v, cum, cum_t, *, name):
    T, WB = q.shape
    HB = WB // HEAD_DIM
    KB = cum_t.shape[2]
    QB = KB

    def body(q_ref, k_ref, v_ref, c_ref, ct_ref, o_ref, lse_ref):
        h = pl.program_id(0)
        i = pl.program_id(1)
        qv = q_ref[...]
        lane = lax.broadcasted_iota(jnp.int32, (QB, LANE), 1)
        cq = jnp.sum(jnp.where(lane == h, c_ref[...], 0.0), axis=-1, keepdims=True)
        qpos = i * QB + lax.broadcasted_iota(jnp.int32, (QB, KB), 0)
        koff = lax.broadcasted_iota(jnp.int32, (QB, KB), 1)

        def step(kb, carry):
            m, l, acc = carry
            k0 = pl.multiple_of(kb * KB, KB)
            s = _fox_logits(qv, k_ref[pl.ds(k0, KB), :], cq, ct_ref[pl.ds(kb, 1), :], k0, qpos, koff)
            m_new = jnp.maximum(m, jnp.max(s, axis=-1, keepdims=True))
            alpha = jnp.exp(m - m_new)
            p = jnp.exp(s - m_new)
            l = alpha * l + jnp.sum(p, axis=-1, keepdims=True)
            acc = alpha * acc + jnp.dot(p.astype(BF16), v_ref[pl.ds(k0, KB), :], preferred_element_type=F32)
            return m_new, l, acc

        m, l, acc = lax.fori_loop(
            0, i + 1, step,
            (jnp.full((QB, 1), NEG_INF, F32), jnp.zeros((QB, 1), F32), jnp.zeros((QB, HEAD_DIM), F32)))
        o_ref[...] = (acc / l).astype(BF16)
        lse_ref[...] = jnp.broadcast_to(m + jnp.log(l), (QB, HEAD_DIM))

    qspec = pl.BlockSpec((QB, HEAD_DIM), lambda h, i: (i, h))
    kspec = pl.BlockSpec((T, HEAD_DIM), lambda h, i: (0, h))
    return pl.pallas_call(
        body, name=name, grid=(HB, T // QB),
        in_specs=[qspec, kspec, kspec, pl.BlockSpec((QB, LANE), lambda h, i: (i, 0)),
                  pl.BlockSpec((None, T // KB, KB), lambda h, i: (h, 0, 0))],
        out_specs=[qspec, qspec],
        out_shape=[jax.ShapeDtypeStruct((T, WB), BF16), jax.ShapeDtypeStruct((T, WB), F32)],
        compiler_params=_cp("parallel", "arbitrary"))(q, k, v, cum, cum_t)


def _fox_bwd(q, k, v, cum, cum_t, o, do, lse, *, name):
    T, WB = q.shape
    HB = WB // HEAD_DIM
    KB = cum_t.shape[2]
    QB = KB

    def body(q_ref, k_ref, v_ref, c_ref, ct_ref, o_ref, do_ref, lse_ref,
             dq_ref, dk_ref, dv_ref, dcq_ref, dck_ref):
        h = pl.program_id(0)
        i = pl.program_id(1)

        @pl.when(i == 0)
        def _():
            dk_ref[...] = jnp.zeros_like(dk_ref)
            dv_ref[...] = jnp.zeros_like(dv_ref)
            dck_ref[...] = jnp.zeros_like(dck_ref)

        qv = q_ref[...]
        dov = do_ref[...]
        lse_v = lse_ref[:, :1]
        delta = jnp.sum(dov.astype(F32) * o_ref[...].astype(F32), axis=-1, keepdims=True)
        lane = lax.broadcasted_iota(jnp.int32, (QB, LANE), 1)
        cq = jnp.sum(jnp.where(lane == h, c_ref[...], 0.0), axis=-1, keepdims=True)
        qpos = i * QB + lax.broadcasted_iota(jnp.int32, (QB, KB), 0)
        koff = lax.broadcasted_iota(jnp.int32, (QB, KB), 1)

        def step(kb, carry):
            dq, dcq = carry
            k0 = pl.multiple_of(kb * KB, KB)
            kk = k_ref[pl.ds(k0, KB), :]
            s = _fox_logits(qv, kk, cq, ct_ref[pl.ds(kb, 1), :], k0, qpos, koff)
            p = jnp.exp(s - lse_v)
            dp = lax.dot_general(dov, v_ref[pl.ds(k0, KB), :], NT_DIMS, preferred_element_type=F32)
            ds = p * (dp - delta)
            dsb = ds.astype(BF16)
            dk_ref[pl.ds(k0, KB), :] += lax.dot_general(dsb, qv, TN_DIMS, preferred_element_type=F32) * ATTN_SCALE
            dv_ref[pl.ds(k0, KB), :] += lax.dot_general(p.astype(BF16), dov, TN_DIMS, preferred_element_type=F32)
            dck_ref[pl.ds(kb, 1), :] -= jnp.sum(ds, axis=0, keepdims=True)
            return (dq + jnp.dot(dsb, kk, preferred_element_type=F32),
                    dcq + jnp.sum(ds, axis=-1, keepdims=True))

        dq, dcq = lax.fori_loop(0, i + 1, step, (jnp.zeros((QB, HEAD_DIM), F32), jnp.zeros((QB, 1), F32)))
        dq_ref[...] = (dq * ATTN_SCALE).astype(BF16)
        dcq_ref[...] = jnp.broadcast_to(dcq, (QB, HEAD_DIM))

    qspec = pl.BlockSpec((QB, HEAD_DIM), lambda h, i: (i, h))
    kspec = pl.BlockSpec((T, HEAD_DIM), lambda h, i: (0, h))
    cspec = pl.BlockSpec((QB, LANE), lambda h, i: (i, 0))
    tspec = pl.BlockSpec((None, T // KB, KB), lambda h, i: (h, 0, 0))
    wide = jax.ShapeDtypeStruct((T, WB), F32)
    return pl.pallas_call(
        body, name=name, grid=(HB, T // QB),
        in_specs=[qspec, kspec, kspec, cspec, tspec, qspec, qspec, qspec],
        out_specs=[qspec, kspec, kspec, qspec, tspec],
        out_shape=[jax.ShapeDtypeStruct((T, WB), BF16), wide, wide, wide,
                   jax.ShapeDtypeStruct((HB, T // KB, KB), F32)],
        compiler_params=_cp("parallel", "arbitrary"))(q, k, v, cum, cum_t, o, do, lse)


def _place():
    return lax.axis_index("x"), lax.axis_index("y"), lax.axis_index("c")


def _other_chips(x, y):
    return [(1 - x, y), (x, 1 - y), (1 - x, 1 - y)]


def _any_specs(n):
    return [pl.BlockSpec(memory_space=pl.ANY)] * n


def _all_gather(shards, *, name):
    n = len(shards)

    def body(*refs):
        srcs, outs = refs[:n], refs[n:2 * n]
        send_sems, recv_sems, local_sems = refs[2 * n:]
        x, y, c = _place()
        me, sib = (x, y, c), (x, y, 1 - c)
        chips = _other_chips(x, y)

        def copy(t, k, block, to, src=None):
            rows = outs[t].at[4 * block[0] + 2 * block[1] + block[2]]
            return pltpu.make_async_remote_copy(
                src_ref=rows if src is None else src, dst_ref=rows,
                send_sem=send_sems.at[7 * t + k], recv_sem=recv_sems.at[7 * t + k],
                device_id=to, device_id_type=MESH)

        mine = [pltpu.make_async_copy(srcs[t], outs[t].at[4 * x + 2 * y + c], local_sems.at[t]) for t in range(n)]
        for cp in mine:
            cp.start()
        first = []
        for t in range(n):
            first.append(copy(t, 0, me, sib, src=srcs[t]))
            first += [copy(t, 1 + j, me, (*chip, c), src=srcs[t]) for j, chip in enumerate(chips)]
        for cp in first:
            cp.start()
        passed = []
        for j, chip in enumerate(chips):
            for t in range(n):
                copy(t, 1 + j, (*chip, c), me).wait_recv()
                fwd = copy(t, 4 + j, (*chip, c), sib)
                fwd.start()
                passed.append(fwd)
        for t in range(n):
            copy(t, 0, sib, me).wait_recv()
            for j, chip in enumerate(chips):
                copy(t, 4 + j, (*chip, 1 - c), me).wait_recv()
        for cp in first + passed:
            cp.wait_send()
        for cp in mine:
            cp.wait()

    return pl.pallas_call(
        body, name=name, in_specs=_any_specs(n), out_specs=_any_specs(n),
        out_shape=[jax.ShapeDtypeStruct((N_DEV,) + s.shape, s.dtype) for s in shards],
        scratch_shapes=[pltpu.SemaphoreType.DMA((7 * n,)), pltpu.SemaphoreType.DMA((7 * n,)),
                        pltpu.SemaphoreType.DMA((n,))])(*shards)


def _exchange_sibling(grads, *, name):
    n = len(grads)

    def body(*refs):
        srcs, outs = refs[:n], refs[n:2 * n]
        send_sems, recv_sems = refs[2 * n:]
        x, y, c = _place()
        copies = []
        for t in range(n):
            for p in range(4):
                copies.append(pltpu.make_async_remote_copy(
                    src_ref=srcs[t].at[2 * p + (1 - c)], dst_ref=outs[t].at[p],
                    send_sem=send_sems.at[4 * t + p], recv_sem=recv_sems.at[4 * t + p],
                    device_id=(x, y, 1 - c), device_id_type=MESH))
        for cp in copies:
            cp.start()
        for cp in copies:
            cp.wait_recv()
        for cp in copies:
            cp.wait_send()

    return pl.pallas_call(
        body, name=name, in_specs=_any_specs(n), out_specs=_any_specs(n),
        out_shape=[jax.ShapeDtypeStruct((4,) + g.shape[1:], g.dtype) for g in grads],
        scratch_shapes=[pltpu.SemaphoreType.DMA((4 * n,)), pltpu.SemaphoreType.DMA((4 * n,))])(*grads)


def _exchange_chips(sums, *, name):
    n = len(sums)

    def body(*refs):
        srcs, outs = refs[:n], refs[n:2 * n]
        send_sems, recv_sems = refs[2 * n:]
        x, y, c = _place()
        copies = []
        for t in range(n):
            for j, (cx, cy) in enumerate(_other_chips(x, y)):
                copies.append(pltpu.make_async_remote_copy(
                    src_ref=srcs[t].at[2 * cx + cy], dst_ref=outs[t].at[j],
                    send_sem=send_sems.at[3 * t + j], recv_sem=recv_sems.at[3 * t + j],
                    device_id=(cx, cy, c), device_id_type=MESH))
        for cp in copies:
            cp.start()
        for cp in copies:
            cp.wait_recv()
        for cp in copies:
            cp.wait_send()

    return pl.pallas_call(
        body, name=name, in_specs=_any_specs(n), out_specs=_any_specs(n),
        out_shape=[jax.ShapeDtypeStruct((3,) + s.shape[1:], s.dtype) for s in sums],
        scratch_shapes=[pltpu.SemaphoreType.DMA((3 * n,)), pltpu.SemaphoreType.DMA((3 * n,))])(*sums)


def _chip_sum(g, r1, core, *, name, tr=256):
    _, R, C = g.shape
    tr = _tile(R, tr, 16)

    def body(c_ref, g_ref, r_ref, o_ref):
        o_ref[...] = (g_ref[...].astype(F32) + r_ref[...].astype(F32)).astype(BF16)

    return pl.pallas_call(
        body, name=name,
        grid_spec=pltpu.PrefetchScalarGridSpec(
            num_scalar_prefetch=1, grid=(4, R // tr),
            in_specs=[pl.BlockSpec((None, tr, C), lambda p, i, c: (2 * p + c[0], i, 0)),
                      pl.BlockSpec((None, tr, C), lambda p, i, c: (p, i, 0))],
            out_specs=pl.BlockSpec((None, tr, C), lambda p, i, c: (p, i, 0))),
        out_shape=jax.ShapeDtypeStruct((4, R, C), BF16),
        compiler_params=_cp("parallel", "parallel"))(core, g, r1)


def _adamw_math(w, g, m, v):
    m = ADAM_B1 * m + (1.0 - ADAM_B1) * g
    v = ADAM_B2 * v + (1.0 - ADAM_B2) * (g * g)
    m_hat = m / (1.0 - ADAM_B1 ** ADAM_STEP)
    v_hat = v / (1.0 - ADAM_B2 ** ADAM_STEP)
    delta = -ADAM_LR * (m_hat / (jnp.sqrt(v_hat) + ADAM_EPS) + ADAM_WD * w)
    return delta, m, v


def _adamw_shard(s1, r2, chip, w, m, v, *, name, tr=64):
    R0, C0 = w.shape
    C = s1.shape[2]
    tr = _tile(R0, tr, 8)

    def body(p_ref, s_ref, r0_ref, r1_ref, r2_ref, w_ref, m_ref, v_ref, g_out, d_out, m_out, v_out):
        g = s_ref[:, :C0].astype(F32)
        for r_ref in (r0_ref, r1_ref, r2_ref):
            g = g + r_ref[:, :C0].astype(F32)
        delta, mn, vn = _adamw_math(w_ref[...], g, m_ref[...], v_ref[...])
        g_out[...] = g
        d_out[...] = delta
        m_out[...] = mn
        v_out[...] = vn

    full = pl.BlockSpec((tr, C0), lambda i, p: (i, 0))
    out = jax.ShapeDtypeStruct((R0, C0), F32)
    return pl.pallas_call(
        body, name=name,
        grid_spec=pltpu.PrefetchScalarGridSpec(
            num_scalar_prefetch=1, grid=(R0 // tr,),
            in_specs=[pl.BlockSpec((None, tr, C), lambda i, p: (p[0], i, 0)),
                      pl.BlockSpec((None, tr, C), lambda i, p: (0, i, 0)),
                      pl.BlockSpec((None, tr, C), lambda i, p: (1, i, 0)),
                      pl.BlockSpec((None, tr, C), lambda i, p: (2, i, 0)),
                      full, full, full],
            out_specs=[full, full, full, full]),
        out_shape=[out, out, out, out],
        compiler_params=_cp("parallel"))(chip, s1, r2, r2, r2, w, m, v)


def _all_reduce_small(part, *, name):
    SR = part.shape[0]

    def body(p_ref, o_ref, buf, send_sems, recv_sems):
        x, y, c = _place()
        me = 4 * x + 2 * y + c
        buf[0] = p_ref[...]
        copies = []
        for k in range(1, N_DEV):
            peer = (1 - x if k & 4 else x, 1 - y if k & 2 else y, 1 - c if k & 1 else c)
            copies.append(pltpu.make_async_remote_copy(
                src_ref=p_ref, dst_ref=buf.at[k], send_sem=send_sems.at[k - 1], recv_sem=recv_sems.at[k - 1],
                device_id=peer, device_id_type=MESH))
        for cp in copies:
            cp.start()
        for cp in copies:
            cp.wait_recv()
        for cp in copies:
            cp.wait_send()
        acc = buf[me]
        for d in range(1, N_DEV):
            acc = acc + buf[jnp.bitwise_xor(me, d)]
        o_ref[...] = acc

    return pl.pallas_call(
        body, name=name,
        in_specs=[pl.BlockSpec(memory_space=pltpu.VMEM)], out_specs=pl.BlockSpec(memory_space=pltpu.VMEM),
        out_shape=jax.ShapeDtypeStruct(part.shape, F32),
        scratch_shapes=[pltpu.VMEM((N_DEV, SR, LANE), F32), pltpu.SemaphoreType.DMA((N_DEV - 1,)),
                        pltpu.SemaphoreType.DMA((N_DEV - 1,))])(part)


def _adamw_small(w, g, m, v, *, name):
    def body(w_ref, g_ref, m_ref, v_ref, d_out, m_out, v_out):
        delta, mn, vn = _adamw_math(w_ref[...], g_ref[...], m_ref[...], v_ref[...])
        d_out[...] = delta
        m_out[...] = mn
        v_out[...] = vn

    out = jax.ShapeDtypeStruct(w.shape, F32)
    return pl.pallas_call(body, name=name, out_shape=[out, out, out], compiler_params=_cp())(w, g, m, v)


def _pack(parts, rows):
    flat = jnp.concatenate([p.reshape(-1).astype(F32) for p in parts])
    return jnp.pad(flat, (0, rows * LANE - flat.shape[0])).reshape(rows, LANE)


def _unpack(packed, shapes):
    flat = packed.reshape(-1)
    out, at = [], 0
    for s in shapes:
        n = 1
        for d in s:
            n *= d
        out.append(flat[at:at + n].reshape(s))
        at += n
    return out


def kernel(x, g_mix, w_in, b_f, b_gate, rel_bias, w_branch_a, w_branch_b, w_out, g_ffn, w_gate_ffn, w_up_ffn, w_down_ffn, g_final, loss_target, m_g_mix, m_w_in, m_b_f, m_b_gate, m_rel_bias, m_w_branch_a, m_w_branch_b, m_w_out, m_g_ffn, m_w_gate_ffn, m_w_up_ffn, m_w_down_ffn, m_g_final, v_g_mix, v_w_in, v_b_f, v_b_gate, v_rel_bias, v_w_branch_a, v_w_branch_b, v_w_out, v_g_ffn, v_w_gate_ffn, v_w_up_ffn, v_w_down_ffn, v_g_final):
    _, T, D = x.shape
    HA, n_rel = rel_bias.shape[1], rel_bias.shape[2]
    HB = b_f.shape[1]
    WA, WB = HA * HEAD_DIM, HB * HEAD_DIM
    din8, f8, d8 = w_in.shape[2], w_gate_ffn.shape[2], D // N_DEV
    DIN = N_DEV * din8
    PIN, PF = _pad_to(din8, LANE), _pad_to(f8, LANE)
    padk = LEFT_CHUNKS * CHUNK
    QA = min(256, T)
    KW = QA + padk
    KB = min(256, T)
    x2d, tgt = x[0], loss_target[0]

    shards = [
        jnp.pad(w_in[0].astype(BF16), ((0, 0), (0, PIN - din8))),
        w_branch_a[0].astype(BF16), w_branch_b[0].astype(BF16), w_out[0].astype(BF16),
        jnp.pad(w_gate_ffn[0].astype(BF16), ((0, 0), (0, PF - f8))),
        jnp.pad(w_up_ffn[0].astype(BF16), ((0, 0), (0, PF - f8))),
        jnp.pad(w_down_ffn[0].astype(BF16), ((0, PF - f8), (0, 0))),
    ]
    Win, Wa, Wb, Wout, Wg, Wu, Wd = _all_gather(shards, name="all_gather_weights")
    Wout = Wout.reshape(1, D, D)
    Wd = Wd.reshape(1, N_DEV * PF, D)
    fcol = 3 * WA + 3 * WB
    Wf = Win[:, :, :din8].transpose(1, 0, 2).reshape(D, DIN)[:, fcol:fcol + HB]
    Wf = jnp.pad(Wf, ((0, 0), (0, LANE - HB))).reshape(1, D, LANE)
    bf_pad = jnp.pad(b_f, ((0, 0), (0, LANE - HB)))

    h = _rms_fwd(x2d, g_mix, name="rms_mix")
    proj = _mm_nn(h, Win, name="proj_in", out_dtype=BF16)
    proj = proj.reshape(T, N_DEV, PIN)[:, :, :din8].reshape(T, DIN)
    qa, ka, va = proj[:, :WA], proj[:, WA:2 * WA], proj[:, 2 * WA:3 * WA]
    qb, kb, vb = (proj[:, 3 * WA + n * WB:3 * WA + (n + 1) * WB] for n in range(3))
    gates = proj[:, fcol + HB:]
    flog = _mm_nn(h, Wf, name="proj_forget", out_dtype=F32)

    kap = jnp.pad(ka, ((padk, 0), (0, 0)))
    vap = jnp.pad(va, ((padk, 0), (0, 0)))
    bidx, band = _bias_index(QA, KW)
    btab = jnp.where(band[None], rel_bias[0][:, bidx], NEG_INF)
    o_a, lse_a = _attn_a_fwd(qa, kap, vap, btab, name="attn_a_fwd")

    cum = _fgate_fwd(flog, bf_pad, name="forget_cumsum")
    cum_t = cum[:, :HB].T.reshape(HB, T // KB, KB)
    o_b, lse_b = _fox_fwd(qb, kb, vb, cum, cum_t, name="fox_fwd")

    u_a = _mm_nn(o_a, Wa, name="branch_a", out_dtype=BF16)
    u_b = _mm_nn(o_b, Wb, name="branch_b", out_dtype=BF16)
    merged = _merge_fwd(gates, b_gate, u_a, u_b, name="merge_fwd")
    x1 = _mm_nn(merged, Wout, name="out_proj", out_dtype=F32, res=x2d, tn=2048)
    h2 = _rms_fwd(x1, g_ffn, name="rms_ffn")
    gt = _mm_nn(h2, Wg, name="ffn_gate", out_dtype=BF16)
    up = _mm_nn(h2, Wu, name="ffn_up", out_dtype=BF16)
    hid = _swiglu_fwd(gt, up, name="swiglu_fwd")
    x2 = _mm_nn(hid, Wd, name="ffn_down", out_dtype=F32, res=x1, tn=2048)
    dx2, dx2b, dg_final, loss_part = _final_loss(x2, tgt, g_final.reshape(1, D), name="final_loss")

    dhid = _mm_nt(dx2b, Wd, name="d_hidden", out_dtype=BF16, tc=2048)
    gWd = _mm_tn(hid, dx2b, 1, name="gw_down", tn=2048).reshape(N_DEV, PF, D)
    dgt, dup = _swiglu_bwd(dhid, gt, up, name="swiglu_bwd")
    dh2 = _mm_nt(dgt, Wg, name="d_h2_gate", out_dtype=F32)
    dh2 = _mm_nt(dup, Wu, name="d_h2_up", out_dtype=F32, res=dh2)
    gWg = _mm_tn(h2, dgt, N_DEV, name="gw_gate")
    gWu = _mm_tn(h2, dup, N_DEV, name="gw_up")
    dx1, dx1b, dg_ffn = _rms_bwd(dh2, x1, g_ffn, dx2, name="rms_ffn_bwd")
    dmer = _mm_nt(dx1b, Wout, name="d_merged", out_dtype=BF16, tc=2048)
    gWout = _mm_tn(merged, dx1b, 1, name="gw_out", tn=2048).reshape(N_DEV, d8, D)
    dua, dub, dga, dgb, dbga, dbgb = _merge_bwd(dmer, gates, b_gate, u_a, u_b, name="merge_bwd")
    doa = _mm_nt(dua, Wa, name="d_o_a", out_dtype=BF16)
    dob = _mm_nt(dub, Wb, name="d_o_b", out_dtype=BF16)
    gWa = _mm_tn(o_a, dua, N_DEV, name="gw_branch_a")
    gWb = _mm_tn(o_b, dub, N_DEV, name="gw_branch_b")

    dqa, dkap, dvap, dtab = _attn_a_bwd(qa, kap, vap, btab, o_a, doa, lse_a, name="attn_a_bwd")
    skew = jnp.pad(dtab.reshape(HA, QA * KW), ((0, 0), (0, QA))).reshape(HA, QA, KW + 1)
    colsum, tail = _relbias_fold(skew, dtab, name="relbias_fold")
    rev = colsum[:, 0, ::-1]
    d_rel = jnp.concatenate([rev[:, :2 * MAX_REL], tail[:, 0, :1]], axis=1)

    dqb, dkb, dvb, dcq, dck = _fox_bwd(qb, kb, vb, cum, cum_t, o_b, dob, lse_b, name="fox_bwd")
    dcq_p = jnp.pad(dcq.reshape(T, HB, HEAD_DIM)[:, :, 0], ((0, 0), (0, LANE - HB)))
    dck_p = jnp.pad(dck.reshape(HB, T).T, ((0, 0), (0, LANE - HB)))
    dflog, dbf = _fgate_bwd(dcq_p, dck_p, flog, bf_pad, name="forget_bwd")

    dproj = jnp.concatenate(
        [dqa, dkap[padk:].astype(BF16), dvap[padk:].astype(BF16), dqb, dkb.astype(BF16), dvb.astype(BF16),
         dflog[:, :HB].astype(BF16), dga, dgb], axis=1)
    dproj = jnp.pad(dproj.reshape(T, N_DEV, din8), ((0, 0), (0, 0), (0, PIN - din8))).reshape(T, N_DEV * PIN)
    dh = _mm_nt(dproj, Win, name="d_h", out_dtype=F32)
    gWin = _mm_tn(h, dproj, N_DEV, name="gw_in")
    dx, _, dg_mix = _rms_bwd(dh, x2d, g_mix, dx1, name="rms_mix_bwd")

    cx, cy, cc = _place()
    core = jnp.reshape(cc, (1,)).astype(jnp.int32)
    chip = jnp.reshape(2 * cx + cy, (1,)).astype(jnp.int32)
    names = ["w_in", "w_branch_a", "w_branch_b", "w_out", "w_gate_ffn", "w_up_ffn", "w_down_ffn"]
    partials = [gWin, gWa, gWb, gWout, gWg, gWu, gWd]
    from_sib = _exchange_sibling(partials, name="exchange_sibling")
    sums = [_chip_sum(g, r, core, name="chip_sum_" + nm) for g, r, nm in zip(partials, from_sib, names)]
    from_chips = _exchange_chips(sums, name="exchange_chips")
    big = {}
    for nm, s1, r2, w, m, v in zip(
            names, sums, from_chips,
            [w_in, w_branch_a, w_branch_b, w_out, w_gate_ffn, w_up_ffn, w_down_ffn],
            [m_w_in, m_w_branch_a, m_w_branch_b, m_w_out, m_w_gate_ffn, m_w_up_ffn, m_w_down_ffn],
            [v_w_in, v_w_branch_a, v_w_branch_b, v_w_out, v_w_gate_ffn, v_w_up_ffn, v_w_down_ffn]):
        outs = _adamw_shard(s1, r2, chip, w[0], m[0], v[0], name="adamw_" + nm)
        big[nm] = [o[None] for o in outs]

    small_names = ["g_mix", "b_f", "b_gate", "rel_bias", "g_ffn", "g_final"]
    small_w = [g_mix, b_f, b_gate, rel_bias, g_ffn, g_final]
    small_m = [m_g_mix, m_b_f, m_b_gate, m_rel_bias, m_g_ffn, m_g_final]
    small_v = [v_g_mix, v_b_f, v_b_gate, v_rel_bias, v_g_ffn, v_g_final]
    small_g = [dg_mix, dbf[:, :HB], jnp.concatenate([dbga, dbgb], axis=1), d_rel[None], dg_ffn, dg_final[0]]
    shapes = [w.shape for w in small_w]
    n_small = sum(w.size for w in small_w) + 1
    rows = _pad_to(-(-n_small // LANE), 8)
    total = _all_reduce_small(_pack(small_g + [loss_part[:1, :1]], rows), name="all_reduce_small")
    d_s, m_s, v_s = _adamw_small(_pack(small_w, rows), total, _pack(small_m, rows), _pack(small_v, rows),
                                 name="adamw_small")
    g_small = dict(zip(small_names, _unpack(total, shapes)))
    d_small = dict(zip(small_names, _unpack(d_s, shapes)))
    m_small = dict(zip(small_names, _unpack(m_s, shapes)))
    v_small = dict(zip(small_names, _unpack(v_s, shapes)))
    loss = total.reshape(-1)[n_small - 1]

    order = ["g_mix", "w_in", "b_f", "b_gate", "rel_bias", "w_branch_a", "w_branch_b", "w_out", "g_ffn",
             "w_gate_ffn", "w_up_ffn", "w_down_ffn", "g_final"]

    def pick(idx, small):
        return [big[nm][idx] if nm in big else small[nm] for nm in order]

    return (loss, dx[None], *pick(0, g_small), *pick(1, d_small), *pick(2, m_small), *pick(3, v_small))
```

```python
import functools

import jax
import jax.numpy as jnp
from jax import lax
from jax.experimental import pallas as pl
from jax.experimental.pallas import tpu as pltpu

F32 = jnp.float32
BF16 = jnp.bfloat16
MESH = pl.DeviceIdType.MESH

N_DEV = 8
LANE = 128
HEAD_DIM = 128
CHUNK = 64
LEFT_CHUNKS = 8
MAX_REL = 256
RMS_EPS = 1e-6
NEG_INF = -1e30
ATTN_SCALE = HEAD_DIM ** -0.5
VMEM_LIMIT_BYTES = 56 * 1024 * 1024

ADAM_LR = 0.001
ADAM_B1 = 0.9
ADAM_B2 = 0.999
ADAM_EPS = 1e-08
ADAM_WD = 0.01
ADAM_STEP = 10

NT_DIMS = (((1,), (1,)), ((), ()))
TN_DIMS = (((0,), (0,)), ((), ()))


def _cp(*sem):
    return pltpu.CompilerParams(dimension_semantics=sem if sem else None, vmem_limit_bytes=VMEM_LIMIT_BYTES)


def _pad_to(n, mult):
    return -(-n // mult) * mult


def _tile(dim, pref, align):
    t = (min(pref, dim) // align) * align
    while t >= align:
        if dim % t == 0:
            return t
        t -= align
    return dim


def _sigmoid(z):
    return 1.0 / (1.0 + jnp.exp(-z))


def _mm_nn(x, w, *, name, out_dtype, res=None, tm=1024, tn=1024, tk=None):
    M, K = x.shape
    NB, _, NW = w.shape
    tm, tn = _tile(M, tm, 16), _tile(NW, tn, LANE)
    tk = K if tk is None else _tile(K, tk, LANE)
    nj, nk = NW // tn, K // tk

    def body(*refs):
        x_ref, w_ref = refs[:2]
        r_ref = refs[2] if res is not None else None
        o_ref = refs[3] if res is not None else refs[2]

        def emit(v):
            if r_ref is not None:
                v = v + r_ref[...]
            o_ref[...] = v.astype(o_ref.dtype)

        if nk == 1:
            emit(jnp.dot(x_ref[...], w_ref[...], preferred_element_type=F32))
            return
        acc = refs[-1]
        k = pl.program_id(2)

        @pl.when(k == 0)
        def _():
            acc[...] = jnp.zeros_like(acc)

        acc[...] += jnp.dot(x_ref[...], w_ref[...], preferred_element_type=F32)

        @pl.when(k == nk - 1)
        def _():
            emit(acc[...])

    in_specs = [pl.BlockSpec((tm, tk), lambda j, i, k: (i, k)),
                pl.BlockSpec((None, tk, tn), lambda j, i, k: (j // nj, k, j % nj))]
    args = [x, w]
    if res is not None:
        in_specs.append(pl.BlockSpec((tm, tn), lambda j, i, k: (i, j)))
        args.append(res)
    return pl.pallas_call(
        body, name=name, grid=(NB * nj, M // tm, nk), in_specs=in_specs,
        out_specs=pl.BlockSpec((tm, tn), lambda j, i, k: (i, j)),
        out_shape=jax.ShapeDtypeStruct((M, NB * NW), out_dtype),
        scratch_shapes=[pltpu.VMEM((tm, tn), F32)] if nk > 1 else [],
        compiler_params=_cp("parallel", "parallel", "arbitrary"))(*args)


def _mm_nt(dy, w, *, name, out_dtype, res=None, tm=1024, tn=1024, tc=2816):
    M, N = dy.shape
    NB, Kw, NW = w.shape
    assert N == NB * NW
    tm, tn, tc = _tile(M, tm, 16), _tile(Kw, tn, LANE), _tile(NW, tc, LANE)
    nc = NW // tc
    nk = NB * nc

    def body(*refs):
        a_ref, w_ref = refs[:2]
        r_ref = refs[2] if res is not None else None
        o_ref = refs[3] if res is not None else refs[2]

        def emit(v):
            if r_ref is not None:
                v = v + r_ref[...]
            o_ref[...] = v.astype(o_ref.dtype)

        if nk == 1:
            emit(lax.dot_general(a_ref[...], w_ref[...], NT_DIMS, preferred_element_type=F32))
            return
        acc = refs[-1]
        k = pl.program_id(2)

        @pl.when(k == 0)
        def _():
            acc[...] = jnp.zeros_like(acc)

        acc[...] += lax.dot_general(a_ref[...], w_ref[...], NT_DIMS, preferred_element_type=F32)

        @pl.when(k == nk - 1)
        def _():
            emit(acc[...])

    in_specs = [pl.BlockSpec((tm, tc), lambda i, j, k: (i, k)),
                pl.BlockSpec((None, tn, tc), lambda i, j, k: (k // nc, j, k % nc))]
    args = [dy, w]
    if res is not None:
        in_specs.append(pl.BlockSpec((tm, tn), lambda i, j, k: (i, j)))
        args.append(res)
    return pl.pallas_call(
        body, name=name, grid=(M // tm, Kw // tn, nk), in_specs=in_specs,
        out_specs=pl.BlockSpec((tm, tn), lambda i, j, k: (i, j)),
        out_shape=jax.ShapeDtypeStruct((M, Kw), out_dtype),
        scratch_shapes=[pltpu.VMEM((tm, tn), F32)] if nk > 1 else [],
        compiler_params=_cp("parallel", "parallel", "arbitrary"))(*args)


def _mm_tn(x, dy, nb, *, name, tm=512, tn=1024):
    M, Kw = x.shape
    NW = dy.shape[1] // nb
    tm, tn = _tile(Kw, tm, LANE), _tile(NW, tn, LANE)
    nj = NW // tn

    def body(x_ref, d_ref, o_ref):
        o_ref[...] = lax.dot_general(x_ref[...], d_ref[...], TN_DIMS, preferred_element_type=F32).astype(o_ref.dtype)

    return pl.pallas_call(
        body, name=name, grid=(nb * nj, Kw // tm),
        in_specs=[pl.BlockSpec((M, tm), lambda j, i: (0, i)),
                  pl.BlockSpec((M, tn), lambda j, i: (0, j))],
        out_specs=pl.BlockSpec((None, tm, tn), lambda j, i: (j // nj, i, j % nj)),
        out_shape=jax.ShapeDtypeStruct((nb, Kw, NW), BF16),
        compiler_params=_cp("parallel", "parallel"))(x, dy)


def _rms_fwd(x, g, *, name, tr=256):
    T, D = x.shape
    tr = _tile(T, tr, 16)

    def body(x_ref, g_ref, h_ref):
        xv = x_ref[...]
        r = lax.rsqrt(jnp.mean(xv * xv, axis=-1, keepdims=True) + RMS_EPS)
        h_ref[...] = ((xv * r) * g_ref[...]).astype(BF16)

    return pl.pallas_call(
        body, name=name, grid=(T // tr,),
        in_specs=[pl.BlockSpec((tr, D), lambda i: (i, 0)), pl.BlockSpec((1, D), lambda i: (0, 0))],
        out_specs=pl.BlockSpec((tr, D), lambda i: (i, 0)),
        out_shape=jax.ShapeDtypeStruct((T, D), BF16),
        compiler_params=_cp("parallel"))(x, g)


def _rms_bwd(dh, x, g, dres, *, name, tr=128):
    T, D = x.shape
    tr = _tile(T, tr, 16)

    def body(dh_ref, x_ref, g_ref, dr_ref, dx_ref, dxb_ref, dg_ref):
        i = pl.program_id(0)
        xv = x_ref[...]
        dhv = dh_ref[...]
        r = lax.rsqrt(jnp.mean(xv * xv, axis=-1, keepdims=True) + RMS_EPS)
        a = dhv * g_ref[...]
        dot = jnp.mean(a * xv, axis=-1, keepdims=True)
        dx = dr_ref[...] + r * a - xv * (r * r * r * dot)
        dx_ref[...] = dx
        dxb_ref[...] = dx.astype(BF16)
        part = jnp.sum(dhv * (xv * r), axis=0, keepdims=True)

        @pl.when(i == 0)
        def _():
            dg_ref[...] = part

        @pl.when(i > 0)
        def _():
            dg_ref[...] += part

    row = pl.BlockSpec((tr, D), lambda i: (i, 0))
    vec = pl.BlockSpec((1, D), lambda i: (0, 0))
    return pl.pallas_call(
        body, name=name, grid=(T // tr,),
        in_specs=[row, row, vec, row], out_specs=[row, row, vec],
        out_shape=[jax.ShapeDtypeStruct((T, D), F32), jax.ShapeDtypeStruct((T, D), BF16),
                   jax.ShapeDtypeStruct((1, D), F32)],
        compiler_params=_cp("arbitrary"))(dh, x, g, dres)


def _final_loss(x2, tgt, g, *, name, tr=128):
    T, D = x2.shape
    tr = _tile(T, tr, 16)

    def body(x_ref, t_ref, g_ref, dx_ref, dxb_ref, dg_ref, loss_ref):
        i = pl.program_id(0)
        xv = x_ref[...]
        gv = g_ref[...]
        r = lax.rsqrt(jnp.mean(xv * xv, axis=-1, keepdims=True) + RMS_EPS)
        xh = xv * r
        diff = xh * gv - t_ref[...]
        lpart = 0.5 * jnp.sum(jnp.mean(diff * diff, axis=-1, keepdims=True))
        dy = diff * (1.0 / D)
        a = dy * gv
        dot = jnp.mean(a * xv, axis=-1, keepdims=True)
        dx = r * a - xv * (r * r * r * dot)
        dx_ref[...] = dx
        dxb_ref[...] = dx.astype(BF16)
        part = jnp.sum(dy * xh, axis=0, keepdims=True)
        lblk = jnp.full((8, LANE), lpart, F32)

        @pl.when(i == 0)
        def _():
            dg_ref[...] = part
            loss_ref[...] = lblk

        @pl.when(i > 0)
        def _():
            dg_ref[...] += part
            loss_ref[...] += lblk

    row = pl.BlockSpec((tr, D), lambda i: (i, 0))
    vec = pl.BlockSpec((1, D), lambda i: (0, 0))
    return pl.pallas_call(
        body, name=name, grid=(T // tr,),
        in_specs=[row, row, vec],
        out_specs=[row, row, vec, pl.BlockSpec((8, LANE), lambda i: (0, 0))],
        out_shape=[jax.ShapeDtypeStruct((T, D), F32), jax.ShapeDtypeStruct((T, D), BF16),
                   jax.ShapeDtypeStruct((1, D), F32), jax.ShapeDtypeStruct((8, LANE), F32)],
        compiler_params=_cp("arbitrary"))(x2, tgt, g)


def _merge_fwd(gates, b_gate, ua, ub, *, name, tr=256):
    T, D = ua.shape
    tr = _tile(T, tr, 16)

    def body(ga_ref, gb_ref, ba_ref, bb_ref, ua_ref, ub_ref, o_ref):
        sa = _sigmoid(ga_ref[...].astype(F32) + ba_ref[...])
        sb = _sigmoid(gb_ref[...].astype(F32) + bb_ref[...])
        o_ref[...] = (sa * ua_ref[...].astype(F32) + sb * ub_ref[...].astype(F32)).astype(BF16)

    row = pl.BlockSpec((tr, D), lambda i: (i, 0))
    return pl.pallas_call(
        body, name=name, grid=(T // tr,),
        in_specs=[row, pl.BlockSpec((tr, D), lambda i: (i, 1)),
                  pl.BlockSpec((1, D), lambda i: (0, 0)), pl.BlockSpec((1, D), lambda i: (0, 1)), row, row],
        out_specs=row, out_shape=jax.ShapeDtypeStruct((T, D), BF16),
        compiler_params=_cp("parallel"))(gates, gates, b_gate, b_gate, ua, ub)


def _merge_bwd(dm, gates, b_gate, ua, ub, *, name, tr=256):
    T, D = ua.shape
    tr = _tile(T, tr, 16)

    def body(dm_ref, ga_ref, gb_ref, ba_ref, bb_ref, ua_ref, ub_ref, dua_ref, dub_ref, dg_ref, db_ref):
        i = pl.program_id(0)
        dmv = dm_ref[...].astype(F32)
        sa = _sigmoid(ga_ref[...].astype(F32) + ba_ref[...])
        sb = _sigmoid(gb_ref[...].astype(F32) + bb_ref[...])
        dua_ref[...] = (dmv * sa).astype(BF16)
        dub_ref[...] = (dmv * sb).astype(BF16)
        dga = dmv * ua_ref[...].astype(F32) * (sa * (1.0 - sa))
        dgb = dmv * ub_ref[...].astype(F32) * (sb * (1.0 - sb))
        dg_ref[:, :D] = dga.astype(BF16)
        dg_ref[:, D:] = dgb.astype(BF16)
        pa = jnp.sum(dga, axis=0, keepdims=True)
        pb = jnp.sum(dgb, axis=0, keepdims=True)

        @pl.when(i == 0)
        def _():
            db_ref[:, :D] = pa
            db_ref[:, D:] = pb

        @pl.when(i > 0)
        def _():
            db_ref[:, :D] += pa
            db_ref[:, D:] += pb

    row = pl.BlockSpec((tr, D), lambda i: (i, 0))
    vec = pl.BlockSpec((1, D), lambda i: (0, 0))
    act = jax.ShapeDtypeStruct((T, D), BF16)
    return pl.pallas_call(
        body, name=name, grid=(T // tr,),
        in_specs=[row, row, pl.BlockSpec((tr, D), lambda i: (i, 1)),
                  vec, pl.BlockSpec((1, D), lambda i: (0, 1)), row, row],
        out_specs=[row, row, pl.BlockSpec((tr, 2 * D), lambda i: (i, 0)), pl.BlockSpec((1, 2 * D), lambda i: (0, 0))],
        out_shape=[act, act, jax.ShapeDtypeStruct((T, 2 * D), BF16), jax.ShapeDtypeStruct((1, 2 * D), F32)],
        compiler_params=_cp("arbitrary"))(dm, gates, gates, b_gate, b_gate, ua, ub)


def _swiglu_fwd(gt, up, *, name, tr=256, tc=1408):
    T, F = gt.shape
    tr, tc = _tile(T, tr, 16), _tile(F, tc, LANE)

    def body(g_ref, u_ref, o_ref):
        g = g_ref[...].astype(F32)
        o_ref[...] = (g * _sigmoid(g) * u_ref[...].astype(F32)).astype(BF16)

    blk = pl.BlockSpec((tr, tc), lambda i, j: (i, j))
    return pl.pallas_call(
        body, name=name, grid=(T // tr, F // tc), in_specs=[blk, blk], out_specs=blk,
        out_shape=jax.ShapeDtypeStruct((T, F), BF16), compiler_params=_cp("parallel", "parallel"))(gt, up)


def _swiglu_bwd(dh, gt, up, *, name, tr=256, tc=1408):
    T, F = gt.shape
    tr, tc = _tile(T, tr, 16), _tile(F, tc, LANE)

    def body(d_ref, g_ref, u_ref, dg_ref, du_ref):
        d = d_ref[...].astype(F32)
        g = g_ref[...].astype(F32)
        s = _sigmoid(g)
        du_ref[...] = (d * (g * s)).astype(BF16)
        dg_ref[...] = (d * u_ref[...].astype(F32) * (s * (1.0 + g * (1.0 - s)))).astype(BF16)

    blk = pl.BlockSpec((tr, tc), lambda i, j: (i, j))
    act = jax.ShapeDtypeStruct((T, F), BF16)
    return pl.pallas_call(
        body, name=name, grid=(T // tr, F // tc), in_specs=[blk, blk, blk], out_specs=[blk, blk],
        out_shape=[act, act], compiler_params=_cp("parallel", "parallel"))(dh, gt, up)


def _bias_table(rel, qb, kw):
    assert qb - 1 <= MAX_REL
    ha = rel.shape[0]
    n_clipped = kw - MAX_REL
    e = jnp.concatenate([jnp.broadcast_to(rel[:, 2 * MAX_REL:], (ha, n_clipped)),
                         rel[:, 2 * MAX_REL - 1:MAX_REL - qb:-1]], axis=1)
    period = qb + kw
    e = jnp.pad(e, ((0, 0), (0, period - e.shape[1])))
    z = jnp.broadcast_to(e[:, None, :], (ha, qb, period)).reshape(ha, qb * period)
    z = z[:, :qb * (period - 1)].reshape(ha, qb, period - 1)
    i = jnp.arange(qb)[:, None]
    j = jnp.arange(kw)[None, :]
    band = (j // CHUNK >= i // CHUNK) & (j // CHUNK <= i // CHUNK + LEFT_CHUNKS)
    return jnp.where(band[None], z[:, :, qb - 1:qb - 1 + kw], NEG_INF)


def _attn_a_fwd(q, kp, vp, btab, *, name):
    T, WA = q.shape[0], kp.shape[1]
    HA = WA // HEAD_DIM
    _, QB, KW = btab.shape
    padk = KW - QB

    def body(q_ref, k_ref, v_ref, b_ref, o_ref, lse_ref):
        i = pl.program_id(1)
        start = pl.multiple_of(i * QB, QB)
        k = k_ref[pl.ds(start, KW), :]
        v = v_ref[pl.ds(start, KW), :]
        s = lax.dot_general(q_ref[...], k, NT_DIMS, preferred_element_type=F32) * ATTN_SCALE + b_ref[...]
        kpos = start - padk + lax.broadcasted_iota(jnp.int32, (QB, KW), 1)
        s = jnp.where(kpos >= 0, s, NEG_INF)
        m = jnp.max(s, axis=-1, keepdims=True)
        p = jnp.exp(s - m)
        l = jnp.sum(p, axis=-1, keepdims=True)
        o = jnp.dot(p.astype(BF16), v, preferred_element_type=F32) / l
        o_ref[...] = o.astype(BF16)
        lse_ref[...] = jnp.broadcast_to(m + jnp.log(l), (QB, HEAD_DIM))

    qspec = pl.BlockSpec((QB, HEAD_DIM), lambda h, i: (i, h))
    kspec = pl.BlockSpec((T + padk, HEAD_DIM), lambda h, i: (0, h))
    return pl.pallas_call(
        body, name=name, grid=(HA, T // QB),
        in_specs=[qspec, kspec, kspec, pl.BlockSpec((None, QB, KW), lambda h, i: (h, 0, 0))],
        out_specs=[qspec, qspec],
        out_shape=[jax.ShapeDtypeStruct((T, WA), BF16), jax.ShapeDtypeStruct((T, WA), F32)],
        compiler_params=_cp("parallel", "arbitrary"))(q, kp, vp, btab)


def _attn_a_bwd(q, kp, vp, btab, o, do, lse, *, name):
    T, WA = q.shape[0], kp.shape[1]
    HA = WA // HEAD_DIM
    _, QB, KW = btab.shape
    padk = KW - QB

    def body(q_ref, k_ref, v_ref, b_ref, o_ref, do_ref, lse_ref, dq_ref, dk_ref, dv_ref, db_ref):
        i = pl.program_id(1)

        @pl.when(i == 0)
        def _():
            dk_ref[...] = jnp.zeros_like(dk_ref)
            dv_ref[...] = jnp.zeros_like(dv_ref)
            db_ref[...] = jnp.zeros_like(db_ref)

        start = pl.multiple_of(i * QB, QB)
        qv = q_ref[...]
        dov = do_ref[...]
        k = k_ref[pl.ds(start, KW), :]
        v = v_ref[pl.ds(start, KW), :]
        s = lax.dot_general(qv, k, NT_DIMS, preferred_element_type=F32) * ATTN_SCALE + b_ref[...]
        kpos = start - padk + lax.broadcasted_iota(jnp.int32, (QB, KW), 1)
        s = jnp.where(kpos >= 0, s, NEG_INF)
        p = jnp.exp(s - lse_ref[:, :1])
        dp = lax.dot_general(dov, v, NT_DIMS, preferred_element_type=F32)
        delta = jnp.sum(dov.astype(F32) * o_ref[...].astype(F32), axis=-1, keepdims=True)
        ds = p * (dp - delta)
        db_ref[...] += ds
        dsb = ds.astype(BF16)
        dq_ref[...] = (jnp.dot(dsb, k, preferred_element_type=F32) * ATTN_SCALE).astype(BF16)
        dk_ref[pl.ds(start, KW), :] += lax.dot_general(dsb, qv, TN_DIMS, preferred_element_type=F32) * ATTN_SCALE
        dv_ref[pl.ds(start, KW), :] += lax.dot_general(p.astype(BF16), dov, TN_DIMS, preferred_element_type=F32)

    qspec = pl.BlockSpec((QB, HEAD_DIM), lambda h, i: (i, h))
    kspec = pl.BlockSpec((T + padk, HEAD_DIM), lambda h, i: (0, h))
    bspec = pl.BlockSpec((None, QB, KW), lambda h, i: (h, 0, 0))
    return pl.pallas_call(
        body, name=name, grid=(HA, T // QB),
        in_specs=[qspec, kspec, kspec, bspec, qspec, qspec, qspec],
        out_specs=[qspec, kspec, kspec, bspec],
        out_shape=[jax.ShapeDtypeStruct((T, WA), BF16), jax.ShapeDtypeStruct((T + padk, WA), F32),
                   jax.ShapeDtypeStruct((T + padk, WA), F32), jax.ShapeDtypeStruct((HA, QB, KW), F32)],
        compiler_params=_cp("parallel", "arbitrary"))(q, kp, vp, btab, o, do, lse)


def _relbias_fold(skew, dtab, *, name):
    HA, qb, kw = dtab.shape
    W = kw + 1

    def body(s_ref, t_ref, col_ref, tail_ref):
        row = lax.broadcasted_iota(jnp.int32, (qb, W), 0)
        col = lax.broadcasted_iota(jnp.int32, (qb, W), 1)
        col_ref[...] = jnp.sum(jnp.where(row + col < kw, s_ref[...], 0.0), axis=0, keepdims=True)
        i = lax.broadcasted_iota(jnp.int32, (qb, kw), 0)
        j = lax.broadcasted_iota(jnp.int32, (qb, kw), 1)
        tail = jnp.sum(jnp.where((kw - qb) + i - j >= MAX_REL, t_ref[...], 0.0))
        tail_ref[...] = jnp.full((1, LANE), tail, F32)

    return pl.pallas_call(
        body, name=name, grid=(HA,),
        in_specs=[pl.BlockSpec((None, qb, W), lambda h: (h, 0, 0)), pl.BlockSpec((None, qb, kw), lambda h: (h, 0, 0))],
        out_specs=[pl.BlockSpec((None, 1, W), lambda h: (h, 0, 0)), pl.BlockSpec((None, 1, LANE), lambda h: (h, 0, 0))],
        out_shape=[jax.ShapeDtypeStruct((HA, 1, W), F32), jax.ShapeDtypeStruct((HA, 1, LANE), F32)],
        compiler_params=_cp("parallel"))(skew, dtab)


CUM_BLOCK = 128


def _fgate_fwd(f, bf, *, name):
    T = f.shape[0]
    nb = T // CUM_BLOCK

    def body(f_ref, b_ref, cum_ref):
        r = lax.broadcasted_iota(jnp.int32, (CUM_BLOCK, CUM_BLOCK), 0)
        c = lax.broadcasted_iota(jnp.int32, (CUM_BLOCK, CUM_BLOCK), 1)
        tri = (c <= r).astype(F32)

        def step(b, carry):
            r0 = pl.multiple_of(b * CUM_BLOCK, CUM_BLOCK)
            z = f_ref[pl.ds(r0, CUM_BLOCK), :] + b_ref[...]
            lf = jnp.minimum(z, 0.0) - jnp.log(1.0 + jnp.exp(-jnp.abs(z)))
            cs = jnp.dot(tri, lf, precision=lax.Precision.HIGHEST, preferred_element_type=F32) + carry
            cum_ref[pl.ds(r0, CUM_BLOCK), :] = cs
            return carry + jnp.sum(lf, axis=0, keepdims=True)

        lax.fori_loop(0, nb, step, jnp.zeros((1, LANE), F32))

    return pl.pallas_call(
        body, name=name, out_shape=jax.ShapeDtypeStruct((T, LANE), F32),
        compiler_params=_cp())(f, bf)


def _fgate_bwd(dcq, dck, f, bf, *, name):
    T = f.shape[0]
    nb = T // CUM_BLOCK

    def body(dq_ref, dk_ref, f_ref, b_ref, df_ref, dbf_ref):
        r = lax.broadcasted_iota(jnp.int32, (CUM_BLOCK, CUM_BLOCK), 0)
        c = lax.broadcasted_iota(jnp.int32, (CUM_BLOCK, CUM_BLOCK), 1)
        tri = (c >= r).astype(F32)

        def step(n, carry):
            tail, dbf = carry
            r0 = pl.multiple_of((nb - 1 - n) * CUM_BLOCK, CUM_BLOCK)
            dc = dq_ref[pl.ds(r0, CUM_BLOCK), :] + dk_ref[pl.ds(r0, CUM_BLOCK), :]
            ss = jnp.dot(tri, dc, precision=lax.Precision.HIGHEST, preferred_element_type=F32) + tail
            z = f_ref[pl.ds(r0, CUM_BLOCK), :] + b_ref[...]
            df = ss * _sigmoid(-z)
            df_ref[pl.ds(r0, CUM_BLOCK), :] = df
            return tail + jnp.sum(dc, axis=0, keepdims=True), dbf + jnp.sum(df, axis=0, keepdims=True)

        zero = jnp.zeros((1, LANE), F32)
        _, dbf = lax.fori_loop(0, nb, step, (zero, zero))
        dbf_ref[...] = dbf

    return pl.pallas_call(
        body, name=name,
        out_shape=[jax.ShapeDtypeStruct((T, LANE), F32), jax.ShapeDtypeStruct((1, LANE), F32)],
        compiler_params=_cp())(dcq, dck, f, bf)


def _fox_logits(qv, kk, cq, ck, k0, qpos, koff):
    s = lax.dot_general(qv, kk, NT_DIMS, preferred_element_type=F32) * ATTN_SCALE + cq - ck
    return jnp.where(k0 + koff <= qpos, s, NEG_INF)


def _fox_fwd(qkv, cols, cum, cum_t, *, name):
    T = qkv.shape[0]
    HB, _, KB = cum_t.shape
    WB = HB * HEAD_DIM
    QB = KB
    qc, kc, vc = cols

    def body(q_ref, k_ref, v_ref, c_ref, ct_ref, o_ref, lse_ref):
        h = pl.program_id(0)
        i = pl.program_id(1)
        qv = q_ref[...]
        lane = lax.broadcasted_iota(jnp.int32, (QB, LANE), 1)
        cq = jnp.sum(jnp.where(lane == h, c_ref[...], 0.0), axis=-1, keepdims=True)
        qpos = i * QB + lax.broadcasted_iota(jnp.int32, (QB, KB), 0)
        koff = lax.broadcasted_iota(jnp.int32, (QB, KB), 1)

        def step(kb, carry):
            m, l, acc = carry
            k0 = pl.multiple_of(kb * KB, KB)
            s = _fox_logits(qv, k_ref[pl.ds(k0, KB), :], cq, ct_ref[pl.ds(kb, 1), :], k0, qpos, koff)
            m_new = jnp.maximum(m, jnp.max(s, axis=-1, keepdims=True))
            alpha = jnp.exp(m - m_new)
            p = jnp.exp(s - m_new)
            l = alpha * l + jnp.sum(p, axis=-1, keepdims=True)
            acc = alpha * acc + jnp.dot(p.astype(BF16), v_ref[pl.ds(k0, KB), :], preferred_element_type=F32)
            return m_new, l, acc

        m, l, acc = lax.fori_loop(
            0, i + 1, step,
            (jnp.full((QB, 1), NEG_INF, F32), jnp.zeros((QB, 1), F32), jnp.zeros((QB, HEAD_DIM), F32)))
        o_ref[...] = (acc / l).astype(BF16)
        lse_ref[...] = jnp.broadcast_to(m + jnp.log(l), (QB, HEAD_DIM))

    ospec = pl.BlockSpec((QB, HEAD_DIM), lambda h, i: (i, h))
    return pl.pallas_call(
        body, name=name, grid=(HB, T // QB),
        in_specs=[pl.BlockSpec((QB, HEAD_DIM), lambda h, i: (i, qc + h)),
                  pl.BlockSpec((T, HEAD_DIM), lambda h, i: (0, kc + h)),
                  pl.BlockSpec((T, HEAD_DIM), lambda h, i: (0, vc + h)),
                  pl.BlockSpec((QB, LANE), lambda h, i: (i, 0)),
                  pl.BlockSpec((None, T // KB, KB), lambda h, i: (h, 0, 0))],
        out_specs=[ospec, ospec],
        out_shape=[jax.ShapeDtypeStruct((T, WB), BF16), jax.ShapeDtypeStruct((T, WB), F32)],
        compiler_params=_cp("parallel", "arbitrary"))(qkv, qkv, qkv, cum, cum_t)


def _fox_bwd(qkv, cols, cum, cum_t, o, do, lse, *, name):
    T = qkv.shape[0]
    HB, _, KB = cum_t.shape
    WB = HB * HEAD_DIM
    QB = KB
    qc, kc, vc = cols

    def body(q_ref, k_ref, v_ref, c_ref, ct_ref, o_ref, do_ref, lse_ref,
             dq_ref, dk_ref, dv_ref, dcq_ref, dck_ref):
        h = pl.program_id(0)
        i = pl.program_id(1)

        @pl.when(i == 0)
        def _():
            dk_ref[...] = jnp.zeros_like(dk_ref)
            dv_ref[...] = jnp.zeros_like(dv_ref)
            dck_ref[...] = jnp.zeros_like(dck_ref)

        qv = q_ref[...]
        dov = do_ref[...]
        lse_v = lse_ref[:, :1]
        delta = jnp.sum(dov.astype(F32) * o_ref[...].astype(F32), axis=-1, keepdims=True)
        lane = lax.broadcasted_iota(jnp.int32, (QB, LANE), 1)
        cq = jnp.sum(jnp.where(lane == h, c_ref[...], 0.0), axis=-1, keepdims=True)
        qpos = i * QB + lax.broadcasted_iota(jnp.int32, (QB, KB), 0)
        koff = lax.broadcasted_iota(jnp.int32, (QB, KB), 1)

        def step(kb, carry):
            dq, dcq = carry
            k0 = pl.multiple_of(kb * KB, KB)
            kk = k_ref[pl.ds(k0, KB), :]
            s = _fox_logits(qv, kk, cq, ct_ref[pl.ds(kb, 1), :], k0, qpos, koff)
            p = jnp.exp(s - lse_v)
            dp = lax.dot_general(dov, v_ref[pl.ds(k0, KB), :], NT_DIMS, preferred_element_type=F32)
            ds = p * (dp - delta)
            dsb = ds.astype(BF16)
            dk_ref[pl.ds(k0, KB), :] += lax.dot_general(dsb, qv, TN_DIMS, preferred_element_type=F32) * ATTN_SCALE
            dv_ref[pl.ds(k0, KB), :] += lax.dot_general(p.astype(BF16), dov, TN_DIMS, preferred_element_type=F32)
            dck_ref[pl.ds(kb, 1), :] -= jnp.sum(ds, axis=0, keepdims=True)
            return (dq + jnp.dot(dsb, kk, preferred_element_type=F32),
                    dcq + jnp.sum(ds, axis=-1, keepdims=True))

        dq, dcq = lax.fori_loop(0, i + 1, step, (jnp.zeros((QB, HEAD_DIM), F32), jnp.zeros((QB, 1), F32)))
        dq_ref[...] = (dq * ATTN_SCALE).astype(BF16)
        dcq_ref[...] = jnp.broadcast_to(dcq, (QB, HEAD_DIM))

    qspec = pl.BlockSpec((QB, HEAD_DIM), lambda h, i: (i, h))
    kspec = pl.BlockSpec((T, HEAD_DIM), lambda h, i: (0, h))
    cspec = pl.BlockSpec((QB, LANE), lambda h, i: (i, 0))
    tspec = pl.BlockSpec((None, T // KB, KB), lambda h, i: (h, 0, 0))
    wide = jax.ShapeDtypeStruct((T, WB), F32)
    return pl.pallas_call(
        body, name=name, grid=(HB, T // QB),
        in_specs=[pl.BlockSpec((QB, HEAD_DIM), lambda h, i: (i, qc + h)),
                  pl.BlockSpec((T, HEAD_DIM), lambda h, i: (0, kc + h)),
                  pl.BlockSpec((T, HEAD_DIM), lambda h, i: (0, vc + h)),
                  cspec, tspec, qspec, qspec, qspec],
        out_specs=[qspec, kspec, kspec, qspec, tspec],
        out_shape=[jax.ShapeDtypeStruct((T, WB), BF16), wide, wide, wide,
                   jax.ShapeDtypeStruct((HB, T // KB, KB), F32)],
        compiler_params=_cp("parallel", "arbitrary"))(qkv, qkv, qkv, cum, cum_t, o, do, lse)


def _place():
    return lax.axis_index("x"), lax.axis_index("y"), lax.axis_index("c")


def _other_chips(x, y):
    return [(1 - x, y), (x, 1 - y), (1 - x, 1 - y)]


def _any_specs(n):
    return [pl.BlockSpec(memory_space=pl.ANY)] * n


def _all_gather(shards, *, name):
    n = len(shards)

    def body(*refs):
        srcs, outs = refs[:n], refs[n:2 * n]
        send_sems, recv_sems, local_sems = refs[2 * n:]
        x, y, c = _place()
        me, sib = (x, y, c), (x, y, 1 - c)
        chips = _other_chips(x, y)

        def copy(t, k, block, to, src=None):
            rows = outs[t].at[4 * block[0] + 2 * block[1] + block[2]]
            return pltpu.make_async_remote_copy(
                src_ref=rows if src is None else src, dst_ref=rows,
                send_sem=send_sems.at[7 * t + k], recv_sem=recv_sems.at[7 * t + k],
                device_id=to, device_id_type=MESH)

        mine = [pltpu.make_async_copy(srcs[t], outs[t].at[4 * x + 2 * y + c], local_sems.at[t]) for t in range(n)]
        for cp in mine:
            cp.start()
        first = []
        for t in range(n):
            first.append(copy(t, 0, me, sib, src=srcs[t]))
            first += [copy(t, 1 + j, me, (*chip, c), src=srcs[t]) for j, chip in enumerate(chips)]
        for cp in first:
            cp.start()
        passed = []
        for j, chip in enumerate(chips):
            for t in range(n):
                copy(t, 1 + j, (*chip, c), me).wait_recv()
                fwd = copy(t, 4 + j, (*chip, c), sib)
                fwd.start()
                passed.append(fwd)
        for t in range(n):
            copy(t, 0, sib, me).wait_recv()
            for j, chip in enumerate(chips):
                copy(t, 4 + j, (*chip, 1 - c), me).wait_recv()
        for cp in first + passed:
            cp.wait_send()
        for cp in mine:
            cp.wait()

    return pl.pallas_call(
        body, name=name, in_specs=_any_specs(n), out_specs=_any_specs(n),
        out_shape=[jax.ShapeDtypeStruct((N_DEV,) + s.shape, s.dtype) for s in shards],
        scratch_shapes=[pltpu.SemaphoreType.DMA((7 * n,)), pltpu.SemaphoreType.DMA((7 * n,)),
                        pltpu.SemaphoreType.DMA((n,))])(*shards)


def _exchange_sibling(grads, *, name):
    n = len(grads)

    def body(*refs):
        srcs, outs = refs[:n], refs[n:2 * n]
        send_sems, recv_sems = refs[2 * n:]
        x, y, c = _place()
        copies = []
        for t in range(n):
            for p in range(4):
                copies.append(pltpu.make_async_remote_copy(
                    src_ref=srcs[t].at[2 * p + (1 - c)], dst_ref=outs[t].at[p],
                    send_sem=send_sems.at[4 * t + p], recv_sem=recv_sems.at[4 * t + p],
                    device_id=(x, y, 1 - c), device_id_type=MESH))
        for cp in copies:
            cp.start()
        for cp in copies:
            cp.wait_recv()
        for cp in copies:
            cp.wait_send()

    return pl.pallas_call(
        body, name=name, in_specs=_any_specs(n), out_specs=_any_specs(n),
        out_shape=[jax.ShapeDtypeStruct((4,) + g.shape[1:], g.dtype) for g in grads],
        scratch_shapes=[pltpu.SemaphoreType.DMA((4 * n,)), pltpu.SemaphoreType.DMA((4 * n,))])(*grads)


def _exchange_chips(sums, *, name):
    n = len(sums)

    def body(*refs):
        srcs, outs = refs[:n], refs[n:2 * n]
        send_sems, recv_sems = refs[2 * n:]
        x, y, c = _place()
        copies = []
        for t in range(n):
            for j, (cx, cy) in enumerate(_other_chips(x, y)):
                copies.append(pltpu.make_async_remote_copy(
                    src_ref=srcs[t].at[2 * cx + cy], dst_ref=outs[t].at[j],
                    send_sem=send_sems.at[3 * t + j], recv_sem=recv_sems.at[3 * t + j],
                    device_id=(cx, cy, c), device_id_type=MESH))
        for cp in copies:
            cp.start()
        for cp in copies:
            cp.wait_recv()
        for cp in copies:
            cp.wait_send()

    return pl.pallas_call(
        body, name=name, in_specs=_any_specs(n), out_specs=_any_specs(n),
        out_shape=[jax.ShapeDtypeStruct((3,) + s.shape[1:], s.dtype) for s in sums],
        scratch_shapes=[pltpu.SemaphoreType.DMA((3 * n,)), pltpu.SemaphoreType.DMA((3 * n,))])(*sums)


def _chip_sum(g, r1, core, *, name, tr=256):
    _, R, C = g.shape
    tr = _tile(R, tr, 16)

    def body(c_ref, g_ref, r_ref, o_ref):
        o_ref[...] = (g_ref[...].astype(F32) + r_ref[...].astype(F32)).astype(BF16)

    return pl.pallas_call(
        body, name=name,
        grid_spec=pltpu.PrefetchScalarGridSpec(
            num_scalar_prefetch=1, grid=(4, R // tr),
            in_specs=[pl.BlockSpec((None, tr, C), lambda p, i, c: (2 * p + c[0], i, 0)),
                      pl.BlockSpec((None, tr, C), lambda p, i, c: (p, i, 0))],
            out_specs=pl.BlockSpec((None, tr, C), lambda p, i, c: (p, i, 0))),
        out_shape=jax.ShapeDtypeStruct((4, R, C), BF16),
        compiler_params=_cp("parallel", "parallel"))(core, g, r1)


def _adamw_math(w, g, m, v):
    m = ADAM_B1 * m + (1.0 - ADAM_B1) * g
    v = ADAM_B2 * v + (1.0 - ADAM_B2) * (g * g)
    m_hat = m / (1.0 - ADAM_B1 ** ADAM_STEP)
    v_hat = v / (1.0 - ADAM_B2 ** ADAM_STEP)
    delta = -ADAM_LR * (m_hat / (jnp.sqrt(v_hat) + ADAM_EPS) + ADAM_WD * w)
    return delta, m, v


def _adamw_shard(s1, r2, chip, w, m, v, *, name, tr=64):
    R0, C0 = w.shape
    C = s1.shape[2]
    tr = _tile(R0, tr, 8)

    def body(p_ref, s_ref, r0_ref, r1_ref, r2_ref, w_ref, m_ref, v_ref, g_out, d_out, m_out, v_out):
        g = s_ref[:, :C0].astype(F32)
        for r_ref in (r0_ref, r1_ref, r2_ref):
            g = g + r_ref[:, :C0].astype(F32)
        delta, mn, vn = _adamw_math(w_ref[...], g, m_ref[...], v_ref[...])
        g_out[...] = g
        d_out[...] = delta
        m_out[...] = mn
        v_out[...] = vn

    full = pl.BlockSpec((tr, C0), lambda i, p: (i, 0))
    out = jax.ShapeDtypeStruct((R0, C0), F32)
    return pl.pallas_call(
        body, name=name,
        grid_spec=pltpu.PrefetchScalarGridSpec(
            num_scalar_prefetch=1, grid=(R0 // tr,),
            in_specs=[pl.BlockSpec((None, tr, C), lambda i, p: (p[0], i, 0)),
                      pl.BlockSpec((None, tr, C), lambda i, p: (0, i, 0)),
                      pl.BlockSpec((None, tr, C), lambda i, p: (1, i, 0)),
                      pl.BlockSpec((None, tr, C), lambda i, p: (2, i, 0)),
                      full, full, full],
            out_specs=[full, full, full, full]),
        out_shape=[out, out, out, out],
        compiler_params=_cp("parallel"))(chip, s1, r2, r2, r2, w, m, v)


def _all_reduce_small(part, *, name):
    SR = part.shape[0]

    def body(p_ref, o_ref, buf, send_sems, recv_sems):
        x, y, c = _place()
        me = 4 * x + 2 * y + c
        buf[0] = p_ref[...]
        copies = []
        for k in range(1, N_DEV):
            peer = (1 - x if k & 4 else x, 1 - y if k & 2 else y, 1 - c if k & 1 else c)
            copies.append(pltpu.make_async_remote_copy(
                src_ref=p_ref, dst_ref=buf.at[k], send_sem=send_sems.at[k - 1], recv_sem=recv_sems.at[k - 1],
                device_id=peer, device_id_type=MESH))
        for cp in copies:
            cp.start()
        for cp in copies:
            cp.wait_recv()
        for cp in copies:
            cp.wait_send()
        acc = buf[me]
        for d in range(1, N_DEV):
            acc = acc + buf[jnp.bitwise_xor(me, d)]
        o_ref[...] = acc

    return pl.pallas_call(
        body, name=name,
        in_specs=[pl.BlockSpec(memory_space=pltpu.VMEM)], out_specs=pl.BlockSpec(memory_space=pltpu.VMEM),
        out_shape=jax.ShapeDtypeStruct(part.shape, F32),
        scratch_shapes=[pltpu.VMEM((N_DEV, SR, LANE), F32), pltpu.SemaphoreType.DMA((N_DEV - 1,)),
                        pltpu.SemaphoreType.DMA((N_DEV - 1,))])(part)


def _adamw_small(w, g, m, v, *, name):
    def body(w_ref, g_ref, m_ref, v_ref, d_out, m_out, v_out):
        delta, mn, vn = _adamw_math(w_ref[...], g_ref[...], m_ref[...], v_ref[...])
        d_out[...] = delta
        m_out[...] = mn
        v_out[...] = vn

    out = jax.ShapeDtypeStruct(w.shape, F32)
    return pl.pallas_call(body, name=name, out_shape=[out, out, out], compiler_params=_cp())(w, g, m, v)


def _pack(parts, rows):
    flat = jnp.concatenate([p.reshape(-1).astype(F32) for p in parts])
    return jnp.pad(flat, (0, rows * LANE - flat.shape[0])).reshape(rows, LANE)


def _unpack(packed, shapes):
    flat = packed.reshape(-1)
    out, at = [], 0
    for s in shapes:
        n = 1
        for d in s:
            n *= d
        out.append(flat[at:at + n].reshape(s))
        at += n
    return out


def kernel(x, g_mix, w_in, b_f, b_gate, rel_bias, w_branch_a, w_branch_b, w_out, g_ffn, w_gate_ffn, w_up_ffn, w_down_ffn, g_final, loss_target, m_g_mix, m_w_in, m_b_f, m_b_gate, m_rel_bias, m_w_branch_a, m_w_branch_b, m_w_out, m_g_ffn, m_w_gate_ffn, m_w_up_ffn, m_w_down_ffn, m_g_final, v_g_mix, v_w_in, v_b_f, v_b_gate, v_rel_bias, v_w_branch_a, v_w_branch_b, v_w_out, v_g_ffn, v_w_gate_ffn, v_w_up_ffn, v_w_down_ffn, v_g_final):
    _, T, D = x.shape
    HA, n_rel = rel_bias.shape[1], rel_bias.shape[2]
    HB = b_f.shape[1]
    WA, WB = HA * HEAD_DIM, HB * HEAD_DIM
    din8, f8, d8 = w_in.shape[2], w_gate_ffn.shape[2], D // N_DEV
    DIN = N_DEV * din8
    PIN, PF = _pad_to(din8, LANE), _pad_to(f8, LANE)
    padk = LEFT_CHUNKS * CHUNK
    QA = min(256, T)
    KW = QA + padk
    KB = min(512, T)
    x2d, tgt = x[0], loss_target[0]

    shards = [
        jnp.pad(w_in[0].astype(BF16), ((0, 0), (0, PIN - din8))),
        w_branch_a[0].astype(BF16), w_branch_b[0].astype(BF16), w_out[0].astype(BF16),
        jnp.pad(w_gate_ffn[0].astype(BF16), ((0, 0), (0, PF - f8))),
        jnp.pad(w_up_ffn[0].astype(BF16), ((0, 0), (0, PF - f8))),
        jnp.pad(w_down_ffn[0].astype(BF16), ((0, PF - f8), (0, 0))),
    ]
    Win, Wa, Wb, Wout, Wg, Wu, Wd = _all_gather(shards, name="all_gather_weights")
    Wout = Wout.reshape(1, D, D)
    Wd = Wd.reshape(1, N_DEV * PF, D)
    fcol = 3 * WA + 3 * WB
    Wfull = Win[:, :, :din8].transpose(1, 0, 2).reshape(D, DIN)
    Wqkv = Wfull[:, :fcol].reshape(1, D, fcol)
    Wf = jnp.pad(Wfull[:, fcol:fcol + HB], ((0, 0), (0, LANE - HB))).reshape(1, D, LANE)
    Wgt = Wfull[:, fcol + HB:].reshape(1, D, 2 * D)
    bf_pad = jnp.pad(b_f, ((0, 0), (0, LANE - HB)))
    fox_cols = (3 * HA, 3 * HA + HB, 3 * HA + 2 * HB)

    h = _rms_fwd(x2d, g_mix, name="rms_mix")
    qkv = _mm_nn(h, Wqkv, name="proj_qkv", out_dtype=BF16)
    gates = _mm_nn(h, Wgt, name="proj_gates", out_dtype=BF16)
    flog = _mm_nn(h, Wf, name="proj_forget", out_dtype=F32)

    kap = jnp.pad(qkv[:, WA:2 * WA], ((padk, 0), (0, 0)))
    vap = jnp.pad(qkv[:, 2 * WA:3 * WA], ((padk, 0), (0, 0)))
    btab = _bias_table(rel_bias[0], QA, KW)
    o_a, lse_a = _attn_a_fwd(qkv, kap, vap, btab, name="attn_a_fwd")

    cum = _fgate_fwd(flog, bf_pad, name="forget_cumsum")
    cum_t = cum[:, :HB].T.reshape(HB, T // KB, KB)
    o_b, lse_b = _fox_fwd(qkv, fox_cols, cum, cum_t, name="fox_fwd")

    u_a = _mm_nn(o_a, Wa, name="branch_a", out_dtype=BF16)
    u_b = _mm_nn(o_b, Wb, name="branch_b", out_dtype=BF16)
    merged = _merge_fwd(gates, b_gate, u_a, u_b, name="merge_fwd")
    x1 = _mm_nn(merged, Wout, name="out_proj", out_dtype=F32, res=x2d)
    h2 = _rms_fwd(x1, g_ffn, name="rms_ffn")
    gt = _mm_nn(h2, Wg, name="ffn_gate", out_dtype=BF16, tn=PF)
    up = _mm_nn(h2, Wu, name="ffn_up", out_dtype=BF16, tn=PF)
    hid = _swiglu_fwd(gt, up, name="swiglu_fwd")
    x2 = _mm_nn(hid, Wd, name="ffn_down", out_dtype=F32, res=x1, tk=2 * PF)
    dx2, dx2b, dg_final, loss_part = _final_loss(x2, tgt, g_final.reshape(1, D), name="final_loss")

    dhid = _mm_nt(dx2b, Wd, name="d_hidden", out_dtype=BF16)
    gWd = _mm_tn(hid, dx2b, 1, name="gw_down").reshape(N_DEV, PF, D)
    dgt, dup = _swiglu_bwd(dhid, gt, up, name="swiglu_bwd")
    dh2 = _mm_nt(dgt, Wg, name="d_h2_gate", out_dtype=F32)
    dh2 = _mm_nt(dup, Wu, name="d_h2_up", out_dtype=F32, res=dh2)
    gWg = _mm_tn(h2, dgt, N_DEV, name="gw_gate", tn=PF)
    gWu = _mm_tn(h2, dup, N_DEV, name="gw_up", tn=PF)
    dx1, dx1b, dg_ffn = _rms_bwd(dh2, x1, g_ffn, dx2, name="rms_ffn_bwd")
    dmer = _mm_nt(dx1b, Wout, name="d_merged", out_dtype=BF16)
    gWout = _mm_tn(merged, dx1b, 1, name="gw_out").reshape(N_DEV, d8, D)
    dua, dub, dgates, dbg = _merge_bwd(dmer, gates, b_gate, u_a, u_b, name="merge_bwd")
    doa = _mm_nt(dua, Wa, name="d_o_a", out_dtype=BF16)
    dob = _mm_nt(dub, Wb, name="d_o_b", out_dtype=BF16)
    gWa = _mm_tn(o_a, dua, N_DEV, name="gw_branch_a")
    gWb = _mm_tn(o_b, dub, N_DEV, name="gw_branch_b")

    dqa, dkap, dvap, dtab = _attn_a_bwd(qkv, kap, vap, btab, o_a, doa, lse_a, name="attn_a_bwd")
    skew = jnp.pad(dtab.reshape(HA, QA * KW), ((0, 0), (0, QA))).reshape(HA, QA, KW + 1)
    colsum, tail = _relbias_fold(skew, dtab, name="relbias_fold")
    rev = colsum[:, 0, ::-1]
    d_rel = jnp.concatenate([rev[:, :2 * MAX_REL], tail[:, 0, :1]], axis=1)

    dqb, dkb, dvb, dcq, dck = _fox_bwd(qkv, fox_cols, cum, cum_t, o_b, dob, lse_b, name="fox_bwd")
    dcq_p = jnp.pad(dcq.reshape(T, HB, HEAD_DIM)[:, :, 0], ((0, 0), (0, LANE - HB)))
    dck_p = jnp.pad(dck.reshape(HB, T).T, ((0, 0), (0, LANE - HB)))
    dflog, dbf = _fgate_bwd(dcq_p, dck_p, flog, bf_pad, name="forget_bwd")

    dqkv = jnp.concatenate(
        [dqa, dkap[padk:].astype(BF16), dvap[padk:].astype(BF16), dqb, dkb.astype(BF16), dvb.astype(BF16)], axis=1)
    dfb = dflog.astype(BF16)
    dh = _mm_nt(dfb, Wf, name="d_h_forget", out_dtype=F32)
    dh = _mm_nt(dqkv, Wqkv, name="d_h_qkv", out_dtype=F32, res=dh)
    dh = _mm_nt(dgates, Wgt, name="d_h_gates", out_dtype=F32, res=dh)
    g_in = jnp.concatenate([_mm_tn(h, dqkv, 1, name="gw_qkv")[0], _mm_tn(h, dfb, 1, name="gw_forget")[0][:, :HB],
                            _mm_tn(h, dgates, 1, name="gw_gates")[0]], axis=1)
    gWin = jnp.pad(g_in.reshape(D, N_DEV, din8).transpose(1, 0, 2), ((0, 0), (0, 0), (0, PIN - din8)))
    dx, _, dg_mix = _rms_bwd(dh, x2d, g_mix, dx1, name="rms_mix_bwd")

    cx, cy, cc = _place()
    core = jnp.reshape(cc, (1,)).astype(jnp.int32)
    chip = jnp.reshape(2 * cx + cy, (1,)).astype(jnp.int32)
    names = ["w_in", "w_branch_a", "w_branch_b", "w_out", "w_gate_ffn", "w_up_ffn", "w_down_ffn"]
    partials = [gWin, gWa, gWb, gWout, gWg, gWu, gWd]
    from_sib = _exchange_sibling(partials, name="exchange_sibling")
    sums = [_chip_sum(g, r, core, name="chip_sum_" + nm) for g, r, nm in zip(partials, from_sib, names)]
    from_chips = _exchange_chips(sums, name="exchange_chips")
    big = {}
    for nm, s1, r2, w, m, v in zip(
            names, sums, from_chips,
            [w_in, w_branch_a, w_branch_b, w_out, w_gate_ffn, w_up_ffn, w_down_ffn],
            [m_w_in, m_w_branch_a, m_w_branch_b, m_w_out, m_w_gate_ffn, m_w_up_ffn, m_w_down_ffn],
            [v_w_in, v_w_branch_a, v_w_branch_b, v_w_out, v_w_gate_ffn, v_w_up_ffn, v_w_down_ffn]):
        outs = _adamw_shard(s1, r2, chip, w[0], m[0], v[0], name="adamw_" + nm)
        big[nm] = [o[None] for o in outs]

    small_names = ["g_mix", "b_f", "b_gate", "rel_bias", "g_ffn", "g_final"]
    small_w = [g_mix, b_f, b_gate, rel_bias, g_ffn, g_final]
    small_m = [m_g_mix, m_b_f, m_b_gate, m_rel_bias, m_g_ffn, m_g_final]
    small_v = [v_g_mix, v_b_f, v_b_gate, v_rel_bias, v_g_ffn, v_g_final]
    small_g = [dg_mix, dbf[:, :HB], dbg, d_rel[None], dg_ffn, dg_final[0]]
    shapes = [w.shape for w in small_w]
    n_small = sum(w.size for w in small_w) + 1
    rows = _pad_to(-(-n_small // LANE), 8)
    total = _all_reduce_small(_pack(small_g + [loss_part[:1, :1]], rows), name="all_reduce_small")
    d_s, m_s, v_s = _adamw_small(_pack(small_w, rows), total, _pack(small_m, rows), _pack(small_v, rows),
                                 name="adamw_small")
    g_small = dict(zip(small_names, _unpack(total, shapes)))
    d_small = dict(zip(small_names, _unpack(d_s, shapes)))
    m_small = dict(zip(small_names, _unpack(m_s, shapes)))
    v_small = dict(zip(small_names, _unpack(v_s, shapes)))
    loss = total.reshape(-1)[n_small - 1]

    order = ["g_mix", "w_in", "b_f", "b_gate", "rel_bias", "w_branch_a", "w_branch_b", "w_out", "g_ffn",
             "w_gate_ffn", "w_up_ffn", "w_down_ffn", "g_final"]

    def pick(idx, small):
        return [big[nm][idx] if nm in big else small[nm] for nm in order]

    return (loss, dx[None], *pick(0, g_small), *pick(1, d_small), *pick(2, m_small), *pick(3, v_small))
```

```python
import functools

import jax
import jax.numpy as jnp
from jax import lax
from jax.experimental import pallas as pl
from jax.experimental.pallas import tpu as pltpu

F32 = jnp.float32
BF16 = jnp.bfloat16
MESH = pl.DeviceIdType.MESH

N_DEV = 8
LANE = 128
HEAD_DIM = 128
CHUNK = 64
LEFT_CHUNKS = 8
MAX_REL = 256
RMS_EPS = 1e-6
NEG_INF = -1e30
ATTN_SCALE = HEAD_DIM ** -0.5
VMEM_LIMIT_BYTES = 56 * 1024 * 1024

ADAM_LR = 0.001
ADAM_B1 = 0.9
ADAM_B2 = 0.999
ADAM_EPS = 1e-08
ADAM_WD = 0.01
ADAM_STEP = 10

NT_DIMS = (((1,), (1,)), ((), ()))
TN_DIMS = (((0,), (0,)), ((), ()))


def _cp(*sem):
    return pltpu.CompilerParams(dimension_semantics=sem if sem else None, vmem_limit_bytes=VMEM_LIMIT_BYTES)


def _pad_to(n, mult):
    return -(-n // mult) * mult


def _tile(dim, pref, align):
    t = (min(pref, dim) // align) * align
    while t >= align:
        if dim % t == 0:
            return t
        t -= align
    return dim


def _sigmoid(z):
    return 1.0 / (1.0 + jnp.exp(-z))


class _Comm:
    def __init__(self, ins, outs, scratch, begin, end, relay=None):
        self.ins, self.outs, self.scratch = list(ins), list(outs), list(scratch)
        self.begin, self.end, self.relay = begin, end, relay


def _call(body, args, *, name, grid, in_specs, out_specs, out_shape, scratch=(), sem=(), comm=None):
    single = not isinstance(out_shape, (list, tuple))
    out_shape = [out_shape] if single else list(out_shape)
    out_specs = [out_specs] if single else list(out_specs)
    if comm is None:
        res = pl.pallas_call(
            body, name=name, grid=grid, in_specs=list(in_specs), out_specs=out_specs, out_shape=out_shape,
            scratch_shapes=list(scratch), compiler_params=_cp(*sem))(*args)
        return res[0] if single else res
    ni, no, ns = len(args), len(out_shape), len(scratch)
    ci, co = len(comm.ins), len(comm.outs)
    total = 1
    for g in grid:
        total *= g
    relay_at = total - 1 - total // 8

    def hosted(*refs):
        a, cin = refs[:ni], refs[ni:ni + ci]
        o, cout = refs[ni + ci:ni + ci + no], refs[ni + ci + no:ni + ci + no + co]
        s, csc = refs[ni + ci + no + co:ni + ci + no + co + ns], refs[ni + ci + no + co + ns:]
        step = pl.program_id(0)
        for d in range(1, len(grid)):
            step = step * grid[d] + pl.program_id(d)

        @pl.when(step == 0)
        def _():
            comm.begin(cin, cout, csc)

        if comm.relay is not None:
            @pl.when(step == relay_at)
            def _():
                comm.relay(cin, cout, csc)

        body(*a, *o, *s)

        @pl.when(step == total - 1)
        def _():
            comm.end(cin, cout, csc)

    res = pl.pallas_call(
        hosted, name=name, grid=grid, in_specs=list(in_specs) + _any_specs(ci),
        out_specs=out_specs + _any_specs(co), out_shape=out_shape + comm.outs,
        scratch_shapes=list(scratch) + comm.scratch,
        compiler_params=_cp(*(("arbitrary",) * len(grid))))(*args, *comm.ins)
    return (res[0] if single else res[:no]), res[no:]


def _any_specs(n):
    return [pl.BlockSpec(memory_space=pl.ANY)] * n


def _mm_nn(x, w, *, name, out_dtype, res=None, tm=1024, tn=1024, tk=None, comm=None):
    M, K = x.shape
    NB, _, NW = w.shape
    tm, tn = _tile(M, tm, 16), _tile(NW, tn, LANE)
    tk = K if tk is None else _tile(K, tk, LANE)
    nj, nk = NW // tn, K // tk

    def body(*refs):
        x_ref, w_ref = refs[:2]
        r_ref = refs[2] if res is not None else None
        o_ref = refs[3] if res is not None else refs[2]

        def emit(v):
            if r_ref is not None:
                v = v + r_ref[...]
            o_ref[...] = v.astype(o_ref.dtype)

        if nk == 1:
            emit(jnp.dot(x_ref[...], w_ref[...], preferred_element_type=F32))
            return
        acc = refs[-1]
        k = pl.program_id(2)

        @pl.when(k == 0)
        def _():
            acc[...] = jnp.zeros_like(acc)

        acc[...] += jnp.dot(x_ref[...], w_ref[...], preferred_element_type=F32)

        @pl.when(k == nk - 1)
        def _():
            emit(acc[...])

    in_specs = [pl.BlockSpec((tm, tk), lambda j, i, k: (i, k)),
                pl.BlockSpec((None, tk, tn), lambda j, i, k: (j // nj, k, j % nj))]
    args = [x, w]
    if res is not None:
        in_specs.append(pl.BlockSpec((tm, tn), lambda j, i, k: (i, j)))
        args.append(res)
    return _call(
        body, args, name=name, grid=(NB * nj, M // tm, nk), in_specs=in_specs,
        out_specs=pl.BlockSpec((tm, tn), lambda j, i, k: (i, j)),
        out_shape=jax.ShapeDtypeStruct((M, NB * NW), out_dtype),
        scratch=[pltpu.VMEM((tm, tn), F32)] if nk > 1 else [],
        sem=("parallel", "parallel", "arbitrary"), comm=comm)


def _mm_nt(dy, w, *, name, out_dtype, res=None, tm=1024, tn=1024, tc=2816, comm=None):
    M, N = dy.shape
    NB, Kw, NW = w.shape
    assert N == NB * NW
    tm, tn, tc = _tile(M, tm, 16), _tile(Kw, tn, LANE), _tile(NW, tc, LANE)
    nc = NW // tc
    nk = NB * nc

    def body(*refs):
        a_ref, w_ref = refs[:2]
        r_ref = refs[2] if res is not None else None
        o_ref = refs[3] if res is not None else refs[2]

        def emit(v):
            if r_ref is not None:
                v = v + r_ref[...]
            o_ref[...] = v.astype(o_ref.dtype)

        if nk == 1:
            emit(lax.dot_general(a_ref[...], w_ref[...], NT_DIMS, preferred_element_type=F32))
            return
        acc = refs[-1]
        k = pl.program_id(2)

        @pl.when(k == 0)
        def _():
            acc[...] = jnp.zeros_like(acc)

        acc[...] += lax.dot_general(a_ref[...], w_ref[...], NT_DIMS, preferred_element_type=F32)

        @pl.when(k == nk - 1)
        def _():
            emit(acc[...])

    in_specs = [pl.BlockSpec((tm, tc), lambda i, j, k: (i, k)),
                pl.BlockSpec((None, tn, tc), lambda i, j, k: (k // nc, j, k % nc))]
    args = [dy, w]
    if res is not None:
        in_specs.append(pl.BlockSpec((tm, tn), lambda i, j, k: (i, j)))
        args.append(res)
    return _call(
        body, args, name=name, grid=(M // tm, Kw // tn, nk), in_specs=in_specs,
        out_specs=pl.BlockSpec((tm, tn), lambda i, j, k: (i, j)),
        out_shape=jax.ShapeDtypeStruct((M, Kw), out_dtype),
        scratch=[pltpu.VMEM((tm, tn), F32)] if nk > 1 else [],
        sem=("parallel", "parallel", "arbitrary"), comm=comm)


def _mm_tn(x, dy, nb, *, name, tm=512, tn=1024, comm=None):
    M, Kw = x.shape
    NW = dy.shape[1] // nb
    tm, tn = _tile(Kw, tm, LANE), _tile(NW, tn, LANE)
    nj = NW // tn

    def body(x_ref, d_ref, o_ref):
        o_ref[...] = lax.dot_general(x_ref[...], d_ref[...], TN_DIMS, preferred_element_type=F32).astype(o_ref.dtype)

    return _call(
        body, [x, dy], name=name, grid=(nb * nj, Kw // tm),
        in_specs=[pl.BlockSpec((M, tm), lambda j, i: (0, i)),
                  pl.BlockSpec((M, tn), lambda j, i: (0, j))],
        out_specs=pl.BlockSpec((None, tm, tn), lambda j, i: (j // nj, i, j % nj)),
        out_shape=jax.ShapeDtypeStruct((nb, Kw, NW), BF16),
        sem=("parallel", "parallel"), comm=comm)


def _rms_fwd(x, g, *, name, tr=256):
    T, D = x.shape
    tr = _tile(T, tr, 16)

    def body(x_ref, g_ref, h_ref):
        xv = x_ref[...]
        r = lax.rsqrt(jnp.mean(xv * xv, axis=-1, keepdims=True) + RMS_EPS)
        h_ref[...] = ((xv * r) * g_ref[...]).astype(BF16)

    return pl.pallas_call(
        body, name=name, grid=(T // tr,),
        in_specs=[pl.BlockSpec((tr, D), lambda i: (i, 0)), pl.BlockSpec((1, D), lambda i: (0, 0))],
        out_specs=pl.BlockSpec((tr, D), lambda i: (i, 0)),
        out_shape=jax.ShapeDtypeStruct((T, D), BF16),
        compiler_params=_cp("parallel"))(x, g)


def _rms_bwd(dh, x, g, dres, *, name, tr=128):
    T, D = x.shape
    tr = _tile(T, tr, 16)

    def body(dh_ref, x_ref, g_ref, dr_ref, dx_ref, dxb_ref, dg_ref):
        i = pl.program_id(0)
        xv = x_ref[...]
        dhv = dh_ref[...]
        r = lax.rsqrt(jnp.mean(xv * xv, axis=-1, keepdims=True) + RMS_EPS)
        a = dhv * g_ref[...]
        dot = jnp.mean(a * xv, axis=-1, keepdims=True)
        dx = dr_ref[...] + r * a - xv * (r * r * r * dot)
        dx_ref[...] = dx
        dxb_ref[...] = dx.astype(BF16)
        part = jnp.sum(dhv * (xv * r), axis=0, keepdims=True)

        @pl.when(i == 0)
        def _():
            dg_ref[...] = part

        @pl.when(i > 0)
        def _():
            dg_ref[...] += part

    row = pl.BlockSpec((tr, D), lambda i: (i, 0))
    vec = pl.BlockSpec((1, D), lambda i: (0, 0))
    return pl.pallas_call(
        body, name=name, grid=(T // tr,),
        in_specs=[row, row, vec, row], out_specs=[row, row, vec],
        out_shape=[jax.ShapeDtypeStruct((T, D), F32), jax.ShapeDtypeStruct((T, D), BF16),
                   jax.ShapeDtypeStruct((1, D), F32)],
        compiler_params=_cp("arbitrary"))(dh, x, g, dres)


def _final_loss(x2, tgt, g, *, name, tr=128):
    T, D = x2.shape
    tr = _tile(T, tr, 16)

    def body(x_ref, t_ref, g_ref, dx_ref, dxb_ref, dg_ref, loss_ref):
        i = pl.program_id(0)
        xv = x_ref[...]
        gv = g_ref[...]
        r = lax.rsqrt(jnp.mean(xv * xv, axis=-1, keepdims=True) + RMS_EPS)
        xh = xv * r
        diff = xh * gv - t_ref[...]
        lpart = 0.5 * jnp.sum(jnp.mean(diff * diff, axis=-1, keepdims=True))
        dy = diff * (1.0 / D)
        a = dy * gv
        dot = jnp.mean(a * xv, axis=-1, keepdims=True)
        dx = r * a - xv * (r * r * r * dot)
        dx_ref[...] = dx
        dxb_ref[...] = dx.astype(BF16)
        part = jnp.sum(dy * xh, axis=0, keepdims=True)
        lblk = jnp.full((8, LANE), lpart, F32)

        @pl.when(i == 0)
        def _():
            dg_ref[...] = part
            loss_ref[...] = lblk

        @pl.when(i > 0)
        def _():
            dg_ref[...] += part
            loss_ref[...] += lblk

    row = pl.BlockSpec((tr, D), lambda i: (i, 0))
    vec = pl.BlockSpec((1, D), lambda i: (0, 0))
    return pl.pallas_call(
        body, name=name, grid=(T // tr,),
        in_specs=[row, row, vec],
        out_specs=[row, row, vec, pl.BlockSpec((8, LANE), lambda i: (0, 0))],
        out_shape=[jax.ShapeDtypeStruct((T, D), F32), jax.ShapeDtypeStruct((T, D), BF16),
                   jax.ShapeDtypeStruct((1, D), F32), jax.ShapeDtypeStruct((8, LANE), F32)],
        compiler_params=_cp("arbitrary"))(x2, tgt, g)


def _merge_fwd(gates, b_gate, ua, ub, *, name, tr=256):
    T, D = ua.shape
    tr = _tile(T, tr, 16)

    def body(ga_ref, gb_ref, ba_ref, bb_ref, ua_ref, ub_ref, o_ref):
        sa = _sigmoid(ga_ref[...].astype(F32) + ba_ref[...])
        sb = _sigmoid(gb_ref[...].astype(F32) + bb_ref[...])
        o_ref[...] = (sa * ua_ref[...].astype(F32) + sb * ub_ref[...].astype(F32)).astype(BF16)

    row = pl.BlockSpec((tr, D), lambda i: (i, 0))
    return pl.pallas_call(
        body, name=name, grid=(T // tr,),
        in_specs=[row, pl.BlockSpec((tr, D), lambda i: (i, 1)),
                  pl.BlockSpec((1, D), lambda i: (0, 0)), pl.BlockSpec((1, D), lambda i: (0, 1)), row, row],
        out_specs=row, out_shape=jax.ShapeDtypeStruct((T, D), BF16),
        compiler_params=_cp("parallel"))(gates, gates, b_gate, b_gate, ua, ub)


def _merge_bwd(dm, gates, b_gate, ua, ub, *, name, tr=256):
    T, D = ua.shape
    tr = _tile(T, tr, 16)

    def body(dm_ref, ga_ref, gb_ref, ba_ref, bb_ref, ua_ref, ub_ref, dua_ref, dub_ref, dg_ref, db_ref):
        i = pl.program_id(0)
        dmv = dm_ref[...].astype(F32)
        sa = _sigmoid(ga_ref[...].astype(F32) + ba_ref[...])
        sb = _sigmoid(gb_ref[...].astype(F32) + bb_ref[...])
        dua_ref[...] = (dmv * sa).astype(BF16)
        dub_ref[...] = (dmv * sb).astype(BF16)
        dga = dmv * ua_ref[...].astype(F32) * (sa * (1.0 - sa))
        dgb = dmv * ub_ref[...].astype(F32) * (sb * (1.0 - sb))
        dg_ref[:, :D] = dga.astype(BF16)
        dg_ref[:, D:] = dgb.astype(BF16)
        pa = jnp.sum(dga, axis=0, keepdims=True)
        pb = jnp.sum(dgb, axis=0, keepdims=True)

        @pl.when(i == 0)
        def _():
            db_ref[:, :D] = pa
            db_ref[:, D:] = pb

        @pl.when(i > 0)
        def _():
            db_ref[:, :D] += pa
            db_ref[:, D:] += pb

    row = pl.BlockSpec((tr, D), lambda i: (i, 0))
    vec = pl.BlockSpec((1, D), lambda i: (0, 0))
    act = jax.ShapeDtypeStruct((T, D), BF16)
    return pl.pallas_call(
        body, name=name, grid=(T // tr,),
        in_specs=[row, row, pl.BlockSpec((tr, D), lambda i: (i, 1)),
                  vec, pl.BlockSpec((1, D), lambda i: (0, 1)), row, row],
        out_specs=[row, row, pl.BlockSpec((tr, 2 * D), lambda i: (i, 0)), pl.BlockSpec((1, 2 * D), lambda i: (0, 0))],
        out_shape=[act, act, jax.ShapeDtypeStruct((T, 2 * D), BF16), jax.ShapeDtypeStruct((1, 2 * D), F32)],
        compiler_params=_cp("arbitrary"))(dm, gates, gates, b_gate, b_gate, ua, ub)


def _swiglu_fwd(gt, up, *, name, tr=256, tc=1408):
    T, F = gt.shape
    tr, tc = _tile(T, tr, 16), _tile(F, tc, LANE)

    def body(g_ref, u_ref, o_ref):
        g = g_ref[...].astype(F32)
        o_ref[...] = (g * _sigmoid(g) * u_ref[...].astype(F32)).astype(BF16)

    blk = pl.BlockSpec((tr, tc), lambda i, j: (i, j))
    return pl.pallas_call(
        body, name=name, grid=(T // tr, F // tc), in_specs=[blk, blk], out_specs=blk,
        out_shape=jax.ShapeDtypeStruct((T, F), BF16), compiler_params=_cp("parallel", "parallel"))(gt, up)


def _swiglu_bwd(dh, gt, up, *, name, tr=256, tc=1408, comm=None):
    T, F = gt.shape
    tr, tc = _tile(T, tr, 16), _tile(F, tc, LANE)

    def body(d_ref, g_ref, u_ref, dg_ref, du_ref):
        d = d_ref[...].astype(F32)
        g = g_ref[...].astype(F32)
        s = _sigmoid(g)
        du_ref[...] = (d * (g * s)).astype(BF16)
        dg_ref[...] = (d * u_ref[...].astype(F32) * (s * (1.0 + g * (1.0 - s)))).astype(BF16)

    blk = pl.BlockSpec((tr, tc), lambda i, j: (i, j))
    act = jax.ShapeDtypeStruct((T, F), BF16)
    return _call(
        body, [dh, gt, up], name=name, grid=(T // tr, F // tc), in_specs=[blk, blk, blk], out_specs=[blk, blk],
        out_shape=[act, act], sem=("parallel", "parallel"), comm=comm)


def _bias_table(rel, qb, kw):
    assert qb - 1 <= MAX_REL
    ha = rel.shape[0]
    n_clipped = kw - MAX_REL
    e = jnp.concatenate([jnp.broadcast_to(rel[:, 2 * MAX_REL:], (ha, n_clipped)),
                         rel[:, 2 * MAX_REL - 1:MAX_REL - qb:-1]], axis=1)
    period = qb + kw
    e = jnp.pad(e, ((0, 0), (0, period - e.shape[1])))
    z = jnp.broadcast_to(e[:, None, :], (ha, qb, period)).reshape(ha, qb * period)
    z = z[:, :qb * (period - 1)].reshape(ha, qb, period - 1)
    i = jnp.arange(qb)[:, None]
    j = jnp.arange(kw)[None, :]
    band = (j // CHUNK >= i // CHUNK) & (j // CHUNK <= i // CHUNK + LEFT_CHUNKS)
    return jnp.where(band[None], z[:, :, qb - 1:qb - 1 + kw], NEG_INF)


def _attn_a_fwd(q, kp, vp, btab, *, name):
    T, WA = q.shape[0], kp.shape[1]
    HA = WA // HEAD_DIM
    _, QB, KW = btab.shape
    padk = KW - QB

    def body(q_ref, k_ref, v_ref, b_ref, o_ref, lse_ref):
        i = pl.program_id(1)
        start = pl.multiple_of(i * QB, QB)
        k = k_ref[pl.ds(start, KW), :]
        v = v_ref[pl.ds(start, KW), :]
        s = lax.dot_general(q_ref[...], k, NT_DIMS, preferred_element_type=F32) * ATTN_SCALE + b_ref[...]
        kpos = start - padk + lax.broadcasted_iota(jnp.int32, (QB, KW), 1)
        s = jnp.where(kpos >= 0, s, NEG_INF)
        m = jnp.max(s, axis=-1, keepdims=True)
        p = jnp.exp(s - m)
        l = jnp.sum(p, axis=-1, keepdims=True)
        o = jnp.dot(p.astype(BF16), v, preferred_element_type=F32) / l
        o_ref[...] = o.astype(BF16)
        lse_ref[...] = jnp.broadcast_to(m + jnp.log(l), (QB, HEAD_DIM))

    qspec = pl.BlockSpec((QB, HEAD_DIM), lambda h, i: (i, h))
    kspec = pl.BlockSpec((T + padk, HEAD_DIM), lambda h, i: (0, h))
    return pl.pallas_call(
        body, name=name, grid=(HA, T // QB),
        in_specs=[qspec, kspec, kspec, pl.BlockSpec((None, QB, KW), lambda h, i: (h, 0, 0))],
        out_specs=[qspec, qspec],
        out_shape=[jax.ShapeDtypeStruct((T, WA), BF16), jax.ShapeDtypeStruct((T, WA), F32)],
        compiler_params=_cp("parallel", "arbitrary"))(q, kp, vp, btab)


def _attn_a_bwd(q, kp, vp, btab, o, do, lse, *, name, comm=None):
    T, WA = q.shape[0], kp.shape[1]
    HA = WA // HEAD_DIM
    _, QB, KW = btab.shape
    padk = KW - QB

    def body(q_ref, k_ref, v_ref, b_ref, o_ref, do_ref, lse_ref, dq_ref, dk_ref, dv_ref, db_ref):
        i = pl.program_id(1)

        @pl.when(i == 0)
        def _():
            dk_ref[...] = jnp.zeros_like(dk_ref)
            dv_ref[...] = jnp.zeros_like(dv_ref)
            db_ref[...] = jnp.zeros_like(db_ref)

        start = pl.multiple_of(i * QB, QB)
        qv = q_ref[...]
        dov = do_ref[...]
        k = k_ref[pl.ds(start, KW), :]
        v = v_ref[pl.ds(start, KW), :]
        s = lax.dot_general(qv, k, NT_DIMS, preferred_element_type=F32) * ATTN_SCALE + b_ref[...]
        kpos = start - padk + lax.broadcasted_iota(jnp.int32, (QB, KW), 1)
        s = jnp.where(kpos >= 0, s, NEG_INF)
        p = jnp.exp(s - lse_ref[:, :1])
        dp = lax.dot_general(dov, v, NT_DIMS, preferred_element_type=F32)
        delta = jnp.sum(dov.astype(F32) * o_ref[...].astype(F32), axis=-1, keepdims=True)
        ds = p * (dp - delta)
        db_ref[...] += ds
        dsb = ds.astype(BF16)
        dq_ref[...] = (jnp.dot(dsb, k, preferred_element_type=F32) * ATTN_SCALE).astype(BF16)
        dk_ref[pl.ds(start, KW), :] += lax.dot_general(dsb, qv, TN_DIMS, preferred_element_type=F32) * ATTN_SCALE
        dv_ref[pl.ds(start, KW), :] += lax.dot_general(p.astype(BF16), dov, TN_DIMS, preferred_element_type=F32)

    qspec = pl.BlockSpec((QB, HEAD_DIM), lambda h, i: (i, h))
    kspec = pl.BlockSpec((T + padk, HEAD_DIM), lambda h, i: (0, h))
    bspec = pl.BlockSpec((None, QB, KW), lambda h, i: (h, 0, 0))
    return _call(
        body, [q, kp, vp, btab, o, do, lse], name=name, grid=(HA, T // QB),
        in_specs=[qspec, kspec, kspec, bspec, qspec, qspec, qspec],
        out_specs=[qspec, kspec, kspec, bspec],
        out_shape=[jax.ShapeDtypeStruct((T, WA), BF16), jax.ShapeDtypeStruct((T + padk, WA), F32),
                   jax.ShapeDtypeStruct((T + padk, WA), F32), jax.ShapeDtypeStruct((HA, QB, KW), F32)],
        sem=("parallel", "arbitrary"), comm=comm)


def _relbias_fold(skew, dtab, *, name):
    HA, qb, kw = dtab.shape
    W = kw + 1

    def body(s_ref, t_ref, col_ref, tail_ref):
        row = lax.broadcasted_iota(jnp.int32, (qb, W), 0)
        col = lax.broadcasted_iota(jnp.int32, (qb, W), 1)
        col_ref[...] = jnp.sum(jnp.where(row + col < kw, s_ref[...], 0.0), axis=0, keepdims=True)
        i = lax.broadcasted_iota(jnp.int32, (qb, kw), 0)
        j = lax.broadcasted_iota(jnp.int32, (qb, kw), 1)
        tail = jnp.sum(jnp.where((kw - qb) + i - j >= MAX_REL, t_ref[...], 0.0))
        tail_ref[...] = jnp.full((1, LANE), tail, F32)

    return pl.pallas_call(
        body, name=name, grid=(HA,),
        in_specs=[pl.BlockSpec((None, qb, W), lambda h: (h, 0, 0)), pl.BlockSpec((None, qb, kw), lambda h: (h, 0, 0))],
        out_specs=[pl.BlockSpec((None, 1, W), lambda h: (h, 0, 0)), pl.BlockSpec((None, 1, LANE), lambda h: (h, 0, 0))],
        out_shape=[jax.ShapeDtypeStruct((HA, 1, W), F32), jax.ShapeDtypeStruct((HA, 1, LANE), F32)],
        compiler_params=_cp("parallel"))(skew, dtab)


CUM_BLOCK = 128


def _fgate_fwd(f, bf, *, name):
    T = f.shape[0]
    nb = T // CUM_BLOCK

    def body(f_ref, b_ref, cum_ref):
        r = lax.broadcasted_iota(jnp.int32, (CUM_BLOCK, CUM_BLOCK), 0)
        c = lax.broadcasted_iota(jnp.int32, (CUM_BLOCK, CUM_BLOCK), 1)
        tri = (c <= r).astype(F32)

        def step(b, carry):
            r0 = pl.multiple_of(b * CUM_BLOCK, CUM_BLOCK)
            z = f_ref[pl.ds(r0, CUM_BLOCK), :] + b_ref[...]
            lf = jnp.minimum(z, 0.0) - jnp.log(1.0 + jnp.exp(-jnp.abs(z)))
            cs = jnp.dot(tri, lf, precision=lax.Precision.HIGHEST, preferred_element_type=F32) + carry
            cum_ref[pl.ds(r0, CUM_BLOCK), :] = cs
            return carry + jnp.sum(lf, axis=0, keepdims=True)

        lax.fori_loop(0, nb, step, jnp.zeros((1, LANE), F32))

    return pl.pallas_call(
        body, name=name, out_shape=jax.ShapeDtypeStruct((T, LANE), F32),
        compiler_params=_cp())(f, bf)


def _fgate_bwd(dcq, dck, f, bf, *, name):
    T = f.shape[0]
    nb = T // CUM_BLOCK

    def body(dq_ref, dk_ref, f_ref, b_ref, df_ref, dbf_ref):
        r = lax.broadcasted_iota(jnp.int32, (CUM_BLOCK, CUM_BLOCK), 0)
        c = lax.broadcasted_iota(jnp.int32, (CUM_BLOCK, CUM_BLOCK), 1)
        tri = (c >= r).astype(F32)

        def step(n, carry):
            tail, dbf = carry
            r0 = pl.multiple_of((nb - 1 - n) * CUM_BLOCK, CUM_BLOCK)
            dc = dq_ref[pl.ds(r0, CUM_BLOCK), :] + dk_ref[pl.ds(r0, CUM_BLOCK), :]
            ss = jnp.dot(tri, dc, precision=lax.Precision.HIGHEST, preferred_element_type=F32) + tail
            z = f_ref[pl.ds(r0, CUM_BLOCK), :] + b_ref[...]
            df = ss * _sigmoid(-z)
            df_ref[pl.ds(r0, CUM_BLOCK), :] = df
            return tail + jnp.sum(dc, axis=0, keepdims=True), dbf + jnp.sum(df, axis=0, keepdims=True)

        zero = jnp.zeros((1, LANE), F32)
        _, dbf = lax.fori_loop(0, nb, step, (zero, zero))
        dbf_ref[...] = dbf

    return pl.pallas_call(
        body, name=name,
        out_shape=[jax.ShapeDtypeStruct((T, LANE), F32), jax.ShapeDtypeStruct((1, LANE), F32)],
        compiler_params=_cp())(dcq, dck, f, bf)


def _fox_logits(qv, kk, cq, ck, k0, qpos, koff):
    s = lax.dot_general(qv, kk, NT_DIMS, preferred_element_type=F32) * ATTN_SCALE + cq - ck
    return jnp.where(k0 + koff <= qpos, s, NEG_INF)


def _fox_fwd(qkv, cols, cum, cum_t, *, name, comm=None):
    T = qkv.shape[0]
    HB, _, KB = cum_t.shape
    WB = HB * HEAD_DIM
    QB = KB
    qc, kc, vc = cols

    def body(q_ref, k_ref, v_ref, c_ref, ct_ref, o_ref, lse_ref):
        h = pl.program_id(0)
        i = pl.program_id(1)
        qv = q_ref[...]
        lane = lax.broadcasted_iota(jnp.int32, (QB, LANE), 1)
        cq = jnp.sum(jnp.where(lane == h, c_ref[...], 0.0), axis=-1, keepdims=True)
        qpos = i * QB + lax.broadcasted_iota(jnp.int32, (QB, KB), 0)
        koff = lax.broadcasted_iota(jnp.int32, (QB, KB), 1)

        def step(kb, carry):
            m, l, acc = carry
            k0 = pl.multiple_of(kb * KB, KB)
            s = _fox_logits(qv, k_ref[pl.ds(k0, KB), :], cq, ct_ref[pl.ds(kb, 1), :], k0, qpos, koff)
            m_new = jnp.maximum(m, jnp.max(s, axis=-1, keepdims=True))
            alpha = jnp.exp(m - m_new)
            p = jnp.exp(s - m_new)
            l = alpha * l + jnp.sum(p, axis=-1, keepdims=True)
            acc = alpha * acc + jnp.dot(p.astype(BF16), v_ref[pl.ds(k0, KB), :], preferred_element_type=F32)
            return m_new, l, acc

        m, l, acc = lax.fori_loop(
            0, i + 1, step,
            (jnp.full((QB, 1), NEG_INF, F32), jnp.zeros((QB, 1), F32), jnp.zeros((QB, HEAD_DIM), F32)))
        o_ref[...] = (acc / l).astype(BF16)
        lse_ref[...] = jnp.broadcast_to(m + jnp.log(l), (QB, HEAD_DIM))

    ospec = pl.BlockSpec((QB, HEAD_DIM), lambda h, i: (i, h))
    return _call(
        body, [qkv, qkv, qkv, cum, cum_t], name=name, grid=(HB, T // QB),
        in_specs=[pl.BlockSpec((QB, HEAD_DIM), lambda h, i: (i, qc + h)),
                  pl.BlockSpec((T, HEAD_DIM), lambda h, i: (0, kc + h)),
                  pl.BlockSpec((T, HEAD_DIM), lambda h, i: (0, vc + h)),
                  pl.BlockSpec((QB, LANE), lambda h, i: (i, 0)),
                  pl.BlockSpec((None, T // KB, KB), lambda h, i: (h, 0, 0))],
        out_specs=[ospec, ospec],
        out_shape=[jax.ShapeDtypeStruct((T, WB), BF16), jax.ShapeDtypeStruct((T, WB), F32)],
        sem=("parallel", "arbitrary"), comm=comm)


def _fox_bwd(qkv, cols, cum, cum_t, o, do, lse, *, name, comm=None):
    T = qkv.shape[0]
    HB, _, KB = cum_t.shape
    WB = HB * HEAD_DIM
    QB = KB
    qc, kc, vc = cols

    def body(q_ref, k_ref, v_ref, c_ref, ct_ref, o_ref, do_ref, lse_ref,
             dq_ref, dk_ref, dv_ref, dcq_ref, dck_ref):
        h = pl.program_id(0)
        i = pl.program_id(1)

        @pl.when(i == 0)
        def _():
            dk_ref[...] = jnp.zeros_like(dk_ref)
            dv_ref[...] = jnp.zeros_like(dv_ref)
            dck_ref[...] = jnp.zeros_like(dck_ref)

        qv = q_ref[...]
        dov = do_ref[...]
        lse_v = lse_ref[:, :1]
        delta = jnp.sum(dov.astype(F32) * o_ref[...].astype(F32), axis=-1, keepdims=True)
        lane = lax.broadcasted_iota(jnp.int32, (QB, LANE), 1)
        cq = jnp.sum(jnp.where(lane == h, c_ref[...], 0.0), axis=-1, keepdims=True)
        qpos = i * QB + lax.broadcasted_iota(jnp.int32, (QB, KB), 0)
        koff = lax.broadcasted_iota(jnp.int32, (QB, KB), 1)

        def step(kb, carry):
            dq, dcq = carry
            k0 = pl.multiple_of(kb * KB, KB)
            kk = k_ref[pl.ds(k0, KB), :]
            s = _fox_logits(qv, kk, cq, ct_ref[pl.ds(kb, 1), :], k0, qpos, koff)
            p = jnp.exp(s - lse_v)
            dp = lax.dot_general(dov, v_ref[pl.ds(k0, KB), :], NT_DIMS, preferred_element_type=F32)
            ds = p * (dp - delta)
            dsb = ds.astype(BF16)
            dk_ref[pl.ds(k0, KB), :] += lax.dot_general(dsb, qv, TN_DIMS, preferred_element_type=F32) * ATTN_SCALE
            dv_ref[pl.ds(k0, KB), :] += lax.dot_general(p.astype(BF16), dov, TN_DIMS, preferred_element_type=F32)
            dck_ref[pl.ds(kb, 1), :] -= jnp.sum(ds, axis=0, keepdims=True)
            return (dq + jnp.dot(dsb, kk, preferred_element_type=F32),
                    dcq + jnp.sum(ds, axis=-1, keepdims=True))

        dq, dcq = lax.fori_loop(0, i + 1, step, (jnp.zeros((QB, HEAD_DIM), F32), jnp.zeros((QB, 1), F32)))
        dq_ref[...] = (dq * ATTN_SCALE).astype(BF16)
        dcq_ref[...] = jnp.broadcast_to(dcq, (QB, HEAD_DIM))

    qspec = pl.BlockSpec((QB, HEAD_DIM), lambda h, i: (i, h))
    kspec = pl.BlockSpec((T, HEAD_DIM), lambda h, i: (0, h))
    cspec = pl.BlockSpec((QB, LANE), lambda h, i: (i, 0))
    tspec = pl.BlockSpec((None, T // KB, KB), lambda h, i: (h, 0, 0))
    wide = jax.ShapeDtypeStruct((T, WB), F32)
    return _call(
        body, [qkv, qkv, qkv, cum, cum_t, o, do, lse], name=name, grid=(HB, T // QB),
        in_specs=[pl.BlockSpec((QB, HEAD_DIM), lambda h, i: (i, qc + h)),
                  pl.BlockSpec((T, HEAD_DIM), lambda h, i: (0, kc + h)),
                  pl.BlockSpec((T, HEAD_DIM), lambda h, i: (0, vc + h)),
                  cspec, tspec, qspec, qspec, qspec],
        out_specs=[qspec, kspec, kspec, qspec, tspec],
        out_shape=[jax.ShapeDtypeStruct((T, WB), BF16), wide, wide, wide,
                   jax.ShapeDtypeStruct((HB, T // KB, KB), F32)],
        sem=("parallel", "arbitrary"), comm=comm)


def _place():
    return lax.axis_index("x"), lax.axis_index("y"), lax.axis_index("c")


def _other_chips(x, y):
    return [(1 - x, y), (x, 1 - y), (1 - x, 1 - y)]


def _gather_comm(shards):
    n = len(shards)

    def copies(srcs, outs, scratch, want):
        send_sems, recv_sems, local_sems = scratch
        x, y, c = _place()
        me, sib = (x, y, c), (x, y, 1 - c)
        chips = _other_chips(x, y)

        def copy(t, k, block, to, src=None):
            rows = outs[t].at[4 * block[0] + 2 * block[1] + block[2]]
            return pltpu.make_async_remote_copy(
                src_ref=rows if src is None else src, dst_ref=rows,
                send_sem=send_sems.at[7 * t + k], recv_sem=recv_sems.at[7 * t + k],
                device_id=to, device_id_type=MESH)

        got = {}
        if "mine" in want:
            got["mine"] = [pltpu.make_async_copy(srcs[t], outs[t].at[4 * x + 2 * y + c], local_sems.at[t])
                           for t in range(n)]
        if "first" in want:
            got["first"] = [cp for t in range(n) for cp in
                            [copy(t, 0, me, sib, src=srcs[t])]
                            + [copy(t, 1 + j, me, (*chip, c), src=srcs[t]) for j, chip in enumerate(chips)]]
        if "arrive" in want:
            got["arrive"] = [copy(t, 1 + j, (*chip, c), me) for t in range(n) for j, chip in enumerate(chips)]
        if "passed" in want:
            got["passed"] = [copy(t, 4 + j, (*chip, c), sib) for t in range(n) for j, chip in enumerate(chips)]
        if "late" in want:
            got["late"] = [cp for t in range(n) for cp in
                           [copy(t, 0, sib, me)]
                           + [copy(t, 4 + j, (*chip, 1 - c), me) for j, chip in enumerate(chips)]]
        return got

    def begin(srcs, outs, scratch):
        cps = copies(srcs, outs, scratch, ("mine", "first"))
        for cp in cps["mine"] + cps["first"]:
            cp.start()

    def relay(srcs, outs, scratch):
        cps = copies(srcs, outs, scratch, ("arrive", "passed"))
        for landed, fwd in zip(cps["arrive"], cps["passed"]):
            landed.wait_recv()
            fwd.start()

    def end(srcs, outs, scratch):
        cps = copies(srcs, outs, scratch, ("mine", "first", "passed", "late"))
        for cp in cps["late"]:
            cp.wait_recv()
        for cp in cps["first"] + cps["passed"]:
            cp.wait_send()
        for cp in cps["mine"]:
            cp.wait()

    return _Comm(
        shards, [jax.ShapeDtypeStruct((N_DEV,) + s.shape, s.dtype) for s in shards],
        [pltpu.SemaphoreType.DMA((7 * n,)), pltpu.SemaphoreType.DMA((7 * n,)), pltpu.SemaphoreType.DMA((n,))],
        begin, end, relay)


def _sibling_comm(grads):
    n = len(grads)

    def copies(srcs, outs, scratch):
        send_sems, recv_sems = scratch
        x, y, c = _place()
        return [pltpu.make_async_remote_copy(
            src_ref=srcs[t].at[2 * p + (1 - c)], dst_ref=outs[t].at[p],
            send_sem=send_sems.at[4 * t + p], recv_sem=recv_sems.at[4 * t + p],
            device_id=(x, y, 1 - c), device_id_type=MESH) for t in range(n) for p in range(4)]

    def begin(srcs, outs, scratch):
        for cp in copies(srcs, outs, scratch):
            cp.start()

    def end(srcs, outs, scratch):
        cps = copies(srcs, outs, scratch)
        for cp in cps:
            cp.wait_recv()
        for cp in cps:
            cp.wait_send()

    return _Comm(
        grads, [jax.ShapeDtypeStruct((4,) + g.shape[1:], g.dtype) for g in grads],
        [pltpu.SemaphoreType.DMA((4 * n,)), pltpu.SemaphoreType.DMA((4 * n,))], begin, end)


def _chips_comm(sums, relations):
    n, nr = len(sums), len(relations)

    def copies(srcs, outs, scratch):
        send_sems, recv_sems = scratch
        x, y, c = _place()
        chips = _other_chips(x, y)
        return [pltpu.make_async_remote_copy(
            src_ref=srcs[t].at[2 * chips[j][0] + chips[j][1]], dst_ref=outs[t].at[s],
            send_sem=send_sems.at[nr * t + s], recv_sem=recv_sems.at[nr * t + s],
            device_id=(*chips[j], c), device_id_type=MESH) for t in range(n) for s, j in enumerate(relations)]

    def begin(srcs, outs, scratch):
        for cp in copies(srcs, outs, scratch):
            cp.start()

    def end(srcs, outs, scratch):
        cps = copies(srcs, outs, scratch)
        for cp in cps:
            cp.wait_recv()
        for cp in cps:
            cp.wait_send()

    return _Comm(
        sums, [jax.ShapeDtypeStruct((nr,) + s.shape[1:], s.dtype) for s in sums],
        [pltpu.SemaphoreType.DMA((nr * n,)), pltpu.SemaphoreType.DMA((nr * n,))], begin, end)


def _run_comm(comm, *, name):
    ci, co = len(comm.ins), len(comm.outs)

    def body(*refs):
        cin, cout, csc = refs[:ci], refs[ci:ci + co], refs[ci + co:]
        comm.begin(cin, cout, csc)
        if comm.relay is not None:
            comm.relay(cin, cout, csc)
        comm.end(cin, cout, csc)

    return pl.pallas_call(
        body, name=name, in_specs=_any_specs(ci), out_specs=_any_specs(co), out_shape=comm.outs,
        scratch_shapes=comm.scratch)(*comm.ins)


def _chip_sum(g, r1, core, *, name, tr=256):
    _, R, C = g.shape
    tr = _tile(R, tr, 16)

    def body(c_ref, g_ref, r_ref, o_ref):
        o_ref[...] = (g_ref[...].astype(F32) + r_ref[...].astype(F32)).astype(BF16)

    return pl.pallas_call(
        body, name=name,
        grid_spec=pltpu.PrefetchScalarGridSpec(
            num_scalar_prefetch=1, grid=(4, R // tr),
            in_specs=[pl.BlockSpec((None, tr, C), lambda p, i, c: (2 * p + c[0], i, 0)),
                      pl.BlockSpec((None, tr, C), lambda p, i, c: (p, i, 0))],
            out_specs=pl.BlockSpec((None, tr, C), lambda p, i, c: (p, i, 0))),
        out_shape=jax.ShapeDtypeStruct((4, R, C), BF16),
        compiler_params=_cp("parallel", "parallel"))(core, g, r1)


def _adamw_math(w, g, m, v):
    m = ADAM_B1 * m + (1.0 - ADAM_B1) * g
    v = ADAM_B2 * v + (1.0 - ADAM_B2) * (g * g)
    m_hat = m / (1.0 - ADAM_B1 ** ADAM_STEP)
    v_hat = v / (1.0 - ADAM_B2 ** ADAM_STEP)
    delta = -ADAM_LR * (m_hat / (jnp.sqrt(v_hat) + ADAM_EPS) + ADAM_WD * w)
    return delta, m, v


def _adamw_shard(s1, others, chip, w, m, v, *, name, tr=64):
    R0, C0 = w.shape
    C = s1.shape[2]
    tr = _tile(R0, tr, 8)
    (a0, k0), (a1, k1), (a2, k2) = others

    def body(p_ref, s_ref, r0_ref, r1_ref, r2_ref, w_ref, m_ref, v_ref, g_out, d_out, m_out, v_out):
        g = s_ref[:, :C0].astype(F32)
        for r_ref in (r0_ref, r1_ref, r2_ref):
            g = g + r_ref[:, :C0].astype(F32)
        delta, mn, vn = _adamw_math(w_ref[...], g, m_ref[...], v_ref[...])
        g_out[...] = g
        d_out[...] = delta
        m_out[...] = mn
        v_out[...] = vn

    full = pl.BlockSpec((tr, C0), lambda i, p: (i, 0))
    out = jax.ShapeDtypeStruct((R0, C0), F32)
    return pl.pallas_call(
        body, name=name,
        grid_spec=pltpu.PrefetchScalarGridSpec(
            num_scalar_prefetch=1, grid=(R0 // tr,),
            in_specs=[pl.BlockSpec((None, tr, C), lambda i, p: (p[0], i, 0)),
                      pl.BlockSpec((None, tr, C), lambda i, p: (k0, i, 0)),
                      pl.BlockSpec((None, tr, C), lambda i, p: (k1, i, 0)),
                      pl.BlockSpec((None, tr, C), lambda i, p: (k2, i, 0)),
                      full, full, full],
            out_specs=[full, full, full, full]),
        out_shape=[out, out, out, out],
        compiler_params=_cp("parallel"))(chip, s1, a0, a1, a2, w, m, v)


def _all_reduce_small(part, *, name):
    SR = part.shape[0]

    def body(p_ref, o_ref, buf, send_sems, recv_sems):
        x, y, c = _place()
        me = 4 * x + 2 * y + c
        buf[0] = p_ref[...]
        copies = []
        for k in range(1, N_DEV):
            peer = (1 - x if k & 4 else x, 1 - y if k & 2 else y, 1 - c if k & 1 else c)
            copies.append(pltpu.make_async_remote_copy(
                src_ref=p_ref, dst_ref=buf.at[k], send_sem=send_sems.at[k - 1], recv_sem=recv_sems.at[k - 1],
                device_id=peer, device_id_type=MESH))
        for cp in copies:
            cp.start()
        for cp in copies:
            cp.wait_recv()
        for cp in copies:
            cp.wait_send()
        acc = buf[me]
        for d in range(1, N_DEV):
            acc = acc + buf[jnp.bitwise_xor(me, d)]
        o_ref[...] = acc

    return pl.pallas_call(
        body, name=name,
        in_specs=[pl.BlockSpec(memory_space=pltpu.VMEM)], out_specs=pl.BlockSpec(memory_space=pltpu.VMEM),
        out_shape=jax.ShapeDtypeStruct(part.shape, F32),
        scratch_shapes=[pltpu.VMEM((N_DEV, SR, LANE), F32), pltpu.SemaphoreType.DMA((N_DEV - 1,)),
                        pltpu.SemaphoreType.DMA((N_DEV - 1,))])(part)


def _adamw_small(w, g, m, v, *, name):
    def body(w_ref, g_ref, m_ref, v_ref, d_out, m_out, v_out):
        delta, mn, vn = _adamw_math(w_ref[...], g_ref[...], m_ref[...], v_ref[...])
        d_out[...] = delta
        m_out[...] = mn
        v_out[...] = vn

    out = jax.ShapeDtypeStruct(w.shape, F32)
    return pl.pallas_call(body, name=name, out_shape=[out, out, out], compiler_params=_cp())(w, g, m, v)


def _pack(parts, rows):
    flat = jnp.concatenate([p.reshape(-1).astype(F32) for p in parts])
    return jnp.pad(flat, (0, rows * LANE - flat.shape[0])).reshape(rows, LANE)


def _unpack(packed, shapes):
    flat = packed.reshape(-1)
    out, at = [], 0
    for s in shapes:
        n = 1
        for d in s:
            n *= d
        out.append(flat[at:at + n].reshape(s))
        at += n
    return out


def kernel(x, g_mix, w_in, b_f, b_gate, rel_bias, w_branch_a, w_branch_b, w_out, g_ffn, w_gate_ffn, w_up_ffn, w_down_ffn, g_final, loss_target, m_g_mix, m_w_in, m_b_f, m_b_gate, m_rel_bias, m_w_branch_a, m_w_branch_b, m_w_out, m_g_ffn, m_w_gate_ffn, m_w_up_ffn, m_w_down_ffn, m_g_final, v_g_mix, v_w_in, v_b_f, v_b_gate, v_rel_bias, v_w_branch_a, v_w_branch_b, v_w_out, v_g_ffn, v_w_gate_ffn, v_w_up_ffn, v_w_down_ffn, v_g_final):
    _, T, D = x.shape
    HA, n_rel = rel_bias.shape[1], rel_bias.shape[2]
    HB = b_f.shape[1]
    WA, WB = HA * HEAD_DIM, HB * HEAD_DIM
    din8, f8, d8 = w_in.shape[2], w_gate_ffn.shape[2], D // N_DEV
    DIN = N_DEV * din8
    PIN, PF = _pad_to(din8, LANE), _pad_to(f8, LANE)
    padk = LEFT_CHUNKS * CHUNK
    QA = min(256, T)
    KW = QA + padk
    KB = min(512, T)
    x2d, tgt = x[0], loss_target[0]

    s_in = jnp.pad(w_in[0].astype(BF16), ((0, 0), (0, PIN - din8)))
    s_a, s_b, s_out = w_branch_a[0].astype(BF16), w_branch_b[0].astype(BF16), w_out[0].astype(BF16)
    s_g = jnp.pad(w_gate_ffn[0].astype(BF16), ((0, 0), (0, PF - f8)))
    s_u = jnp.pad(w_up_ffn[0].astype(BF16), ((0, 0), (0, PF - f8)))
    s_d = jnp.pad(w_down_ffn[0].astype(BF16), ((0, PF - f8), (0, 0)))
    Win, = _run_comm(_gather_comm([s_in]), name="gather_w_in")
    cx, cy, cc = _place()
    core = jnp.reshape(cc, (1,)).astype(jnp.int32)
    chip = jnp.reshape(2 * cx + cy, (1,)).astype(jnp.int32)
    every = [0, 1, 2]
    fcol = 3 * WA + 3 * WB
    Wfull = Win[:, :, :din8].transpose(1, 0, 2).reshape(D, DIN)
    Wqkv = Wfull[:, :fcol].reshape(1, D, fcol)
    Wf = jnp.pad(Wfull[:, fcol:fcol + HB], ((0, 0), (0, LANE - HB))).reshape(1, D, LANE)
    Wgt = Wfull[:, fcol + HB:].reshape(1, D, 2 * D)
    bf_pad = jnp.pad(b_f, ((0, 0), (0, LANE - HB)))
    fox_cols = (3 * HA, 3 * HA + HB, 3 * HA + 2 * HB)

    h = _rms_fwd(x2d, g_mix, name="rms_mix")
    qkv, (Wa, Wb, Wout) = _mm_nn(h, Wqkv, name="proj_qkv", out_dtype=BF16, comm=_gather_comm([s_a, s_b, s_out]))
    Wout = Wout.reshape(1, D, D)
    gates = _mm_nn(h, Wgt, name="proj_gates", out_dtype=BF16)
    flog = _mm_nn(h, Wf, name="proj_forget", out_dtype=F32)

    kap = jnp.pad(qkv[:, WA:2 * WA], ((padk, 0), (0, 0)))
    vap = jnp.pad(qkv[:, 2 * WA:3 * WA], ((padk, 0), (0, 0)))
    btab = _bias_table(rel_bias[0], QA, KW)
    o_a, lse_a = _attn_a_fwd(qkv, kap, vap, btab, name="attn_a_fwd")

    cum = _fgate_fwd(flog, bf_pad, name="forget_cumsum")
    cum_t = cum[:, :HB].T.reshape(HB, T // KB, KB)
    (o_b, lse_b), (Wg,) = _fox_fwd(qkv, fox_cols, cum, cum_t, name="fox_fwd", comm=_gather_comm([s_g]))

    u_a = _mm_nn(o_a, Wa, name="branch_a", out_dtype=BF16)
    u_b = _mm_nn(o_b, Wb, name="branch_b", out_dtype=BF16)
    merged = _merge_fwd(gates, b_gate, u_a, u_b, name="merge_fwd")
    x1 = _mm_nn(merged, Wout, name="out_proj", out_dtype=F32, res=x2d)
    h2 = _rms_fwd(x1, g_ffn, name="rms_ffn")
    gt, (Wu,) = _mm_nn(h2, Wg, name="ffn_gate", out_dtype=BF16, tn=PF, comm=_gather_comm([s_u]))
    up, (Wd,) = _mm_nn(h2, Wu, name="ffn_up", out_dtype=BF16, tn=PF, comm=_gather_comm([s_d]))
    Wd = Wd.reshape(1, N_DEV * PF, D)
    hid = _swiglu_fwd(gt, up, name="swiglu_fwd")
    x2 = _mm_nn(hid, Wd, name="ffn_down", out_dtype=F32, res=x1, tk=2 * PF)
    dx2, dx2b, dg_final, loss_part = _final_loss(x2, tgt, g_final.reshape(1, D), name="final_loss")

    dhid = _mm_nt(dx2b, Wd, name="d_hidden", out_dtype=BF16)
    gWd = _mm_tn(hid, dx2b, 1, name="gw_down").reshape(N_DEV, PF, D)
    (dgt, dup), (sWd,) = _swiglu_bwd(dhid, gt, up, name="swiglu_bwd", comm=_sibling_comm([gWd]))
    cWd = _chip_sum(gWd, sWd, core, name="chip_sum_w_down_ffn")
    dh2, (rWd,) = _mm_nt(dgt, Wg, name="d_h2_gate", out_dtype=F32, comm=_chips_comm([cWd], every))
    dh2 = _mm_nt(dup, Wu, name="d_h2_up", out_dtype=F32, res=dh2)
    gWg = _mm_tn(h2, dgt, N_DEV, name="gw_gate", tn=PF)
    gWu = _mm_tn(h2, dup, N_DEV, name="gw_up", tn=PF)
    dx1, dx1b, dg_ffn = _rms_bwd(dh2, x1, g_ffn, dx2, name="rms_ffn_bwd")
    dmer, (sWg, sWu) = _mm_nt(dx1b, Wout, name="d_merged", out_dtype=BF16, comm=_sibling_comm([gWg, gWu]))
    cWg = _chip_sum(gWg, sWg, core, name="chip_sum_w_gate_ffn")
    cWu = _chip_sum(gWu, sWu, core, name="chip_sum_w_up_ffn")
    gWout = _mm_tn(merged, dx1b, 1, name="gw_out").reshape(N_DEV, d8, D)
    dua, dub, dgates, dbg = _merge_bwd(dmer, gates, b_gate, u_a, u_b, name="merge_bwd")
    doa = _mm_nt(dua, Wa, name="d_o_a", out_dtype=BF16)
    dob = _mm_nt(dub, Wb, name="d_o_b", out_dtype=BF16)
    gWa = _mm_tn(o_a, dua, N_DEV, name="gw_branch_a")
    gWb = _mm_tn(o_b, dub, N_DEV, name="gw_branch_b")

    (dqa, dkap, dvap, dtab), (sWout, sWa, sWb) = _attn_a_bwd(
        qkv, kap, vap, btab, o_a, doa, lse_a, name="attn_a_bwd", comm=_sibling_comm([gWout, gWa, gWb]))
    cWout = _chip_sum(gWout, sWout, core, name="chip_sum_w_out")
    cWa = _chip_sum(gWa, sWa, core, name="chip_sum_w_branch_a")
    cWb = _chip_sum(gWb, sWb, core, name="chip_sum_w_branch_b")
    skew = jnp.pad(dtab.reshape(HA, QA * KW), ((0, 0), (0, QA))).reshape(HA, QA, KW + 1)
    colsum, tail = _relbias_fold(skew, dtab, name="relbias_fold")
    rev = colsum[:, 0, ::-1]
    d_rel = jnp.concatenate([rev[:, :2 * MAX_REL], tail[:, 0, :1]], axis=1)

    (dqb, dkb, dvb, dcq, dck), (rWg, rWout, rWa, rWb) = _fox_bwd(
        qkv, fox_cols, cum, cum_t, o_b, dob, lse_b, name="fox_bwd", comm=_chips_comm([cWg, cWout, cWa, cWb], every))
    dcq_p = jnp.pad(dcq.reshape(T, HB, HEAD_DIM)[:, :, 0], ((0, 0), (0, LANE - HB)))
    dck_p = jnp.pad(dck.reshape(HB, T).T, ((0, 0), (0, LANE - HB)))
    dflog, dbf = _fgate_bwd(dcq_p, dck_p, flog, bf_pad, name="forget_bwd")

    dqkv = jnp.concatenate(
        [dqa, dkap[padk:].astype(BF16), dvap[padk:].astype(BF16), dqb, dkb.astype(BF16), dvb.astype(BF16)], axis=1)
    dfb = dflog.astype(BF16)
    g_qkv, (rWu,) = _mm_tn(h, dqkv, 1, name="gw_qkv", comm=_chips_comm([cWu], every))
    g_in = jnp.concatenate([g_qkv[0], _mm_tn(h, dfb, 1, name="gw_forget")[0][:, :HB],
                            _mm_tn(h, dgates, 1, name="gw_gates")[0]], axis=1)
    gWin = jnp.pad(g_in.reshape(D, N_DEV, din8).transpose(1, 0, 2), ((0, 0), (0, 0), (0, PIN - din8)))
    dh, (sWin,) = _mm_nt(dfb, Wf, name="d_h_forget", out_dtype=F32, comm=_sibling_comm([gWin]))
    cWin = _chip_sum(gWin, sWin, core, name="chip_sum_w_in")
    dh, (rWin01,) = _mm_nt(dqkv, Wqkv, name="d_h_qkv", out_dtype=F32, res=dh, comm=_chips_comm([cWin], [0, 1]))
    dh, (rWin2,) = _mm_nt(dgates, Wgt, name="d_h_gates", out_dtype=F32, res=dh, comm=_chips_comm([cWin], [2]))
    dx, _, dg_mix = _rms_bwd(dh, x2d, g_mix, dx1, name="rms_mix_bwd")

    def slots(r):
        return [(r, 0), (r, 1), (r, 2)]

    big = {}
    for nm, s1, others, w, m, v in [
            ("w_down_ffn", cWd, slots(rWd), w_down_ffn, m_w_down_ffn, v_w_down_ffn),
            ("w_gate_ffn", cWg, slots(rWg), w_gate_ffn, m_w_gate_ffn, v_w_gate_ffn),
            ("w_out", cWout, slots(rWout), w_out, m_w_out, v_w_out),
            ("w_branch_a", cWa, slots(rWa), w_branch_a, m_w_branch_a, v_w_branch_a),
            ("w_branch_b", cWb, slots(rWb), w_branch_b, m_w_branch_b, v_w_branch_b),
            ("w_up_ffn", cWu, slots(rWu), w_up_ffn, m_w_up_ffn, v_w_up_ffn),
            ("w_in", cWin, [(rWin01, 0), (rWin01, 1), (rWin2, 0)], w_in, m_w_in, v_w_in)]:
        outs = _adamw_shard(s1, others, chip, w[0], m[0], v[0], name="adamw_" + nm)
        big[nm] = [o[None] for o in outs]

    small_names = ["g_mix", "b_f", "b_gate", "rel_bias", "g_ffn", "g_final"]
    small_w = [g_mix, b_f, b_gate, rel_bias, g_ffn, g_final]
    small_m = [m_g_mix, m_b_f, m_b_gate, m_rel_bias, m_g_ffn, m_g_final]
    small_v = [v_g_mix, v_b_f, v_b_gate, v_rel_bias, v_g_ffn, v_g_final]
    small_g = [dg_mix, dbf[:, :HB], dbg, d_rel[None], dg_ffn, dg_final[0]]
    shapes = [w.shape for w in small_w]
    n_small = sum(w.size for w in small_w) + 1
    rows = _pad_to(-(-n_small // LANE), 8)
    total = _all_reduce_small(_pack(small_g + [loss_part[:1, :1]], rows), name="all_reduce_small")
    d_s, m_s, v_s = _adamw_small(_pack(small_w, rows), total, _pack(small_m, rows), _pack(small_v, rows),
                                 name="adamw_small")
    g_small = dict(zip(small_names, _unpack(total, shapes)))
    d_small = dict(zip(small_names, _unpack(d_s, shapes)))
    m_small = dict(zip(small_names, _unpack(m_s, shapes)))
    v_small = dict(zip(small_names, _unpack(v_s, shapes)))
    loss = total.reshape(-1)[n_small - 1]

    order = ["g_mix", "w_in", "b_f", "b_gate", "rel_bias", "w_branch_a", "w_branch_b", "w_out", "g_ffn",
             "w_gate_ffn", "w_up_ffn", "w_down_ffn", "g_final"]

    def pick(idx, small):
        return [big[nm][idx] if nm in big else small[nm] for nm in order]

    return (loss, dx[None], *pick(0, g_small), *pick(1, d_small), *pick(2, m_small), *pick(3, v_small))
```

```python
import functools

import jax
import jax.numpy as jnp
from jax import lax
from jax.experimental import pallas as pl
from jax.experimental.pallas import tpu as pltpu

F32 = jnp.float32
BF16 = jnp.bfloat16
MESH = pl.DeviceIdType.MESH

N_DEV = 8
LANE = 128
HEAD_DIM = 128
CHUNK = 64
LEFT_CHUNKS = 8
MAX_REL = 256
RMS_EPS = 1e-6
NEG_INF = -1e30
ATTN_SCALE = HEAD_DIM ** -0.5
VMEM_LIMIT_BYTES = 56 * 1024 * 1024

ADAM_LR = 0.001
ADAM_B1 = 0.9
ADAM_B2 = 0.999
ADAM_EPS = 1e-08
ADAM_WD = 0.01
ADAM_STEP = 10

NT_DIMS = (((1,), (1,)), ((), ()))
TN_DIMS = (((0,), (0,)), ((), ()))


def _cp(*sem):
    return pltpu.CompilerParams(dimension_semantics=sem if sem else None, vmem_limit_bytes=VMEM_LIMIT_BYTES)


def _pad_to(n, mult):
    return -(-n // mult) * mult


def _tile(dim, pref, align):
    t = (min(pref, dim) // align) * align
    while t >= align:
        if dim % t == 0:
            return t
        t -= align
    return dim


def _sigmoid(z):
    return 1.0 / (1.0 + jnp.exp(-z))


class _Comm:
    def __init__(self, ins, outs, scratch, begin, end, relay=None):
        self.ins, self.outs, self.scratch = list(ins), list(outs), list(scratch)
        self.begin, self.end, self.relay = begin, end, relay


def _call(body, args, *, name, grid, in_specs, out_specs, out_shape, scratch=(), sem=(), comm=None, aliases=None):
    single = not isinstance(out_shape, (list, tuple))
    out_shape = [out_shape] if single else list(out_shape)
    out_specs = [out_specs] if single else list(out_specs)
    aliases = dict(aliases or {})
    if comm is None:
        res = pl.pallas_call(
            body, name=name, grid=grid, in_specs=list(in_specs), out_specs=out_specs, out_shape=out_shape,
            scratch_shapes=list(scratch), input_output_aliases=aliases, compiler_params=_cp(*sem))(*args)
        return res[0] if single else res
    ni, no, ns = len(args), len(out_shape), len(scratch)
    ci, co = len(comm.ins), len(comm.outs)
    total = 1
    for g in grid:
        total *= g
    relay_at = total - 1 - total // 8

    def hosted(*refs):
        a, cin = refs[:ni], refs[ni:ni + ci]
        o, cout = refs[ni + ci:ni + ci + no], refs[ni + ci + no:ni + ci + no + co]
        s, csc = refs[ni + ci + no + co:ni + ci + no + co + ns], refs[ni + ci + no + co + ns:]
        step = pl.program_id(0)
        for d in range(1, len(grid)):
            step = step * grid[d] + pl.program_id(d)

        @pl.when(step == 0)
        def _():
            comm.begin(cin, cout, csc)

        if comm.relay is not None:
            @pl.when(step == relay_at)
            def _():
                comm.relay(cin, cout, csc)

        body(*a, *o, *s)

        @pl.when(step == total - 1)
        def _():
            comm.end(cin, cout, csc)

    res = pl.pallas_call(
        hosted, name=name, grid=grid, in_specs=list(in_specs) + _any_specs(ci),
        out_specs=out_specs + _any_specs(co), out_shape=out_shape + comm.outs,
        scratch_shapes=list(scratch) + comm.scratch, input_output_aliases=aliases,
        compiler_params=_cp(*(("arbitrary",) * len(grid))))(*args, *comm.ins)
    return (res[0] if single else res[:no]), res[no:]


def _any_specs(n):
    return [pl.BlockSpec(memory_space=pl.ANY)] * n


def _mm_body(dims, nk, has_res, n_ext, n_out, epi):
    def body(*refs):
        a_ref, w_ref = refs[:2]
        at = 2 + has_res
        r_ref = refs[2] if has_res else None
        e_refs, o_refs = refs[at:at + n_ext], refs[at + n_ext:at + n_ext + n_out]

        def emit(v):
            if r_ref is not None:
                v = v + r_ref[...]
            vals = (v,) if epi is None else epi(v, *[e[...] for e in e_refs])
            for o_ref, val in zip(o_refs, vals):
                o_ref[...] = val.astype(o_ref.dtype)

        if nk == 1:
            emit(lax.dot_general(a_ref[...], w_ref[...], dims, preferred_element_type=F32))
            return
        acc = refs[-1]
        k = pl.program_id(2)

        @pl.when(k == 0)
        def _():
            acc[...] = jnp.zeros_like(acc)

        acc[...] += lax.dot_general(a_ref[...], w_ref[...], dims, preferred_element_type=F32)

        @pl.when(k == nk - 1)
        def _():
            emit(acc[...])

    return body


def _mm_nn(x, w, *, name, out_dtype, res=None, tm=1024, tn=1024, tk=None, comm=None, epi=None, extras=(), n_out=1):
    M, K = x.shape
    NB, _, NW = w.shape
    tm, tn = _tile(M, tm, 16), _tile(NW, tn, LANE)
    tk = K if tk is None else _tile(K, tk, LANE)
    nj, nk = NW // tn, K // tk
    tile = pl.BlockSpec((tm, tn), lambda j, i, k: (i, j))
    in_specs = [pl.BlockSpec((tm, tk), lambda j, i, k: (i, k)),
                pl.BlockSpec((None, tk, tn), lambda j, i, k: (j // nj, k, j % nj))]
    args = [x, w] + ([res] if res is not None else []) + list(extras)
    in_specs += [tile] * (len(args) - 2)
    out = jax.ShapeDtypeStruct((M, NB * NW), out_dtype)
    return _call(
        _mm_body((((1,), (0,)), ((), ())), nk, res is not None, len(extras), n_out, epi), args,
        name=name, grid=(NB * nj, M // tm, nk), in_specs=in_specs,
        out_specs=tile if n_out == 1 else [tile] * n_out, out_shape=out if n_out == 1 else [out] * n_out,
        scratch=[pltpu.VMEM((tm, tn), F32)] if nk > 1 else [],
        sem=("parallel", "parallel", "arbitrary"), comm=comm)


def _mm_nt(dy, w, *, name, out_dtype, res=None, tm=1024, tn=1024, tc=2816, comm=None, epi=None, extras=(), n_out=1):
    M, N = dy.shape
    NB, Kw, NW = w.shape
    assert N == NB * NW
    tm, tn, tc = _tile(M, tm, 16), _tile(Kw, tn, LANE), _tile(NW, tc, LANE)
    nc = NW // tc
    nk = NB * nc
    tile = pl.BlockSpec((tm, tn), lambda i, j, k: (i, j))
    in_specs = [pl.BlockSpec((tm, tc), lambda i, j, k: (i, k)),
                pl.BlockSpec((None, tn, tc), lambda i, j, k: (k // nc, j, k % nc))]
    args = [dy, w] + ([res] if res is not None else []) + list(extras)
    in_specs += [tile] * (len(args) - 2)
    out = jax.ShapeDtypeStruct((M, Kw), out_dtype)
    return _call(
        _mm_body(NT_DIMS, nk, res is not None, len(extras), n_out, epi), args,
        name=name, grid=(M // tm, Kw // tn, nk), in_specs=in_specs,
        out_specs=tile if n_out == 1 else [tile] * n_out, out_shape=out if n_out == 1 else [out] * n_out,
        scratch=[pltpu.VMEM((tm, tn), F32)] if nk > 1 else [],
        sem=("parallel", "parallel", "arbitrary"), comm=comm)


def _mm_tn(x, dy, nb, *, name, tm=512, tn=1024, comm=None):
    M, Kw = x.shape
    NW = dy.shape[1] // nb
    tm, tn = _tile(Kw, tm, LANE), _tile(NW, tn, LANE)
    nj = NW // tn

    def body(x_ref, d_ref, o_ref):
        o_ref[...] = lax.dot_general(x_ref[...], d_ref[...], TN_DIMS, preferred_element_type=F32).astype(o_ref.dtype)

    return _call(
        body, [x, dy], name=name, grid=(nb * nj, Kw // tm),
        in_specs=[pl.BlockSpec((M, tm), lambda j, i: (0, i)),
                  pl.BlockSpec((M, tn), lambda j, i: (0, j))],
        out_specs=pl.BlockSpec((None, tm, tn), lambda j, i: (j // nj, i, j % nj)),
        out_shape=jax.ShapeDtypeStruct((nb, Kw, NW), BF16),
        sem=("parallel", "parallel"), comm=comm)


def _rms_fwd(x, g, *, name, tr=256):
    T, D = x.shape
    tr = _tile(T, tr, 16)

    def body(x_ref, g_ref, h_ref):
        xv = x_ref[...]
        r = lax.rsqrt(jnp.mean(xv * xv, axis=-1, keepdims=True) + RMS_EPS)
        h_ref[...] = ((xv * r) * g_ref[...]).astype(BF16)

    return pl.pallas_call(
        body, name=name, grid=(T // tr,),
        in_specs=[pl.BlockSpec((tr, D), lambda i: (i, 0)), pl.BlockSpec((1, D), lambda i: (0, 0))],
        out_specs=pl.BlockSpec((tr, D), lambda i: (i, 0)),
        out_shape=jax.ShapeDtypeStruct((T, D), BF16),
        compiler_params=_cp("parallel"))(x, g)


def _rms_bwd(dh, x, g, dres, *, name, tr=128):
    T, D = x.shape
    tr = _tile(T, tr, 16)

    def body(dh_ref, x_ref, g_ref, dr_ref, dx_ref, dxb_ref, dg_ref):
        i = pl.program_id(0)
        xv = x_ref[...]
        dhv = dh_ref[...]
        r = lax.rsqrt(jnp.mean(xv * xv, axis=-1, keepdims=True) + RMS_EPS)
        a = dhv * g_ref[...]
        dot = jnp.mean(a * xv, axis=-1, keepdims=True)
        dx = dr_ref[...] + r * a - xv * (r * r * r * dot)
        dx_ref[...] = dx
        dxb_ref[...] = dx.astype(BF16)
        part = jnp.sum(dhv * (xv * r), axis=0, keepdims=True)

        @pl.when(i == 0)
        def _():
            dg_ref[...] = part

        @pl.when(i > 0)
        def _():
            dg_ref[...] += part

    row = pl.BlockSpec((tr, D), lambda i: (i, 0))
    vec = pl.BlockSpec((1, D), lambda i: (0, 0))
    return pl.pallas_call(
        body, name=name, grid=(T // tr,),
        in_specs=[row, row, vec, row], out_specs=[row, row, vec],
        out_shape=[jax.ShapeDtypeStruct((T, D), F32), jax.ShapeDtypeStruct((T, D), BF16),
                   jax.ShapeDtypeStruct((1, D), F32)],
        compiler_params=_cp("arbitrary"))(dh, x, g, dres)


def _final_loss(x2, tgt, g, *, name, tr=128):
    T, D = x2.shape
    tr = _tile(T, tr, 16)

    def body(x_ref, t_ref, g_ref, dx_ref, dxb_ref, dg_ref, loss_ref):
        i = pl.program_id(0)
        xv = x_ref[...]
        gv = g_ref[...]
        r = lax.rsqrt(jnp.mean(xv * xv, axis=-1, keepdims=True) + RMS_EPS)
        xh = xv * r
        diff = xh * gv - t_ref[...]
        lpart = 0.5 * jnp.sum(jnp.mean(diff * diff, axis=-1, keepdims=True))
        dy = diff * (1.0 / D)
        a = dy * gv
        dot = jnp.mean(a * xv, axis=-1, keepdims=True)
        dx = r * a - xv * (r * r * r * dot)
        dx_ref[...] = dx
        dxb_ref[...] = dx.astype(BF16)
        part = jnp.sum(dy * xh, axis=0, keepdims=True)
        lblk = jnp.full((8, LANE), lpart, F32)

        @pl.when(i == 0)
        def _():
            dg_ref[...] = part
            loss_ref[...] = lblk

        @pl.when(i > 0)
        def _():
            dg_ref[...] += part
            loss_ref[...] += lblk

    row = pl.BlockSpec((tr, D), lambda i: (i, 0))
    vec = pl.BlockSpec((1, D), lambda i: (0, 0))
    return pl.pallas_call(
        body, name=name, grid=(T // tr,),
        in_specs=[row, row, vec],
        out_specs=[row, row, vec, pl.BlockSpec((8, LANE), lambda i: (0, 0))],
        out_shape=[jax.ShapeDtypeStruct((T, D), F32), jax.ShapeDtypeStruct((T, D), BF16),
                   jax.ShapeDtypeStruct((1, D), F32), jax.ShapeDtypeStruct((8, LANE), F32)],
        compiler_params=_cp("arbitrary"))(x2, tgt, g)


def _merge_fwd(gates, b_gate, ua, ub, *, name, tr=256):
    T, D = ua.shape
    tr = _tile(T, tr, 16)

    def body(ga_ref, gb_ref, ba_ref, bb_ref, ua_ref, ub_ref, o_ref):
        sa = _sigmoid(ga_ref[...].astype(F32) + ba_ref[...])
        sb = _sigmoid(gb_ref[...].astype(F32) + bb_ref[...])
        o_ref[...] = (sa * ua_ref[...].astype(F32) + sb * ub_ref[...].astype(F32)).astype(BF16)

    row = pl.BlockSpec((tr, D), lambda i: (i, 0))
    return pl.pallas_call(
        body, name=name, grid=(T // tr,),
        in_specs=[row, pl.BlockSpec((tr, D), lambda i: (i, 1)),
                  pl.BlockSpec((1, D), lambda i: (0, 0)), pl.BlockSpec((1, D), lambda i: (0, 1)), row, row],
        out_specs=row, out_shape=jax.ShapeDtypeStruct((T, D), BF16),
        compiler_params=_cp("parallel"))(gates, gates, b_gate, b_gate, ua, ub)


def _merge_bwd(dm, gates, b_gate, ua, ub, *, name, tr=256):
    T, D = ua.shape
    tr = _tile(T, tr, 16)

    def body(dm_ref, ga_ref, gb_ref, ba_ref, bb_ref, ua_ref, ub_ref, dua_ref, dub_ref, dg_ref, db_ref):
        i = pl.program_id(0)
        dmv = dm_ref[...].astype(F32)
        sa = _sigmoid(ga_ref[...].astype(F32) + ba_ref[...])
        sb = _sigmoid(gb_ref[...].astype(F32) + bb_ref[...])
        dua_ref[...] = (dmv * sa).astype(BF16)
        dub_ref[...] = (dmv * sb).astype(BF16)
        dga = dmv * ua_ref[...].astype(F32) * (sa * (1.0 - sa))
        dgb = dmv * ub_ref[...].astype(F32) * (sb * (1.0 - sb))
        dg_ref[:, :D] = dga.astype(BF16)
        dg_ref[:, D:] = dgb.astype(BF16)
        pa = jnp.sum(dga, axis=0, keepdims=True)
        pb = jnp.sum(dgb, axis=0, keepdims=True)

        @pl.when(i == 0)
        def _():
            db_ref[:, :D] = pa
            db_ref[:, D:] = pb

        @pl.when(i > 0)
        def _():
            db_ref[:, :D] += pa
            db_ref[:, D:] += pb

    row = pl.BlockSpec((tr, D), lambda i: (i, 0))
    vec = pl.BlockSpec((1, D), lambda i: (0, 0))
    act = jax.ShapeDtypeStruct((T, D), BF16)
    return pl.pallas_call(
        body, name=name, grid=(T // tr,),
        in_specs=[row, row, pl.BlockSpec((tr, D), lambda i: (i, 1)),
                  vec, pl.BlockSpec((1, D), lambda i: (0, 1)), row, row],
        out_specs=[row, row, pl.BlockSpec((tr, 2 * D), lambda i: (i, 0)), pl.BlockSpec((1, 2 * D), lambda i: (0, 0))],
        out_shape=[act, act, jax.ShapeDtypeStruct((T, 2 * D), BF16), jax.ShapeDtypeStruct((1, 2 * D), F32)],
        compiler_params=_cp("arbitrary"))(dm, gates, gates, b_gate, b_gate, ua, ub)


def _swiglu_fwd_epi(up, gate):
    g = gate.astype(F32)
    up = up.astype(BF16).astype(F32)
    return up, g * _sigmoid(g) * up


def _swiglu_bwd_epi(d, gate, up):
    g = gate.astype(F32)
    s = _sigmoid(g)
    return d * up.astype(F32) * (s * (1.0 + g * (1.0 - s))), d * (g * s)


def _cast_into(buf, src, row_off, col_off, *, name, tr=256):
    T = buf.shape[0]
    W = src.shape[1]
    tr = _tile(T, tr, 16)
    cw = W
    while col_off % cw or W % cw:
        cw -= LANE
    assert row_off % tr == 0
    c0, r0 = col_off // cw, row_off // tr

    def body(s_ref, b_ref, o_ref):
        o_ref[...] = s_ref[...].astype(o_ref.dtype)

    return pl.pallas_call(
        body, name=name, grid=(T // tr, W // cw),
        in_specs=[pl.BlockSpec((tr, cw), lambda i, j: (i + r0, j)), pl.BlockSpec(memory_space=pl.ANY)],
        out_specs=pl.BlockSpec((tr, cw), lambda i, j: (i, c0 + j)),
        out_shape=jax.ShapeDtypeStruct(buf.shape, buf.dtype), input_output_aliases={1: 0},
        compiler_params=_cp("parallel", "parallel"))(src, buf)


def _bias_table(rel, qb, kw):
    assert qb - 1 <= MAX_REL
    ha = rel.shape[0]
    n_clipped = kw - MAX_REL
    e = jnp.concatenate([jnp.broadcast_to(rel[:, 2 * MAX_REL:], (ha, n_clipped)),
                         rel[:, 2 * MAX_REL - 1:MAX_REL - qb:-1]], axis=1)
    period = qb + kw
    e = jnp.pad(e, ((0, 0), (0, period - e.shape[1])))
    z = jnp.broadcast_to(e[:, None, :], (ha, qb, period)).reshape(ha, qb * period)
    z = z[:, :qb * (period - 1)].reshape(ha, qb, period - 1)
    i = jnp.arange(qb)[:, None]
    j = jnp.arange(kw)[None, :]
    band = (j // CHUNK >= i // CHUNK) & (j // CHUNK <= i // CHUNK + LEFT_CHUNKS)
    return jnp.where(band[None], z[:, :, qb - 1:qb - 1 + kw], NEG_INF)


def _attn_a_fwd(q, kp, vp, btab, *, name):
    T, WA = q.shape[0], kp.shape[1]
    HA = WA // HEAD_DIM
    _, QB, KW = btab.shape
    padk = KW - QB

    def body(q_ref, k_ref, v_ref, b_ref, o_ref, lse_ref):
        i = pl.program_id(1)
        start = pl.multiple_of(i * QB, QB)
        k = k_ref[pl.ds(start, KW), :]
        v = v_ref[pl.ds(start, KW), :]
        s = lax.dot_general(q_ref[...], k, NT_DIMS, preferred_element_type=F32) * ATTN_SCALE + b_ref[...]
        kpos = start - padk + lax.broadcasted_iota(jnp.int32, (QB, KW), 1)
        s = jnp.where(kpos >= 0, s, NEG_INF)
        m = jnp.max(s, axis=-1, keepdims=True)
        p = jnp.exp(s - m)
        l = jnp.sum(p, axis=-1, keepdims=True)
        o = jnp.dot(p.astype(BF16), v, preferred_element_type=F32) / l
        o_ref[...] = o.astype(BF16)
        lse_ref[...] = jnp.broadcast_to(m + jnp.log(l), (QB, HEAD_DIM))

    qspec = pl.BlockSpec((QB, HEAD_DIM), lambda h, i: (i, h))
    kspec = pl.BlockSpec((T + padk, HEAD_DIM), lambda h, i: (0, h))
    return pl.pallas_call(
        body, name=name, grid=(HA, T // QB),
        in_specs=[qspec, kspec, kspec, pl.BlockSpec((None, QB, KW), lambda h, i: (h, 0, 0))],
        out_specs=[qspec, qspec],
        out_shape=[jax.ShapeDtypeStruct((T, WA), BF16), jax.ShapeDtypeStruct((T, WA), F32)],
        compiler_params=_cp("parallel", "arbitrary"))(q, kp, vp, btab)


def _attn_a_bwd(q, kp, vp, btab, o, do, lse, *, name, comm=None, dq_width=None):
    T, WA = q.shape[0], kp.shape[1]
    HA = WA // HEAD_DIM
    _, QB, KW = btab.shape
    padk = KW - QB

    def body(q_ref, k_ref, v_ref, b_ref, o_ref, do_ref, lse_ref, dq_ref, dk_ref, dv_ref, db_ref):
        i = pl.program_id(1)

        @pl.when(i == 0)
        def _():
            dk_ref[...] = jnp.zeros_like(dk_ref)
            dv_ref[...] = jnp.zeros_like(dv_ref)
            db_ref[...] = jnp.zeros_like(db_ref)

        start = pl.multiple_of(i * QB, QB)
        qv = q_ref[...]
        dov = do_ref[...]
        k = k_ref[pl.ds(start, KW), :]
        v = v_ref[pl.ds(start, KW), :]
        s = lax.dot_general(qv, k, NT_DIMS, preferred_element_type=F32) * ATTN_SCALE + b_ref[...]
        kpos = start - padk + lax.broadcasted_iota(jnp.int32, (QB, KW), 1)
        s = jnp.where(kpos >= 0, s, NEG_INF)
        p = jnp.exp(s - lse_ref[:, :1])
        dp = lax.dot_general(dov, v, NT_DIMS, preferred_element_type=F32)
        delta = jnp.sum(dov.astype(F32) * o_ref[...].astype(F32), axis=-1, keepdims=True)
        ds = p * (dp - delta)
        db_ref[...] += ds
        dsb = ds.astype(BF16)
        dq_ref[...] = (jnp.dot(dsb, k, preferred_element_type=F32) * ATTN_SCALE).astype(BF16)
        dk_ref[pl.ds(start, KW), :] += lax.dot_general(dsb, qv, TN_DIMS, preferred_element_type=F32) * ATTN_SCALE
        dv_ref[pl.ds(start, KW), :] += lax.dot_general(p.astype(BF16), dov, TN_DIMS, preferred_element_type=F32)

    qspec = pl.BlockSpec((QB, HEAD_DIM), lambda h, i: (i, h))
    kspec = pl.BlockSpec((T + padk, HEAD_DIM), lambda h, i: (0, h))
    bspec = pl.BlockSpec((None, QB, KW), lambda h, i: (h, 0, 0))
    return _call(
        body, [q, kp, vp, btab, o, do, lse], name=name, grid=(HA, T // QB),
        in_specs=[qspec, kspec, kspec, bspec, qspec, qspec, qspec],
        out_specs=[qspec, kspec, kspec, bspec],
        out_shape=[jax.ShapeDtypeStruct((T, dq_width or WA), BF16), jax.ShapeDtypeStruct((T + padk, WA), F32),
                   jax.ShapeDtypeStruct((T + padk, WA), F32), jax.ShapeDtypeStruct((HA, QB, KW), F32)],
        sem=("parallel", "arbitrary"), comm=comm)


def _relbias_fold(skew, dtab, *, name):
    HA, qb, kw = dtab.shape
    W = kw + 1

    def body(s_ref, t_ref, col_ref, tail_ref):
        row = lax.broadcasted_iota(jnp.int32, (qb, W), 0)
        col = lax.broadcasted_iota(jnp.int32, (qb, W), 1)
        col_ref[...] = jnp.sum(jnp.where(row + col < kw, s_ref[...], 0.0), axis=0, keepdims=True)
        i = lax.broadcasted_iota(jnp.int32, (qb, kw), 0)
        j = lax.broadcasted_iota(jnp.int32, (qb, kw), 1)
        tail = jnp.sum(jnp.where((kw - qb) + i - j >= MAX_REL, t_ref[...], 0.0))
        tail_ref[...] = jnp.full((1, LANE), tail, F32)

    return pl.pallas_call(
        body, name=name, grid=(HA,),
        in_specs=[pl.BlockSpec((None, qb, W), lambda h: (h, 0, 0)), pl.BlockSpec((None, qb, kw), lambda h: (h, 0, 0))],
        out_specs=[pl.BlockSpec((None, 1, W), lambda h: (h, 0, 0)), pl.BlockSpec((None, 1, LANE), lambda h: (h, 0, 0))],
        out_shape=[jax.ShapeDtypeStruct((HA, 1, W), F32), jax.ShapeDtypeStruct((HA, 1, LANE), F32)],
        compiler_params=_cp("parallel"))(skew, dtab)


CUM_BLOCK = 128


def _fgate_fwd(f, bf, *, name):
    T = f.shape[0]
    nb = T // CUM_BLOCK

    def body(f_ref, b_ref, cum_ref):
        r = lax.broadcasted_iota(jnp.int32, (CUM_BLOCK, CUM_BLOCK), 0)
        c = lax.broadcasted_iota(jnp.int32, (CUM_BLOCK, CUM_BLOCK), 1)
        tri = (c <= r).astype(F32)

        def step(b, carry):
            r0 = pl.multiple_of(b * CUM_BLOCK, CUM_BLOCK)
            z = f_ref[pl.ds(r0, CUM_BLOCK), :] + b_ref[...]
            lf = jnp.minimum(z, 0.0) - jnp.log(1.0 + jnp.exp(-jnp.abs(z)))
            cs = jnp.dot(tri, lf, precision=lax.Precision.HIGHEST, preferred_element_type=F32) + carry
            cum_ref[pl.ds(r0, CUM_BLOCK), :] = cs
            return carry + jnp.sum(lf, axis=0, keepdims=True)

        lax.fori_loop(0, nb, step, jnp.zeros((1, LANE), F32))

    return pl.pallas_call(
        body, name=name, out_shape=jax.ShapeDtypeStruct((T, LANE), F32),
        compiler_params=_cp())(f, bf)


def _fgate_bwd(dcq, dck, f, bf, *, name):
    T = f.shape[0]
    nb = T // CUM_BLOCK

    def body(dq_ref, dk_ref, f_ref, b_ref, df_ref, dbf_ref):
        r = lax.broadcasted_iota(jnp.int32, (CUM_BLOCK, CUM_BLOCK), 0)
        c = lax.broadcasted_iota(jnp.int32, (CUM_BLOCK, CUM_BLOCK), 1)
        tri = (c >= r).astype(F32)

        def step(n, carry):
            tail, dbf = carry
            r0 = pl.multiple_of((nb - 1 - n) * CUM_BLOCK, CUM_BLOCK)
            dc = dq_ref[pl.ds(r0, CUM_BLOCK), :] + dk_ref[pl.ds(r0, CUM_BLOCK), :]
            ss = jnp.dot(tri, dc, precision=lax.Precision.HIGHEST, preferred_element_type=F32) + tail
            z = f_ref[pl.ds(r0, CUM_BLOCK), :] + b_ref[...]
            df = ss * _sigmoid(-z)
            df_ref[pl.ds(r0, CUM_BLOCK), :] = df
            return tail + jnp.sum(dc, axis=0, keepdims=True), dbf + jnp.sum(df, axis=0, keepdims=True)

        zero = jnp.zeros((1, LANE), F32)
        _, dbf = lax.fori_loop(0, nb, step, (zero, zero))
        dbf_ref[...] = dbf

    return pl.pallas_call(
        body, name=name,
        out_shape=[jax.ShapeDtypeStruct((T, LANE), F32), jax.ShapeDtypeStruct((1, LANE), F32)],
        compiler_params=_cp())(dcq, dck, f, bf)


def _fox_logits(qv, kk, cq, ck, diagonal):
    s = lax.dot_general(qv, kk, NT_DIMS, preferred_element_type=F32) * ATTN_SCALE + cq - ck
    if diagonal:
        row = lax.broadcasted_iota(jnp.int32, s.shape, 0)
        col = lax.broadcasted_iota(jnp.int32, s.shape, 1)
        s = jnp.where(col <= row, s, NEG_INF)
    return s


def _fox_fwd(qkv, cols, cum, cum_t, *, name, comm=None):
    T = qkv.shape[0]
    HB, _, KB = cum_t.shape
    WB = HB * HEAD_DIM
    QB = KB
    qc, kc, vc = cols

    def body(q_ref, k_ref, v_ref, c_ref, ct_ref, o_ref, lse_ref):
        h = pl.program_id(0)
        i = pl.program_id(1)
        qv = q_ref[...]
        lane = lax.broadcasted_iota(jnp.int32, (QB, LANE), 1)
        cq = jnp.sum(jnp.where(lane == h, c_ref[...], 0.0), axis=-1, keepdims=True)

        def block(kb, carry, diagonal):
            m, l, acc = carry
            k0 = pl.multiple_of(kb * KB, KB)
            s = _fox_logits(qv, k_ref[pl.ds(k0, KB), :], cq, ct_ref[pl.ds(kb, 1), :], diagonal)
            m_new = jnp.maximum(m, jnp.max(s, axis=-1, keepdims=True))
            alpha = jnp.exp(m - m_new)
            p = jnp.exp(s - m_new)
            l = alpha * l + jnp.sum(p, axis=-1, keepdims=True)
            acc = alpha * acc + jnp.dot(p.astype(BF16), v_ref[pl.ds(k0, KB), :], preferred_element_type=F32)
            return m_new, l, acc

        past = lax.fori_loop(
            0, i, lambda kb, carry: block(kb, carry, False),
            (jnp.full((QB, 1), NEG_INF, F32), jnp.zeros((QB, 1), F32), jnp.zeros((QB, HEAD_DIM), F32)))
        m, l, acc = block(i, past, True)
        o_ref[...] = (acc / l).astype(BF16)
        lse_ref[...] = jnp.broadcast_to(m + jnp.log(l), (QB, HEAD_DIM))

    ospec = pl.BlockSpec((QB, HEAD_DIM), lambda h, i: (i, h))
    return _call(
        body, [qkv, qkv, qkv, cum, cum_t], name=name, grid=(HB, T // QB),
        in_specs=[pl.BlockSpec((QB, HEAD_DIM), lambda h, i: (i, qc + h)),
                  pl.BlockSpec((T, HEAD_DIM), lambda h, i: (0, kc + h)),
                  pl.BlockSpec((T, HEAD_DIM), lambda h, i: (0, vc + h)),
                  pl.BlockSpec((QB, LANE), lambda h, i: (i, 0)),
                  pl.BlockSpec((None, T // KB, KB), lambda h, i: (h, 0, 0))],
        out_specs=[ospec, ospec],
        out_shape=[jax.ShapeDtypeStruct((T, WB), BF16), jax.ShapeDtypeStruct((T, WB), F32)],
        sem=("parallel", "arbitrary"), comm=comm)


def _fox_bwd(qkv, cols, cum, cum_t, o, do, lse, *, name, comm=None, dq_into=None):
    T = qkv.shape[0]
    HB, _, KB = cum_t.shape
    WB = HB * HEAD_DIM
    QB = KB
    qc, kc, vc = cols

    def body(*refs):
        q_ref, k_ref, v_ref, c_ref, ct_ref, o_ref, do_ref, lse_ref = refs[:8]
        dq_ref, dk_ref, dv_ref, dcq_ref, dck_ref = refs[-5:]
        h = pl.program_id(0)
        i = pl.program_id(1)

        @pl.when(i == 0)
        def _():
            dk_ref[...] = jnp.zeros_like(dk_ref)
            dv_ref[...] = jnp.zeros_like(dv_ref)
            dck_ref[...] = jnp.zeros_like(dck_ref)

        qv = q_ref[...]
        dov = do_ref[...]
        lse_v = lse_ref[:, :1]
        delta = jnp.sum(dov.astype(F32) * o_ref[...].astype(F32), axis=-1, keepdims=True)
        lane = lax.broadcasted_iota(jnp.int32, (QB, LANE), 1)
        cq = jnp.sum(jnp.where(lane == h, c_ref[...], 0.0), axis=-1, keepdims=True)

        def block(kb, carry, diagonal):
            dq, dcq = carry
            k0 = pl.multiple_of(kb * KB, KB)
            kk = k_ref[pl.ds(k0, KB), :]
            s = _fox_logits(qv, kk, cq, ct_ref[pl.ds(kb, 1), :], diagonal)
            p = jnp.exp(s - lse_v)
            dp = lax.dot_general(dov, v_ref[pl.ds(k0, KB), :], NT_DIMS, preferred_element_type=F32)
            ds = p * (dp - delta)
            dsb = ds.astype(BF16)
            dk_ref[pl.ds(k0, KB), :] += lax.dot_general(dsb, qv, TN_DIMS, preferred_element_type=F32) * ATTN_SCALE
            dv_ref[pl.ds(k0, KB), :] += lax.dot_general(p.astype(BF16), dov, TN_DIMS, preferred_element_type=F32)
            dck_ref[pl.ds(kb, 1), :] -= jnp.sum(ds, axis=0, keepdims=True)
            return (dq + jnp.dot(dsb, kk, preferred_element_type=F32),
                    dcq + jnp.sum(ds, axis=-1, keepdims=True))

        past = lax.fori_loop(0, i, lambda kb, carry: block(kb, carry, False),
                             (jnp.zeros((QB, HEAD_DIM), F32), jnp.zeros((QB, 1), F32)))
        dq, dcq = block(i, past, True)
        dq_ref[...] = (dq * ATTN_SCALE).astype(BF16)
        dcq_ref[...] = jnp.broadcast_to(dcq, (QB, HEAD_DIM))

    qspec = pl.BlockSpec((QB, HEAD_DIM), lambda h, i: (i, h))
    kspec = pl.BlockSpec((T, HEAD_DIM), lambda h, i: (0, h))
    cspec = pl.BlockSpec((QB, LANE), lambda h, i: (i, 0))
    tspec = pl.BlockSpec((None, T // KB, KB), lambda h, i: (h, 0, 0))
    wide = jax.ShapeDtypeStruct((T, WB), F32)
    args = [qkv, qkv, qkv, cum, cum_t, o, do, lse]
    in_specs = [pl.BlockSpec((QB, HEAD_DIM), lambda h, i: (i, qc + h)),
                pl.BlockSpec((T, HEAD_DIM), lambda h, i: (0, kc + h)),
                pl.BlockSpec((T, HEAD_DIM), lambda h, i: (0, vc + h)),
                cspec, tspec, qspec, qspec, qspec]
    dq_spec, dq_shape, aliases = qspec, jax.ShapeDtypeStruct((T, WB), BF16), None
    if dq_into is not None:
        buf, dq_col = dq_into
        args.append(buf)
        in_specs.append(pl.BlockSpec(memory_space=pl.ANY))
        dq_spec = pl.BlockSpec((QB, HEAD_DIM), lambda h, i: (i, dq_col + h))
        dq_shape, aliases = jax.ShapeDtypeStruct(buf.shape, buf.dtype), {8: 0}
    return _call(
        body, args, name=name, grid=(HB, T // QB), in_specs=in_specs,
        out_specs=[dq_spec, kspec, kspec, qspec, tspec],
        out_shape=[dq_shape, wide, wide, wide, jax.ShapeDtypeStruct((HB, T // KB, KB), F32)],
        sem=("parallel", "arbitrary"), comm=comm, aliases=aliases)


def _place():
    return lax.axis_index("x"), lax.axis_index("y"), lax.axis_index("c")


def _other_chips(x, y):
    return [(1 - x, y), (x, 1 - y), (1 - x, 1 - y)]


def _gather_comm(shards):
    n = len(shards)

    def copies(srcs, outs, scratch, want):
        send_sems, recv_sems, local_sems = scratch
        x, y, c = _place()
        me, sib = (x, y, c), (x, y, 1 - c)
        chips = _other_chips(x, y)

        def copy(t, k, block, to, src=None):
            rows = outs[t].at[4 * block[0] + 2 * block[1] + block[2]]
            return pltpu.make_async_remote_copy(
                src_ref=rows if src is None else src, dst_ref=rows,
                send_sem=send_sems.at[7 * t + k], recv_sem=recv_sems.at[7 * t + k],
                device_id=to, device_id_type=MESH)

        got = {}
        if "mine" in want:
            got["mine"] = [pltpu.make_async_copy(srcs[t], outs[t].at[4 * x + 2 * y + c], local_sems.at[t])
                           for t in range(n)]
        if "first" in want:
            got["first"] = [cp for t in range(n) for cp in
                            [copy(t, 0, me, sib, src=srcs[t])]
                            + [copy(t, 1 + j, me, (*chip, c), src=srcs[t]) for j, chip in enumerate(chips)]]
        if "arrive" in want:
            got["arrive"] = [copy(t, 1 + j, (*chip, c), me) for t in range(n) for j, chip in enumerate(chips)]
        if "passed" in want:
            got["passed"] = [copy(t, 4 + j, (*chip, c), sib) for t in range(n) for j, chip in enumerate(chips)]
        if "late" in want:
            got["late"] = [cp for t in range(n) for cp in
                           [copy(t, 0, sib, me)]
                           + [copy(t, 4 + j, (*chip, 1 - c), me) for j, chip in enumerate(chips)]]
        return got

    def begin(srcs, outs, scratch):
        cps = copies(srcs, outs, scratch, ("mine", "first"))
        for cp in cps["mine"] + cps["first"]:
            cp.start()

    def relay(srcs, outs, scratch):
        cps = copies(srcs, outs, scratch, ("arrive", "passed"))
        for landed, fwd in zip(cps["arrive"], cps["passed"]):
            landed.wait_recv()
            fwd.start()

    def end(srcs, outs, scratch):
        cps = copies(srcs, outs, scratch, ("mine", "first", "passed", "late"))
        for cp in cps["late"]:
            cp.wait_recv()
        for cp in cps["first"] + cps["passed"]:
            cp.wait_send()
        for cp in cps["mine"]:
            cp.wait()

    return _Comm(
        shards, [jax.ShapeDtypeStruct((N_DEV,) + s.shape, s.dtype) for s in shards],
        [pltpu.SemaphoreType.DMA((7 * n,)), pltpu.SemaphoreType.DMA((7 * n,)), pltpu.SemaphoreType.DMA((n,))],
        begin, end, relay)


def _sibling_comm(grads):
    n = len(grads)

    def copies(srcs, outs, scratch):
        send_sems, recv_sems = scratch
        x, y, c = _place()
        return [pltpu.make_async_remote_copy(
            src_ref=srcs[t].at[2 * p + (1 - c)], dst_ref=outs[t].at[p],
            send_sem=send_sems.at[4 * t + p], recv_sem=recv_sems.at[4 * t + p],
            device_id=(x, y, 1 - c), device_id_type=MESH) for t in range(n) for p in range(4)]

    def begin(srcs, outs, scratch):
        for cp in copies(srcs, outs, scratch):
            cp.start()

    def end(srcs, outs, scratch):
        cps = copies(srcs, outs, scratch)
        for cp in cps:
            cp.wait_recv()
        for cp in cps:
            cp.wait_send()

    return _Comm(
        grads, [jax.ShapeDtypeStruct((4,) + g.shape[1:], g.dtype) for g in grads],
        [pltpu.SemaphoreType.DMA((4 * n,)), pltpu.SemaphoreType.DMA((4 * n,))], begin, end)


def _chips_comm(sums, relations):
    n, nr = len(sums), len(relations)

    def copies(srcs, outs, scratch):
        send_sems, recv_sems = scratch
        x, y, c = _place()
        chips = _other_chips(x, y)
        return [pltpu.make_async_remote_copy(
            src_ref=srcs[t].at[2 * chips[j][0] + chips[j][1]], dst_ref=outs[t].at[s],
            send_sem=send_sems.at[nr * t + s], recv_sem=recv_sems.at[nr * t + s],
            device_id=(*chips[j], c), device_id_type=MESH) for t in range(n) for s, j in enumerate(relations)]

    def begin(srcs, outs, scratch):
        for cp in copies(srcs, outs, scratch):
            cp.start()

    def end(srcs, outs, scratch):
        cps = copies(srcs, outs, scratch)
        for cp in cps:
            cp.wait_recv()
        for cp in cps:
            cp.wait_send()

    return _Comm(
        sums, [jax.ShapeDtypeStruct((nr,) + s.shape[1:], s.dtype) for s in sums],
        [pltpu.SemaphoreType.DMA((nr * n,)), pltpu.SemaphoreType.DMA((nr * n,))], begin, end)


def _run_comm(comm, *, name):
    ci, co = len(comm.ins), len(comm.outs)

    def body(*refs):
        cin, cout, csc = refs[:ci], refs[ci:ci + co], refs[ci + co:]
        comm.begin(cin, cout, csc)
        if comm.relay is not None:
            comm.relay(cin, cout, csc)
        comm.end(cin, cout, csc)

    return pl.pallas_call(
        body, name=name, in_specs=_any_specs(ci), out_specs=_any_specs(co), out_shape=comm.outs,
        scratch_shapes=comm.scratch)(*comm.ins)


def _chip_sum(g, r1, core, *, name, tr=256):
    _, R, C = g.shape
    tr = _tile(R, tr, 16)

    def body(c_ref, g_ref, r_ref, o_ref):
        o_ref[...] = (g_ref[...].astype(F32) + r_ref[...].astype(F32)).astype(BF16)

    return pl.pallas_call(
        body, name=name,
        grid_spec=pltpu.PrefetchScalarGridSpec(
            num_scalar_prefetch=1, grid=(4, R // tr),
            in_specs=[pl.BlockSpec((None, tr, C), lambda p, i, c: (2 * p + c[0], i, 0)),
                      pl.BlockSpec((None, tr, C), lambda p, i, c: (p, i, 0))],
            out_specs=pl.BlockSpec((None, tr, C), lambda p, i, c: (p, i, 0))),
        out_shape=jax.ShapeDtypeStruct((4, R, C), BF16),
        compiler_params=_cp("parallel", "parallel"))(core, g, r1)


def _adamw_math(w, g, m, v):
    m = ADAM_B1 * m + (1.0 - ADAM_B1) * g
    v = ADAM_B2 * v + (1.0 - ADAM_B2) * (g * g)
    m_hat = m / (1.0 - ADAM_B1 ** ADAM_STEP)
    v_hat = v / (1.0 - ADAM_B2 ** ADAM_STEP)
    delta = -ADAM_LR * (m_hat / (jnp.sqrt(v_hat) + ADAM_EPS) + ADAM_WD * w)
    return delta, m, v


def _adamw_shard(s1, others, chip, w, m, v, *, name, tr=64):
    R0, C0 = w.shape
    C = s1.shape[2]
    tr = _tile(R0, tr, 8)
    (a0, k0), (a1, k1), (a2, k2) = others

    def body(p_ref, s_ref, r0_ref, r1_ref, r2_ref, w_ref, m_ref, v_ref, g_out, d_out, m_out, v_out):
        g = s_ref[:, :C0].astype(F32)
        for r_ref in (r0_ref, r1_ref, r2_ref):
            g = g + r_ref[:, :C0].astype(F32)
        delta, mn, vn = _adamw_math(w_ref[...], g, m_ref[...], v_ref[...])
        g_out[...] = g
        d_out[...] = delta
        m_out[...] = mn
        v_out[...] = vn

    full = pl.BlockSpec((tr, C0), lambda i, p: (i, 0))
    out = jax.ShapeDtypeStruct((R0, C0), F32)
    return pl.pallas_call(
        body, name=name,
        grid_spec=pltpu.PrefetchScalarGridSpec(
            num_scalar_prefetch=1, grid=(R0 // tr,),
            in_specs=[pl.BlockSpec((None, tr, C), lambda i, p: (p[0], i, 0)),
                      pl.BlockSpec((None, tr, C), lambda i, p: (k0, i, 0)),
                      pl.BlockSpec((None, tr, C), lambda i, p: (k1, i, 0)),
                      pl.BlockSpec((None, tr, C), lambda i, p: (k2, i, 0)),
                      full, full, full],
            out_specs=[full, full, full, full]),
        out_shape=[out, out, out, out],
        compiler_params=_cp("parallel"))(chip, s1, a0, a1, a2, w, m, v)


def _all_reduce_small(part, *, name):
    SR = part.shape[0]

    def body(p_ref, o_ref, buf, send_sems, recv_sems):
        x, y, c = _place()
        me = 4 * x + 2 * y + c
        buf[0] = p_ref[...]
        copies = []
        for k in range(1, N_DEV):
            peer = (1 - x if k & 4 else x, 1 - y if k & 2 else y, 1 - c if k & 1 else c)
            copies.append(pltpu.make_async_remote_copy(
                src_ref=p_ref, dst_ref=buf.at[k], send_sem=send_sems.at[k - 1], recv_sem=recv_sems.at[k - 1],
                device_id=peer, device_id_type=MESH))
        for cp in copies:
            cp.start()
        for cp in copies:
            cp.wait_recv()
        for cp in copies:
            cp.wait_send()
        acc = buf[me]
        for d in range(1, N_DEV):
            acc = acc + buf[jnp.bitwise_xor(me, d)]
        o_ref[...] = acc

    return pl.pallas_call(
        body, name=name,
        in_specs=[pl.BlockSpec(memory_space=pltpu.VMEM)], out_specs=pl.BlockSpec(memory_space=pltpu.VMEM),
        out_shape=jax.ShapeDtypeStruct(part.shape, F32),
        scratch_shapes=[pltpu.VMEM((N_DEV, SR, LANE), F32), pltpu.SemaphoreType.DMA((N_DEV - 1,)),
                        pltpu.SemaphoreType.DMA((N_DEV - 1,))])(part)


def _adamw_small(w, g, m, v, *, name):
    def body(w_ref, g_ref, m_ref, v_ref, d_out, m_out, v_out):
        delta, mn, vn = _adamw_math(w_ref[...], g_ref[...], m_ref[...], v_ref[...])
        d_out[...] = delta
        m_out[...] = mn
        v_out[...] = vn

    out = jax.ShapeDtypeStruct(w.shape, F32)
    return pl.pallas_call(body, name=name, out_shape=[out, out, out], compiler_params=_cp())(w, g, m, v)


def _pack(parts, rows):
    flat = jnp.concatenate([p.reshape(-1).astype(F32) for p in parts])
    return jnp.pad(flat, (0, rows * LANE - flat.shape[0])).reshape(rows, LANE)


def _unpack(packed, shapes):
    flat = packed.reshape(-1)
    out, at = [], 0
    for s in shapes:
        n = 1
        for d in s:
            n *= d
        out.append(flat[at:at + n].reshape(s))
        at += n
    return out


def kernel(x, g_mix, w_in, b_f, b_gate, rel_bias, w_branch_a, w_branch_b, w_out, g_ffn, w_gate_ffn, w_up_ffn, w_down_ffn, g_final, loss_target, m_g_mix, m_w_in, m_b_f, m_b_gate, m_rel_bias, m_w_branch_a, m_w_branch_b, m_w_out, m_g_ffn, m_w_gate_ffn, m_w_up_ffn, m_w_down_ffn, m_g_final, v_g_mix, v_w_in, v_b_f, v_b_gate, v_rel_bias, v_w_branch_a, v_w_branch_b, v_w_out, v_g_ffn, v_w_gate_ffn, v_w_up_ffn, v_w_down_ffn, v_g_final):
    _, T, D = x.shape
    HA, n_rel = rel_bias.shape[1], rel_bias.shape[2]
    HB = b_f.shape[1]
    WA, WB = HA * HEAD_DIM, HB * HEAD_DIM
    din8, f8, d8 = w_in.shape[2], w_gate_ffn.shape[2], D // N_DEV
    DIN = N_DEV * din8
    PIN, PF = _pad_to(din8, LANE), _pad_to(f8, LANE)
    padk = LEFT_CHUNKS * CHUNK
    QA = min(256, T)
    KW = QA + padk
    KB = min(512, T)
    x2d, tgt = x[0], loss_target[0]

    s_in = jnp.pad(w_in[0].astype(BF16), ((0, 0), (0, PIN - din8)))
    s_a, s_b, s_out = w_branch_a[0].astype(BF16), w_branch_b[0].astype(BF16), w_out[0].astype(BF16)
    s_g = jnp.pad(w_gate_ffn[0].astype(BF16), ((0, 0), (0, PF - f8)))
    s_u = jnp.pad(w_up_ffn[0].astype(BF16), ((0, 0), (0, PF - f8)))
    s_d = jnp.pad(w_down_ffn[0].astype(BF16), ((0, PF - f8), (0, 0)))
    Win, = _run_comm(_gather_comm([s_in]), name="gather_w_in")
    cx, cy, cc = _place()
    core = jnp.reshape(cc, (1,)).astype(jnp.int32)
    chip = jnp.reshape(2 * cx + cy, (1,)).astype(jnp.int32)
    every = [0, 1, 2]
    fcol = 3 * WA + 3 * WB
    Wfull = Win[:, :, :din8].transpose(1, 0, 2).reshape(D, DIN)
    Wqkv = Wfull[:, :fcol].reshape(1, D, fcol)
    Wf = jnp.pad(Wfull[:, fcol:fcol + HB], ((0, 0), (0, LANE - HB))).reshape(1, D, LANE)
    Wgt = Wfull[:, fcol + HB:].reshape(1, D, 2 * D)
    bf_pad = jnp.pad(b_f, ((0, 0), (0, LANE - HB)))
    fox_cols = (3 * HA, 3 * HA + HB, 3 * HA + 2 * HB)

    h = _rms_fwd(x2d, g_mix, name="rms_mix")
    qkv, (Wg,) = _mm_nn(h, Wqkv, name="proj_qkv", out_dtype=BF16, comm=_gather_comm([s_g]))
    gates, (Wa, Wb, Wout) = _mm_nn(h, Wgt, name="proj_gates", out_dtype=BF16, comm=_gather_comm([s_a, s_b, s_out]))
    Wout = Wout.reshape(1, D, D)
    flog = _mm_nn(h, Wf, name="proj_forget", out_dtype=F32)

    kap = jnp.pad(qkv[:, WA:2 * WA], ((padk, 0), (0, 0)))
    vap = jnp.pad(qkv[:, 2 * WA:3 * WA], ((padk, 0), (0, 0)))
    btab = _bias_table(rel_bias[0], QA, KW)
    o_a, lse_a = _attn_a_fwd(qkv, kap, vap, btab, name="attn_a_fwd")

    cum = _fgate_fwd(flog, bf_pad, name="forget_cumsum")
    cum_t = cum[:, :HB].T.reshape(HB, T // KB, KB)
    (o_b, lse_b), (Wu,) = _fox_fwd(qkv, fox_cols, cum, cum_t, name="fox_fwd", comm=_gather_comm([s_u]))

    u_a = _mm_nn(o_a, Wa, name="branch_a", out_dtype=BF16)
    u_b = _mm_nn(o_b, Wb, name="branch_b", out_dtype=BF16)
    merged = _merge_fwd(gates, b_gate, u_a, u_b, name="merge_fwd")
    x1 = _mm_nn(merged, Wout, name="out_proj", out_dtype=F32, res=x2d)
    h2 = _rms_fwd(x1, g_ffn, name="rms_ffn")
    gt, (Wd,) = _mm_nn(h2, Wg, name="ffn_gate", out_dtype=BF16, tn=PF, comm=_gather_comm([s_d]))
    Wd = Wd.reshape(1, N_DEV * PF, D)
    up, hid = _mm_nn(h2, Wu, name="ffn_up", out_dtype=BF16, tn=PF, tm=512,
                     epi=_swiglu_fwd_epi, extras=[gt], n_out=2)
    x2 = _mm_nn(hid, Wd, name="ffn_down", out_dtype=F32, res=x1, tk=2 * PF)
    dx2, dx2b, dg_final, loss_part = _final_loss(x2, tgt, g_final.reshape(1, D), name="final_loss")

    dgt, dup = _mm_nt(dx2b, Wd, name="d_hidden", out_dtype=BF16, epi=_swiglu_bwd_epi, extras=[gt, up], n_out=2)
    gWd = _mm_tn(hid, dx2b, 1, name="gw_down").reshape(N_DEV, PF, D)
    dh2, (sWd,) = _mm_nt(dgt, Wg, name="d_h2_gate", out_dtype=F32, comm=_sibling_comm([gWd]))
    cWd = _chip_sum(gWd, sWd, core, name="chip_sum_w_down_ffn")
    dh2, (rWd,) = _mm_nt(dup, Wu, name="d_h2_up", out_dtype=F32, res=dh2, comm=_chips_comm([cWd], every))
    gWg = _mm_tn(h2, dgt, N_DEV, name="gw_gate", tn=PF)
    gWu = _mm_tn(h2, dup, N_DEV, name="gw_up", tn=PF)
    dx1, dx1b, dg_ffn = _rms_bwd(dh2, x1, g_ffn, dx2, name="rms_ffn_bwd")
    dmer, (sWg, sWu) = _mm_nt(dx1b, Wout, name="d_merged", out_dtype=BF16, comm=_sibling_comm([gWg, gWu]))
    cWg = _chip_sum(gWg, sWg, core, name="chip_sum_w_gate_ffn")
    cWu = _chip_sum(gWu, sWu, core, name="chip_sum_w_up_ffn")
    gWout = _mm_tn(merged, dx1b, 1, name="gw_out").reshape(N_DEV, d8, D)
    dua, dub, dgates, dbg = _merge_bwd(dmer, gates, b_gate, u_a, u_b, name="merge_bwd")
    g_gates = _mm_tn(h, dgates, 1, name="gw_gates")[0]
    doa = _mm_nt(dua, Wa, name="d_o_a", out_dtype=BF16)
    dob = _mm_nt(dub, Wb, name="d_o_b", out_dtype=BF16)
    gWa = _mm_tn(o_a, dua, N_DEV, name="gw_branch_a")
    gWb = _mm_tn(o_b, dub, N_DEV, name="gw_branch_b")

    (dqkv, dkap, dvap, dtab), (sWout, sWa, sWb) = _attn_a_bwd(
        qkv, kap, vap, btab, o_a, doa, lse_a, name="attn_a_bwd", comm=_sibling_comm([gWout, gWa, gWb]),
        dq_width=fcol)
    cWout = _chip_sum(gWout, sWout, core, name="chip_sum_w_out")
    cWa = _chip_sum(gWa, sWa, core, name="chip_sum_w_branch_a")
    cWb = _chip_sum(gWb, sWb, core, name="chip_sum_w_branch_b")
    skew = jnp.pad(dtab.reshape(HA, QA * KW), ((0, 0), (0, QA))).reshape(HA, QA, KW + 1)
    colsum, tail = _relbias_fold(skew, dtab, name="relbias_fold")
    rev = colsum[:, 0, ::-1]
    d_rel = jnp.concatenate([rev[:, :2 * MAX_REL], tail[:, 0, :1]], axis=1)

    (dqkv, dkb, dvb, dcq, dck), (rWg, rWout, rWa, rWb) = _fox_bwd(
        qkv, fox_cols, cum, cum_t, o_b, dob, lse_b, name="fox_bwd", comm=_chips_comm([cWg, cWout, cWa, cWb], every),
        dq_into=(dqkv, fox_cols[0]))
    dcq_p = jnp.pad(dcq.reshape(T, HB, HEAD_DIM)[:, :, 0], ((0, 0), (0, LANE - HB)))
    dck_p = jnp.pad(dck.reshape(HB, T).T, ((0, 0), (0, LANE - HB)))
    dflog, dbf = _fgate_bwd(dcq_p, dck_p, flog, bf_pad, name="forget_bwd")

    dqkv = _cast_into(dqkv, dkap, padk, WA, name="dqkv_k_a")
    dqkv = _cast_into(dqkv, dvap, padk, 2 * WA, name="dqkv_v_a")
    dqkv = _cast_into(dqkv, dkb, 0, 3 * WA + WB, name="dqkv_k_b")
    dqkv = _cast_into(dqkv, dvb, 0, 3 * WA + 2 * WB, name="dqkv_v_b")
    dfb = dflog.astype(BF16)
    g_qkv, (rWu,) = _mm_tn(h, dqkv, 1, name="gw_qkv", comm=_chips_comm([cWu], every))
    g_in = jnp.concatenate([g_qkv[0], _mm_tn(h, dfb, 1, name="gw_forget")[0][:, :HB], g_gates], axis=1)
    gWin = jnp.pad(g_in.reshape(D, N_DEV, din8).transpose(1, 0, 2), ((0, 0), (0, 0), (0, PIN - din8)))
    dh, (sWin,) = _mm_nt(dfb, Wf, name="d_h_forget", out_dtype=F32, comm=_sibling_comm([gWin]))
    cWin = _chip_sum(gWin, sWin, core, name="chip_sum_w_in")
    dh, (rWin01,) = _mm_nt(dqkv, Wqkv, name="d_h_qkv", out_dtype=F32, res=dh, comm=_chips_comm([cWin], [0, 1]))
    dh, (rWin2,) = _mm_nt(dgates, Wgt, name="d_h_gates", out_dtype=F32, res=dh, comm=_chips_comm([cWin], [2]))
    dx, _, dg_mix = _rms_bwd(dh, x2d, g_mix, dx1, name="rms_mix_bwd")

    def slots(r):
        return [(r, 0), (r, 1), (r, 2)]

    big = {}
    for nm, s1, others, w, m, v in [
            ("w_down_ffn", cWd, slots(rWd), w_down_ffn, m_w_down_ffn, v_w_down_ffn),
            ("w_gate_ffn", cWg, slots(rWg), w_gate_ffn, m_w_gate_ffn, v_w_gate_ffn),
            ("w_out", cWout, slots(rWout), w_out, m_w_out, v_w_out),
            ("w_branch_a", cWa, slots(rWa), w_branch_a, m_w_branch_a, v_w_branch_a),
            ("w_branch_b", cWb, slots(rWb), w_branch_b, m_w_branch_b, v_w_branch_b),
            ("w_up_ffn", cWu, slots(rWu), w_up_ffn, m_w_up_ffn, v_w_up_ffn),
            ("w_in", cWin, [(rWin01, 0), (rWin01, 1), (rWin2, 0)], w_in, m_w_in, v_w_in)]:
        outs = _adamw_shard(s1, others, chip, w[0], m[0], v[0], name="adamw_" + nm)
        big[nm] = [o[None] for o in outs]

    small_names = ["g_mix", "b_f", "b_gate", "rel_bias", "g_ffn", "g_final"]
    small_w = [g_mix, b_f, b_gate, rel_bias, g_ffn, g_final]
    small_m = [m_g_mix, m_b_f, m_b_gate, m_rel_bias, m_g_ffn, m_g_final]
    small_v = [v_g_mix, v_b_f, v_b_gate, v_rel_bias, v_g_ffn, v_g_final]
    small_g = [dg_mix, dbf[:, :HB], dbg, d_rel[None], dg_ffn, dg_final[0]]
    shapes = [w.shape for w in small_w]
    n_small = sum(w.size for w in small_w) + 1
    rows = _pad_to(-(-n_small // LANE), 8)
    total = _all_reduce_small(_pack(small_g + [loss_part[:1, :1]], rows), name="all_reduce_small")
    d_s, m_s, v_s = _adamw_small(_pack(small_w, rows), total, _pack(small_m, rows), _pack(small_v, rows),
                                 name="adamw_small")
    g_small = dict(zip(small_names, _unpack(total, shapes)))
    d_small = dict(zip(small_names, _unpack(d_s, shapes)))
    m_small = dict(zip(small_names, _unpack(m_s, shapes)))
    v_small = dict(zip(small_names, _unpack(v_s, shapes)))
    loss = total.reshape(-1)[n_small - 1]

    order = ["g_mix", "w_in", "b_f", "b_gate", "rel_bias", "w_branch_a", "w_branch_b", "w_out", "g_ffn",
             "w_gate_ffn", "w_up_ffn", "w_down_ffn", "g_final"]

    def pick(idx, small):
        return [big[nm][idx] if nm in big else small[nm] for nm in order]

    return (loss, dx[None], *pick(0, g_small), *pick(1, d_small), *pick(2, m_small), *pick(3, v_small))
```

```python
import functools

import jax
import jax.numpy as jnp
from jax import lax
from jax.experimental import pallas as pl
from jax.experimental.pallas import tpu as pltpu

F32 = jnp.float32
BF16 = jnp.bfloat16
MESH = pl.DeviceIdType.MESH

N_DEV = 8
LANE = 128
HEAD_DIM = 128
CHUNK = 64
LEFT_CHUNKS = 8
MAX_REL = 256
RMS_EPS = 1e-6
NEG_INF = -1e30
ATTN_SCALE = HEAD_DIM ** -0.5
VMEM_LIMIT_BYTES = 56 * 1024 * 1024

ADAM_LR = 0.001
ADAM_B1 = 0.9
ADAM_B2 = 0.999
ADAM_EPS = 1e-08
ADAM_WD = 0.01
ADAM_STEP = 10

NT_DIMS = (((1,), (1,)), ((), ()))
TN_DIMS = (((0,), (0,)), ((), ()))


def _cp(*sem):
    return pltpu.CompilerParams(dimension_semantics=sem if sem else None, vmem_limit_bytes=VMEM_LIMIT_BYTES)


def _pad_to(n, mult):
    return -(-n // mult) * mult


def _tile(dim, pref, align):
    t = (min(pref, dim) // align) * align
    while t >= align:
        if dim % t == 0:
            return t
        t -= align
    return dim


def _sigmoid(z):
    return 1.0 / (1.0 + jnp.exp(-z))


class _Comm:
    def __init__(self, ins, outs, scratch, begin, end, relays=()):
        self.ins, self.outs, self.scratch = list(ins), list(outs), list(scratch)
        self.begin, self.end, self.relays = begin, end, list(relays)


def _call(body, args, *, name, grid, in_specs, out_specs, out_shape, scratch=(), sem=(), comm=None, aliases=None):
    single = not isinstance(out_shape, (list, tuple))
    out_shape = [out_shape] if single else list(out_shape)
    out_specs = [out_specs] if single else list(out_specs)
    aliases = dict(aliases or {})
    if comm is None:
        res = pl.pallas_call(
            body, name=name, grid=grid, in_specs=list(in_specs), out_specs=out_specs, out_shape=out_shape,
            scratch_shapes=list(scratch), input_output_aliases=aliases, compiler_params=_cp(*sem))(*args)
        return res[0] if single else res
    ni, no, ns = len(args), len(out_shape), len(scratch)
    ci, co = len(comm.ins), len(comm.outs)
    total = 1
    for g in grid:
        total *= g

    def hosted(*refs):
        a, cin = refs[:ni], refs[ni:ni + ci]
        o, cout = refs[ni + ci:ni + ci + no], refs[ni + ci + no:ni + ci + no + co]
        s, csc = refs[ni + ci + no + co:ni + ci + no + co + ns], refs[ni + ci + no + co + ns:]
        step = pl.program_id(0)
        for d in range(1, len(grid)):
            step = step * grid[d] + pl.program_id(d)

        @pl.when(step == 0)
        def _():
            comm.begin(cin, cout, csc)

        for fraction, relay in comm.relays:
            @pl.when(step == min(int(total * fraction), total - 1))
            def _(relay=relay):
                relay(cin, cout, csc)

        body(*a, *o, *s)

        @pl.when(step == total - 1)
        def _():
            comm.end(cin, cout, csc)

    res = pl.pallas_call(
        hosted, name=name, grid=grid, in_specs=list(in_specs) + _any_specs(ci),
        out_specs=out_specs + _any_specs(co), out_shape=out_shape + comm.outs,
        scratch_shapes=list(scratch) + comm.scratch, input_output_aliases=aliases,
        compiler_params=_cp(*(("arbitrary",) * len(grid))))(*args, *comm.ins)
    return (res[0] if single else res[:no]), res[no:]


def _any_specs(n):
    return [pl.BlockSpec(memory_space=pl.ANY)] * n


def _mm_body(dims, nk, has_res, n_ext, n_out, epi):
    def body(*refs):
        a_ref, w_ref = refs[:2]
        at = 2 + has_res
        r_ref = refs[2] if has_res else None
        e_refs, o_refs = refs[at:at + n_ext], refs[at + n_ext:at + n_ext + n_out]

        def emit(v):
            if r_ref is not None:
                v = v + r_ref[...]
            vals = (v,) if epi is None else epi(v, *[e[...] for e in e_refs])
            for o_ref, val in zip(o_refs, vals):
                o_ref[...] = val.astype(o_ref.dtype)

        if nk == 1:
            emit(lax.dot_general(a_ref[...], w_ref[...], dims, preferred_element_type=F32))
            return
        acc = refs[-1]
        k = pl.program_id(2)

        @pl.when(k == 0)
        def _():
            acc[...] = jnp.zeros_like(acc)

        acc[...] += lax.dot_general(a_ref[...], w_ref[...], dims, preferred_element_type=F32)

        @pl.when(k == nk - 1)
        def _():
            emit(acc[...])

    return body


def _mm_nn(x, w, *, name, out_dtype, res=None, tm=1024, tn=1024, tk=None, comm=None, epi=None, extras=(), n_out=1):
    M, K = x.shape
    NB, _, NW = w.shape
    tm, tn = _tile(M, tm, 16), _tile(NW, tn, LANE)
    tk = K if tk is None else _tile(K, tk, LANE)
    nj, nk = NW // tn, K // tk
    tile = pl.BlockSpec((tm, tn), lambda j, i, k: (i, j))
    in_specs = [pl.BlockSpec((tm, tk), lambda j, i, k: (i, k)),
                pl.BlockSpec((None, tk, tn), lambda j, i, k: (j // nj, k, j % nj))]
    args = [x, w] + ([res] if res is not None else []) + list(extras)
    in_specs += [tile] * (len(args) - 2)
    out = jax.ShapeDtypeStruct((M, NB * NW), out_dtype)
    return _call(
        _mm_body((((1,), (0,)), ((), ())), nk, res is not None, len(extras), n_out, epi), args,
        name=name, grid=(NB * nj, M // tm, nk), in_specs=in_specs,
        out_specs=tile if n_out == 1 else [tile] * n_out, out_shape=out if n_out == 1 else [out] * n_out,
        scratch=[pltpu.VMEM((tm, tn), F32)] if nk > 1 else [],
        sem=("parallel", "parallel", "arbitrary"), comm=comm)


def _mm_nt(dy, w, *, name, out_dtype, res=None, tm=1024, tn=1024, tc=2816, comm=None, epi=None, extras=(), n_out=1):
    M, N = dy.shape
    NB, Kw, NW = w.shape
    assert N == NB * NW
    tm, tn, tc = _tile(M, tm, 16), _tile(Kw, tn, LANE), _tile(NW, tc, LANE)
    nc = NW // tc
    nk = NB * nc
    tile = pl.BlockSpec((tm, tn), lambda i, j, k: (i, j))
    in_specs = [pl.BlockSpec((tm, tc), lambda i, j, k: (i, k)),
                pl.BlockSpec((None, tn, tc), lambda i, j, k: (k // nc, j, k % nc))]
    args = [dy, w] + ([res] if res is not None else []) + list(extras)
    in_specs += [tile] * (len(args) - 2)
    out = jax.ShapeDtypeStruct((M, Kw), out_dtype)
    return _call(
        _mm_body(NT_DIMS, nk, res is not None, len(extras), n_out, epi), args,
        name=name, grid=(M // tm, Kw // tn, nk), in_specs=in_specs,
        out_specs=tile if n_out == 1 else [tile] * n_out, out_shape=out if n_out == 1 else [out] * n_out,
        scratch=[pltpu.VMEM((tm, tn), F32)] if nk > 1 else [],
        sem=("parallel", "parallel", "arbitrary"), comm=comm)


def _mm_tn(x, dy, nb, *, name, tm=512, tn=1024, comm=None):
    M, Kw = x.shape
    NW = dy.shape[1] // nb
    tm, tn = _tile(Kw, tm, LANE), _tile(NW, tn, LANE)
    nj = NW // tn

    def body(x_ref, d_ref, o_ref):
        o_ref[...] = lax.dot_general(x_ref[...], d_ref[...], TN_DIMS, preferred_element_type=F32).astype(o_ref.dtype)

    return _call(
        body, [x, dy], name=name, grid=(nb * nj, Kw // tm),
        in_specs=[pl.BlockSpec((M, tm), lambda j, i: (0, i)),
                  pl.BlockSpec((M, tn), lambda j, i: (0, j))],
        out_specs=pl.BlockSpec((None, tm, tn), lambda j, i: (j // nj, i, j % nj)),
        out_shape=jax.ShapeDtypeStruct((nb, Kw, NW), BF16),
        sem=("parallel", "parallel"), comm=comm)


def _rms_fwd(x, g, *, name, tr=256):
    T, D = x.shape
    tr = _tile(T, tr, 16)

    def body(x_ref, g_ref, h_ref):
        xv = x_ref[...]
        r = lax.rsqrt(jnp.mean(xv * xv, axis=-1, keepdims=True) + RMS_EPS)
        h_ref[...] = ((xv * r) * g_ref[...]).astype(BF16)

    return pl.pallas_call(
        body, name=name, grid=(T // tr,),
        in_specs=[pl.BlockSpec((tr, D), lambda i: (i, 0)), pl.BlockSpec((1, D), lambda i: (0, 0))],
        out_specs=pl.BlockSpec((tr, D), lambda i: (i, 0)),
        out_shape=jax.ShapeDtypeStruct((T, D), BF16),
        compiler_params=_cp("parallel"))(x, g)


def _rms_bwd(dh, x, g, dres, *, name, tr=128):
    T, D = x.shape
    tr = _tile(T, tr, 16)

    def body(dh_ref, x_ref, g_ref, dr_ref, dx_ref, dxb_ref, dg_ref):
        i = pl.program_id(0)
        xv = x_ref[...]
        dhv = dh_ref[...]
        r = lax.rsqrt(jnp.mean(xv * xv, axis=-1, keepdims=True) + RMS_EPS)
        a = dhv * g_ref[...]
        dot = jnp.mean(a * xv, axis=-1, keepdims=True)
        dx = dr_ref[...] + r * a - xv * (r * r * r * dot)
        dx_ref[...] = dx
        dxb_ref[...] = dx.astype(BF16)
        part = jnp.sum(dhv * (xv * r), axis=0, keepdims=True)

        @pl.when(i == 0)
        def _():
            dg_ref[...] = part

        @pl.when(i > 0)
        def _():
            dg_ref[...] += part

    row = pl.BlockSpec((tr, D), lambda i: (i, 0))
    vec = pl.BlockSpec((1, D), lambda i: (0, 0))
    return pl.pallas_call(
        body, name=name, grid=(T // tr,),
        in_specs=[row, row, vec, row], out_specs=[row, row, vec],
        out_shape=[jax.ShapeDtypeStruct((T, D), F32), jax.ShapeDtypeStruct((T, D), BF16),
                   jax.ShapeDtypeStruct((1, D), F32)],
        compiler_params=_cp("arbitrary"))(dh, x, g, dres)


def _final_loss(x2, tgt, g, *, name, tr=128):
    T, D = x2.shape
    tr = _tile(T, tr, 16)

    def body(x_ref, t_ref, g_ref, dx_ref, dxb_ref, dg_ref, loss_ref):
        i = pl.program_id(0)
        xv = x_ref[...]
        gv = g_ref[...]
        r = lax.rsqrt(jnp.mean(xv * xv, axis=-1, keepdims=True) + RMS_EPS)
        xh = xv * r
        diff = xh * gv - t_ref[...]
        lpart = 0.5 * jnp.sum(jnp.mean(diff * diff, axis=-1, keepdims=True))
        dy = diff * (1.0 / D)
        a = dy * gv
        dot = jnp.mean(a * xv, axis=-1, keepdims=True)
        dx = r * a - xv * (r * r * r * dot)
        dx_ref[...] = dx
        dxb_ref[...] = dx.astype(BF16)
        part = jnp.sum(dy * xh, axis=0, keepdims=True)
        lblk = jnp.full((8, LANE), lpart, F32)

        @pl.when(i == 0)
        def _():
            dg_ref[...] = part
            loss_ref[...] = lblk

        @pl.when(i > 0)
        def _():
            dg_ref[...] += part
            loss_ref[...] += lblk

    row = pl.BlockSpec((tr, D), lambda i: (i, 0))
    vec = pl.BlockSpec((1, D), lambda i: (0, 0))
    return pl.pallas_call(
        body, name=name, grid=(T // tr,),
        in_specs=[row, row, vec],
        out_specs=[row, row, vec, pl.BlockSpec((8, LANE), lambda i: (0, 0))],
        out_shape=[jax.ShapeDtypeStruct((T, D), F32), jax.ShapeDtypeStruct((T, D), BF16),
                   jax.ShapeDtypeStruct((1, D), F32), jax.ShapeDtypeStruct((8, LANE), F32)],
        compiler_params=_cp("arbitrary"))(x2, tgt, g)


def _merge_fwd(gates, b_gate, ua, ub, *, name, tr=256):
    T, D = ua.shape
    tr = _tile(T, tr, 16)

    def body(ga_ref, gb_ref, ba_ref, bb_ref, ua_ref, ub_ref, o_ref):
        sa = _sigmoid(ga_ref[...].astype(F32) + ba_ref[...])
        sb = _sigmoid(gb_ref[...].astype(F32) + bb_ref[...])
        o_ref[...] = (sa * ua_ref[...].astype(F32) + sb * ub_ref[...].astype(F32)).astype(BF16)

    row = pl.BlockSpec((tr, D), lambda i: (i, 0))
    return pl.pallas_call(
        body, name=name, grid=(T // tr,),
        in_specs=[row, pl.BlockSpec((tr, D), lambda i: (i, 1)),
                  pl.BlockSpec((1, D), lambda i: (0, 0)), pl.BlockSpec((1, D), lambda i: (0, 1)), row, row],
        out_specs=row, out_shape=jax.ShapeDtypeStruct((T, D), BF16),
        compiler_params=_cp("parallel"))(gates, gates, b_gate, b_gate, ua, ub)


def _merge_bwd(dm, gates, b_gate, ua, ub, *, name, tr=256):
    T, D = ua.shape
    tr = _tile(T, tr, 16)

    def body(dm_ref, ga_ref, gb_ref, ba_ref, bb_ref, ua_ref, ub_ref, dua_ref, dub_ref, dg_ref, db_ref):
        i = pl.program_id(0)
        dmv = dm_ref[...].astype(F32)
        sa = _sigmoid(ga_ref[...].astype(F32) + ba_ref[...])
        sb = _sigmoid(gb_ref[...].astype(F32) + bb_ref[...])
        dua_ref[...] = (dmv * sa).astype(BF16)
        dub_ref[...] = (dmv * sb).astype(BF16)
        dga = dmv * ua_ref[...].astype(F32) * (sa * (1.0 - sa))
        dgb = dmv * ub_ref[...].astype(F32) * (sb * (1.0 - sb))
        dg_ref[:, :D] = dga.astype(BF16)
        dg_ref[:, D:] = dgb.astype(BF16)
        pa = jnp.sum(dga, axis=0, keepdims=True)
        pb = jnp.sum(dgb, axis=0, keepdims=True)

        @pl.when(i == 0)
        def _():
            db_ref[:, :D] = pa
            db_ref[:, D:] = pb

        @pl.when(i > 0)
        def _():
            db_ref[:, :D] += pa
            db_ref[:, D:] += pb

    row = pl.BlockSpec((tr, D), lambda i: (i, 0))
    vec = pl.BlockSpec((1, D), lambda i: (0, 0))
    act = jax.ShapeDtypeStruct((T, D), BF16)
    return pl.pallas_call(
        body, name=name, grid=(T // tr,),
        in_specs=[row, row, pl.BlockSpec((tr, D), lambda i: (i, 1)),
                  vec, pl.BlockSpec((1, D), lambda i: (0, 1)), row, row],
        out_specs=[row, row, pl.BlockSpec((tr, 2 * D), lambda i: (i, 0)), pl.BlockSpec((1, 2 * D), lambda i: (0, 0))],
        out_shape=[act, act, jax.ShapeDtypeStruct((T, 2 * D), BF16), jax.ShapeDtypeStruct((1, 2 * D), F32)],
        compiler_params=_cp("arbitrary"))(dm, gates, gates, b_gate, b_gate, ua, ub)


def _swiglu_fwd_epi(up, gate):
    g = gate.astype(F32)
    up = up.astype(BF16).astype(F32)
    return up, g * _sigmoid(g) * up


def _swiglu_bwd_epi(d, gate, up):
    g = gate.astype(F32)
    s = _sigmoid(g)
    return d * up.astype(F32) * (s * (1.0 + g * (1.0 - s))), d * (g * s)


def _cast_into(buf, src, row_off, col_off, *, name, tr=256):
    T = buf.shape[0]
    W = src.shape[1]
    tr = _tile(T, tr, 16)
    cw = W
    while col_off % cw or W % cw:
        cw -= LANE
    assert row_off % tr == 0
    c0, r0 = col_off // cw, row_off // tr

    def body(s_ref, b_ref, o_ref):
        o_ref[...] = s_ref[...].astype(o_ref.dtype)

    return pl.pallas_call(
        body, name=name, grid=(T // tr, W // cw),
        in_specs=[pl.BlockSpec((tr, cw), lambda i, j: (i + r0, j)), pl.BlockSpec(memory_space=pl.ANY)],
        out_specs=pl.BlockSpec((tr, cw), lambda i, j: (i, c0 + j)),
        out_shape=jax.ShapeDtypeStruct(buf.shape, buf.dtype), input_output_aliases={1: 0},
        compiler_params=_cp("parallel", "parallel"))(src, buf)


def _bias_table(rel, qb, kw):
    assert qb - 1 <= MAX_REL
    ha = rel.shape[0]
    n_clipped = kw - MAX_REL
    e = jnp.concatenate([jnp.broadcast_to(rel[:, 2 * MAX_REL:], (ha, n_clipped)),
                         rel[:, 2 * MAX_REL - 1:MAX_REL - qb:-1]], axis=1)
    period = qb + kw
    e = jnp.pad(e, ((0, 0), (0, period - e.shape[1])))
    z = jnp.broadcast_to(e[:, None, :], (ha, qb, period)).reshape(ha, qb * period)
    z = z[:, :qb * (period - 1)].reshape(ha, qb, period - 1)
    i = jnp.arange(qb)[:, None]
    j = jnp.arange(kw)[None, :]
    band = (j // CHUNK >= i // CHUNK) & (j // CHUNK <= i // CHUNK + LEFT_CHUNKS)
    return jnp.where(band[None], z[:, :, qb - 1:qb - 1 + kw], NEG_INF)


def _attn_a_fwd(q, kp, vp, btab, *, name):
    T, WA = q.shape[0], kp.shape[1]
    HA = WA // HEAD_DIM
    _, QB, KW = btab.shape
    padk = KW - QB

    def body(q_ref, k_ref, v_ref, b_ref, o_ref, lse_ref):
        i = pl.program_id(1)
        start = pl.multiple_of(i * QB, QB)
        k = k_ref[pl.ds(start, KW), :]
        v = v_ref[pl.ds(start, KW), :]
        s = lax.dot_general(q_ref[...], k, NT_DIMS, preferred_element_type=F32) * ATTN_SCALE + b_ref[...]
        kpos = start - padk + lax.broadcasted_iota(jnp.int32, (QB, KW), 1)
        s = jnp.where(kpos >= 0, s, NEG_INF)
        m = jnp.max(s, axis=-1, keepdims=True)
        p = jnp.exp(s - m)
        l = jnp.sum(p, axis=-1, keepdims=True)
        o = jnp.dot(p.astype(BF16), v, preferred_element_type=F32) / l
        o_ref[...] = o.astype(BF16)
        lse_ref[...] = jnp.broadcast_to(m + jnp.log(l), (QB, HEAD_DIM))

    qspec = pl.BlockSpec((QB, HEAD_DIM), lambda h, i: (i, h))
    kspec = pl.BlockSpec((T + padk, HEAD_DIM), lambda h, i: (0, h))
    return pl.pallas_call(
        body, name=name, grid=(HA, T // QB),
        in_specs=[qspec, kspec, kspec, pl.BlockSpec((None, QB, KW), lambda h, i: (h, 0, 0))],
        out_specs=[qspec, qspec],
        out_shape=[jax.ShapeDtypeStruct((T, WA), BF16), jax.ShapeDtypeStruct((T, WA), F32)],
        compiler_params=_cp("parallel", "arbitrary"))(q, kp, vp, btab)


def _attn_a_bwd(q, kp, vp, btab, o, do, lse, *, name, comm=None, dq_width=None):
    T, WA = q.shape[0], kp.shape[1]
    HA = WA // HEAD_DIM
    _, QB, KW = btab.shape
    padk = KW - QB

    def body(q_ref, k_ref, v_ref, b_ref, o_ref, do_ref, lse_ref, dq_ref, dk_ref, dv_ref, db_ref):
        i = pl.program_id(1)

        @pl.when(i == 0)
        def _():
            dk_ref[...] = jnp.zeros_like(dk_ref)
            dv_ref[...] = jnp.zeros_like(dv_ref)
            db_ref[...] = jnp.zeros_like(db_ref)

        start = pl.multiple_of(i * QB, QB)
        qv = q_ref[...]
        dov = do_ref[...]
        k = k_ref[pl.ds(start, KW), :]
        v = v_ref[pl.ds(start, KW), :]
        s = lax.dot_general(qv, k, NT_DIMS, preferred_element_type=F32) * ATTN_SCALE + b_ref[...]
        kpos = start - padk + lax.broadcasted_iota(jnp.int32, (QB, KW), 1)
        s = jnp.where(kpos >= 0, s, NEG_INF)
        p = jnp.exp(s - lse_ref[:, :1])
        dp = lax.dot_general(dov, v, NT_DIMS, preferred_element_type=F32)
        delta = jnp.sum(dov.astype(F32) * o_ref[...].astype(F32), axis=-1, keepdims=True)
        ds = p * (dp - delta)
        db_ref[...] += ds
        dsb = ds.astype(BF16)
        dq_ref[...] = (jnp.dot(dsb, k, preferred_element_type=F32) * ATTN_SCALE).astype(BF16)
        dk_ref[pl.ds(start, KW), :] += lax.dot_general(dsb, qv, TN_DIMS, preferred_element_type=F32) * ATTN_SCALE
        dv_ref[pl.ds(start, KW), :] += lax.dot_general(p.astype(BF16), dov, TN_DIMS, preferred_element_type=F32)

    qspec = pl.BlockSpec((QB, HEAD_DIM), lambda h, i: (i, h))
    kspec = pl.BlockSpec((T + padk, HEAD_DIM), lambda h, i: (0, h))
    bspec = pl.BlockSpec((None, QB, KW), lambda h, i: (h, 0, 0))
    return _call(
        body, [q, kp, vp, btab, o, do, lse], name=name, grid=(HA, T // QB),
        in_specs=[qspec, kspec, kspec, bspec, qspec, qspec, qspec],
        out_specs=[qspec, kspec, kspec, bspec],
        out_shape=[jax.ShapeDtypeStruct((T, dq_width or WA), BF16), jax.ShapeDtypeStruct((T + padk, WA), F32),
                   jax.ShapeDtypeStruct((T + padk, WA), F32), jax.ShapeDtypeStruct((HA, QB, KW), F32)],
        sem=("parallel", "arbitrary"), comm=comm)


def _relbias_fold(skew, dtab, *, name):
    HA, qb, kw = dtab.shape
    W = kw + 1

    def body(s_ref, t_ref, col_ref, tail_ref):
        row = lax.broadcasted_iota(jnp.int32, (qb, W), 0)
        col = lax.broadcasted_iota(jnp.int32, (qb, W), 1)
        col_ref[...] = jnp.sum(jnp.where(row + col < kw, s_ref[...], 0.0), axis=0, keepdims=True)
        i = lax.broadcasted_iota(jnp.int32, (qb, kw), 0)
        j = lax.broadcasted_iota(jnp.int32, (qb, kw), 1)
        tail = jnp.sum(jnp.where((kw - qb) + i - j >= MAX_REL, t_ref[...], 0.0))
        tail_ref[...] = jnp.full((1, LANE), tail, F32)

    return pl.pallas_call(
        body, name=name, grid=(HA,),
        in_specs=[pl.BlockSpec((None, qb, W), lambda h: (h, 0, 0)), pl.BlockSpec((None, qb, kw), lambda h: (h, 0, 0))],
        out_specs=[pl.BlockSpec((None, 1, W), lambda h: (h, 0, 0)), pl.BlockSpec((None, 1, LANE), lambda h: (h, 0, 0))],
        out_shape=[jax.ShapeDtypeStruct((HA, 1, W), F32), jax.ShapeDtypeStruct((HA, 1, LANE), F32)],
        compiler_params=_cp("parallel"))(skew, dtab)


CUM_BLOCK = 128


def _fgate_fwd(f, bf, *, name):
    T = f.shape[0]
    nb = T // CUM_BLOCK

    def body(f_ref, b_ref, cum_ref):
        r = lax.broadcasted_iota(jnp.int32, (CUM_BLOCK, CUM_BLOCK), 0)
        c = lax.broadcasted_iota(jnp.int32, (CUM_BLOCK, CUM_BLOCK), 1)
        tri = (c <= r).astype(F32)

        def step(b, carry):
            r0 = pl.multiple_of(b * CUM_BLOCK, CUM_BLOCK)
            z = f_ref[pl.ds(r0, CUM_BLOCK), :] + b_ref[...]
            lf = jnp.minimum(z, 0.0) - jnp.log(1.0 + jnp.exp(-jnp.abs(z)))
            cs = jnp.dot(tri, lf, precision=lax.Precision.HIGHEST, preferred_element_type=F32) + carry
            cum_ref[pl.ds(r0, CUM_BLOCK), :] = cs
            return carry + jnp.sum(lf, axis=0, keepdims=True)

        lax.fori_loop(0, nb, step, jnp.zeros((1, LANE), F32))

    return pl.pallas_call(
        body, name=name, out_shape=jax.ShapeDtypeStruct((T, LANE), F32),
        compiler_params=_cp())(f, bf)


def _fgate_bwd(dcq, dck, f, bf, *, name):
    T = f.shape[0]
    nb = T // CUM_BLOCK

    def body(dq_ref, dk_ref, f_ref, b_ref, df_ref, dbf_ref):
        r = lax.broadcasted_iota(jnp.int32, (CUM_BLOCK, CUM_BLOCK), 0)
        c = lax.broadcasted_iota(jnp.int32, (CUM_BLOCK, CUM_BLOCK), 1)
        tri = (c >= r).astype(F32)

        def step(n, carry):
            tail, dbf = carry
            r0 = pl.multiple_of((nb - 1 - n) * CUM_BLOCK, CUM_BLOCK)
            dc = dq_ref[pl.ds(r0, CUM_BLOCK), :] + dk_ref[pl.ds(r0, CUM_BLOCK), :]
            ss = jnp.dot(tri, dc, precision=lax.Precision.HIGHEST, preferred_element_type=F32) + tail
            z = f_ref[pl.ds(r0, CUM_BLOCK), :] + b_ref[...]
            df = ss * _sigmoid(-z)
            df_ref[pl.ds(r0, CUM_BLOCK), :] = df
            return tail + jnp.sum(dc, axis=0, keepdims=True), dbf + jnp.sum(df, axis=0, keepdims=True)

        zero = jnp.zeros((1, LANE), F32)
        _, dbf = lax.fori_loop(0, nb, step, (zero, zero))
        dbf_ref[...] = dbf

    return pl.pallas_call(
        body, name=name,
        out_shape=[jax.ShapeDtypeStruct((T, LANE), F32), jax.ShapeDtypeStruct((1, LANE), F32)],
        compiler_params=_cp())(dcq, dck, f, bf)


def _fox_logits(qv, kk, cq, ck, diagonal):
    s = lax.dot_general(qv, kk, NT_DIMS, preferred_element_type=F32) * ATTN_SCALE + cq - ck
    if diagonal:
        row = lax.broadcasted_iota(jnp.int32, s.shape, 0)
        col = lax.broadcasted_iota(jnp.int32, s.shape, 1)
        s = jnp.where(col <= row, s, NEG_INF)
    return s


def _fox_fwd(qkv, cols, cum, cum_t, *, name, comm=None):
    T = qkv.shape[0]
    HB, _, KB = cum_t.shape
    WB = HB * HEAD_DIM
    QB = KB
    qc, kc, vc = cols

    def body(q_ref, k_ref, v_ref, c_ref, ct_ref, o_ref, lse_ref):
        h = pl.program_id(0)
        i = pl.program_id(1)
        qv = q_ref[...]
        lane = lax.broadcasted_iota(jnp.int32, (QB, LANE), 1)
        cq = jnp.sum(jnp.where(lane == h, c_ref[...], 0.0), axis=-1, keepdims=True)

        def block(kb, carry, diagonal):
            m, l, acc = carry
            k0 = pl.multiple_of(kb * KB, KB)
            s = _fox_logits(qv, k_ref[pl.ds(k0, KB), :], cq, ct_ref[pl.ds(kb, 1), :], diagonal)
            m_new = jnp.maximum(m, jnp.max(s, axis=-1, keepdims=True))
            alpha = jnp.exp(m - m_new)
            p = jnp.exp(s - m_new)
            l = alpha * l + jnp.sum(p, axis=-1, keepdims=True)
            acc = alpha * acc + jnp.dot(p.astype(BF16), v_ref[pl.ds(k0, KB), :], preferred_element_type=F32)
            return m_new, l, acc

        past = lax.fori_loop(
            0, i, lambda kb, carry: block(kb, carry, False),
            (jnp.full((QB, 1), NEG_INF, F32), jnp.zeros((QB, 1), F32), jnp.zeros((QB, HEAD_DIM), F32)))
        m, l, acc = block(i, past, True)
        o_ref[...] = (acc / l).astype(BF16)
        lse_ref[...] = jnp.broadcast_to(m + jnp.log(l), (QB, HEAD_DIM))

    ospec = pl.BlockSpec((QB, HEAD_DIM), lambda h, i: (i, h))
    return _call(
        body, [qkv, qkv, qkv, cum, cum_t], name=name, grid=(HB, T // QB),
        in_specs=[pl.BlockSpec((QB, HEAD_DIM), lambda h, i: (i, qc + h)),
                  pl.BlockSpec((T, HEAD_DIM), lambda h, i: (0, kc + h)),
                  pl.BlockSpec((T, HEAD_DIM), lambda h, i: (0, vc + h)),
                  pl.BlockSpec((QB, LANE), lambda h, i: (i, 0)),
                  pl.BlockSpec((None, T // KB, KB), lambda h, i: (h, 0, 0))],
        out_specs=[ospec, ospec],
        out_shape=[jax.ShapeDtypeStruct((T, WB), BF16), jax.ShapeDtypeStruct((T, WB), F32)],
        sem=("parallel", "arbitrary"), comm=comm)


def _fox_bwd(qkv, cols, cum, cum_t, o, do, lse, *, name, comm=None, dq_into=None):
    T = qkv.shape[0]
    HB, _, KB = cum_t.shape
    WB = HB * HEAD_DIM
    QB = KB
    qc, kc, vc = cols

    def body(*refs):
        q_ref, k_ref, v_ref, c_ref, ct_ref, o_ref, do_ref, lse_ref = refs[:8]
        dq_ref, dk_ref, dv_ref, dcq_ref, dck_ref = refs[-5:]
        h = pl.program_id(0)
        i = pl.program_id(1)

        @pl.when(i == 0)
        def _():
            dk_ref[...] = jnp.zeros_like(dk_ref)
            dv_ref[...] = jnp.zeros_like(dv_ref)
            dck_ref[...] = jnp.zeros_like(dck_ref)

        qv = q_ref[...]
        dov = do_ref[...]
        lse_v = lse_ref[:, :1]
        delta = jnp.sum(dov.astype(F32) * o_ref[...].astype(F32), axis=-1, keepdims=True)
        lane = lax.broadcasted_iota(jnp.int32, (QB, LANE), 1)
        cq = jnp.sum(jnp.where(lane == h, c_ref[...], 0.0), axis=-1, keepdims=True)

        def block(kb, carry, diagonal):
            dq, dcq = carry
            k0 = pl.multiple_of(kb * KB, KB)
            kk = k_ref[pl.ds(k0, KB), :]
            s = _fox_logits(qv, kk, cq, ct_ref[pl.ds(kb, 1), :], diagonal)
            p = jnp.exp(s - lse_v)
            dp = lax.dot_general(dov, v_ref[pl.ds(k0, KB), :], NT_DIMS, preferred_element_type=F32)
            ds = p * (dp - delta)
            dsb = ds.astype(BF16)
            dk_ref[pl.ds(k0, KB), :] += lax.dot_general(dsb, qv, TN_DIMS, preferred_element_type=F32) * ATTN_SCALE
            dv_ref[pl.ds(k0, KB), :] += lax.dot_general(p.astype(BF16), dov, TN_DIMS, preferred_element_type=F32)
            dck_ref[pl.ds(kb, 1), :] -= jnp.sum(ds, axis=0, keepdims=True)
            return (dq + jnp.dot(dsb, kk, preferred_element_type=F32),
                    dcq + jnp.sum(ds, axis=-1, keepdims=True))

        past = lax.fori_loop(0, i, lambda kb, carry: block(kb, carry, False),
                             (jnp.zeros((QB, HEAD_DIM), F32), jnp.zeros((QB, 1), F32)))
        dq, dcq = block(i, past, True)
        dq_ref[...] = (dq * ATTN_SCALE).astype(BF16)
        dcq_ref[...] = jnp.broadcast_to(dcq, (QB, HEAD_DIM))

    qspec = pl.BlockSpec((QB, HEAD_DIM), lambda h, i: (i, h))
    kspec = pl.BlockSpec((T, HEAD_DIM), lambda h, i: (0, h))
    cspec = pl.BlockSpec((QB, LANE), lambda h, i: (i, 0))
    tspec = pl.BlockSpec((None, T // KB, KB), lambda h, i: (h, 0, 0))
    wide = jax.ShapeDtypeStruct((T, WB), F32)
    args = [qkv, qkv, qkv, cum, cum_t, o, do, lse]
    in_specs = [pl.BlockSpec((QB, HEAD_DIM), lambda h, i: (i, qc + h)),
                pl.BlockSpec((T, HEAD_DIM), lambda h, i: (0, kc + h)),
                pl.BlockSpec((T, HEAD_DIM), lambda h, i: (0, vc + h)),
                cspec, tspec, qspec, qspec, qspec]
    dq_spec, dq_shape, aliases = qspec, jax.ShapeDtypeStruct((T, WB), BF16), None
    if dq_into is not None:
        buf, dq_col = dq_into
        args.append(buf)
        in_specs.append(pl.BlockSpec(memory_space=pl.ANY))
        dq_spec = pl.BlockSpec((QB, HEAD_DIM), lambda h, i: (i, dq_col + h))
        dq_shape, aliases = jax.ShapeDtypeStruct(buf.shape, buf.dtype), {8: 0}
    return _call(
        body, args, name=name, grid=(HB, T // QB), in_specs=in_specs,
        out_specs=[dq_spec, kspec, kspec, qspec, tspec],
        out_shape=[dq_shape, wide, wide, wide, jax.ShapeDtypeStruct((HB, T // KB, KB), F32)],
        sem=("parallel", "arbitrary"), comm=comm, aliases=aliases)


def _place():
    return lax.axis_index("x"), lax.axis_index("y"), lax.axis_index("c")


def _other_chips(x, y):
    return [(1 - x, y), (x, 1 - y), (1 - x, 1 - y)]


def _gather_comm(shards):
    n = len(shards)

    def copies(srcs, outs, scratch, want):
        send_sems, recv_sems, local_sems = scratch
        x, y, c = _place()
        me, sib = (x, y, c), (x, y, 1 - c)
        chip_a = (jnp.bitwise_xor(x, c), jnp.bitwise_xor(y, 1 - c))
        chip_b = (jnp.bitwise_xor(x, 1 - c), jnp.bitwise_xor(y, c))
        chip_d = (1 - x, 1 - y)

        def copy(t, k, block, to, src=None):
            rows = outs[t].at[4 * block[0] + 2 * block[1] + block[2]]
            return pltpu.make_async_remote_copy(
                src_ref=rows if src is None else src, dst_ref=rows,
                send_sem=send_sems.at[7 * t + k], recv_sem=recv_sems.at[7 * t + k],
                device_id=to, device_id_type=MESH)

        got = {}
        if "mine" in want:
            got["mine"] = [pltpu.make_async_copy(srcs[t], outs[t].at[4 * x + 2 * y + c], local_sems.at[t])
                           for t in range(n)]
        if "first" in want:
            got["first"] = [cp for t in range(n) for cp in
                            (copy(t, 0, me, sib, src=srcs[t]), copy(t, 1, me, (*chip_a, c), src=srcs[t]),
                             copy(t, 2, me, (*chip_b, c), src=srcs[t]))]
        if "from_a" in want:
            got["from_a"] = [copy(t, 1, (*chip_a, c), me) for t in range(n)]
        if "hop" in want:
            got["hop"] = [copy(t, 3, (*chip_a, c), (*chip_b, c)) for t in range(n)]
        if "arrive" in want:
            got["arrive"] = [cp for t in range(n) for cp in
                             (copy(t, 2, (*chip_b, c), me), copy(t, 3, (*chip_d, c), me))]
        if "passed" in want:
            got["passed"] = [cp for t in range(n) for cp in
                             (copy(t, 4, (*chip_a, c), sib), copy(t, 5, (*chip_b, c), sib),
                              copy(t, 6, (*chip_d, c), sib))]
        if "late" in want:
            got["late"] = [cp for t in range(n) for cp in
                           (copy(t, 0, sib, me), copy(t, 4, (*chip_b, 1 - c), me),
                            copy(t, 5, (*chip_a, 1 - c), me), copy(t, 6, (*chip_d, 1 - c), me))]
        return got

    def begin(srcs, outs, scratch):
        cps = copies(srcs, outs, scratch, ("mine", "first"))
        for cp in cps["mine"] + cps["first"]:
            cp.start()

    def hop(srcs, outs, scratch):
        cps = copies(srcs, outs, scratch, ("from_a", "hop"))
        for landed, fwd in zip(cps["from_a"], cps["hop"]):
            landed.wait_recv()
            fwd.start()

    def pass_on(srcs, outs, scratch):
        cps = copies(srcs, outs, scratch, ("arrive", "passed"))
        for cp in cps["arrive"]:
            cp.wait_recv()
        for cp in cps["passed"]:
            cp.start()

    def end(srcs, outs, scratch):
        cps = copies(srcs, outs, scratch, ("mine", "first", "hop", "passed", "late"))
        for cp in cps["late"]:
            cp.wait_recv()
        for cp in cps["first"] + cps["hop"] + cps["passed"]:
            cp.wait_send()
        for cp in cps["mine"]:
            cp.wait()

    return _Comm(
        shards, [jax.ShapeDtypeStruct((N_DEV,) + s.shape, s.dtype) for s in shards],
        [pltpu.SemaphoreType.DMA((7 * n,)), pltpu.SemaphoreType.DMA((7 * n,)), pltpu.SemaphoreType.DMA((n,))],
        begin, end, [(0.67, hop), (0.9, pass_on)])


def _sibling_comm(grads):
    n = len(grads)

    def copies(srcs, outs, scratch):
        send_sems, recv_sems = scratch
        x, y, c = _place()
        return [pltpu.make_async_remote_copy(
            src_ref=srcs[t].at[2 * p + (1 - c)], dst_ref=outs[t].at[p],
            send_sem=send_sems.at[4 * t + p], recv_sem=recv_sems.at[4 * t + p],
            device_id=(x, y, 1 - c), device_id_type=MESH) for t in range(n) for p in range(4)]

    def begin(srcs, outs, scratch):
        for cp in copies(srcs, outs, scratch):
            cp.start()

    def end(srcs, outs, scratch):
        cps = copies(srcs, outs, scratch)
        for cp in cps:
            cp.wait_recv()
        for cp in cps:
            cp.wait_send()

    return _Comm(
        grads, [jax.ShapeDtypeStruct((4,) + g.shape[1:], g.dtype) for g in grads],
        [pltpu.SemaphoreType.DMA((4 * n,)), pltpu.SemaphoreType.DMA((4 * n,))], begin, end)


def _chips_comm(sums, relations):
    n, nr = len(sums), len(relations)

    def copies(srcs, outs, scratch):
        send_sems, recv_sems = scratch
        x, y, c = _place()
        chips = _other_chips(x, y)
        return [pltpu.make_async_remote_copy(
            src_ref=srcs[t].at[2 * chips[j][0] + chips[j][1]], dst_ref=outs[t].at[s],
            send_sem=send_sems.at[nr * t + s], recv_sem=recv_sems.at[nr * t + s],
            device_id=(*chips[j], c), device_id_type=MESH) for t in range(n) for s, j in enumerate(relations)]

    def begin(srcs, outs, scratch):
        for cp in copies(srcs, outs, scratch):
            cp.start()

    def end(srcs, outs, scratch):
        cps = copies(srcs, outs, scratch)
        for cp in cps:
            cp.wait_recv()
        for cp in cps:
            cp.wait_send()

    return _Comm(
        sums, [jax.ShapeDtypeStruct((nr,) + s.shape[1:], s.dtype) for s in sums],
        [pltpu.SemaphoreType.DMA((nr * n,)), pltpu.SemaphoreType.DMA((nr * n,))], begin, end)


def _run_comm(comm, *, name):
    ci, co = len(comm.ins), len(comm.outs)

    def body(*refs):
        cin, cout, csc = refs[:ci], refs[ci:ci + co], refs[ci + co:]
        comm.begin(cin, cout, csc)
        for _, relay in comm.relays:
            relay(cin, cout, csc)
        comm.end(cin, cout, csc)

    return pl.pallas_call(
        body, name=name, in_specs=_any_specs(ci), out_specs=_any_specs(co), out_shape=comm.outs,
        scratch_shapes=comm.scratch)(*comm.ins)


def _chip_sum(g, r1, core, *, name, tr=256):
    _, R, C = g.shape
    tr = _tile(R, tr, 16)

    def body(c_ref, g_ref, r_ref, o_ref):
        o_ref[...] = (g_ref[...].astype(F32) + r_ref[...].astype(F32)).astype(BF16)

    return pl.pallas_call(
        body, name=name,
        grid_spec=pltpu.PrefetchScalarGridSpec(
            num_scalar_prefetch=1, grid=(4, R // tr),
            in_specs=[pl.BlockSpec((None, tr, C), lambda p, i, c: (2 * p + c[0], i, 0)),
                      pl.BlockSpec((None, tr, C), lambda p, i, c: (p, i, 0))],
            out_specs=pl.BlockSpec((None, tr, C), lambda p, i, c: (p, i, 0))),
        out_shape=jax.ShapeDtypeStruct((4, R, C), BF16),
        compiler_params=_cp("parallel", "parallel"))(core, g, r1)


def _adamw_math(w, g, m, v):
    m = ADAM_B1 * m + (1.0 - ADAM_B1) * g
    v = ADAM_B2 * v + (1.0 - ADAM_B2) * (g * g)
    m_hat = m / (1.0 - ADAM_B1 ** ADAM_STEP)
    v_hat = v / (1.0 - ADAM_B2 ** ADAM_STEP)
    delta = -ADAM_LR * (m_hat / (jnp.sqrt(v_hat) + ADAM_EPS) + ADAM_WD * w)
    return delta, m, v


def _adamw_shard(s1, others, chip, w, m, v, *, name, tr=64):
    R0, C0 = w.shape
    C = s1.shape[2]
    tr = _tile(R0, tr, 8)
    (a0, k0), (a1, k1), (a2, k2) = others

    def body(p_ref, s_ref, r0_ref, r1_ref, r2_ref, w_ref, m_ref, v_ref, g_out, d_out, m_out, v_out):
        g = s_ref[:, :C0].astype(F32)
        for r_ref in (r0_ref, r1_ref, r2_ref):
            g = g + r_ref[:, :C0].astype(F32)
        delta, mn, vn = _adamw_math(w_ref[...], g, m_ref[...], v_ref[...])
        g_out[...] = g
        d_out[...] = delta
        m_out[...] = mn
        v_out[...] = vn

    full = pl.BlockSpec((tr, C0), lambda i, p: (i, 0))
    out = jax.ShapeDtypeStruct((R0, C0), F32)
    return pl.pallas_call(
        body, name=name,
        grid_spec=pltpu.PrefetchScalarGridSpec(
            num_scalar_prefetch=1, grid=(R0 // tr,),
            in_specs=[pl.BlockSpec((None, tr, C), lambda i, p: (p[0], i, 0)),
                      pl.BlockSpec((None, tr, C), lambda i, p: (k0, i, 0)),
                      pl.BlockSpec((None, tr, C), lambda i, p: (k1, i, 0)),
                      pl.BlockSpec((None, tr, C), lambda i, p: (k2, i, 0)),
                      full, full, full],
            out_specs=[full, full, full, full]),
        out_shape=[out, out, out, out],
        compiler_params=_cp("parallel"))(chip, s1, a0, a1, a2, w, m, v)


def _all_reduce_small(part, *, name):
    SR = part.shape[0]

    def body(p_ref, o_ref, buf, send_sems, recv_sems):
        x, y, c = _place()
        me = 4 * x + 2 * y + c
        buf[0] = p_ref[...]
        copies = []
        for k in range(1, N_DEV):
            peer = (1 - x if k & 4 else x, 1 - y if k & 2 else y, 1 - c if k & 1 else c)
            copies.append(pltpu.make_async_remote_copy(
                src_ref=p_ref, dst_ref=buf.at[k], send_sem=send_sems.at[k - 1], recv_sem=recv_sems.at[k - 1],
                device_id=peer, device_id_type=MESH))
        for cp in copies:
            cp.start()
        for cp in copies:
            cp.wait_recv()
        for cp in copies:
            cp.wait_send()
        acc = buf[me]
        for d in range(1, N_DEV):
            acc = acc + buf[jnp.bitwise_xor(me, d)]
        o_ref[...] = acc

    return pl.pallas_call(
        body, name=name,
        in_specs=[pl.BlockSpec(memory_space=pltpu.VMEM)], out_specs=pl.BlockSpec(memory_space=pltpu.VMEM),
        out_shape=jax.ShapeDtypeStruct(part.shape, F32),
        scratch_shapes=[pltpu.VMEM((N_DEV, SR, LANE), F32), pltpu.SemaphoreType.DMA((N_DEV - 1,)),
                        pltpu.SemaphoreType.DMA((N_DEV - 1,))])(part)


def _adamw_small(w, g, m, v, *, name):
    def body(w_ref, g_ref, m_ref, v_ref, d_out, m_out, v_out):
        delta, mn, vn = _adamw_math(w_ref[...], g_ref[...], m_ref[...], v_ref[...])
        d_out[...] = delta
        m_out[...] = mn
        v_out[...] = vn

    out = jax.ShapeDtypeStruct(w.shape, F32)
    return pl.pallas_call(body, name=name, out_shape=[out, out, out], compiler_params=_cp())(w, g, m, v)


def _in_layout(din8, fq):
    lay = []
    for b in range(N_DEV):
        g0 = din8 * b
        nq = min(max(fq - g0, 0), din8)
        oq = g0 % LANE if nq else 0
        nqt = -(-(oq + nq) // LANE) if nq else 0
        tq0 = (g0 - oq) // LANE
        pg, tg0, ngt = oq + nq, 0, 0
        if nq < din8:
            u0 = g0 + nq - fq
            pg = nqt * LANE + u0 % LANE
            tg0, ngt = u0 // LANE, -(-(u0 % LANE + din8 - nq) // LANE)
        lay.append(dict(nq=nq, oq=oq, tq0=tq0, nqt=nqt, pg=pg, tg0=tg0, ngt=ngt, width=pg + din8 - nq))
    return lay


def _place_block(shard, lay, pin, b):
    rows, din8 = shard.shape

    def place(s, l):
        nq = l["nq"]
        parts, at = [], 0
        for start, cols in ((l["oq"], s[:, :nq]), (l["pg"], s[:, nq:])):
            if cols.shape[1]:
                parts += [jnp.zeros((rows, start - at), s.dtype), cols]
                at = start + cols.shape[1]
        parts.append(jnp.zeros((rows, pin - at), s.dtype))
        return jnp.concatenate([p for p in parts if p.shape[1]], axis=1)

    return lax.switch(b, [functools.partial(place, l=l) for l in lay], shard)


def _unplace_block(g, lay, din8, b):
    def unplace(a, l):
        nq = l["nq"]
        parts = [a[:, l["oq"]:l["oq"] + nq], a[:, l["pg"]:l["pg"] + din8 - nq]]
        return jnp.concatenate([p for p in parts if p.shape[1]], axis=1)

    return lax.switch(b, [functools.partial(unplace, l=l) for l in lay], g)


def _tile_runs(sources):
    runs = []
    for src in sources:
        if len(src) == 1 and runs and not isinstance(runs[-1], list) \
                and runs[-1][0] == src[0][0] and runs[-1][1] + runs[-1][2] == src[0][1]:
            runs[-1] = (runs[-1][0], runs[-1][1], runs[-1][2] + 1)
        elif len(src) == 1:
            runs.append((src[0][0], src[0][1], 1))
        else:
            runs.append(list(src))
    return runs


def _gather_tiles(arrays, sources):
    rows, dtype = arrays[0].shape[0], arrays[0].dtype
    pieces = []
    for run in _tile_runs(sources):
        if isinstance(run, list):
            tile = jnp.zeros((rows, LANE), dtype)
            for a, t in run:
                tile = tile + arrays[a][:, t * LANE:(t + 1) * LANE]
            pieces.append(tile)
        else:
            a, t0, n = run
            pieces.append(arrays[a][:, t0 * LANE:(t0 + n) * LANE])
    return jnp.concatenate(pieces, axis=1)


def _assemble_w_in(blocks, lay, n_qkv_tiles, n_f_tiles, n_g_tiles):
    q_src = [[] for _ in range(n_qkv_tiles + n_f_tiles)]
    g_src = [[] for _ in range(n_g_tiles)]
    for b, l in enumerate(lay):
        for j in range(l["nqt"]):
            q_src[l["tq0"] + j].append((b, j))
        for j in range(l["ngt"]):
            g_src[l["tg0"] + j].append((b, l["pg"] // LANE + j))
    arrays = [blocks[b] for b in range(N_DEV)]
    return (_gather_tiles(arrays, q_src[:n_qkv_tiles]), _gather_tiles(arrays, q_src[n_qkv_tiles:]),
            _gather_tiles(arrays, g_src))


def _scatter_g_in(g_qkv, g_f, g_gates, lay, pin):
    n_qkv_tiles = g_qkv.shape[1] // LANE
    out = []
    for l in lay:
        src = [[] for _ in range(pin // LANE)]
        for j in range(l["nqt"]):
            t = l["tq0"] + j
            src[j] = [(0, t)] if t < n_qkv_tiles else [(1, t - n_qkv_tiles)]
        for j in range(l["ngt"]):
            src[l["pg"] // LANE + j] = [(2, l["tg0"] + j)]
        out.append(_gather_tiles([g_qkv, g_f, g_gates], src))
    return jnp.stack(out)


def _pack(parts, rows):
    flat = jnp.concatenate([p.reshape(-1).astype(F32) for p in parts])
    return jnp.pad(flat, (0, rows * LANE - flat.shape[0])).reshape(rows, LANE)


def _unpack(packed, shapes):
    flat = packed.reshape(-1)
    out, at = [], 0
    for s in shapes:
        n = 1
        for d in s:
            n *= d
        out.append(flat[at:at + n].reshape(s))
        at += n
    return out


def kernel(x, g_mix, w_in, b_f, b_gate, rel_bias, w_branch_a, w_branch_b, w_out, g_ffn, w_gate_ffn, w_up_ffn, w_down_ffn, g_final, loss_target, m_g_mix, m_w_in, m_b_f, m_b_gate, m_rel_bias, m_w_branch_a, m_w_branch_b, m_w_out, m_g_ffn, m_w_gate_ffn, m_w_up_ffn, m_w_down_ffn, m_g_final, v_g_mix, v_w_in, v_b_f, v_b_gate, v_rel_bias, v_w_branch_a, v_w_branch_b, v_w_out, v_g_ffn, v_w_gate_ffn, v_w_up_ffn, v_w_down_ffn, v_g_final):
    _, T, D = x.shape
    HA, n_rel = rel_bias.shape[1], rel_bias.shape[2]
    HB = b_f.shape[1]
    WA, WB = HA * HEAD_DIM, HB * HEAD_DIM
    din8, f8, d8 = w_in.shape[2], w_gate_ffn.shape[2], D // N_DEV
    DIN = N_DEV * din8
    PF = _pad_to(f8, LANE)
    fcol = 3 * WA + 3 * WB
    lay = _in_layout(din8, fcol + HB)
    PIN = _pad_to(max(l["width"] for l in lay), LANE)
    padk = LEFT_CHUNKS * CHUNK
    QA = min(256, T)
    KW = QA + padk
    KB = min(512, T)
    x2d, tgt = x[0], loss_target[0]
    cx, cy, cc = _place()
    blk = 4 * cx + 2 * cy + cc
    core = jnp.reshape(cc, (1,)).astype(jnp.int32)
    chip = jnp.reshape(2 * cx + cy, (1,)).astype(jnp.int32)
    every = [0, 1, 2]

    s_in = _place_block(w_in[0].astype(BF16), lay, PIN, blk)
    s_a, s_b, s_out = w_branch_a[0].astype(BF16), w_branch_b[0].astype(BF16), w_out[0].astype(BF16)
    s_g = jnp.pad(w_gate_ffn[0].astype(BF16), ((0, 0), (0, PF - f8)))
    s_u = jnp.pad(w_up_ffn[0].astype(BF16), ((0, 0), (0, PF - f8)))
    s_d = jnp.pad(w_down_ffn[0].astype(BF16), ((0, PF - f8), (0, 0)))
    Win, = _run_comm(_gather_comm([s_in]), name="gather_w_in")
    Wqkv, Wf, Wgt = _assemble_w_in(Win, lay, fcol // LANE, 1, 2 * D // LANE)
    Wqkv, Wf, Wgt = Wqkv.reshape(1, D, fcol), Wf.reshape(1, D, LANE), Wgt.reshape(1, D, 2 * D)
    bf_pad = jnp.pad(b_f, ((0, 0), (0, LANE - HB)))
    fox_cols = (3 * HA, 3 * HA + HB, 3 * HA + 2 * HB)

    h = _rms_fwd(x2d, g_mix, name="rms_mix")
    qkv, (Wg,) = _mm_nn(h, Wqkv, name="proj_qkv", out_dtype=BF16, comm=_gather_comm([s_g]))
    gates, (Wa, Wb, Wout) = _mm_nn(h, Wgt, name="proj_gates", out_dtype=BF16, comm=_gather_comm([s_a, s_b, s_out]))
    Wout = Wout.reshape(1, D, D)
    flog = _mm_nn(h, Wf, name="proj_forget", out_dtype=F32)

    kap = jnp.pad(qkv[:, WA:2 * WA], ((padk, 0), (0, 0)))
    vap = jnp.pad(qkv[:, 2 * WA:3 * WA], ((padk, 0), (0, 0)))
    btab = _bias_table(rel_bias[0], QA, KW)
    o_a, lse_a = _attn_a_fwd(qkv, kap, vap, btab, name="attn_a_fwd")

    cum = _fgate_fwd(flog, bf_pad, name="forget_cumsum")
    cum_t = cum[:, :HB].T.reshape(HB, T // KB, KB)
    (o_b, lse_b), (Wu,) = _fox_fwd(qkv, fox_cols, cum, cum_t, name="fox_fwd", comm=_gather_comm([s_u]))

    u_a = _mm_nn(o_a, Wa, name="branch_a", out_dtype=BF16)
    u_b = _mm_nn(o_b, Wb, name="branch_b", out_dtype=BF16)
    merged = _merge_fwd(gates, b_gate, u_a, u_b, name="merge_fwd")
    x1 = _mm_nn(merged, Wout, name="out_proj", out_dtype=F32, res=x2d)
    h2 = _rms_fwd(x1, g_ffn, name="rms_ffn")
    gt, (Wd,) = _mm_nn(h2, Wg, name="ffn_gate", out_dtype=BF16, tn=PF, comm=_gather_comm([s_d]))
    Wd = Wd.reshape(1, N_DEV * PF, D)
    up, hid = _mm_nn(h2, Wu, name="ffn_up", out_dtype=BF16, tn=PF, tm=512,
                     epi=_swiglu_fwd_epi, extras=[gt], n_out=2)
    x2 = _mm_nn(hid, Wd, name="ffn_down", out_dtype=F32, res=x1, tk=2 * PF)
    dx2, dx2b, dg_final, loss_part = _final_loss(x2, tgt, g_final.reshape(1, D), name="final_loss")

    dgt, dup = _mm_nt(dx2b, Wd, name="d_hidden", out_dtype=BF16, epi=_swiglu_bwd_epi, extras=[gt, up], n_out=2)
    gWd = _mm_tn(hid, dx2b, 1, name="gw_down").reshape(N_DEV, PF, D)
    dh2, (sWd,) = _mm_nt(dgt, Wg, name="d_h2_gate", out_dtype=F32, comm=_sibling_comm([gWd]))
    cWd = _chip_sum(gWd, sWd, core, name="chip_sum_w_down_ffn")
    dh2, (rWd,) = _mm_nt(dup, Wu, name="d_h2_up", out_dtype=F32, res=dh2, comm=_chips_comm([cWd], every))
    gWg = _mm_tn(h2, dgt, N_DEV, name="gw_gate", tn=PF)
    gWu = _mm_tn(h2, dup, N_DEV, name="gw_up", tn=PF)
    dx1, dx1b, dg_ffn = _rms_bwd(dh2, x1, g_ffn, dx2, name="rms_ffn_bwd")
    dmer, (sWg, sWu) = _mm_nt(dx1b, Wout, name="d_merged", out_dtype=BF16, comm=_sibling_comm([gWg, gWu]))
    cWg = _chip_sum(gWg, sWg, core, name="chip_sum_w_gate_ffn")
    cWu = _chip_sum(gWu, sWu, core, name="chip_sum_w_up_ffn")
    gWout = _mm_tn(merged, dx1b, 1, name="gw_out").reshape(N_DEV, d8, D)
    dua, dub, dgates, dbg = _merge_bwd(dmer, gates, b_gate, u_a, u_b, name="merge_bwd")
    g_gates = _mm_tn(h, dgates, 1, name="gw_gates")[0]
    doa = _mm_nt(dua, Wa, name="d_o_a", out_dtype=BF16)
    dob = _mm_nt(dub, Wb, name="d_o_b", out_dtype=BF16)
    gWa = _mm_tn(o_a, dua, N_DEV, name="gw_branch_a")
    gWb = _mm_tn(o_b, dub, N_DEV, name="gw_branch_b")

    (dqkv, dkap, dvap, dtab), (sWout, sWa, sWb) = _attn_a_bwd(
        qkv, kap, vap, btab, o_a, doa, lse_a, name="attn_a_bwd", comm=_sibling_comm([gWout, gWa, gWb]),
        dq_width=fcol)
    cWout = _chip_sum(gWout, sWout, core, name="chip_sum_w_out")
    cWa = _chip_sum(gWa, sWa, core, name="chip_sum_w_branch_a")
    cWb = _chip_sum(gWb, sWb, core, name="chip_sum_w_branch_b")
    skew = jnp.pad(dtab.reshape(HA, QA * KW), ((0, 0), (0, QA))).reshape(HA, QA, KW + 1)
    colsum, tail = _relbias_fold(skew, dtab, name="relbias_fold")
    rev = colsum[:, 0, ::-1]
    d_rel = jnp.concatenate([rev[:, :2 * MAX_REL], tail[:, 0, :1]], axis=1)

    (dqkv, dkb, dvb, dcq, dck), (rWg, rWout, rWa, rWb) = _fox_bwd(
        qkv, fox_cols, cum, cum_t, o_b, dob, lse_b, name="fox_bwd", comm=_chips_comm([cWg, cWout, cWa, cWb], every),
        dq_into=(dqkv, fox_cols[0]))
    dcq_p = jnp.pad(dcq.reshape(T, HB, HEAD_DIM)[:, :, 0], ((0, 0), (0, LANE - HB)))
    dck_p = jnp.pad(dck.reshape(HB, T).T, ((0, 0), (0, LANE - HB)))
    dflog, dbf = _fgate_bwd(dcq_p, dck_p, flog, bf_pad, name="forget_bwd")

    dqkv = _cast_into(dqkv, dkap, padk, WA, name="dqkv_k_a")
    dqkv = _cast_into(dqkv, dvap, padk, 2 * WA, name="dqkv_v_a")
    dqkv = _cast_into(dqkv, dkb, 0, 3 * WA + WB, name="dqkv_k_b")
    dqkv = _cast_into(dqkv, dvb, 0, 3 * WA + 2 * WB, name="dqkv_v_b")
    dfb = dflog.astype(BF16)
    g_qkv, (rWu,) = _mm_tn(h, dqkv, 1, name="gw_qkv", comm=_chips_comm([cWu], every))
    gWin = _scatter_g_in(g_qkv[0], _mm_tn(h, dfb, 1, name="gw_forget")[0], g_gates, lay, PIN)
    dh, (sWin,) = _mm_nt(dfb, Wf, name="d_h_forget", out_dtype=F32, comm=_sibling_comm([gWin]))
    cWin = _chip_sum(gWin, sWin, core, name="chip_sum_w_in")
    dh, (rWin01,) = _mm_nt(dqkv, Wqkv, name="d_h_qkv", out_dtype=F32, res=dh, comm=_chips_comm([cWin], [0, 1]))
    dh, (rWin2,) = _mm_nt(dgates, Wgt, name="d_h_gates", out_dtype=F32, res=dh, comm=_chips_comm([cWin], [2]))
    dx, _, dg_mix = _rms_bwd(dh, x2d, g_mix, dx1, name="rms_mix_bwd")

    def slots(r):
        return [(r, 0), (r, 1), (r, 2)]

    in_parts = [_unplace_block(p, lay, din8, blk)[None] for p in
                (lax.dynamic_index_in_dim(cWin, 2 * cx + cy, keepdims=False), rWin01[0], rWin01[1], rWin2[0])]
    first = jnp.zeros((1,), jnp.int32)
    big = {}
    for nm, s1, others, which, w, m, v in [
            ("w_down_ffn", cWd, slots(rWd), chip, w_down_ffn, m_w_down_ffn, v_w_down_ffn),
            ("w_gate_ffn", cWg, slots(rWg), chip, w_gate_ffn, m_w_gate_ffn, v_w_gate_ffn),
            ("w_out", cWout, slots(rWout), chip, w_out, m_w_out, v_w_out),
            ("w_branch_a", cWa, slots(rWa), chip, w_branch_a, m_w_branch_a, v_w_branch_a),
            ("w_branch_b", cWb, slots(rWb), chip, w_branch_b, m_w_branch_b, v_w_branch_b),
            ("w_up_ffn", cWu, slots(rWu), chip, w_up_ffn, m_w_up_ffn, v_w_up_ffn),
            ("w_in", in_parts[0], [(p, 0) for p in in_parts[1:]], first, w_in, m_w_in, v_w_in)]:
        outs = _adamw_shard(s1, others, which, w[0], m[0], v[0], name="adamw_" + nm)
        big[nm] = [o[None] for o in outs]

    small_names = ["g_mix", "b_f", "b_gate", "rel_bias", "g_ffn", "g_final"]
    small_w = [g_mix, b_f, b_gate, rel_bias, g_ffn, g_final]
    small_m = [m_g_mix, m_b_f, m_b_gate, m_rel_bias, m_g_ffn, m_g_final]
    small_v = [v_g_mix, v_b_f, v_b_gate, v_rel_bias, v_g_ffn, v_g_final]
    small_g = [dg_mix, dbf[:, :HB], dbg, d_rel[None], dg_ffn, dg_final[0]]
    shapes = [w.shape for w in small_w]
    n_small = sum(w.size for w in small_w) + 1
    rows = _pad_to(-(-n_small // LANE), 8)
    total = _all_reduce_small(_pack(small_g + [loss_part[:1, :1]], rows), name="all_reduce_small")
    d_s, m_s, v_s = _adamw_small(_pack(small_w, rows), total, _pack(small_m, rows), _pack(small_v, rows),
                                 name="adamw_small")
    g_small = dict(zip(small_names, _unpack(total, shapes)))
    d_small = dict(zip(small_names, _unpack(d_s, shapes)))
    m_small = dict(zip(small_names, _unpack(m_s, shapes)))
    v_small = dict(zip(small_names, _unpack(v_s, shapes)))
    loss = total.reshape(-1)[n_small - 1]

    order = ["g_mix", "w_in", "b_f", "b_gate", "rel_bias", "w_branch_a", "w_branch_b", "w_out", "g_ffn",
             "w_gate_ffn", "w_up_ffn", "w_down_ffn", "g_final"]

    def pick(idx, small):
        return [big[nm][idx] if nm in big else small[nm] for nm in order]

    return (loss, dx[None], *pick(0, g_small), *pick(1, d_small), *pick(2, m_small), *pick(3, v_small))
```

```python
import functools

import jax
import jax.numpy as jnp
from jax import lax
from jax.experimental import pallas as pl
from jax.experimental.pallas import tpu as pltpu

F32 = jnp.float32
BF16 = jnp.bfloat16
MESH = pl.DeviceIdType.MESH

N_DEV = 8
LANE = 128
HEAD_DIM = 128
CHUNK = 64
LEFT_CHUNKS = 8
MAX_REL = 256
RMS_EPS = 1e-6
NEG_INF = -1e30
ATTN_SCALE = HEAD_DIM ** -0.5
VMEM_LIMIT_BYTES = 56 * 1024 * 1024

ADAM_LR = 0.001
ADAM_B1 = 0.9
ADAM_B2 = 0.999
ADAM_EPS = 1e-08
ADAM_WD = 0.01
ADAM_STEP = 10

NT_DIMS = (((1,), (1,)), ((), ()))
TN_DIMS = (((0,), (0,)), ((), ()))


def _cp(*sem):
    return pltpu.CompilerParams(dimension_semantics=sem if sem else None, vmem_limit_bytes=VMEM_LIMIT_BYTES)


def _pad_to(n, mult):
    return -(-n // mult) * mult


def _tile(dim, pref, align):
    t = (min(pref, dim) // align) * align
    while t >= align:
        if dim % t == 0:
            return t
        t -= align
    return dim


def _sigmoid(z):
    return 1.0 / (1.0 + jnp.exp(-z))


class _Comm:
    def __init__(self, ins, outs, scratch, begin, end, relays=()):
        self.ins, self.outs, self.scratch = list(ins), list(outs), list(scratch)
        self.begin, self.end, self.relays = begin, end, list(relays)


def _call(body, args, *, name, grid, in_specs, out_specs, out_shape, scratch=(), sem=(), comm=None, aliases=None):
    single = not isinstance(out_shape, (list, tuple))
    out_shape = [out_shape] if single else list(out_shape)
    out_specs = [out_specs] if single else list(out_specs)
    aliases = dict(aliases or {})
    if comm is None:
        res = pl.pallas_call(
            body, name=name, grid=grid, in_specs=list(in_specs), out_specs=out_specs, out_shape=out_shape,
            scratch_shapes=list(scratch), input_output_aliases=aliases, compiler_params=_cp(*sem))(*args)
        return res[0] if single else res
    ni, no, ns = len(args), len(out_shape), len(scratch)
    ci, co = len(comm.ins), len(comm.outs)
    total = 1
    for g in grid:
        total *= g

    def hosted(*refs):
        a, cin = refs[:ni], refs[ni:ni + ci]
        o, cout = refs[ni + ci:ni + ci + no], refs[ni + ci + no:ni + ci + no + co]
        s, csc = refs[ni + ci + no + co:ni + ci + no + co + ns], refs[ni + ci + no + co + ns:]
        step = pl.program_id(0)
        for d in range(1, len(grid)):
            step = step * grid[d] + pl.program_id(d)

        @pl.when(step == 0)
        def _():
            comm.begin(cin, cout, csc)

        for fraction, relay in comm.relays:
            @pl.when(step == min(int(total * fraction), total - 1))
            def _(relay=relay):
                relay(cin, cout, csc)

        body(*a, *o, *s)

        @pl.when(step == total - 1)
        def _():
            comm.end(cin, cout, csc)

    res = pl.pallas_call(
        hosted, name=name, grid=grid, in_specs=list(in_specs) + _any_specs(ci),
        out_specs=out_specs + _any_specs(co), out_shape=out_shape + comm.outs,
        scratch_shapes=list(scratch) + comm.scratch, input_output_aliases=aliases,
        compiler_params=_cp(*(("arbitrary",) * len(grid))))(*args, *comm.ins)
    return (res[0] if single else res[:no]), res[no:]


def _any_specs(n):
    return [pl.BlockSpec(memory_space=pl.ANY)] * n


def _mm_body(dims, nk, has_res, n_ext, n_out, epi):
    def body(*refs):
        a_ref, w_ref = refs[:2]
        at = 2 + has_res
        r_ref = refs[2] if has_res else None
        e_refs, o_refs = refs[at:at + n_ext], refs[at + n_ext:at + n_ext + n_out]

        def emit(v):
            if r_ref is not None:
                v = v + r_ref[...]
            vals = (v,) if epi is None else epi(v, *[e[...] for e in e_refs])
            for o_ref, val in zip(o_refs, vals):
                o_ref[...] = val.astype(o_ref.dtype)

        if nk == 1:
            emit(lax.dot_general(a_ref[...], w_ref[...], dims, preferred_element_type=F32))
            return
        acc = refs[-1]
        k = pl.program_id(2)

        @pl.when(k == 0)
        def _():
            acc[...] = jnp.zeros_like(acc)

        acc[...] += lax.dot_general(a_ref[...], w_ref[...], dims, preferred_element_type=F32)

        @pl.when(k == nk - 1)
        def _():
            emit(acc[...])

    return body


def _mm_nn(x, w, *, name, out_dtype, res=None, tm=1024, tn=1024, tk=None, comm=None, epi=None, extras=(), n_out=1):
    M, K = x.shape
    NB, _, NW = w.shape
    tm, tn = _tile(M, tm, 16), _tile(NW, tn, LANE)
    tk = K if tk is None else _tile(K, tk, LANE)
    nj, nk = NW // tn, K // tk
    tile = pl.BlockSpec((tm, tn), lambda j, i, k: (i, j))
    in_specs = [pl.BlockSpec((tm, tk), lambda j, i, k: (i, k)),
                pl.BlockSpec((None, tk, tn), lambda j, i, k: (j // nj, k, j % nj))]
    args = [x, w] + ([res] if res is not None else []) + list(extras)
    in_specs += [tile] * (len(args) - 2)
    out = jax.ShapeDtypeStruct((M, NB * NW), out_dtype)
    return _call(
        _mm_body((((1,), (0,)), ((), ())), nk, res is not None, len(extras), n_out, epi), args,
        name=name, grid=(NB * nj, M // tm, nk), in_specs=in_specs,
        out_specs=tile if n_out == 1 else [tile] * n_out, out_shape=out if n_out == 1 else [out] * n_out,
        scratch=[pltpu.VMEM((tm, tn), F32)] if nk > 1 else [],
        sem=("parallel", "parallel", "arbitrary"), comm=comm)


def _mm_nt(dy, w, *, name, out_dtype, res=None, tm=1024, tn=1024, tc=2816, comm=None, epi=None, extras=(), n_out=1):
    M, N = dy.shape
    NB, Kw, NW = w.shape
    assert N == NB * NW
    tm, tn, tc = _tile(M, tm, 16), _tile(Kw, tn, LANE), _tile(NW, tc, LANE)
    nc = NW // tc
    nk = NB * nc
    tile = pl.BlockSpec((tm, tn), lambda i, j, k: (i, j))
    in_specs = [pl.BlockSpec((tm, tc), lambda i, j, k: (i, k)),
                pl.BlockSpec((None, tn, tc), lambda i, j, k: (k // nc, j, k % nc))]
    args = [dy, w] + ([res] if res is not None else []) + list(extras)
    in_specs += [tile] * (len(args) - 2)
    out = jax.ShapeDtypeStruct((M, Kw), out_dtype)
    return _call(
        _mm_body(NT_DIMS, nk, res is not None, len(extras), n_out, epi), args,
        name=name, grid=(M // tm, Kw // tn, nk), in_specs=in_specs,
        out_specs=tile if n_out == 1 else [tile] * n_out, out_shape=out if n_out == 1 else [out] * n_out,
        scratch=[pltpu.VMEM((tm, tn), F32)] if nk > 1 else [],
        sem=("parallel", "parallel", "arbitrary"), comm=comm)


def _mm_tn(x, dy, nb, *, name, tm=512, tn=1024, comm=None):
    M, Kw = x.shape
    NW = dy.shape[1] // nb
    tm, tn = _tile(Kw, tm, LANE), _tile(NW, tn, LANE)
    nj = NW // tn

    def body(x_ref, d_ref, o_ref):
        o_ref[...] = lax.dot_general(x_ref[...], d_ref[...], TN_DIMS, preferred_element_type=F32).astype(o_ref.dtype)

    return _call(
        body, [x, dy], name=name, grid=(nb * nj, Kw // tm),
        in_specs=[pl.BlockSpec((M, tm), lambda j, i: (0, i)),
                  pl.BlockSpec((M, tn), lambda j, i: (0, j))],
        out_specs=pl.BlockSpec((None, tm, tn), lambda j, i: (j // nj, i, j % nj)),
        out_shape=jax.ShapeDtypeStruct((nb, Kw, NW), BF16),
        sem=("parallel", "parallel"), comm=comm)


def _rms_fwd(x, g, *, name, tr=256):
    T, D = x.shape
    tr = _tile(T, tr, 16)

    def body(x_ref, g_ref, h_ref):
        xv = x_ref[...]
        r = lax.rsqrt(jnp.mean(xv * xv, axis=-1, keepdims=True) + RMS_EPS)
        h_ref[...] = ((xv * r) * g_ref[...]).astype(BF16)

    return pl.pallas_call(
        body, name=name, grid=(T // tr,),
        in_specs=[pl.BlockSpec((tr, D), lambda i: (i, 0)), pl.BlockSpec((1, D), lambda i: (0, 0))],
        out_specs=pl.BlockSpec((tr, D), lambda i: (i, 0)),
        out_shape=jax.ShapeDtypeStruct((T, D), BF16),
        compiler_params=_cp("parallel"))(x, g)


def _rms_bwd(dh, x, g, dres, *, name, tr=128):
    T, D = x.shape
    tr = _tile(T, tr, 16)

    def body(dh_ref, x_ref, g_ref, dr_ref, dx_ref, dxb_ref, dg_ref):
        i = pl.program_id(0)
        xv = x_ref[...]
        dhv = dh_ref[...]
        r = lax.rsqrt(jnp.mean(xv * xv, axis=-1, keepdims=True) + RMS_EPS)
        a = dhv * g_ref[...]
        dot = jnp.mean(a * xv, axis=-1, keepdims=True)
        dx = dr_ref[...] + r * a - xv * (r * r * r * dot)
        dx_ref[...] = dx
        dxb_ref[...] = dx.astype(BF16)
        part = jnp.sum(dhv * (xv * r), axis=0, keepdims=True)

        @pl.when(i == 0)
        def _():
            dg_ref[...] = part

        @pl.when(i > 0)
        def _():
            dg_ref[...] += part

    row = pl.BlockSpec((tr, D), lambda i: (i, 0))
    vec = pl.BlockSpec((1, D), lambda i: (0, 0))
    return pl.pallas_call(
        body, name=name, grid=(T // tr,),
        in_specs=[row, row, vec, row], out_specs=[row, row, vec],
        out_shape=[jax.ShapeDtypeStruct((T, D), F32), jax.ShapeDtypeStruct((T, D), BF16),
                   jax.ShapeDtypeStruct((1, D), F32)],
        compiler_params=_cp("arbitrary"))(dh, x, g, dres)


def _final_loss(x2, tgt, g, *, name, tr=128):
    T, D = x2.shape
    tr = _tile(T, tr, 16)

    def body(x_ref, t_ref, g_ref, dx_ref, dxb_ref, dg_ref, loss_ref):
        i = pl.program_id(0)
        xv = x_ref[...]
        gv = g_ref[...]
        r = lax.rsqrt(jnp.mean(xv * xv, axis=-1, keepdims=True) + RMS_EPS)
        xh = xv * r
        diff = xh * gv - t_ref[...]
        lpart = 0.5 * jnp.sum(jnp.mean(diff * diff, axis=-1, keepdims=True))
        dy = diff * (1.0 / D)
        a = dy * gv
        dot = jnp.mean(a * xv, axis=-1, keepdims=True)
        dx = r * a - xv * (r * r * r * dot)
        dx_ref[...] = dx
        dxb_ref[...] = dx.astype(BF16)
        part = jnp.sum(dy * xh, axis=0, keepdims=True)
        lblk = jnp.full((8, LANE), lpart, F32)

        @pl.when(i == 0)
        def _():
            dg_ref[...] = part
            loss_ref[...] = lblk

        @pl.when(i > 0)
        def _():
            dg_ref[...] += part
            loss_ref[...] += lblk

    row = pl.BlockSpec((tr, D), lambda i: (i, 0))
    vec = pl.BlockSpec((1, D), lambda i: (0, 0))
    return pl.pallas_call(
        body, name=name, grid=(T // tr,),
        in_specs=[row, row, vec],
        out_specs=[row, row, vec, pl.BlockSpec((8, LANE), lambda i: (0, 0))],
        out_shape=[jax.ShapeDtypeStruct((T, D), F32), jax.ShapeDtypeStruct((T, D), BF16),
                   jax.ShapeDtypeStruct((1, D), F32), jax.ShapeDtypeStruct((8, LANE), F32)],
        compiler_params=_cp("arbitrary"))(x2, tgt, g)


def _merge_fwd(gates, b_gate, ua, ub, *, name, tr=256):
    T, D = ua.shape
    tr = _tile(T, tr, 16)

    def body(ga_ref, gb_ref, ba_ref, bb_ref, ua_ref, ub_ref, o_ref):
        sa = _sigmoid(ga_ref[...].astype(F32) + ba_ref[...])
        sb = _sigmoid(gb_ref[...].astype(F32) + bb_ref[...])
        o_ref[...] = (sa * ua_ref[...].astype(F32) + sb * ub_ref[...].astype(F32)).astype(BF16)

    row = pl.BlockSpec((tr, D), lambda i: (i, 0))
    return pl.pallas_call(
        body, name=name, grid=(T // tr,),
        in_specs=[row, pl.BlockSpec((tr, D), lambda i: (i, 1)),
                  pl.BlockSpec((1, D), lambda i: (0, 0)), pl.BlockSpec((1, D), lambda i: (0, 1)), row, row],
        out_specs=row, out_shape=jax.ShapeDtypeStruct((T, D), BF16),
        compiler_params=_cp("parallel"))(gates, gates, b_gate, b_gate, ua, ub)


def _merge_bwd(dm, gates, b_gate, ua, ub, *, name, tr=256):
    T, D = ua.shape
    tr = _tile(T, tr, 16)

    def body(dm_ref, ga_ref, gb_ref, ba_ref, bb_ref, ua_ref, ub_ref, dua_ref, dub_ref, dg_ref, db_ref):
        i = pl.program_id(0)
        dmv = dm_ref[...].astype(F32)
        sa = _sigmoid(ga_ref[...].astype(F32) + ba_ref[...])
        sb = _sigmoid(gb_ref[...].astype(F32) + bb_ref[...])
        dua_ref[...] = (dmv * sa).astype(BF16)
        dub_ref[...] = (dmv * sb).astype(BF16)
        dga = dmv * ua_ref[...].astype(F32) * (sa * (1.0 - sa))
        dgb = dmv * ub_ref[...].astype(F32) * (sb * (1.0 - sb))
        dg_ref[:, :D] = dga.astype(BF16)
        dg_ref[:, D:] = dgb.astype(BF16)
        pa = jnp.sum(dga, axis=0, keepdims=True)
        pb = jnp.sum(dgb, axis=0, keepdims=True)

        @pl.when(i == 0)
        def _():
            db_ref[:, :D] = pa
            db_ref[:, D:] = pb

        @pl.when(i > 0)
        def _():
            db_ref[:, :D] += pa
            db_ref[:, D:] += pb

    row = pl.BlockSpec((tr, D), lambda i: (i, 0))
    vec = pl.BlockSpec((1, D), lambda i: (0, 0))
    act = jax.ShapeDtypeStruct((T, D), BF16)
    return pl.pallas_call(
        body, name=name, grid=(T // tr,),
        in_specs=[row, row, pl.BlockSpec((tr, D), lambda i: (i, 1)),
                  vec, pl.BlockSpec((1, D), lambda i: (0, 1)), row, row],
        out_specs=[row, row, pl.BlockSpec((tr, 2 * D), lambda i: (i, 0)), pl.BlockSpec((1, 2 * D), lambda i: (0, 0))],
        out_shape=[act, act, jax.ShapeDtypeStruct((T, 2 * D), BF16), jax.ShapeDtypeStruct((1, 2 * D), F32)],
        compiler_params=_cp("arbitrary"))(dm, gates, gates, b_gate, b_gate, ua, ub)


def _swiglu_fwd_epi(up, gate):
    g = gate.astype(F32)
    up = up.astype(BF16).astype(F32)
    return up, g * _sigmoid(g) * up


def _swiglu_bwd_epi(d, gate, up):
    g = gate.astype(F32)
    s = _sigmoid(g)
    return d * up.astype(F32) * (s * (1.0 + g * (1.0 - s))), d * (g * s)


def _cast_into(buf, src, row_off, col_off, *, name, tr=256):
    T = buf.shape[0]
    W = src.shape[1]
    tr = _tile(T, tr, 16)
    cw = W
    while col_off % cw or W % cw:
        cw -= LANE
    assert row_off % tr == 0
    c0, r0 = col_off // cw, row_off // tr

    def body(s_ref, b_ref, o_ref):
        o_ref[...] = s_ref[...].astype(o_ref.dtype)

    return pl.pallas_call(
        body, name=name, grid=(T // tr, W // cw),
        in_specs=[pl.BlockSpec((tr, cw), lambda i, j: (i + r0, j)), pl.BlockSpec(memory_space=pl.ANY)],
        out_specs=pl.BlockSpec((tr, cw), lambda i, j: (i, c0 + j)),
        out_shape=jax.ShapeDtypeStruct(buf.shape, buf.dtype), input_output_aliases={1: 0},
        compiler_params=_cp("parallel", "parallel"))(src, buf)


def _bias_table(rel, qb, kw):
    assert qb - 1 <= MAX_REL
    ha = rel.shape[0]
    n_clipped = kw - MAX_REL
    e = jnp.concatenate([jnp.broadcast_to(rel[:, 2 * MAX_REL:], (ha, n_clipped)),
                         rel[:, 2 * MAX_REL - 1:MAX_REL - qb:-1]], axis=1)
    period = qb + kw
    e = jnp.pad(e, ((0, 0), (0, period - e.shape[1])))
    z = jnp.broadcast_to(e[:, None, :], (ha, qb, period)).reshape(ha, qb * period)
    z = z[:, :qb * (period - 1)].reshape(ha, qb, period - 1)
    i = jnp.arange(qb)[:, None]
    j = jnp.arange(kw)[None, :]
    band = (j // CHUNK >= i // CHUNK) & (j // CHUNK <= i // CHUNK + LEFT_CHUNKS)
    return jnp.where(band[None], z[:, :, qb - 1:qb - 1 + kw], NEG_INF)


def _attn_a_fwd(q, kp, vp, btab, *, name):
    T, WA = q.shape[0], kp.shape[1]
    HA = WA // HEAD_DIM
    _, QB, KW = btab.shape
    padk = KW - QB

    def body(q_ref, k_ref, v_ref, b_ref, o_ref, lse_ref):
        i = pl.program_id(1)
        start = pl.multiple_of(i * QB, QB)
        k = k_ref[pl.ds(start, KW), :]
        v = v_ref[pl.ds(start, KW), :]
        s = lax.dot_general(q_ref[...], k, NT_DIMS, preferred_element_type=F32) * ATTN_SCALE + b_ref[...]
        kpos = start - padk + lax.broadcasted_iota(jnp.int32, (QB, KW), 1)
        s = jnp.where(kpos >= 0, s, NEG_INF)
        m = jnp.max(s, axis=-1, keepdims=True)
        p = jnp.exp(s - m)
        l = jnp.sum(p, axis=-1, keepdims=True)
        o = jnp.dot(p.astype(BF16), v, preferred_element_type=F32) / l
        o_ref[...] = o.astype(BF16)
        lse_ref[...] = jnp.broadcast_to(m + jnp.log(l), (QB, HEAD_DIM))

    qspec = pl.BlockSpec((QB, HEAD_DIM), lambda h, i: (i, h))
    kspec = pl.BlockSpec((T + padk, HEAD_DIM), lambda h, i: (0, h))
    return pl.pallas_call(
        body, name=name, grid=(HA, T // QB),
        in_specs=[qspec, kspec, kspec, pl.BlockSpec((None, QB, KW), lambda h, i: (h, 0, 0))],
        out_specs=[qspec, qspec],
        out_shape=[jax.ShapeDtypeStruct((T, WA), BF16), jax.ShapeDtypeStruct((T, WA), F32)],
        compiler_params=_cp("parallel", "arbitrary"))(q, kp, vp, btab)


def _attn_a_bwd(q, kp, vp, btab, o, do, lse, *, name, comm=None, dq_width=None):
    T, WA = q.shape[0], kp.shape[1]
    HA = WA // HEAD_DIM
    _, QB, KW = btab.shape
    padk = KW - QB

    def body(q_ref, k_ref, v_ref, b_ref, o_ref, do_ref, lse_ref, dq_ref, dk_ref, dv_ref, db_ref):
        i = pl.program_id(1)

        @pl.when(i == 0)
        def _():
            dk_ref[...] = jnp.zeros_like(dk_ref)
            dv_ref[...] = jnp.zeros_like(dv_ref)
            db_ref[...] = jnp.zeros_like(db_ref)

        start = pl.multiple_of(i * QB, QB)
        qv = q_ref[...]
        dov = do_ref[...]
        k = k_ref[pl.ds(start, KW), :]
        v = v_ref[pl.ds(start, KW), :]
        s = lax.dot_general(qv, k, NT_DIMS, preferred_element_type=F32) * ATTN_SCALE + b_ref[...]
        kpos = start - padk + lax.broadcasted_iota(jnp.int32, (QB, KW), 1)
        s = jnp.where(kpos >= 0, s, NEG_INF)
        p = jnp.exp(s - lse_ref[:, :1])
        dp = lax.dot_general(dov, v, NT_DIMS, preferred_element_type=F32)
        delta = jnp.sum(dov.astype(F32) * o_ref[...].astype(F32), axis=-1, keepdims=True)
        ds = p * (dp - delta)
        db_ref[...] += ds
        dsb = ds.astype(BF16)
        dq_ref[...] = (jnp.dot(dsb, k, preferred_element_type=F32) * ATTN_SCALE).astype(BF16)
        dk_ref[pl.ds(start, KW), :] += lax.dot_general(dsb, qv, TN_DIMS, preferred_element_type=F32) * ATTN_SCALE
        dv_ref[pl.ds(start, KW), :] += lax.dot_general(p.astype(BF16), dov, TN_DIMS, preferred_element_type=F32)

    qspec = pl.BlockSpec((QB, HEAD_DIM), lambda h, i: (i, h))
    kspec = pl.BlockSpec((T + padk, HEAD_DIM), lambda h, i: (0, h))
    bspec = pl.BlockSpec((None, QB, KW), lambda h, i: (h, 0, 0))
    return _call(
        body, [q, kp, vp, btab, o, do, lse], name=name, grid=(HA, T // QB),
        in_specs=[qspec, kspec, kspec, bspec, qspec, qspec, qspec],
        out_specs=[qspec, kspec, kspec, bspec],
        out_shape=[jax.ShapeDtypeStruct((T, dq_width or WA), BF16), jax.ShapeDtypeStruct((T + padk, WA), F32),
                   jax.ShapeDtypeStruct((T + padk, WA), F32), jax.ShapeDtypeStruct((HA, QB, KW), F32)],
        sem=("parallel", "arbitrary"), comm=comm)


def _relbias_fold(skew, dtab, *, name):
    HA, qb, kw = dtab.shape
    W = kw + 1

    def body(s_ref, t_ref, col_ref, tail_ref):
        row = lax.broadcasted_iota(jnp.int32, (qb, W), 0)
        col = lax.broadcasted_iota(jnp.int32, (qb, W), 1)
        col_ref[...] = jnp.sum(jnp.where(row + col < kw, s_ref[...], 0.0), axis=0, keepdims=True)
        i = lax.broadcasted_iota(jnp.int32, (qb, kw), 0)
        j = lax.broadcasted_iota(jnp.int32, (qb, kw), 1)
        tail = jnp.sum(jnp.where((kw - qb) + i - j >= MAX_REL, t_ref[...], 0.0))
        tail_ref[...] = jnp.full((1, LANE), tail, F32)

    return pl.pallas_call(
        body, name=name, grid=(HA,),
        in_specs=[pl.BlockSpec((None, qb, W), lambda h: (h, 0, 0)), pl.BlockSpec((None, qb, kw), lambda h: (h, 0, 0))],
        out_specs=[pl.BlockSpec((None, 1, W), lambda h: (h, 0, 0)), pl.BlockSpec((None, 1, LANE), lambda h: (h, 0, 0))],
        out_shape=[jax.ShapeDtypeStruct((HA, 1, W), F32), jax.ShapeDtypeStruct((HA, 1, LANE), F32)],
        compiler_params=_cp("parallel"))(skew, dtab)


CUM_BLOCK = 128


def _fgate_fwd(f, bf, *, name):
    T = f.shape[0]
    nb = T // CUM_BLOCK

    def body(f_ref, b_ref, cum_ref):
        r = lax.broadcasted_iota(jnp.int32, (CUM_BLOCK, CUM_BLOCK), 0)
        c = lax.broadcasted_iota(jnp.int32, (CUM_BLOCK, CUM_BLOCK), 1)
        tri = (c <= r).astype(F32)

        def step(b, carry):
            r0 = pl.multiple_of(b * CUM_BLOCK, CUM_BLOCK)
            z = f_ref[pl.ds(r0, CUM_BLOCK), :] + b_ref[...]
            lf = jnp.minimum(z, 0.0) - jnp.log(1.0 + jnp.exp(-jnp.abs(z)))
            cs = jnp.dot(tri, lf, precision=lax.Precision.HIGHEST, preferred_element_type=F32) + carry
            cum_ref[pl.ds(r0, CUM_BLOCK), :] = cs
            return carry + jnp.sum(lf, axis=0, keepdims=True)

        lax.fori_loop(0, nb, step, jnp.zeros((1, LANE), F32))

    return pl.pallas_call(
        body, name=name, out_shape=jax.ShapeDtypeStruct((T, LANE), F32),
        compiler_params=_cp())(f, bf)


def _fgate_bwd(dcq, dck, f, bf, *, name):
    T = f.shape[0]
    nb = T // CUM_BLOCK

    def body(dq_ref, dk_ref, f_ref, b_ref, df_ref, dbf_ref):
        r = lax.broadcasted_iota(jnp.int32, (CUM_BLOCK, CUM_BLOCK), 0)
        c = lax.broadcasted_iota(jnp.int32, (CUM_BLOCK, CUM_BLOCK), 1)
        tri = (c >= r).astype(F32)

        def step(n, carry):
            tail, dbf = carry
            r0 = pl.multiple_of((nb - 1 - n) * CUM_BLOCK, CUM_BLOCK)
            dc = dq_ref[pl.ds(r0, CUM_BLOCK), :] + dk_ref[pl.ds(r0, CUM_BLOCK), :]
            ss = jnp.dot(tri, dc, precision=lax.Precision.HIGHEST, preferred_element_type=F32) + tail
            z = f_ref[pl.ds(r0, CUM_BLOCK), :] + b_ref[...]
            df = ss * _sigmoid(-z)
            df_ref[pl.ds(r0, CUM_BLOCK), :] = df
            return tail + jnp.sum(dc, axis=0, keepdims=True), dbf + jnp.sum(df, axis=0, keepdims=True)

        zero = jnp.zeros((1, LANE), F32)
        _, dbf = lax.fori_loop(0, nb, step, (zero, zero))
        dbf_ref[...] = dbf

    return pl.pallas_call(
        body, name=name,
        out_shape=[jax.ShapeDtypeStruct((T, LANE), F32), jax.ShapeDtypeStruct((1, LANE), F32)],
        compiler_params=_cp())(dcq, dck, f, bf)


def _fox_logits(qv, kk, cq, ck, diagonal):
    s = lax.dot_general(qv, kk, NT_DIMS, preferred_element_type=F32) * ATTN_SCALE + cq - ck
    if diagonal:
        row = lax.broadcasted_iota(jnp.int32, s.shape, 0)
        col = lax.broadcasted_iota(jnp.int32, s.shape, 1)
        s = jnp.where(col <= row, s, NEG_INF)
    return s


def _fox_fwd(qkv, cols, cum, cum_t, *, name, comm=None):
    T = qkv.shape[0]
    HB, _, KB = cum_t.shape
    WB = HB * HEAD_DIM
    QB = KB
    qc, kc, vc = cols

    def body(q_ref, k_ref, v_ref, c_ref, ct_ref, o_ref, lse_ref):
        h = pl.program_id(0)
        i = pl.program_id(1)
        qv = q_ref[...]
        lane = lax.broadcasted_iota(jnp.int32, (QB, LANE), 1)
        cq = jnp.sum(jnp.where(lane == h, c_ref[...], 0.0), axis=-1, keepdims=True)

        def block(kb, carry, diagonal):
            m, l, acc = carry
            k0 = pl.multiple_of(kb * KB, KB)
            s = _fox_logits(qv, k_ref[pl.ds(k0, KB), :], cq, ct_ref[pl.ds(kb, 1), :], diagonal)
            m_new = jnp.maximum(m, jnp.max(s, axis=-1, keepdims=True))
            alpha = jnp.exp(m - m_new)
            p = jnp.exp(s - m_new)
            l = alpha * l + jnp.sum(p, axis=-1, keepdims=True)
            acc = alpha * acc + jnp.dot(p.astype(BF16), v_ref[pl.ds(k0, KB), :], preferred_element_type=F32)
            return m_new, l, acc

        past = lax.fori_loop(
            0, i, lambda kb, carry: block(kb, carry, False),
            (jnp.full((QB, 1), NEG_INF, F32), jnp.zeros((QB, 1), F32), jnp.zeros((QB, HEAD_DIM), F32)))
        m, l, acc = block(i, past, True)
        o_ref[...] = (acc / l).astype(BF16)
        lse_ref[...] = jnp.broadcast_to(m + jnp.log(l), (QB, HEAD_DIM))

    ospec = pl.BlockSpec((QB, HEAD_DIM), lambda h, i: (i, h))
    return _call(
        body, [qkv, qkv, qkv, cum, cum_t], name=name, grid=(HB, T // QB),
        in_specs=[pl.BlockSpec((QB, HEAD_DIM), lambda h, i: (i, qc + h)),
                  pl.BlockSpec((T, HEAD_DIM), lambda h, i: (0, kc + h)),
                  pl.BlockSpec((T, HEAD_DIM), lambda h, i: (0, vc + h)),
                  pl.BlockSpec((QB, LANE), lambda h, i: (i, 0)),
                  pl.BlockSpec((None, T // KB, KB), lambda h, i: (h, 0, 0))],
        out_specs=[ospec, ospec],
        out_shape=[jax.ShapeDtypeStruct((T, WB), BF16), jax.ShapeDtypeStruct((T, WB), F32)],
        sem=("parallel", "arbitrary"), comm=comm)


def _fox_bwd(qkv, cols, cum, cum_t, o, do, lse, *, name, comm=None, dq_into=None):
    T = qkv.shape[0]
    HB, _, KB = cum_t.shape
    WB = HB * HEAD_DIM
    QB = KB
    qc, kc, vc = cols

    def body(*refs):
        q_ref, k_ref, v_ref, c_ref, ct_ref, o_ref, do_ref, lse_ref = refs[:8]
        dq_ref, dk_ref, dv_ref, dcq_ref, dck_ref = refs[-5:]
        h = pl.program_id(0)
        i = pl.program_id(1)

        @pl.when(i == 0)
        def _():
            dk_ref[...] = jnp.zeros_like(dk_ref)
            dv_ref[...] = jnp.zeros_like(dv_ref)
            dck_ref[...] = jnp.zeros_like(dck_ref)

        qv = q_ref[...]
        dov = do_ref[...]
        lse_v = lse_ref[:, :1]
        delta = jnp.sum(dov.astype(F32) * o_ref[...].astype(F32), axis=-1, keepdims=True)
        lane = lax.broadcasted_iota(jnp.int32, (QB, LANE), 1)
        cq = jnp.sum(jnp.where(lane == h, c_ref[...], 0.0), axis=-1, keepdims=True)

        def block(kb, carry, diagonal):
            dq, dcq = carry
            k0 = pl.multiple_of(kb * KB, KB)
            kk = k_ref[pl.ds(k0, KB), :]
            s = _fox_logits(qv, kk, cq, ct_ref[pl.ds(kb, 1), :], diagonal)
            p = jnp.exp(s - lse_v)
            dp = lax.dot_general(dov, v_ref[pl.ds(k0, KB), :], NT_DIMS, preferred_element_type=F32)
            ds = p * (dp - delta)
            dsb = ds.astype(BF16)
            dk_ref[pl.ds(k0, KB), :] += lax.dot_general(dsb, qv, TN_DIMS, preferred_element_type=F32) * ATTN_SCALE
            dv_ref[pl.ds(k0, KB), :] += lax.dot_general(p.astype(BF16), dov, TN_DIMS, preferred_element_type=F32)
            dck_ref[pl.ds(kb, 1), :] -= jnp.sum(ds, axis=0, keepdims=True)
            return (dq + jnp.dot(dsb, kk, preferred_element_type=F32),
                    dcq + jnp.sum(ds, axis=-1, keepdims=True))

        past = lax.fori_loop(0, i, lambda kb, carry: block(kb, carry, False),
                             (jnp.zeros((QB, HEAD_DIM), F32), jnp.zeros((QB, 1), F32)))
        dq, dcq = block(i, past, True)
        dq_ref[...] = (dq * ATTN_SCALE).astype(BF16)
        dcq_ref[...] = jnp.broadcast_to(dcq, (QB, HEAD_DIM))

    qspec = pl.BlockSpec((QB, HEAD_DIM), lambda h, i: (i, h))
    kspec = pl.BlockSpec((T, HEAD_DIM), lambda h, i: (0, h))
    cspec = pl.BlockSpec((QB, LANE), lambda h, i: (i, 0))
    tspec = pl.BlockSpec((None, T // KB, KB), lambda h, i: (h, 0, 0))
    wide = jax.ShapeDtypeStruct((T, WB), F32)
    args = [qkv, qkv, qkv, cum, cum_t, o, do, lse]
    in_specs = [pl.BlockSpec((QB, HEAD_DIM), lambda h, i: (i, qc + h)),
                pl.BlockSpec((T, HEAD_DIM), lambda h, i: (0, kc + h)),
                pl.BlockSpec((T, HEAD_DIM), lambda h, i: (0, vc + h)),
                cspec, tspec, qspec, qspec, qspec]
    dq_spec, dq_shape, aliases = qspec, jax.ShapeDtypeStruct((T, WB), BF16), None
    if dq_into is not None:
        buf, dq_col = dq_into
        args.append(buf)
        in_specs.append(pl.BlockSpec(memory_space=pl.ANY))
        dq_spec = pl.BlockSpec((QB, HEAD_DIM), lambda h, i: (i, dq_col + h))
        dq_shape, aliases = jax.ShapeDtypeStruct(buf.shape, buf.dtype), {8: 0}
    return _call(
        body, args, name=name, grid=(HB, T // QB), in_specs=in_specs,
        out_specs=[dq_spec, kspec, kspec, qspec, tspec],
        out_shape=[dq_shape, wide, wide, wide, jax.ShapeDtypeStruct((HB, T // KB, KB), F32)],
        sem=("parallel", "arbitrary"), comm=comm, aliases=aliases)


def _place():
    return lax.axis_index("x"), lax.axis_index("y"), lax.axis_index("c")


def _other_chips(x, y):
    return [(1 - x, y), (x, 1 - y), (1 - x, 1 - y)]


def _gather_comm(shards):
    n = len(shards)

    def copies(srcs, outs, scratch, want):
        send_sems, recv_sems, local_sems = scratch
        x, y, c = _place()
        me, sib = (x, y, c), (x, y, 1 - c)
        chip_a = (jnp.bitwise_xor(x, c), jnp.bitwise_xor(y, 1 - c))
        chip_b = (jnp.bitwise_xor(x, 1 - c), jnp.bitwise_xor(y, c))
        chip_d = (1 - x, 1 - y)

        def copy(t, k, block, to, src=None):
            rows = outs[t].at[4 * block[0] + 2 * block[1] + block[2]]
            return pltpu.make_async_remote_copy(
                src_ref=rows if src is None else src, dst_ref=rows,
                send_sem=send_sems.at[7 * t + k], recv_sem=recv_sems.at[7 * t + k],
                device_id=to, device_id_type=MESH)

        got = {}
        if "mine" in want:
            got["mine"] = [pltpu.make_async_copy(srcs[t], outs[t].at[4 * x + 2 * y + c], local_sems.at[t])
                           for t in range(n)]
        if "first" in want:
            got["first"] = [cp for t in range(n) for cp in
                            (copy(t, 0, me, sib, src=srcs[t]), copy(t, 1, me, (*chip_a, c), src=srcs[t]),
                             copy(t, 2, me, (*chip_b, c), src=srcs[t]))]
        if "from_a" in want:
            got["from_a"] = [copy(t, 1, (*chip_a, c), me) for t in range(n)]
        if "hop" in want:
            got["hop"] = [copy(t, 3, (*chip_a, c), (*chip_b, c)) for t in range(n)]
        if "from_b" in want:
            got["from_b"] = [copy(t, 2, (*chip_b, c), me) for t in range(n)]
        if "from_d" in want:
            got["from_d"] = [copy(t, 3, (*chip_d, c), me) for t in range(n)]
        for key, k, chip in (("pass_a", 4, chip_a), ("pass_b", 5, chip_b), ("pass_d", 6, chip_d)):
            if key in want:
                got[key] = [copy(t, k, (*chip, c), sib) for t in range(n)]
        if "late" in want:
            got["late"] = [cp for t in range(n) for cp in
                           (copy(t, 0, sib, me), copy(t, 4, (*chip_b, 1 - c), me),
                            copy(t, 5, (*chip_a, 1 - c), me), copy(t, 6, (*chip_d, 1 - c), me))]
        return got

    def begin(srcs, outs, scratch):
        cps = copies(srcs, outs, scratch, ("mine", "first"))
        for cp in cps["mine"] + cps["first"]:
            cp.start()

    def hop(srcs, outs, scratch):
        cps = copies(srcs, outs, scratch, ("from_a", "hop", "pass_a"))
        for landed, fwd, sibling in zip(cps["from_a"], cps["hop"], cps["pass_a"]):
            landed.wait_recv()
            fwd.start()
            sibling.start()

    def pass_on(srcs, outs, scratch):
        cps = copies(srcs, outs, scratch, ("from_b", "pass_b", "from_d", "pass_d"))
        for landed, sibling in list(zip(cps["from_b"], cps["pass_b"])) + list(zip(cps["from_d"], cps["pass_d"])):
            landed.wait_recv()
            sibling.start()

    def end(srcs, outs, scratch):
        cps = copies(srcs, outs, scratch, ("mine", "first", "hop", "pass_a", "pass_b", "pass_d", "late"))
        for cp in cps["late"]:
            cp.wait_recv()
        for cp in cps["first"] + cps["hop"] + cps["pass_a"] + cps["pass_b"] + cps["pass_d"]:
            cp.wait_send()
        for cp in cps["mine"]:
            cp.wait()

    return _Comm(
        shards, [jax.ShapeDtypeStruct((N_DEV,) + s.shape, s.dtype) for s in shards],
        [pltpu.SemaphoreType.DMA((7 * n,)), pltpu.SemaphoreType.DMA((7 * n,)), pltpu.SemaphoreType.DMA((n,))],
        begin, end, [(0.67, hop), (0.9, pass_on)])


def _sibling_comm(grads):
    n = len(grads)

    def copies(srcs, outs, scratch):
        send_sems, recv_sems = scratch
        x, y, c = _place()
        return [pltpu.make_async_remote_copy(
            src_ref=srcs[t].at[2 * p + (1 - c)], dst_ref=outs[t].at[p],
            send_sem=send_sems.at[4 * t + p], recv_sem=recv_sems.at[4 * t + p],
            device_id=(x, y, 1 - c), device_id_type=MESH) for t in range(n) for p in range(4)]

    def begin(srcs, outs, scratch):
        for cp in copies(srcs, outs, scratch):
            cp.start()

    def end(srcs, outs, scratch):
        cps = copies(srcs, outs, scratch)
        for cp in cps:
            cp.wait_recv()
        for cp in cps:
            cp.wait_send()

    return _Comm(
        grads, [jax.ShapeDtypeStruct((4,) + g.shape[1:], g.dtype) for g in grads],
        [pltpu.SemaphoreType.DMA((4 * n,)), pltpu.SemaphoreType.DMA((4 * n,))], begin, end)


def _chips_comm(sums, relations):
    n, nr = len(sums), len(relations)

    def copies(srcs, outs, scratch):
        send_sems, recv_sems = scratch
        x, y, c = _place()
        chips = _other_chips(x, y)
        return [pltpu.make_async_remote_copy(
            src_ref=srcs[t].at[2 * chips[j][0] + chips[j][1]], dst_ref=outs[t].at[s],
            send_sem=send_sems.at[nr * t + s], recv_sem=recv_sems.at[nr * t + s],
            device_id=(*chips[j], c), device_id_type=MESH) for t in range(n) for s, j in enumerate(relations)]

    def begin(srcs, outs, scratch):
        for cp in copies(srcs, outs, scratch):
            cp.start()

    def end(srcs, outs, scratch):
        cps = copies(srcs, outs, scratch)
        for cp in cps:
            cp.wait_recv()
        for cp in cps:
            cp.wait_send()

    return _Comm(
        sums, [jax.ShapeDtypeStruct((nr,) + s.shape[1:], s.dtype) for s in sums],
        [pltpu.SemaphoreType.DMA((nr * n,)), pltpu.SemaphoreType.DMA((nr * n,))], begin, end)


def _run_comm(comm, *, name):
    ci, co = len(comm.ins), len(comm.outs)

    def body(*refs):
        cin, cout, csc = refs[:ci], refs[ci:ci + co], refs[ci + co:]
        comm.begin(cin, cout, csc)
        for _, relay in comm.relays:
            relay(cin, cout, csc)
        comm.end(cin, cout, csc)

    return pl.pallas_call(
        body, name=name, in_specs=_any_specs(ci), out_specs=_any_specs(co), out_shape=comm.outs,
        scratch_shapes=comm.scratch)(*comm.ins)


def _chip_sum(g, r1, core, *, name, tr=256):
    _, R, C = g.shape
    tr = _tile(R, tr, 16)

    def body(c_ref, g_ref, r_ref, o_ref):
        o_ref[...] = (g_ref[...].astype(F32) + r_ref[...].astype(F32)).astype(BF16)

    return pl.pallas_call(
        body, name=name,
        grid_spec=pltpu.PrefetchScalarGridSpec(
            num_scalar_prefetch=1, grid=(4, R // tr),
            in_specs=[pl.BlockSpec((None, tr, C), lambda p, i, c: (2 * p + c[0], i, 0)),
                      pl.BlockSpec((None, tr, C), lambda p, i, c: (p, i, 0))],
            out_specs=pl.BlockSpec((None, tr, C), lambda p, i, c: (p, i, 0))),
        out_shape=jax.ShapeDtypeStruct((4, R, C), BF16),
        compiler_params=_cp("parallel", "parallel"))(core, g, r1)


def _adamw_math(w, g, m, v):
    m = ADAM_B1 * m + (1.0 - ADAM_B1) * g
    v = ADAM_B2 * v + (1.0 - ADAM_B2) * (g * g)
    m_hat = m / (1.0 - ADAM_B1 ** ADAM_STEP)
    v_hat = v / (1.0 - ADAM_B2 ** ADAM_STEP)
    delta = -ADAM_LR * (m_hat / (jnp.sqrt(v_hat) + ADAM_EPS) + ADAM_WD * w)
    return delta, m, v


def _adamw_shard(s1, others, chip, w, m, v, *, name, tr=64):
    _, R0, C0 = w.shape
    C = s1.shape[2]
    tr = _tile(R0, tr, 8)
    (a0, k0), (a1, k1), (a2, k2) = others

    def body(p_ref, s_ref, r0_ref, r1_ref, r2_ref, w_ref, m_ref, v_ref, g_out, d_out, m_out, v_out):
        g = s_ref[:, :C0].astype(F32)
        for r_ref in (r0_ref, r1_ref, r2_ref):
            g = g + r_ref[:, :C0].astype(F32)
        delta, mn, vn = _adamw_math(w_ref[...], g, m_ref[...], v_ref[...])
        g_out[...] = g
        d_out[...] = delta
        m_out[...] = mn
        v_out[...] = vn

    full = pl.BlockSpec((None, tr, C0), lambda i, p: (0, i, 0))
    out = jax.ShapeDtypeStruct((1, R0, C0), F32)
    return pl.pallas_call(
        body, name=name,
        grid_spec=pltpu.PrefetchScalarGridSpec(
            num_scalar_prefetch=1, grid=(R0 // tr,),
            in_specs=[pl.BlockSpec((None, tr, C), lambda i, p: (p[0], i, 0)),
                      pl.BlockSpec((None, tr, C), lambda i, p: (k0, i, 0)),
                      pl.BlockSpec((None, tr, C), lambda i, p: (k1, i, 0)),
                      pl.BlockSpec((None, tr, C), lambda i, p: (k2, i, 0)),
                      full, full, full],
            out_specs=[full, full, full, full]),
        out_shape=[out, out, out, out],
        compiler_params=_cp("parallel"))(chip, s1, a0, a1, a2, w, m, v)


def _all_reduce_small(part, *, name):
    SR = part.shape[0]

    def body(p_ref, o_ref, buf, send_sems, recv_sems):
        x, y, c = _place()
        me = 4 * x + 2 * y + c
        buf[0] = p_ref[...]
        copies = []
        for k in range(1, N_DEV):
            peer = (1 - x if k & 4 else x, 1 - y if k & 2 else y, 1 - c if k & 1 else c)
            copies.append(pltpu.make_async_remote_copy(
                src_ref=p_ref, dst_ref=buf.at[k], send_sem=send_sems.at[k - 1], recv_sem=recv_sems.at[k - 1],
                device_id=peer, device_id_type=MESH))
        for cp in copies:
            cp.start()
        for cp in copies:
            cp.wait_recv()
        for cp in copies:
            cp.wait_send()
        acc = buf[me]
        for d in range(1, N_DEV):
            acc = acc + buf[jnp.bitwise_xor(me, d)]
        o_ref[...] = acc

    return pl.pallas_call(
        body, name=name,
        in_specs=[pl.BlockSpec(memory_space=pltpu.VMEM)], out_specs=pl.BlockSpec(memory_space=pltpu.VMEM),
        out_shape=jax.ShapeDtypeStruct(part.shape, F32),
        scratch_shapes=[pltpu.VMEM((N_DEV, SR, LANE), F32), pltpu.SemaphoreType.DMA((N_DEV - 1,)),
                        pltpu.SemaphoreType.DMA((N_DEV - 1,))])(part)


def _adamw_small(w, g, m, v, *, name):
    def body(w_ref, g_ref, m_ref, v_ref, d_out, m_out, v_out):
        delta, mn, vn = _adamw_math(w_ref[...], g_ref[...], m_ref[...], v_ref[...])
        d_out[...] = delta
        m_out[...] = mn
        v_out[...] = vn

    out = jax.ShapeDtypeStruct(w.shape, F32)
    return pl.pallas_call(body, name=name, out_shape=[out, out, out], compiler_params=_cp())(w, g, m, v)


def _in_layout(din8, fq):
    lay = []
    for b in range(N_DEV):
        g0 = din8 * b
        nq = min(max(fq - g0, 0), din8)
        oq = g0 % LANE if nq else 0
        nqt = -(-(oq + nq) // LANE) if nq else 0
        tq0 = (g0 - oq) // LANE
        pg, tg0, ngt = oq + nq, 0, 0
        if nq < din8:
            u0 = g0 + nq - fq
            pg = nqt * LANE + u0 % LANE
            tg0, ngt = u0 // LANE, -(-(u0 % LANE + din8 - nq) // LANE)
        lay.append(dict(nq=nq, oq=oq, tq0=tq0, nqt=nqt, pg=pg, tg0=tg0, ngt=ngt, width=pg + din8 - nq))
    return lay


def _place_block(shard, lay, pin, b):
    rows, din8 = shard.shape

    def place(s, l):
        nq = l["nq"]
        parts, at = [], 0
        for start, cols in ((l["oq"], s[:, :nq]), (l["pg"], s[:, nq:])):
            if cols.shape[1]:
                parts += [jnp.zeros((rows, start - at), s.dtype), cols]
                at = start + cols.shape[1]
        parts.append(jnp.zeros((rows, pin - at), s.dtype))
        return jnp.concatenate([p for p in parts if p.shape[1]], axis=1)

    return lax.switch(b, [functools.partial(place, l=l) for l in lay], shard)


def _unplace_block(g, lay, din8, b):
    def unplace(a, l):
        nq = l["nq"]
        parts = [a[:, l["oq"]:l["oq"] + nq], a[:, l["pg"]:l["pg"] + din8 - nq]]
        return jnp.concatenate([p for p in parts if p.shape[1]], axis=1)

    return lax.switch(b, [functools.partial(unplace, l=l) for l in lay], g)


def _tile_runs(sources):
    runs = []
    for src in sources:
        if len(src) == 1 and runs and not isinstance(runs[-1], list) \
                and runs[-1][0] == src[0][0] and runs[-1][1] + runs[-1][2] == src[0][1]:
            runs[-1] = (runs[-1][0], runs[-1][1], runs[-1][2] + 1)
        elif len(src) == 1:
            runs.append((src[0][0], src[0][1], 1))
        else:
            runs.append(list(src))
    return runs


def _gather_tiles(arrays, sources):
    rows, dtype = arrays[0].shape[0], arrays[0].dtype
    pieces = []
    for run in _tile_runs(sources):
        if isinstance(run, list):
            tile = jnp.zeros((rows, LANE), dtype)
            for a, t in run:
                tile = tile + arrays[a][:, t * LANE:(t + 1) * LANE]
            pieces.append(tile)
        else:
            a, t0, n = run
            pieces.append(arrays[a][:, t0 * LANE:(t0 + n) * LANE])
    return jnp.concatenate(pieces, axis=1)


def _assemble_w_in(blocks, lay, n_qkv_tiles, n_f_tiles, n_g_tiles):
    q_src = [[] for _ in range(n_qkv_tiles + n_f_tiles)]
    g_src = [[] for _ in range(n_g_tiles)]
    for b, l in enumerate(lay):
        for j in range(l["nqt"]):
            q_src[l["tq0"] + j].append((b, j))
        for j in range(l["ngt"]):
            g_src[l["tg0"] + j].append((b, l["pg"] // LANE + j))
    arrays = [blocks[b] for b in range(N_DEV)]
    return (_gather_tiles(arrays, q_src[:n_qkv_tiles]), _gather_tiles(arrays, q_src[n_qkv_tiles:]),
            _gather_tiles(arrays, g_src))


def _scatter_g_in(g_qkv, g_f, g_gates, lay, pin):
    n_qkv_tiles = g_qkv.shape[1] // LANE
    out = []
    for l in lay:
        src = [[] for _ in range(pin // LANE)]
        for j in range(l["nqt"]):
            t = l["tq0"] + j
            src[j] = [(0, t)] if t < n_qkv_tiles else [(1, t - n_qkv_tiles)]
        for j in range(l["ngt"]):
            src[l["pg"] // LANE + j] = [(2, l["tg0"] + j)]
        out.append(_gather_tiles([g_qkv, g_f, g_gates], src))
    return jnp.stack(out)


def _pack(parts, rows):
    flat = jnp.concatenate([p.reshape(-1).astype(F32) for p in parts])
    return jnp.pad(flat, (0, rows * LANE - flat.shape[0])).reshape(rows, LANE)


def _unpack(packed, shapes):
    flat = packed.reshape(-1)
    out, at = [], 0
    for s in shapes:
        n = 1
        for d in s:
            n *= d
        out.append(flat[at:at + n].reshape(s))
        at += n
    return out


def kernel(x, g_mix, w_in, b_f, b_gate, rel_bias, w_branch_a, w_branch_b, w_out, g_ffn, w_gate_ffn, w_up_ffn, w_down_ffn, g_final, loss_target, m_g_mix, m_w_in, m_b_f, m_b_gate, m_rel_bias, m_w_branch_a, m_w_branch_b, m_w_out, m_g_ffn, m_w_gate_ffn, m_w_up_ffn, m_w_down_ffn, m_g_final, v_g_mix, v_w_in, v_b_f, v_b_gate, v_rel_bias, v_w_branch_a, v_w_branch_b, v_w_out, v_g_ffn, v_w_gate_ffn, v_w_up_ffn, v_w_down_ffn, v_g_final):
    _, T, D = x.shape
    HA, n_rel = rel_bias.shape[1], rel_bias.shape[2]
    HB = b_f.shape[1]
    WA, WB = HA * HEAD_DIM, HB * HEAD_DIM
    din8, f8, d8 = w_in.shape[2], w_gate_ffn.shape[2], D // N_DEV
    DIN = N_DEV * din8
    PF = _pad_to(f8, LANE)
    fcol = 3 * WA + 3 * WB
    lay = _in_layout(din8, fcol + HB)
    PIN = _pad_to(max(l["width"] for l in lay), LANE)
    padk = LEFT_CHUNKS * CHUNK
    QA = min(256, T)
    KW = QA + padk
    KB = min(512, T)
    x2d, tgt = x[0], loss_target[0]
    cx, cy, cc = _place()
    blk = 4 * cx + 2 * cy + cc
    core = jnp.reshape(cc, (1,)).astype(jnp.int32)
    chip = jnp.reshape(2 * cx + cy, (1,)).astype(jnp.int32)
    every = [0, 1, 2]

    s_in = _place_block(w_in[0].astype(BF16), lay, PIN, blk)
    s_a, s_b, s_out = w_branch_a[0].astype(BF16), w_branch_b[0].astype(BF16), w_out[0].astype(BF16)
    s_g = jnp.pad(w_gate_ffn[0].astype(BF16), ((0, 0), (0, PF - f8)))
    s_u = jnp.pad(w_up_ffn[0].astype(BF16), ((0, 0), (0, PF - f8)))
    s_d = jnp.pad(w_down_ffn[0].astype(BF16), ((0, PF - f8), (0, 0)))
    Win, = _run_comm(_gather_comm([s_in]), name="gather_w_in")
    Wqkv, Wf, Wgt = _assemble_w_in(Win, lay, fcol // LANE, 1, 2 * D // LANE)
    Wqkv, Wf, Wgt = Wqkv.reshape(1, D, fcol), Wf.reshape(1, D, LANE), Wgt.reshape(1, D, 2 * D)
    bf_pad = jnp.pad(b_f, ((0, 0), (0, LANE - HB)))
    fox_cols = (3 * HA, 3 * HA + HB, 3 * HA + 2 * HB)

    h = _rms_fwd(x2d, g_mix, name="rms_mix")
    qkv, (Wg,) = _mm_nn(h, Wqkv, name="proj_qkv", out_dtype=BF16, comm=_gather_comm([s_g]))
    gates, (Wa, Wb, Wout) = _mm_nn(h, Wgt, name="proj_gates", out_dtype=BF16, comm=_gather_comm([s_a, s_b, s_out]))
    Wout = Wout.reshape(1, D, D)
    flog = _mm_nn(h, Wf, name="proj_forget", out_dtype=F32)

    kap = jnp.pad(qkv[:, WA:2 * WA], ((padk, 0), (0, 0)))
    vap = jnp.pad(qkv[:, 2 * WA:3 * WA], ((padk, 0), (0, 0)))
    btab = _bias_table(rel_bias[0], QA, KW)
    o_a, lse_a = _attn_a_fwd(qkv, kap, vap, btab, name="attn_a_fwd")

    cum = _fgate_fwd(flog, bf_pad, name="forget_cumsum")
    cum_t = cum[:, :HB].T.reshape(HB, T // KB, KB)
    (o_b, lse_b), (Wu,) = _fox_fwd(qkv, fox_cols, cum, cum_t, name="fox_fwd", comm=_gather_comm([s_u]))

    u_a = _mm_nn(o_a, Wa, name="branch_a", out_dtype=BF16)
    u_b = _mm_nn(o_b, Wb, name="branch_b", out_dtype=BF16)
    merged = _merge_fwd(gates, b_gate, u_a, u_b, name="merge_fwd")
    x1 = _mm_nn(merged, Wout, name="out_proj", out_dtype=F32, res=x2d)
    h2 = _rms_fwd(x1, g_ffn, name="rms_ffn")
    gt, (Wd,) = _mm_nn(h2, Wg, name="ffn_gate", out_dtype=BF16, tn=PF, comm=_gather_comm([s_d]))
    Wd = Wd.reshape(1, N_DEV * PF, D)
    up, hid = _mm_nn(h2, Wu, name="ffn_up", out_dtype=BF16, tn=PF, tm=512,
                     epi=_swiglu_fwd_epi, extras=[gt], n_out=2)
    x2 = _mm_nn(hid, Wd, name="ffn_down", out_dtype=F32, res=x1, tk=2 * PF)
    dx2, dx2b, dg_final, loss_part = _final_loss(x2, tgt, g_final.reshape(1, D), name="final_loss")

    dgt, dup = _mm_nt(dx2b, Wd, name="d_hidden", out_dtype=BF16, epi=_swiglu_bwd_epi, extras=[gt, up], n_out=2)
    gWd = _mm_tn(hid, dx2b, 1, name="gw_down").reshape(N_DEV, PF, D)
    dh2, (sWd,) = _mm_nt(dgt, Wg, name="d_h2_gate", out_dtype=F32, comm=_sibling_comm([gWd]))
    cWd = _chip_sum(gWd, sWd, core, name="chip_sum_w_down_ffn")
    dh2, (rWd,) = _mm_nt(dup, Wu, name="d_h2_up", out_dtype=F32, res=dh2, comm=_chips_comm([cWd], every))
    gWg = _mm_tn(h2, dgt, N_DEV, name="gw_gate", tn=PF)
    gWu = _mm_tn(h2, dup, N_DEV, name="gw_up", tn=PF)
    dx1, dx1b, dg_ffn = _rms_bwd(dh2, x1, g_ffn, dx2, name="rms_ffn_bwd")
    dmer, (sWg, sWu) = _mm_nt(dx1b, Wout, name="d_merged", out_dtype=BF16, comm=_sibling_comm([gWg, gWu]))
    cWg = _chip_sum(gWg, sWg, core, name="chip_sum_w_gate_ffn")
    cWu = _chip_sum(gWu, sWu, core, name="chip_sum_w_up_ffn")
    gWout = _mm_tn(merged, dx1b, 1, name="gw_out").reshape(N_DEV, d8, D)
    dua, dub, dgates, dbg = _merge_bwd(dmer, gates, b_gate, u_a, u_b, name="merge_bwd")
    g_gates = _mm_tn(h, dgates, 1, name="gw_gates")[0]
    doa = _mm_nt(dua, Wa, name="d_o_a", out_dtype=BF16)
    dob = _mm_nt(dub, Wb, name="d_o_b", out_dtype=BF16)
    gWa = _mm_tn(o_a, dua, N_DEV, name="gw_branch_a")
    gWb = _mm_tn(o_b, dub, N_DEV, name="gw_branch_b")

    (dqkv, dkap, dvap, dtab), (sWout, sWa, sWb) = _attn_a_bwd(
        qkv, kap, vap, btab, o_a, doa, lse_a, name="attn_a_bwd", comm=_sibling_comm([gWout, gWa, gWb]),
        dq_width=fcol)
    cWout = _chip_sum(gWout, sWout, core, name="chip_sum_w_out")
    cWa = _chip_sum(gWa, sWa, core, name="chip_sum_w_branch_a")
    cWb = _chip_sum(gWb, sWb, core, name="chip_sum_w_branch_b")
    skew = jnp.pad(dtab.reshape(HA, QA * KW), ((0, 0), (0, QA))).reshape(HA, QA, KW + 1)
    colsum, tail = _relbias_fold(skew, dtab, name="relbias_fold")
    rev = colsum[:, 0, ::-1]
    d_rel = jnp.concatenate([rev[:, :2 * MAX_REL], tail[:, 0, :1]], axis=1)

    (dqkv, dkb, dvb, dcq, dck), (rWg, rWout, rWa, rWb) = _fox_bwd(
        qkv, fox_cols, cum, cum_t, o_b, dob, lse_b, name="fox_bwd", comm=_chips_comm([cWg, cWout, cWa, cWb], every),
        dq_into=(dqkv, fox_cols[0]))
    dcq_p = jnp.pad(dcq.reshape(T, HB, HEAD_DIM)[:, :, 0], ((0, 0), (0, LANE - HB)))
    dck_p = jnp.pad(dck.reshape(HB, T).T, ((0, 0), (0, LANE - HB)))
    dflog, dbf = _fgate_bwd(dcq_p, dck_p, flog, bf_pad, name="forget_bwd")

    dqkv = _cast_into(dqkv, dkap, padk, WA, name="dqkv_k_a")
    dqkv = _cast_into(dqkv, dvap, padk, 2 * WA, name="dqkv_v_a")
    dqkv = _cast_into(dqkv, dkb, 0, 3 * WA + WB, name="dqkv_k_b")
    dqkv = _cast_into(dqkv, dvb, 0, 3 * WA + 2 * WB, name="dqkv_v_b")
    dfb = dflog.astype(BF16)
    g_qkv, (rWu,) = _mm_tn(h, dqkv, 1, name="gw_qkv", comm=_chips_comm([cWu], every))
    gWin = _scatter_g_in(g_qkv[0], _mm_tn(h, dfb, 1, name="gw_forget")[0], g_gates, lay, PIN)
    dh, (sWin,) = _mm_nt(dfb, Wf, name="d_h_forget", out_dtype=F32, comm=_sibling_comm([gWin]))
    cWin = _chip_sum(gWin, sWin, core, name="chip_sum_w_in")
    dh, (rWin01,) = _mm_nt(dqkv, Wqkv, name="d_h_qkv", out_dtype=F32, res=dh, comm=_chips_comm([cWin], [0, 1]))
    dh, (rWin2,) = _mm_nt(dgates, Wgt, name="d_h_gates", out_dtype=F32, res=dh, comm=_chips_comm([cWin], [2]))
    dx, _, dg_mix = _rms_bwd(dh, x2d, g_mix, dx1, name="rms_mix_bwd")

    def slots(r):
        return [(r, 0), (r, 1), (r, 2)]

    in_parts = [_unplace_block(p, lay, din8, blk)[None] for p in
                (lax.dynamic_index_in_dim(cWin, 2 * cx + cy, keepdims=False), rWin01[0], rWin01[1], rWin2[0])]
    first = jnp.zeros((1,), jnp.int32)
    big = {}
    for nm, s1, others, which, w, m, v in [
            ("w_down_ffn", cWd, slots(rWd), chip, w_down_ffn, m_w_down_ffn, v_w_down_ffn),
            ("w_gate_ffn", cWg, slots(rWg), chip, w_gate_ffn, m_w_gate_ffn, v_w_gate_ffn),
            ("w_out", cWout, slots(rWout), chip, w_out, m_w_out, v_w_out),
            ("w_branch_a", cWa, slots(rWa), chip, w_branch_a, m_w_branch_a, v_w_branch_a),
            ("w_branch_b", cWb, slots(rWb), chip, w_branch_b, m_w_branch_b, v_w_branch_b),
            ("w_up_ffn", cWu, slots(rWu), chip, w_up_ffn, m_w_up_ffn, v_w_up_ffn),
            ("w_in", in_parts[0], [(p, 0) for p in in_parts[1:]], first, w_in, m_w_in, v_w_in)]:
        big[nm] = _adamw_shard(s1, others, which, w, m, v, name="adamw_" + nm)

    small_names = ["g_mix", "b_f", "b_gate", "rel_bias", "g_ffn", "g_final"]
    small_w = [g_mix, b_f, b_gate, rel_bias, g_ffn, g_final]
    small_m = [m_g_mix, m_b_f, m_b_gate, m_rel_bias, m_g_ffn, m_g_final]
    small_v = [v_g_mix, v_b_f, v_b_gate, v_rel_bias, v_g_ffn, v_g_final]
    small_g = [dg_mix, dbf[:, :HB], dbg, d_rel[None], dg_ffn, dg_final[0]]
    shapes = [w.shape for w in small_w]
    n_small = sum(w.size for w in small_w) + 1
    rows = _pad_to(-(-n_small // LANE), 8)
    total = _all_reduce_small(_pack(small_g + [loss_part[:1, :1]], rows), name="all_reduce_small")
    d_s, m_s, v_s = _adamw_small(_pack(small_w, rows), total, _pack(small_m, rows), _pack(small_v, rows),
                                 name="adamw_small")
    g_small = dict(zip(small_names, _unpack(total, shapes)))
    d_small = dict(zip(small_names, _unpack(d_s, shapes)))
    m_small = dict(zip(small_names, _unpack(m_s, shapes)))
    v_small = dict(zip(small_names, _unpack(v_s, shapes)))
    loss = total.reshape(-1)[n_small - 1]

    order = ["g_mix", "w_in", "b_f", "b_gate", "rel_bias", "w_branch_a", "w_branch_b", "w_out", "g_ffn",
             "w_gate_ffn", "w_up_ffn", "w_down_ffn", "g_final"]

    def pick(idx, small):
        return [big[nm][idx] if nm in big else small[nm] for nm in order]

    return (loss, dx[None], *pick(0, g_small), *pick(1, d_small), *pick(2, m_small), *pick(3, v_small))
```

```python
import functools

import jax
import jax.numpy as jnp
from jax import lax
from jax.experimental import pallas as pl
from jax.experimental.pallas import tpu as pltpu

F32 = jnp.float32
BF16 = jnp.bfloat16
MESH = pl.DeviceIdType.MESH

N_DEV = 8
LANE = 128
HEAD_DIM = 128
CHUNK = 64
LEFT_CHUNKS = 8
MAX_REL = 256
RMS_EPS = 1e-6
NEG_INF = -1e30
ATTN_SCALE = HEAD_DIM ** -0.5
VMEM_LIMIT_BYTES = 56 * 1024 * 1024

ADAM_LR = 0.001
ADAM_B1 = 0.9
ADAM_B2 = 0.999
ADAM_EPS = 1e-08
ADAM_WD = 0.01
ADAM_STEP = 10

NT_DIMS = (((1,), (1,)), ((), ()))
TN_DIMS = (((0,), (0,)), ((), ()))


def _cp(*sem):
    return pltpu.CompilerParams(dimension_semantics=sem if sem else None, vmem_limit_bytes=VMEM_LIMIT_BYTES)


def _pad_to(n, mult):
    return -(-n // mult) * mult


def _tile(dim, pref, align):
    t = (min(pref, dim) // align) * align
    while t >= align:
        if dim % t == 0:
            return t
        t -= align
    return dim


def _sigmoid(z):
    return 1.0 / (1.0 + jnp.exp(-z))


class _Comm:
    def __init__(self, ins, outs, scratch, begin, end, relays=()):
        self.ins, self.outs, self.scratch = list(ins), list(outs), list(scratch)
        self.begin, self.end, self.relays = begin, end, list(relays)


def _call(body, args, *, name, grid, in_specs, out_specs, out_shape, scratch=(), sem=(), comm=None, aliases=None):
    single = not isinstance(out_shape, (list, tuple))
    out_shape = [out_shape] if single else list(out_shape)
    out_specs = [out_specs] if single else list(out_specs)
    aliases = dict(aliases or {})
    if comm is None:
        res = pl.pallas_call(
            body, name=name, grid=grid, in_specs=list(in_specs), out_specs=out_specs, out_shape=out_shape,
            scratch_shapes=list(scratch), input_output_aliases=aliases, compiler_params=_cp(*sem))(*args)
        return res[0] if single else res
    ni, no, ns = len(args), len(out_shape), len(scratch)
    ci, co = len(comm.ins), len(comm.outs)
    total = 1
    for g in grid:
        total *= g

    def hosted(*refs):
        a, cin = refs[:ni], refs[ni:ni + ci]
        o, cout = refs[ni + ci:ni + ci + no], refs[ni + ci + no:ni + ci + no + co]
        s, csc = refs[ni + ci + no + co:ni + ci + no + co + ns], refs[ni + ci + no + co + ns:]
        step = pl.program_id(0)
        for d in range(1, len(grid)):
            step = step * grid[d] + pl.program_id(d)

        @pl.when(step == 0)
        def _():
            comm.begin(cin, cout, csc)

        for fraction, relay in comm.relays:
            @pl.when(step == min(int(total * fraction), total - 1))
            def _(relay=relay):
                relay(cin, cout, csc)

        body(*a, *o, *s)

        @pl.when(step == total - 1)
        def _():
            comm.end(cin, cout, csc)

    res = pl.pallas_call(
        hosted, name=name, grid=grid, in_specs=list(in_specs) + _any_specs(ci),
        out_specs=out_specs + _any_specs(co), out_shape=out_shape + comm.outs,
        scratch_shapes=list(scratch) + comm.scratch, input_output_aliases=aliases,
        compiler_params=_cp(*(("arbitrary",) * len(grid))))(*args, *comm.ins)
    return (res[0] if single else res[:no]), res[no:]


def _any_specs(n):
    return [pl.BlockSpec(memory_space=pl.ANY)] * n


def _mm_body(dims, nk, has_res, n_ext, n_out, epi):
    def body(*refs):
        a_ref, w_ref = refs[:2]
        at = 2 + has_res
        r_ref = refs[2] if has_res else None
        e_refs, o_refs = refs[at:at + n_ext], refs[at + n_ext:at + n_ext + n_out]

        def emit(v):
            if r_ref is not None:
                v = v + r_ref[...]
            vals = (v,) if epi is None else epi(v, *[e[...] for e in e_refs])
            for o_ref, val in zip(o_refs, vals):
                o_ref[...] = val.astype(o_ref.dtype)

        if nk == 1:
            emit(lax.dot_general(a_ref[...], w_ref[...], dims, preferred_element_type=F32))
            return
        acc = refs[-1]
        k = pl.program_id(2)

        @pl.when(k == 0)
        def _():
            acc[...] = jnp.zeros_like(acc)

        acc[...] += lax.dot_general(a_ref[...], w_ref[...], dims, preferred_element_type=F32)

        @pl.when(k == nk - 1)
        def _():
            emit(acc[...])

    return body


def _mm_nn(x, w, *, name, out_dtype, res=None, tm=1024, tn=1024, tk=None, comm=None, epi=None, extras=(), n_out=1):
    M, K = x.shape
    NB, _, NW = w.shape
    tm, tn = _tile(M, tm, 16), _tile(NW, tn, LANE)
    tk = K if tk is None else _tile(K, tk, LANE)
    nj, nk = NW // tn, K // tk
    tile = pl.BlockSpec((tm, tn), lambda j, i, k: (i, j))
    in_specs = [pl.BlockSpec((tm, tk), lambda j, i, k: (i, k)),
                pl.BlockSpec((None, tk, tn), lambda j, i, k: (j // nj, k, j % nj))]
    args = [x, w] + ([res] if res is not None else []) + list(extras)
    in_specs += [tile] * (len(args) - 2)
    out = jax.ShapeDtypeStruct((M, NB * NW), out_dtype)
    return _call(
        _mm_body((((1,), (0,)), ((), ())), nk, res is not None, len(extras), n_out, epi), args,
        name=name, grid=(NB * nj, M // tm, nk), in_specs=in_specs,
        out_specs=tile if n_out == 1 else [tile] * n_out, out_shape=out if n_out == 1 else [out] * n_out,
        scratch=[pltpu.VMEM((tm, tn), F32)] if nk > 1 else [],
        sem=("parallel", "parallel", "arbitrary"), comm=comm)


def _mm_nt(dy, w, *, name, out_dtype, res=None, tm=1024, tn=1024, tc=2816, comm=None, epi=None, extras=(), n_out=1):
    M, N = dy.shape
    NB, Kw, NW = w.shape
    assert N == NB * NW
    tm, tn, tc = _tile(M, tm, 16), _tile(Kw, tn, LANE), _tile(NW, tc, LANE)
    nc = NW // tc
    nk = NB * nc
    tile = pl.BlockSpec((tm, tn), lambda i, j, k: (i, j))
    in_specs = [pl.BlockSpec((tm, tc), lambda i, j, k: (i, k)),
                pl.BlockSpec((None, tn, tc), lambda i, j, k: (k // nc, j, k % nc))]
    args = [dy, w] + ([res] if res is not None else []) + list(extras)
    in_specs += [tile] * (len(args) - 2)
    out = jax.ShapeDtypeStruct((M, Kw), out_dtype)
    return _call(
        _mm_body(NT_DIMS, nk, res is not None, len(extras), n_out, epi), args,
        name=name, grid=(M // tm, Kw // tn, nk), in_specs=in_specs,
        out_specs=tile if n_out == 1 else [tile] * n_out, out_shape=out if n_out == 1 else [out] * n_out,
        scratch=[pltpu.VMEM((tm, tn), F32)] if nk > 1 else [],
        sem=("parallel", "parallel", "arbitrary"), comm=comm)


def _mm_tn(x, dy, nb, *, name, tm=512, tn=1024, comm=None):
    M, Kw = x.shape
    NW = dy.shape[1] // nb
    tm, tn = _tile(Kw, tm, LANE), _tile(NW, tn, LANE)
    nj = NW // tn

    def body(x_ref, d_ref, o_ref):
        o_ref[...] = lax.dot_general(x_ref[...], d_ref[...], TN_DIMS, preferred_element_type=F32).astype(o_ref.dtype)

    return _call(
        body, [x, dy], name=name, grid=(nb * nj, Kw // tm),
        in_specs=[pl.BlockSpec((M, tm), lambda j, i: (0, i)),
                  pl.BlockSpec((M, tn), lambda j, i: (0, j))],
        out_specs=pl.BlockSpec((None, tm, tn), lambda j, i: (j // nj, i, j % nj)),
        out_shape=jax.ShapeDtypeStruct((nb, Kw, NW), BF16),
        sem=("parallel", "parallel"), comm=comm)


def _rms_fwd(x, g, *, name, tr=256):
    T, D = x.shape
    tr = _tile(T, tr, 16)

    def body(x_ref, g_ref, h_ref):
        xv = x_ref[...]
        r = lax.rsqrt(jnp.mean(xv * xv, axis=-1, keepdims=True) + RMS_EPS)
        h_ref[...] = ((xv * r) * g_ref[...]).astype(BF16)

    return pl.pallas_call(
        body, name=name, grid=(T // tr,),
        in_specs=[pl.BlockSpec((tr, D), lambda i: (i, 0)), pl.BlockSpec((1, D), lambda i: (0, 0))],
        out_specs=pl.BlockSpec((tr, D), lambda i: (i, 0)),
        out_shape=jax.ShapeDtypeStruct((T, D), BF16),
        compiler_params=_cp("parallel"))(x, g)


def _rms_bwd(dh, x, g, dres, *, name, tr=128):
    T, D = x.shape
    tr = _tile(T, tr, 16)

    def body(dh_ref, x_ref, g_ref, dr_ref, dx_ref, dxb_ref, dg_ref):
        i = pl.program_id(0)
        xv = x_ref[...]
        dhv = dh_ref[...]
        r = lax.rsqrt(jnp.mean(xv * xv, axis=-1, keepdims=True) + RMS_EPS)
        a = dhv * g_ref[...]
        dot = jnp.mean(a * xv, axis=-1, keepdims=True)
        dx = dr_ref[...] + r * a - xv * (r * r * r * dot)
        dx_ref[...] = dx
        dxb_ref[...] = dx.astype(BF16)
        part = jnp.sum(dhv * (xv * r), axis=0, keepdims=True)

        @pl.when(i == 0)
        def _():
            dg_ref[...] = part

        @pl.when(i > 0)
        def _():
            dg_ref[...] += part

    row = pl.BlockSpec((tr, D), lambda i: (i, 0))
    vec = pl.BlockSpec((1, D), lambda i: (0, 0))
    return pl.pallas_call(
        body, name=name, grid=(T // tr,),
        in_specs=[row, row, vec, row], out_specs=[row, row, vec],
        out_shape=[jax.ShapeDtypeStruct((T, D), F32), jax.ShapeDtypeStruct((T, D), BF16),
                   jax.ShapeDtypeStruct((1, D), F32)],
        compiler_params=_cp("arbitrary"))(dh, x, g, dres)


def _final_loss(x2, tgt, g, *, name, tr=128):
    T, D = x2.shape
    tr = _tile(T, tr, 16)

    def body(x_ref, t_ref, g_ref, dx_ref, dxb_ref, dg_ref, loss_ref):
        i = pl.program_id(0)
        xv = x_ref[...]
        gv = g_ref[...]
        r = lax.rsqrt(jnp.mean(xv * xv, axis=-1, keepdims=True) + RMS_EPS)
        xh = xv * r
        diff = xh * gv - t_ref[...]
        lpart = 0.5 * jnp.sum(jnp.mean(diff * diff, axis=-1, keepdims=True))
        dy = diff * (1.0 / D)
        a = dy * gv
        dot = jnp.mean(a * xv, axis=-1, keepdims=True)
        dx = r * a - xv * (r * r * r * dot)
        dx_ref[...] = dx
        dxb_ref[...] = dx.astype(BF16)
        part = jnp.sum(dy * xh, axis=0, keepdims=True)
        lblk = jnp.full((8, LANE), lpart, F32)

        @pl.when(i == 0)
        def _():
            dg_ref[...] = part
            loss_ref[...] = lblk

        @pl.when(i > 0)
        def _():
            dg_ref[...] += part
            loss_ref[...] += lblk

    row = pl.BlockSpec((tr, D), lambda i: (i, 0))
    vec = pl.BlockSpec((1, D), lambda i: (0, 0))
    return pl.pallas_call(
        body, name=name, grid=(T // tr,),
        in_specs=[row, row, vec],
        out_specs=[row, row, vec, pl.BlockSpec((8, LANE), lambda i: (0, 0))],
        out_shape=[jax.ShapeDtypeStruct((T, D), F32), jax.ShapeDtypeStruct((T, D), BF16),
                   jax.ShapeDtypeStruct((1, D), F32), jax.ShapeDtypeStruct((8, LANE), F32)],
        compiler_params=_cp("arbitrary"))(x2, tgt, g)


def _merge_fwd(gates, b_gate, ua, ub, *, name, tr=256):
    T, D = ua.shape
    tr = _tile(T, tr, 16)

    def body(ga_ref, gb_ref, ba_ref, bb_ref, ua_ref, ub_ref, o_ref):
        sa = _sigmoid(ga_ref[...].astype(F32) + ba_ref[...])
        sb = _sigmoid(gb_ref[...].astype(F32) + bb_ref[...])
        o_ref[...] = (sa * ua_ref[...].astype(F32) + sb * ub_ref[...].astype(F32)).astype(BF16)

    row = pl.BlockSpec((tr, D), lambda i: (i, 0))
    return pl.pallas_call(
        body, name=name, grid=(T // tr,),
        in_specs=[row, pl.BlockSpec((tr, D), lambda i: (i, 1)),
                  pl.BlockSpec((1, D), lambda i: (0, 0)), pl.BlockSpec((1, D), lambda i: (0, 1)), row, row],
        out_specs=row, out_shape=jax.ShapeDtypeStruct((T, D), BF16),
        compiler_params=_cp("parallel"))(gates, gates, b_gate, b_gate, ua, ub)


def _merge_bwd(dm, gates, b_gate, ua, ub, *, name, tr=256):
    T, D = ua.shape
    tr = _tile(T, tr, 16)

    def body(dm_ref, ga_ref, gb_ref, ba_ref, bb_ref, ua_ref, ub_ref, dua_ref, dub_ref, dg_ref, db_ref):
        i = pl.program_id(0)
        dmv = dm_ref[...].astype(F32)
        sa = _sigmoid(ga_ref[...].astype(F32) + ba_ref[...])
        sb = _sigmoid(gb_ref[...].astype(F32) + bb_ref[...])
        dua_ref[...] = (dmv * sa).astype(BF16)
        dub_ref[...] = (dmv * sb).astype(BF16)
        dga = dmv * ua_ref[...].astype(F32) * (sa * (1.0 - sa))
        dgb = dmv * ub_ref[...].astype(F32) * (sb * (1.0 - sb))
        dg_ref[:, :D] = dga.astype(BF16)
        dg_ref[:, D:] = dgb.astype(BF16)
        pa = jnp.sum(dga, axis=0, keepdims=True)
        pb = jnp.sum(dgb, axis=0, keepdims=True)

        @pl.when(i == 0)
        def _():
            db_ref[:, :D] = pa
            db_ref[:, D:] = pb

        @pl.when(i > 0)
        def _():
            db_ref[:, :D] += pa
            db_ref[:, D:] += pb

    row = pl.BlockSpec((tr, D), lambda i: (i, 0))
    vec = pl.BlockSpec((1, D), lambda i: (0, 0))
    act = jax.ShapeDtypeStruct((T, D), BF16)
    return pl.pallas_call(
        body, name=name, grid=(T // tr,),
        in_specs=[row, row, pl.BlockSpec((tr, D), lambda i: (i, 1)),
                  vec, pl.BlockSpec((1, D), lambda i: (0, 1)), row, row],
        out_specs=[row, row, pl.BlockSpec((tr, 2 * D), lambda i: (i, 0)), pl.BlockSpec((1, 2 * D), lambda i: (0, 0))],
        out_shape=[act, act, jax.ShapeDtypeStruct((T, 2 * D), BF16), jax.ShapeDtypeStruct((1, 2 * D), F32)],
        compiler_params=_cp("arbitrary"))(dm, gates, gates, b_gate, b_gate, ua, ub)


def _swiglu_fwd_epi(up, gate):
    g = gate.astype(F32)
    up = up.astype(BF16).astype(F32)
    return up, g * _sigmoid(g) * up


def _swiglu_bwd_epi(d, gate, up):
    g = gate.astype(F32)
    s = _sigmoid(g)
    return d * up.astype(F32) * (s * (1.0 + g * (1.0 - s))), d * (g * s)


def _cast_into(buf, src, row_off, col_off, *, name, tr=256):
    T = buf.shape[0]
    W = src.shape[1]
    tr = _tile(T, tr, 16)
    cw = W
    while col_off % cw or W % cw:
        cw -= LANE
    assert row_off % tr == 0
    c0, r0 = col_off // cw, row_off // tr

    def body(s_ref, b_ref, o_ref):
        o_ref[...] = s_ref[...].astype(o_ref.dtype)

    return pl.pallas_call(
        body, name=name, grid=(T // tr, W // cw),
        in_specs=[pl.BlockSpec((tr, cw), lambda i, j: (i + r0, j)), pl.BlockSpec(memory_space=pl.ANY)],
        out_specs=pl.BlockSpec((tr, cw), lambda i, j: (i, c0 + j)),
        out_shape=jax.ShapeDtypeStruct(buf.shape, buf.dtype), input_output_aliases={1: 0},
        compiler_params=_cp("parallel", "parallel"))(src, buf)


def _bias_table(rel, qb, kw):
    assert qb - 1 <= MAX_REL
    ha = rel.shape[0]
    n_clipped = kw - MAX_REL
    e = jnp.concatenate([jnp.broadcast_to(rel[:, 2 * MAX_REL:], (ha, n_clipped)),
                         rel[:, 2 * MAX_REL - 1:MAX_REL - qb:-1]], axis=1)
    period = qb + kw
    e = jnp.pad(e, ((0, 0), (0, period - e.shape[1])))
    z = jnp.broadcast_to(e[:, None, :], (ha, qb, period)).reshape(ha, qb * period)
    z = z[:, :qb * (period - 1)].reshape(ha, qb, period - 1)
    i = jnp.arange(qb)[:, None]
    j = jnp.arange(kw)[None, :]
    band = (j // CHUNK >= i // CHUNK) & (j // CHUNK <= i // CHUNK + LEFT_CHUNKS)
    return jnp.where(band[None], z[:, :, qb - 1:qb - 1 + kw], NEG_INF)


def _attn_a_fwd(q, kp, vp, btab, *, name):
    T, WA = q.shape[0], kp.shape[1]
    HA = WA // HEAD_DIM
    _, QB, KW = btab.shape
    padk = KW - QB

    def body(q_ref, k_ref, v_ref, b_ref, o_ref, lse_ref):
        i = pl.program_id(1)
        start = pl.multiple_of(i * QB, QB)
        k = k_ref[pl.ds(start, KW), :]
        v = v_ref[pl.ds(start, KW), :]
        s = lax.dot_general(q_ref[...], k, NT_DIMS, preferred_element_type=F32) * ATTN_SCALE + b_ref[...]
        kpos = start - padk + lax.broadcasted_iota(jnp.int32, (QB, KW), 1)
        s = jnp.where(kpos >= 0, s, NEG_INF)
        m = jnp.max(s, axis=-1, keepdims=True)
        p = jnp.exp(s - m)
        l = jnp.sum(p, axis=-1, keepdims=True)
        o = jnp.dot(p.astype(BF16), v, preferred_element_type=F32) / l
        o_ref[...] = o.astype(BF16)
        lse_ref[...] = jnp.broadcast_to(m + jnp.log(l), (QB, HEAD_DIM))

    qspec = pl.BlockSpec((QB, HEAD_DIM), lambda h, i: (i, h))
    kspec = pl.BlockSpec((T + padk, HEAD_DIM), lambda h, i: (0, h))
    return pl.pallas_call(
        body, name=name, grid=(HA, T // QB),
        in_specs=[qspec, kspec, kspec, pl.BlockSpec((None, QB, KW), lambda h, i: (h, 0, 0))],
        out_specs=[qspec, qspec],
        out_shape=[jax.ShapeDtypeStruct((T, WA), BF16), jax.ShapeDtypeStruct((T, WA), F32)],
        compiler_params=_cp("parallel", "arbitrary"))(q, kp, vp, btab)


def _attn_a_bwd(q, kp, vp, btab, o, do, lse, *, name, comm=None, dq_width=None):
    T, WA = q.shape[0], kp.shape[1]
    HA = WA // HEAD_DIM
    _, QB, KW = btab.shape
    padk = KW - QB

    def body(q_ref, k_ref, v_ref, b_ref, o_ref, do_ref, lse_ref, dq_ref, dk_ref, dv_ref, db_ref):
        i = pl.program_id(1)

        @pl.when(i == 0)
        def _():
            dk_ref[...] = jnp.zeros_like(dk_ref)
            dv_ref[...] = jnp.zeros_like(dv_ref)
            db_ref[...] = jnp.zeros_like(db_ref)

        start = pl.multiple_of(i * QB, QB)
        qv = q_ref[...]
        dov = do_ref[...]
        k = k_ref[pl.ds(start, KW), :]
        v = v_ref[pl.ds(start, KW), :]
        s = lax.dot_general(qv, k, NT_DIMS, preferred_element_type=F32) * ATTN_SCALE + b_ref[...]
        kpos = start - padk + lax.broadcasted_iota(jnp.int32, (QB, KW), 1)
        s = jnp.where(kpos >= 0, s, NEG_INF)
        p = jnp.exp(s - lse_ref[:, :1])
        dp = lax.dot_general(dov, v, NT_DIMS, preferred_element_type=F32)
        delta = jnp.sum(dov.astype(F32) * o_ref[...].astype(F32), axis=-1, keepdims=True)
        ds = p * (dp - delta)
        db_ref[...] += ds
        dsb = ds.astype(BF16)
        dq_ref[...] = (jnp.dot(dsb, k, preferred_element_type=F32) * ATTN_SCALE).astype(BF16)
        dk_ref[pl.ds(start, KW), :] += lax.dot_general(dsb, qv, TN_DIMS, preferred_element_type=F32) * ATTN_SCALE
        dv_ref[pl.ds(start, KW), :] += lax.dot_general(p.astype(BF16), dov, TN_DIMS, preferred_element_type=F32)

    qspec = pl.BlockSpec((QB, HEAD_DIM), lambda h, i: (i, h))
    kspec = pl.BlockSpec((T + padk, HEAD_DIM), lambda h, i: (0, h))
    bspec = pl.BlockSpec((None, QB, KW), lambda h, i: (h, 0, 0))
    return _call(
        body, [q, kp, vp, btab, o, do, lse], name=name, grid=(HA, T // QB),
        in_specs=[qspec, kspec, kspec, bspec, qspec, qspec, qspec],
        out_specs=[qspec, kspec, kspec, bspec],
        out_shape=[jax.ShapeDtypeStruct((T, dq_width or WA), BF16), jax.ShapeDtypeStruct((T + padk, WA), F32),
                   jax.ShapeDtypeStruct((T + padk, WA), F32), jax.ShapeDtypeStruct((HA, QB, KW), F32)],
        sem=("parallel", "arbitrary"), comm=comm)


def _relbias_fold(skew, dtab, *, name):
    HA, qb, kw = dtab.shape
    W = kw + 1

    def body(s_ref, t_ref, col_ref, tail_ref):
        row = lax.broadcasted_iota(jnp.int32, (qb, W), 0)
        col = lax.broadcasted_iota(jnp.int32, (qb, W), 1)
        col_ref[...] = jnp.sum(jnp.where(row + col < kw, s_ref[...], 0.0), axis=0, keepdims=True)
        i = lax.broadcasted_iota(jnp.int32, (qb, kw), 0)
        j = lax.broadcasted_iota(jnp.int32, (qb, kw), 1)
        tail = jnp.sum(jnp.where((kw - qb) + i - j >= MAX_REL, t_ref[...], 0.0))
        tail_ref[...] = jnp.full((1, LANE), tail, F32)

    return pl.pallas_call(
        body, name=name, grid=(HA,),
        in_specs=[pl.BlockSpec((None, qb, W), lambda h: (h, 0, 0)), pl.BlockSpec((None, qb, kw), lambda h: (h, 0, 0))],
        out_specs=[pl.BlockSpec((None, 1, W), lambda h: (h, 0, 0)), pl.BlockSpec((None, 1, LANE), lambda h: (h, 0, 0))],
        out_shape=[jax.ShapeDtypeStruct((HA, 1, W), F32), jax.ShapeDtypeStruct((HA, 1, LANE), F32)],
        compiler_params=_cp("parallel"))(skew, dtab)


CUM_BLOCK = 128


def _fgate_fwd(f, bf, *, name):
    T = f.shape[0]
    nb = T // CUM_BLOCK

    def body(f_ref, b_ref, cum_ref):
        r = lax.broadcasted_iota(jnp.int32, (CUM_BLOCK, CUM_BLOCK), 0)
        c = lax.broadcasted_iota(jnp.int32, (CUM_BLOCK, CUM_BLOCK), 1)
        tri = (c <= r).astype(F32)

        def step(b, carry):
            r0 = pl.multiple_of(b * CUM_BLOCK, CUM_BLOCK)
            z = f_ref[pl.ds(r0, CUM_BLOCK), :] + b_ref[...]
            lf = jnp.minimum(z, 0.0) - jnp.log(1.0 + jnp.exp(-jnp.abs(z)))
            cs = jnp.dot(tri, lf, precision=lax.Precision.HIGHEST, preferred_element_type=F32) + carry
            cum_ref[pl.ds(r0, CUM_BLOCK), :] = cs
            return carry + jnp.sum(lf, axis=0, keepdims=True)

        lax.fori_loop(0, nb, step, jnp.zeros((1, LANE), F32))

    return pl.pallas_call(
        body, name=name, out_shape=jax.ShapeDtypeStruct((T, LANE), F32),
        compiler_params=_cp())(f, bf)


def _fgate_bwd(dcq, dck, f, bf, *, name):
    T = f.shape[0]
    nb = T // CUM_BLOCK

    def body(dq_ref, dk_ref, f_ref, b_ref, df_ref, dbf_ref):
        r = lax.broadcasted_iota(jnp.int32, (CUM_BLOCK, CUM_BLOCK), 0)
        c = lax.broadcasted_iota(jnp.int32, (CUM_BLOCK, CUM_BLOCK), 1)
        tri = (c >= r).astype(F32)

        def step(n, carry):
            tail, dbf = carry
            r0 = pl.multiple_of((nb - 1 - n) * CUM_BLOCK, CUM_BLOCK)
            dc = dq_ref[pl.ds(r0, CUM_BLOCK), :] + dk_ref[pl.ds(r0, CUM_BLOCK), :]
            ss = jnp.dot(tri, dc, precision=lax.Precision.HIGHEST, preferred_element_type=F32) + tail
            z = f_ref[pl.ds(r0, CUM_BLOCK), :] + b_ref[...]
            df = ss * _sigmoid(-z)
            df_ref[pl.ds(r0, CUM_BLOCK), :] = df
            return tail + jnp.sum(dc, axis=0, keepdims=True), dbf + jnp.sum(df, axis=0, keepdims=True)

        zero = jnp.zeros((1, LANE), F32)
        _, dbf = lax.fori_loop(0, nb, step, (zero, zero))
        dbf_ref[...] = dbf

    return pl.pallas_call(
        body, name=name,
        out_shape=[jax.ShapeDtypeStruct((T, LANE), F32), jax.ShapeDtypeStruct((1, LANE), F32)],
        compiler_params=_cp())(dcq, dck, f, bf)


def _fox_logits(qv, kk, cq, ck, diagonal):
    s = lax.dot_general(qv, kk, NT_DIMS, preferred_element_type=F32) * ATTN_SCALE + cq - ck
    if diagonal:
        row = lax.broadcasted_iota(jnp.int32, s.shape, 0)
        col = lax.broadcasted_iota(jnp.int32, s.shape, 1)
        s = jnp.where(col <= row, s, NEG_INF)
    return s


def _fox_fwd(qkv, cols, cum, cum_t, *, name, comm=None):
    T = qkv.shape[0]
    HB, _, KB = cum_t.shape
    WB = HB * HEAD_DIM
    QB = KB
    qc, kc, vc = cols

    def body(q_ref, k_ref, v_ref, c_ref, ct_ref, o_ref, lse_ref):
        h = pl.program_id(0)
        i = pl.program_id(1)
        qv = q_ref[...]
        lane = lax.broadcasted_iota(jnp.int32, (QB, LANE), 1)
        cq = jnp.sum(jnp.where(lane == h, c_ref[...], 0.0), axis=-1, keepdims=True)

        def block(kb, carry, diagonal):
            m, l, acc = carry
            k0 = pl.multiple_of(kb * KB, KB)
            s = _fox_logits(qv, k_ref[pl.ds(k0, KB), :], cq, ct_ref[pl.ds(kb, 1), :], diagonal)
            m_new = jnp.maximum(m, jnp.max(s, axis=-1, keepdims=True))
            alpha = jnp.exp(m - m_new)
            p = jnp.exp(s - m_new)
            l = alpha * l + jnp.sum(p, axis=-1, keepdims=True)
            acc = alpha * acc + jnp.dot(p.astype(BF16), v_ref[pl.ds(k0, KB), :], preferred_element_type=F32)
            return m_new, l, acc

        past = lax.fori_loop(
            0, i, lambda kb, carry: block(kb, carry, False),
            (jnp.full((QB, 1), NEG_INF, F32), jnp.zeros((QB, 1), F32), jnp.zeros((QB, HEAD_DIM), F32)))
        m, l, acc = block(i, past, True)
        o_ref[...] = (acc / l).astype(BF16)
        lse_ref[...] = jnp.broadcast_to(m + jnp.log(l), (QB, HEAD_DIM))

    ospec = pl.BlockSpec((QB, HEAD_DIM), lambda h, i: (i, h))
    return _call(
        body, [qkv, qkv, qkv, cum, cum_t], name=name, grid=(HB, T // QB),
        in_specs=[pl.BlockSpec((QB, HEAD_DIM), lambda h, i: (i, qc + h)),
                  pl.BlockSpec((T, HEAD_DIM), lambda h, i: (0, kc + h)),
                  pl.BlockSpec((T, HEAD_DIM), lambda h, i: (0, vc + h)),
                  pl.BlockSpec((QB, LANE), lambda h, i: (i, 0)),
                  pl.BlockSpec((None, T // KB, KB), lambda h, i: (h, 0, 0))],
        out_specs=[ospec, ospec],
        out_shape=[jax.ShapeDtypeStruct((T, WB), BF16), jax.ShapeDtypeStruct((T, WB), F32)],
        sem=("parallel", "arbitrary"), comm=comm)


def _fox_bwd(qkv, cols, cum, cum_t, o, do, lse, *, name, comm=None, dq_into=None):
    T = qkv.shape[0]
    HB, _, KB = cum_t.shape
    WB = HB * HEAD_DIM
    QB = KB
    qc, kc, vc = cols

    def body(*refs):
        q_ref, k_ref, v_ref, c_ref, ct_ref, o_ref, do_ref, lse_ref = refs[:8]
        dq_ref, dk_ref, dv_ref, dcq_ref, dck_ref = refs[-5:]
        h = pl.program_id(0)
        i = pl.program_id(1)

        @pl.when(i == 0)
        def _():
            dk_ref[...] = jnp.zeros_like(dk_ref)
            dv_ref[...] = jnp.zeros_like(dv_ref)
            dck_ref[...] = jnp.zeros_like(dck_ref)

        qv = q_ref[...]
        dov = do_ref[...]
        lse_v = lse_ref[:, :1]
        delta = jnp.sum(dov.astype(F32) * o_ref[...].astype(F32), axis=-1, keepdims=True)
        lane = lax.broadcasted_iota(jnp.int32, (QB, LANE), 1)
        cq = jnp.sum(jnp.where(lane == h, c_ref[...], 0.0), axis=-1, keepdims=True)

        def block(kb, carry, diagonal):
            dq, dcq = carry
            k0 = pl.multiple_of(kb * KB, KB)
            kk = k_ref[pl.ds(k0, KB), :]
            s = _fox_logits(qv, kk, cq, ct_ref[pl.ds(kb, 1), :], diagonal)
            p = jnp.exp(s - lse_v)
            dp = lax.dot_general(dov, v_ref[pl.ds(k0, KB), :], NT_DIMS, preferred_element_type=F32)
            ds = p * (dp - delta)
            dsb = ds.astype(BF16)
            dk_ref[pl.ds(k0, KB), :] += lax.dot_general(dsb, qv, TN_DIMS, preferred_element_type=F32) * ATTN_SCALE
            dv_ref[pl.ds(k0, KB), :] += lax.dot_general(p.astype(BF16), dov, TN_DIMS, preferred_element_type=F32)
            dck_ref[pl.ds(kb, 1), :] -= jnp.sum(ds, axis=0, keepdims=True)
            return (dq + jnp.dot(dsb, kk, preferred_element_type=F32),
                    dcq + jnp.sum(ds, axis=-1, keepdims=True))

        past = lax.fori_loop(0, i, lambda kb, carry: block(kb, carry, False),
                             (jnp.zeros((QB, HEAD_DIM), F32), jnp.zeros((QB, 1), F32)))
        dq, dcq = block(i, past, True)
        dq_ref[...] = (dq * ATTN_SCALE).astype(BF16)
        dcq_ref[...] = jnp.broadcast_to(dcq, (QB, HEAD_DIM))

    qspec = pl.BlockSpec((QB, HEAD_DIM), lambda h, i: (i, h))
    kspec = pl.BlockSpec((T, HEAD_DIM), lambda h, i: (0, h))
    cspec = pl.BlockSpec((QB, LANE), lambda h, i: (i, 0))
    tspec = pl.BlockSpec((None, T // KB, KB), lambda h, i: (h, 0, 0))
    wide = jax.ShapeDtypeStruct((T, WB), F32)
    args = [qkv, qkv, qkv, cum, cum_t, o, do, lse]
    in_specs = [pl.BlockSpec((QB, HEAD_DIM), lambda h, i: (i, qc + h)),
                pl.BlockSpec((T, HEAD_DIM), lambda h, i: (0, kc + h)),
                pl.BlockSpec((T, HEAD_DIM), lambda h, i: (0, vc + h)),
                cspec, tspec, qspec, qspec, qspec]
    dq_spec, dq_shape, aliases = qspec, jax.ShapeDtypeStruct((T, WB), BF16), None
    if dq_into is not None:
        buf, dq_col = dq_into
        args.append(buf)
        in_specs.append(pl.BlockSpec(memory_space=pl.ANY))
        dq_spec = pl.BlockSpec((QB, HEAD_DIM), lambda h, i: (i, dq_col + h))
        dq_shape, aliases = jax.ShapeDtypeStruct(buf.shape, buf.dtype), {8: 0}
    return _call(
        body, args, name=name, grid=(HB, T // QB), in_specs=in_specs,
        out_specs=[dq_spec, kspec, kspec, qspec, tspec],
        out_shape=[dq_shape, wide, wide, wide, jax.ShapeDtypeStruct((HB, T // KB, KB), F32)],
        sem=("parallel", "arbitrary"), comm=comm, aliases=aliases)


def _place():
    return lax.axis_index("x"), lax.axis_index("y"), lax.axis_index("c")


def _other_chips(x, y):
    return [(1 - x, y), (x, 1 - y), (1 - x, 1 - y)]


def _gather_comm(shards):
    n = len(shards)

    def copies(srcs, outs, scratch, want):
        send_sems, recv_sems, local_sems = scratch
        x, y, c = _place()
        me, sib = (x, y, c), (x, y, 1 - c)
        chip_a = (jnp.bitwise_xor(x, c), jnp.bitwise_xor(y, 1 - c))
        chip_b = (jnp.bitwise_xor(x, 1 - c), jnp.bitwise_xor(y, c))
        chip_d = (1 - x, 1 - y)

        def copy(t, k, block, to, src=None):
            rows = outs[t].at[4 * block[0] + 2 * block[1] + block[2]]
            return pltpu.make_async_remote_copy(
                src_ref=rows if src is None else src, dst_ref=rows,
                send_sem=send_sems.at[7 * t + k], recv_sem=recv_sems.at[7 * t + k],
                device_id=to, device_id_type=MESH)

        got = {}
        if "mine" in want:
            got["mine"] = [pltpu.make_async_copy(srcs[t], outs[t].at[4 * x + 2 * y + c], local_sems.at[t])
                           for t in range(n)]
        if "first" in want:
            got["first"] = [cp for t in range(n) for cp in
                            (copy(t, 0, me, sib, src=srcs[t]), copy(t, 1, me, (*chip_a, c), src=srcs[t]),
                             copy(t, 2, me, (*chip_b, c), src=srcs[t]))]
        if "from_a" in want:
            got["from_a"] = [copy(t, 1, (*chip_a, c), me) for t in range(n)]
        if "hop" in want:
            got["hop"] = [copy(t, 3, (*chip_a, c), (*chip_b, c)) for t in range(n)]
        if "from_b" in want:
            got["from_b"] = [copy(t, 2, (*chip_b, c), me) for t in range(n)]
        if "from_d" in want:
            got["from_d"] = [copy(t, 3, (*chip_d, c), me) for t in range(n)]
        for key, k, chip in (("pass_a", 4, chip_a), ("pass_b", 5, chip_b), ("pass_d", 6, chip_d)):
            if key in want:
                got[key] = [copy(t, k, (*chip, c), sib) for t in range(n)]
        if "late" in want:
            got["late"] = [cp for t in range(n) for cp in
                           (copy(t, 0, sib, me), copy(t, 4, (*chip_b, 1 - c), me),
                            copy(t, 5, (*chip_a, 1 - c), me), copy(t, 6, (*chip_d, 1 - c), me))]
        return got

    def begin(srcs, outs, scratch):
        cps = copies(srcs, outs, scratch, ("mine", "first"))
        for cp in cps["mine"] + cps["first"]:
            cp.start()

    def hop(srcs, outs, scratch):
        cps = copies(srcs, outs, scratch, ("from_a", "hop", "pass_a"))
        for landed, fwd, sibling in zip(cps["from_a"], cps["hop"], cps["pass_a"]):
            landed.wait_recv()
            fwd.start()
            sibling.start()

    def pass_on(srcs, outs, scratch):
        cps = copies(srcs, outs, scratch, ("from_b", "pass_b", "from_d", "pass_d"))
        for landed, sibling in list(zip(cps["from_b"], cps["pass_b"])) + list(zip(cps["from_d"], cps["pass_d"])):
            landed.wait_recv()
            sibling.start()

    def end(srcs, outs, scratch):
        cps = copies(srcs, outs, scratch, ("mine", "first", "hop", "pass_a", "pass_b", "pass_d", "late"))
        for cp in cps["late"]:
            cp.wait_recv()
        for cp in cps["first"] + cps["hop"] + cps["pass_a"] + cps["pass_b"] + cps["pass_d"]:
            cp.wait_send()
        for cp in cps["mine"]:
            cp.wait()

    return _Comm(
        shards, [jax.ShapeDtypeStruct((N_DEV,) + s.shape, s.dtype) for s in shards],
        [pltpu.SemaphoreType.DMA((7 * n,)), pltpu.SemaphoreType.DMA((7 * n,)), pltpu.SemaphoreType.DMA((n,))],
        begin, end, [(0.67, hop), (0.9, pass_on)])


def _sibling_comm(grads):
    n = len(grads)

    def copies(srcs, outs, scratch):
        send_sems, recv_sems = scratch
        x, y, c = _place()
        return [pltpu.make_async_remote_copy(
            src_ref=srcs[t].at[2 * p + (1 - c)], dst_ref=outs[t].at[p],
            send_sem=send_sems.at[4 * t + p], recv_sem=recv_sems.at[4 * t + p],
            device_id=(x, y, 1 - c), device_id_type=MESH) for t in range(n) for p in range(4)]

    def begin(srcs, outs, scratch):
        for cp in copies(srcs, outs, scratch):
            cp.start()

    def end(srcs, outs, scratch):
        cps = copies(srcs, outs, scratch)
        for cp in cps:
            cp.wait_recv()
        for cp in cps:
            cp.wait_send()

    return _Comm(
        grads, [jax.ShapeDtypeStruct((4,) + g.shape[1:], g.dtype) for g in grads],
        [pltpu.SemaphoreType.DMA((4 * n,)), pltpu.SemaphoreType.DMA((4 * n,))], begin, end)


def _chips_comm(sums, relations):
    n, nr = len(sums), len(relations)

    def copies(srcs, outs, scratch):
        send_sems, recv_sems = scratch
        x, y, c = _place()
        chips = _other_chips(x, y)
        return [pltpu.make_async_remote_copy(
            src_ref=srcs[t].at[2 * chips[j][0] + chips[j][1]], dst_ref=outs[t].at[s],
            send_sem=send_sems.at[nr * t + s], recv_sem=recv_sems.at[nr * t + s],
            device_id=(*chips[j], c), device_id_type=MESH) for t in range(n) for s, j in enumerate(relations)]

    def begin(srcs, outs, scratch):
        for cp in copies(srcs, outs, scratch):
            cp.start()

    def end(srcs, outs, scratch):
        cps = copies(srcs, outs, scratch)
        for cp in cps:
            cp.wait_recv()
        for cp in cps:
            cp.wait_send()

    return _Comm(
        sums, [jax.ShapeDtypeStruct((nr,) + s.shape[1:], s.dtype) for s in sums],
        [pltpu.SemaphoreType.DMA((nr * n,)), pltpu.SemaphoreType.DMA((nr * n,))], begin, end)


def _run_comm(comm, *, name):
    ci, co = len(comm.ins), len(comm.outs)

    def body(*refs):
        cin, cout, csc = refs[:ci], refs[ci:ci + co], refs[ci + co:]
        comm.begin(cin, cout, csc)
        for _, relay in comm.relays:
            relay(cin, cout, csc)
        comm.end(cin, cout, csc)

    return pl.pallas_call(
        body, name=name, in_specs=_any_specs(ci), out_specs=_any_specs(co), out_shape=comm.outs,
        scratch_shapes=comm.scratch)(*comm.ins)


def _chip_sum(g, r1, core, *, name, tr=256):
    _, R, C = g.shape
    tr = _tile(R, tr, 16)

    def body(c_ref, g_ref, r_ref, o_ref):
        o_ref[...] = (g_ref[...].astype(F32) + r_ref[...].astype(F32)).astype(BF16)

    return pl.pallas_call(
        body, name=name,
        grid_spec=pltpu.PrefetchScalarGridSpec(
            num_scalar_prefetch=1, grid=(4, R // tr),
            in_specs=[pl.BlockSpec((None, tr, C), lambda p, i, c: (2 * p + c[0], i, 0)),
                      pl.BlockSpec((None, tr, C), lambda p, i, c: (p, i, 0))],
            out_specs=pl.BlockSpec((None, tr, C), lambda p, i, c: (p, i, 0))),
        out_shape=jax.ShapeDtypeStruct((4, R, C), BF16),
        compiler_params=_cp("parallel", "parallel"))(core, g, r1)


def _adamw_math(w, g, m, v):
    m = ADAM_B1 * m + (1.0 - ADAM_B1) * g
    v = ADAM_B2 * v + (1.0 - ADAM_B2) * (g * g)
    m_hat = m / (1.0 - ADAM_B1 ** ADAM_STEP)
    v_hat = v / (1.0 - ADAM_B2 ** ADAM_STEP)
    delta = -ADAM_LR * (m_hat / (jnp.sqrt(v_hat) + ADAM_EPS) + ADAM_WD * w)
    return delta, m, v


def _adamw_shard(s1, others, chip, w, m, v, *, name, tr=64):
    R0, C0 = w.shape
    C = s1.shape[2]
    tr = _tile(R0, tr, 8)
    (a0, k0), (a1, k1), (a2, k2) = others

    def body(p_ref, s_ref, r0_ref, r1_ref, r2_ref, w_ref, m_ref, v_ref, g_out, d_out, m_out, v_out):
        g = s_ref[:, :C0].astype(F32)
        for r_ref in (r0_ref, r1_ref, r2_ref):
            g = g + r_ref[:, :C0].astype(F32)
        delta, mn, vn = _adamw_math(w_ref[...], g, m_ref[...], v_ref[...])
        g_out[...] = g
        d_out[...] = delta
        m_out[...] = mn
        v_out[...] = vn

    full = pl.BlockSpec((tr, C0), lambda i, p: (i, 0))
    out = jax.ShapeDtypeStruct((R0, C0), F32)
    return pl.pallas_call(
        body, name=name,
        grid_spec=pltpu.PrefetchScalarGridSpec(
            num_scalar_prefetch=1, grid=(R0 // tr,),
            in_specs=[pl.BlockSpec((None, tr, C), lambda i, p: (p[0], i, 0)),
                      pl.BlockSpec((None, tr, C), lambda i, p: (k0, i, 0)),
                      pl.BlockSpec((None, tr, C), lambda i, p: (k1, i, 0)),
                      pl.BlockSpec((None, tr, C), lambda i, p: (k2, i, 0)),
                      full, full, full],
            out_specs=[full, full, full, full]),
        out_shape=[out, out, out, out],
        compiler_params=_cp("parallel"))(chip, s1, a0, a1, a2, w, m, v)


def _all_reduce_small(part, *, name):
    SR = part.shape[0]

    def body(p_ref, o_ref, buf, send_sems, recv_sems):
        x, y, c = _place()
        me = 4 * x + 2 * y + c
        buf[0] = p_ref[...]
        copies = []
        for k in range(1, N_DEV):
            peer = (1 - x if k & 4 else x, 1 - y if k & 2 else y, 1 - c if k & 1 else c)
            copies.append(pltpu.make_async_remote_copy(
                src_ref=p_ref, dst_ref=buf.at[k], send_sem=send_sems.at[k - 1], recv_sem=recv_sems.at[k - 1],
                device_id=peer, device_id_type=MESH))
        for cp in copies:
            cp.start()
        for cp in copies:
            cp.wait_recv()
        for cp in copies:
            cp.wait_send()
        acc = buf[me]
        for d in range(1, N_DEV):
            acc = acc + buf[jnp.bitwise_xor(me, d)]
        o_ref[...] = acc

    return pl.pallas_call(
        body, name=name,
        in_specs=[pl.BlockSpec(memory_space=pltpu.VMEM)], out_specs=pl.BlockSpec(memory_space=pltpu.VMEM),
        out_shape=jax.ShapeDtypeStruct(part.shape, F32),
        scratch_shapes=[pltpu.VMEM((N_DEV, SR, LANE), F32), pltpu.SemaphoreType.DMA((N_DEV - 1,)),
                        pltpu.SemaphoreType.DMA((N_DEV - 1,))])(part)


def _adamw_small(w, g, m, v, *, name):
    def body(w_ref, g_ref, m_ref, v_ref, d_out, m_out, v_out):
        delta, mn, vn = _adamw_math(w_ref[...], g_ref[...], m_ref[...], v_ref[...])
        d_out[...] = delta
        m_out[...] = mn
        v_out[...] = vn

    out = jax.ShapeDtypeStruct(w.shape, F32)
    return pl.pallas_call(body, name=name, out_shape=[out, out, out], compiler_params=_cp())(w, g, m, v)


def _in_layout(din8, fq):
    lay = []
    for b in range(N_DEV):
        g0 = din8 * b
        nq = min(max(fq - g0, 0), din8)
        oq = g0 % LANE if nq else 0
        nqt = -(-(oq + nq) // LANE) if nq else 0
        tq0 = (g0 - oq) // LANE
        pg, tg0, ngt = oq + nq, 0, 0
        if nq < din8:
            u0 = g0 + nq - fq
            pg = nqt * LANE + u0 % LANE
            tg0, ngt = u0 // LANE, -(-(u0 % LANE + din8 - nq) // LANE)
        lay.append(dict(nq=nq, oq=oq, tq0=tq0, nqt=nqt, pg=pg, tg0=tg0, ngt=ngt, width=pg + din8 - nq))
    return lay


def _place_block(shard, lay, pin, b):
    rows, din8 = shard.shape

    def place(s, l):
        nq = l["nq"]
        parts, at = [], 0
        for start, cols in ((l["oq"], s[:, :nq]), (l["pg"], s[:, nq:])):
            if cols.shape[1]:
                parts += [jnp.zeros((rows, start - at), s.dtype), cols]
                at = start + cols.shape[1]
        parts.append(jnp.zeros((rows, pin - at), s.dtype))
        return jnp.concatenate([p for p in parts if p.shape[1]], axis=1)

    return lax.switch(b, [functools.partial(place, l=l) for l in lay], shard)


def _unplace_block(g, lay, din8, b):
    def unplace(a, l):
        nq = l["nq"]
        parts = [a[:, l["oq"]:l["oq"] + nq], a[:, l["pg"]:l["pg"] + din8 - nq]]
        return jnp.concatenate([p for p in parts if p.shape[1]], axis=1)

    return lax.switch(b, [functools.partial(unplace, l=l) for l in lay], g)


def _tile_runs(sources):
    runs = []
    for src in sources:
        if len(src) == 1 and runs and not isinstance(runs[-1], list) \
                and runs[-1][0] == src[0][0] and runs[-1][1] + runs[-1][2] == src[0][1]:
            runs[-1] = (runs[-1][0], runs[-1][1], runs[-1][2] + 1)
        elif len(src) == 1:
            runs.append((src[0][0], src[0][1], 1))
        else:
            runs.append(list(src))
    return runs


def _gather_tiles(arrays, sources):
    rows, dtype = arrays[0].shape[0], arrays[0].dtype
    pieces = []
    for run in _tile_runs(sources):
        if isinstance(run, list):
            tile = jnp.zeros((rows, LANE), dtype)
            for a, t in run:
                tile = tile + arrays[a][:, t * LANE:(t + 1) * LANE]
            pieces.append(tile)
        else:
            a, t0, n = run
            pieces.append(arrays[a][:, t0 * LANE:(t0 + n) * LANE])
    return jnp.concatenate(pieces, axis=1)


def _assemble_w_in(blocks, lay, n_qkv_tiles, n_f_tiles, n_g_tiles):
    q_src = [[] for _ in range(n_qkv_tiles + n_f_tiles)]
    g_src = [[] for _ in range(n_g_tiles)]
    for b, l in enumerate(lay):
        for j in range(l["nqt"]):
            q_src[l["tq0"] + j].append((b, j))
        for j in range(l["ngt"]):
            g_src[l["tg0"] + j].append((b, l["pg"] // LANE + j))
    arrays = [blocks[b] for b in range(N_DEV)]
    return (_gather_tiles(arrays, q_src[:n_qkv_tiles]), _gather_tiles(arrays, q_src[n_qkv_tiles:]),
            _gather_tiles(arrays, g_src))


def _scatter_g_in(g_qkv, g_f, g_gates, lay, pin):
    n_qkv_tiles = g_qkv.shape[1] // LANE
    out = []
    for l in lay:
        src = [[] for _ in range(pin // LANE)]
        for j in range(l["nqt"]):
            t = l["tq0"] + j
            src[j] = [(0, t)] if t < n_qkv_tiles else [(1, t - n_qkv_tiles)]
        for j in range(l["ngt"]):
            src[l["pg"] // LANE + j] = [(2, l["tg0"] + j)]
        out.append(_gather_tiles([g_qkv, g_f, g_gates], src))
    return jnp.stack(out)


def _pack(parts, rows):
    flat = jnp.concatenate([p.reshape(-1).astype(F32) for p in parts])
    return jnp.pad(flat, (0, rows * LANE - flat.shape[0])).reshape(rows, LANE)


def _unpack(packed, shapes):
    flat = packed.reshape(-1)
    out, at = [], 0
    for s in shapes:
        n = 1
        for d in s:
            n *= d
        out.append(flat[at:at + n].reshape(s))
        at += n
    return out


def kernel(x, g_mix, w_in, b_f, b_gate, rel_bias, w_branch_a, w_branch_b, w_out, g_ffn, w_gate_ffn, w_up_ffn, w_down_ffn, g_final, loss_target, m_g_mix, m_w_in, m_b_f, m_b_gate, m_rel_bias, m_w_branch_a, m_w_branch_b, m_w_out, m_g_ffn, m_w_gate_ffn, m_w_up_ffn, m_w_down_ffn, m_g_final, v_g_mix, v_w_in, v_b_f, v_b_gate, v_rel_bias, v_w_branch_a, v_w_branch_b, v_w_out, v_g_ffn, v_w_gate_ffn, v_w_up_ffn, v_w_down_ffn, v_g_final):
    _, T, D = x.shape
    HA, n_rel = rel_bias.shape[1], rel_bias.shape[2]
    HB = b_f.shape[1]
    WA, WB = HA * HEAD_DIM, HB * HEAD_DIM
    din8, f8, d8 = w_in.shape[2], w_gate_ffn.shape[2], D // N_DEV
    DIN = N_DEV * din8
    PF = _pad_to(f8, LANE)
    fcol = 3 * WA + 3 * WB
    lay = _in_layout(din8, fcol + HB)
    PIN = _pad_to(max(l["width"] for l in lay), LANE)
    padk = LEFT_CHUNKS * CHUNK
    QA = min(256, T)
    KW = QA + padk
    KB = min(512, T)
    x2d, tgt = x[0], loss_target[0]
    cx, cy, cc = _place()
    blk = 4 * cx + 2 * cy + cc
    core = jnp.reshape(cc, (1,)).astype(jnp.int32)
    chip = jnp.reshape(2 * cx + cy, (1,)).astype(jnp.int32)
    every = [0, 1, 2]

    s_in = _place_block(w_in[0].astype(BF16), lay, PIN, blk)
    s_a, s_b, s_out = w_branch_a[0].astype(BF16), w_branch_b[0].astype(BF16), w_out[0].astype(BF16)
    s_g = jnp.pad(w_gate_ffn[0].T.astype(BF16), ((0, PF - f8), (0, 0)))
    s_u = jnp.pad(w_up_ffn[0].T.astype(BF16), ((0, PF - f8), (0, 0)))
    s_d = jnp.pad(w_down_ffn[0].astype(BF16), ((0, PF - f8), (0, 0)))
    Win, = _run_comm(_gather_comm([s_in]), name="gather_w_in")
    Wqkv, Wf, Wgt = _assemble_w_in(Win, lay, fcol // LANE, 1, 2 * D // LANE)
    Wqkv, Wf, Wgt = Wqkv.reshape(1, D, fcol), Wf.reshape(1, D, LANE), Wgt.reshape(1, D, 2 * D)
    bf_pad = jnp.pad(b_f, ((0, 0), (0, LANE - HB)))
    fox_cols = (3 * HA, 3 * HA + HB, 3 * HA + 2 * HB)

    h = _rms_fwd(x2d, g_mix, name="rms_mix")
    qkv, (Wg,) = _mm_nn(h, Wqkv, name="proj_qkv", out_dtype=BF16, comm=_gather_comm([s_g]))
    Wg = Wg.reshape(1, N_DEV * PF, D)
    gates, (Wa, Wb, Wout) = _mm_nn(h, Wgt, name="proj_gates", out_dtype=BF16, comm=_gather_comm([s_a, s_b, s_out]))
    Wout = Wout.reshape(1, D, D)
    flog = _mm_nn(h, Wf, name="proj_forget", out_dtype=F32)

    kap = jnp.pad(qkv[:, WA:2 * WA], ((padk, 0), (0, 0)))
    vap = jnp.pad(qkv[:, 2 * WA:3 * WA], ((padk, 0), (0, 0)))
    btab = _bias_table(rel_bias[0], QA, KW)
    o_a, lse_a = _attn_a_fwd(qkv, kap, vap, btab, name="attn_a_fwd")

    cum = _fgate_fwd(flog, bf_pad, name="forget_cumsum")
    cum_t = cum[:, :HB].T.reshape(HB, T // KB, KB)
    (o_b, lse_b), (Wu,) = _fox_fwd(qkv, fox_cols, cum, cum_t, name="fox_fwd", comm=_gather_comm([s_u]))
    Wu = Wu.reshape(1, N_DEV * PF, D)

    u_a = _mm_nn(o_a, Wa, name="branch_a", out_dtype=BF16)
    u_b = _mm_nn(o_b, Wb, name="branch_b", out_dtype=BF16)
    merged = _merge_fwd(gates, b_gate, u_a, u_b, name="merge_fwd")
    x1 = _mm_nn(merged, Wout, name="out_proj", out_dtype=F32, res=x2d)
    h2 = _rms_fwd(x1, g_ffn, name="rms_ffn")
    gt, (Wd,) = _mm_nt(h2, Wg, name="ffn_gate", out_dtype=BF16, comm=_gather_comm([s_d]))
    Wd = Wd.reshape(1, N_DEV * PF, D)
    up, hid = _mm_nt(h2, Wu, name="ffn_up", out_dtype=BF16, epi=_swiglu_fwd_epi, extras=[gt], n_out=2)
    x2 = _mm_nn(hid, Wd, name="ffn_down", out_dtype=F32, res=x1, tk=2 * PF)
    dx2, dx2b, dg_final, loss_part = _final_loss(x2, tgt, g_final.reshape(1, D), name="final_loss")

    dgt, dup = _mm_nt(dx2b, Wd, name="d_hidden", out_dtype=BF16, epi=_swiglu_bwd_epi, extras=[gt, up], n_out=2)
    gWd = _mm_tn(hid, dx2b, 1, name="gw_down").reshape(N_DEV, PF, D)
    dh2, (sWd,) = _mm_nn(dgt, Wg, name="d_h2_gate", out_dtype=F32, tk=2 * PF, comm=_sibling_comm([gWd]))
    cWd = _chip_sum(gWd, sWd, core, name="chip_sum_w_down_ffn")
    dh2, (rWd,) = _mm_nn(dup, Wu, name="d_h2_up", out_dtype=F32, res=dh2, tk=2 * PF,
                         comm=_chips_comm([cWd], every))
    gWg = _mm_tn(dgt, h2, 1, name="gw_gate").reshape(N_DEV, PF, D)
    gWu = _mm_tn(dup, h2, 1, name="gw_up").reshape(N_DEV, PF, D)
    dx1, dx1b, dg_ffn = _rms_bwd(dh2, x1, g_ffn, dx2, name="rms_ffn_bwd")
    dmer, (sWg, sWu) = _mm_nt(dx1b, Wout, name="d_merged", out_dtype=BF16, comm=_sibling_comm([gWg, gWu]))
    cWg = _chip_sum(gWg, sWg, core, name="chip_sum_w_gate_ffn")
    cWu = _chip_sum(gWu, sWu, core, name="chip_sum_w_up_ffn")
    gWout = _mm_tn(merged, dx1b, 1, name="gw_out").reshape(N_DEV, d8, D)
    dua, dub, dgates, dbg = _merge_bwd(dmer, gates, b_gate, u_a, u_b, name="merge_bwd")
    g_gates = _mm_tn(h, dgates, 1, name="gw_gates")[0]
    doa = _mm_nt(dua, Wa, name="d_o_a", out_dtype=BF16)
    dob = _mm_nt(dub, Wb, name="d_o_b", out_dtype=BF16)
    gWa = _mm_tn(o_a, dua, N_DEV, name="gw_branch_a")
    gWb = _mm_tn(o_b, dub, N_DEV, name="gw_branch_b")

    (dqkv, dkap, dvap, dtab), (sWout, sWa, sWb) = _attn_a_bwd(
        qkv, kap, vap, btab, o_a, doa, lse_a, name="attn_a_bwd", comm=_sibling_comm([gWout, gWa, gWb]),
        dq_width=fcol)
    cWout = _chip_sum(gWout, sWout, core, name="chip_sum_w_out")
    cWa = _chip_sum(gWa, sWa, core, name="chip_sum_w_branch_a")
    cWb = _chip_sum(gWb, sWb, core, name="chip_sum_w_branch_b")
    skew = jnp.pad(dtab.reshape(HA, QA * KW), ((0, 0), (0, QA))).reshape(HA, QA, KW + 1)
    colsum, tail = _relbias_fold(skew, dtab, name="relbias_fold")
    rev = colsum[:, 0, ::-1]
    d_rel = jnp.concatenate([rev[:, :2 * MAX_REL], tail[:, 0, :1]], axis=1)

    (dqkv, dkb, dvb, dcq, dck), (rWg, rWout, rWa, rWb) = _fox_bwd(
        qkv, fox_cols, cum, cum_t, o_b, dob, lse_b, name="fox_bwd", comm=_chips_comm([cWg, cWout, cWa, cWb], every),
        dq_into=(dqkv, fox_cols[0]))
    dcq_p = jnp.pad(dcq.reshape(T, HB, HEAD_DIM)[:, :, 0], ((0, 0), (0, LANE - HB)))
    dck_p = jnp.pad(dck.reshape(HB, T).T, ((0, 0), (0, LANE - HB)))
    dflog, dbf = _fgate_bwd(dcq_p, dck_p, flog, bf_pad, name="forget_bwd")

    dqkv = _cast_into(dqkv, dkap, padk, WA, name="dqkv_k_a")
    dqkv = _cast_into(dqkv, dvap, padk, 2 * WA, name="dqkv_v_a")
    dqkv = _cast_into(dqkv, dkb, 0, 3 * WA + WB, name="dqkv_k_b")
    dqkv = _cast_into(dqkv, dvb, 0, 3 * WA + 2 * WB, name="dqkv_v_b")
    dfb = dflog.astype(BF16)
    g_qkv, (rWu,) = _mm_tn(h, dqkv, 1, name="gw_qkv", comm=_chips_comm([cWu], every))
    gWin = _scatter_g_in(g_qkv[0], _mm_tn(h, dfb, 1, name="gw_forget")[0], g_gates, lay, PIN)
    dh, (sWin,) = _mm_nt(dfb, Wf, name="d_h_forget", out_dtype=F32, comm=_sibling_comm([gWin]))
    cWin = _chip_sum(gWin, sWin, core, name="chip_sum_w_in")
    dh, (rWin01,) = _mm_nt(dqkv, Wqkv, name="d_h_qkv", out_dtype=F32, res=dh, comm=_chips_comm([cWin], [0, 1]))
    dh, (rWin2,) = _mm_nt(dgates, Wgt, name="d_h_gates", out_dtype=F32, res=dh, comm=_chips_comm([cWin], [2]))
    dx, _, dg_mix = _rms_bwd(dh, x2d, g_mix, dx1, name="rms_mix_bwd")

    def slots(r):
        return [(r, 0), (r, 1), (r, 2)]

    in_parts = [_unplace_block(p, lay, din8, blk)[None] for p in
                (lax.dynamic_index_in_dim(cWin, 2 * cx + cy, keepdims=False), rWin01[0], rWin01[1], rWin2[0])]
    first = jnp.zeros((1,), jnp.int32)
    big = {}
    for nm, s1, others, which, w, m, v in [
            ("w_down_ffn", cWd, slots(rWd), chip, w_down_ffn, m_w_down_ffn, v_w_down_ffn),
            ("w_gate_ffn", cWg, slots(rWg), chip, w_gate_ffn, m_w_gate_ffn, v_w_gate_ffn),
            ("w_out", cWout, slots(rWout), chip, w_out, m_w_out, v_w_out),
            ("w_branch_a", cWa, slots(rWa), chip, w_branch_a, m_w_branch_a, v_w_branch_a),
            ("w_branch_b", cWb, slots(rWb), chip, w_branch_b, m_w_branch_b, v_w_branch_b),
            ("w_up_ffn", cWu, slots(rWu), chip, w_up_ffn, m_w_up_ffn, v_w_up_ffn),
            ("w_in", in_parts[0], [(p, 0) for p in in_parts[1:]], first, w_in, m_w_in, v_w_in)]:
        flipped = nm in ("w_gate_ffn", "w_up_ffn")
        w2, m2, v2 = ((a[0].T if flipped else a[0]) for a in (w, m, v))
        outs = _adamw_shard(s1, others, which, w2, m2, v2, name="adamw_" + nm)
        big[nm] = [(o.T if flipped else o)[None] for o in outs]

    small_names = ["g_mix", "b_f", "b_gate", "rel_bias", "g_ffn", "g_final"]
    small_w = [g_mix, b_f, b_gate, rel_bias, g_ffn, g_final]
    small_m = [m_g_mix, m_b_f, m_b_gate, m_rel_bias, m_g_ffn, m_g_final]
    small_v = [v_g_mix, v_b_f, v_b_gate, v_rel_bias, v_g_ffn, v_g_final]
    small_g = [dg_mix, dbf[:, :HB], dbg, d_rel[None], dg_ffn, dg_final[0]]
    shapes = [w.shape for w in small_w]
    n_small = sum(w.size for w in small_w) + 1
    rows = _pad_to(-(-n_small // LANE), 8)
    total = _all_reduce_small(_pack(small_g + [loss_part[:1, :1]], rows), name="all_reduce_small")
    d_s, m_s, v_s = _adamw_small(_pack(small_w, rows), total, _pack(small_m, rows), _pack(small_v, rows),
                                 name="adamw_small")
    g_small = dict(zip(small_names, _unpack(total, shapes)))
    d_small = dict(zip(small_names, _unpack(d_s, shapes)))
    m_small = dict(zip(small_names, _unpack(m_s, shapes)))
    v_small = dict(zip(small_names, _unpack(v_s, shapes)))
    loss = total.reshape(-1)[n_small - 1]

    order = ["g_mix", "w_in", "b_f", "b_gate", "rel_bias", "w_branch_a", "w_branch_b", "w_out", "g_ffn",
             "w_gate_ffn", "w_up_ffn", "w_down_ffn", "g_final"]

    def pick(idx, small):
        return [big[nm][idx] if nm in big else small[nm] for nm in order]

    return (loss, dx[None], *pick(0, g_small), *pick(1, d_small), *pick(2, m_small), *pick(3, v_small))
```

```python
import functools

import jax
import jax.numpy as jnp
from jax import lax
from jax.experimental import pallas as pl
from jax.experimental.pallas import tpu as pltpu

F32 = jnp.float32
BF16 = jnp.bfloat16
MESH = pl.DeviceIdType.MESH

N_DEV = 8
LANE = 128
HEAD_DIM = 128
CHUNK = 64
LEFT_CHUNKS = 8
MAX_REL = 256
RMS_EPS = 1e-6
NEG_INF = -1e30
ATTN_SCALE = HEAD_DIM ** -0.5
VMEM_LIMIT_BYTES = 56 * 1024 * 1024

ADAM_LR = 0.001
ADAM_B1 = 0.9
ADAM_B2 = 0.999
ADAM_EPS = 1e-08
ADAM_WD = 0.01
ADAM_STEP = 10

NT_DIMS = (((1,), (1,)), ((), ()))
TN_DIMS = (((0,), (0,)), ((), ()))


def _cp(*sem):
    return pltpu.CompilerParams(dimension_semantics=sem if sem else None, vmem_limit_bytes=VMEM_LIMIT_BYTES)


def _pad_to(n, mult):
    return -(-n // mult) * mult


def _tile(dim, pref, align):
    t = (min(pref, dim) // align) * align
    while t >= align:
        if dim % t == 0:
            return t
        t -= align
    return dim


def _sigmoid(z):
    return 1.0 / (1.0 + jnp.exp(-z))


class _Comm:
    def __init__(self, ins, outs, scratch, begin, end, relays=()):
        self.ins, self.outs, self.scratch = list(ins), list(outs), list(scratch)
        self.begin, self.end, self.relays = begin, end, list(relays)


def _call(body, args, *, name, grid, in_specs, out_specs, out_shape, scratch=(), sem=(), comm=None, aliases=None):
    single = not isinstance(out_shape, (list, tuple))
    out_shape = [out_shape] if single else list(out_shape)
    out_specs = [out_specs] if single else list(out_specs)
    aliases = dict(aliases or {})
    if comm is None:
        res = pl.pallas_call(
            body, name=name, grid=grid, in_specs=list(in_specs), out_specs=out_specs, out_shape=out_shape,
            scratch_shapes=list(scratch), input_output_aliases=aliases, compiler_params=_cp(*sem))(*args)
        return res[0] if single else res
    ni, no, ns = len(args), len(out_shape), len(scratch)
    ci, co = len(comm.ins), len(comm.outs)
    total = 1
    for g in grid:
        total *= g

    def hosted(*refs):
        a, cin = refs[:ni], refs[ni:ni + ci]
        o, cout = refs[ni + ci:ni + ci + no], refs[ni + ci + no:ni + ci + no + co]
        s, csc = refs[ni + ci + no + co:ni + ci + no + co + ns], refs[ni + ci + no + co + ns:]
        step = pl.program_id(0)
        for d in range(1, len(grid)):
            step = step * grid[d] + pl.program_id(d)

        @pl.when(step == 0)
        def _():
            comm.begin(cin, cout, csc)

        for fraction, relay in comm.relays:
            @pl.when(step == min(int(total * fraction), total - 1))
            def _(relay=relay):
                relay(cin, cout, csc)

        body(*a, *o, *s)

        @pl.when(step == total - 1)
        def _():
            comm.end(cin, cout, csc)

    res = pl.pallas_call(
        hosted, name=name, grid=grid, in_specs=list(in_specs) + _any_specs(ci),
        out_specs=out_specs + _any_specs(co), out_shape=out_shape + comm.outs,
        scratch_shapes=list(scratch) + comm.scratch, input_output_aliases=aliases,
        compiler_params=_cp(*(("arbitrary",) * len(grid))))(*args, *comm.ins)
    return (res[0] if single else res[:no]), res[no:]


def _any_specs(n):
    return [pl.BlockSpec(memory_space=pl.ANY)] * n


def _mm_body(dims, nk, has_res, n_ext, n_out, epi):
    def body(*refs):
        a_ref, w_ref = refs[:2]
        at = 2 + has_res
        r_ref = refs[2] if has_res else None
        e_refs, o_refs = refs[at:at + n_ext], refs[at + n_ext:at + n_ext + n_out]

        def emit(v):
            if r_ref is not None:
                v = v + r_ref[...]
            vals = (v,) if epi is None else epi(v, *[e[...] for e in e_refs])
            for o_ref, val in zip(o_refs, vals):
                o_ref[...] = val.astype(o_ref.dtype)

        if nk == 1:
            emit(lax.dot_general(a_ref[...], w_ref[...], dims, preferred_element_type=F32))
            return
        acc = refs[-1]
        k = pl.program_id(2)

        @pl.when(k == 0)
        def _():
            acc[...] = jnp.zeros_like(acc)

        acc[...] += lax.dot_general(a_ref[...], w_ref[...], dims, preferred_element_type=F32)

        @pl.when(k == nk - 1)
        def _():
            emit(acc[...])

    return body


def _mm_nn(x, w, *, name, out_dtype, res=None, tm=1024, tn=1024, tk=None, comm=None, epi=None, extras=(), n_out=1):
    M, K = x.shape
    NB, _, NW = w.shape
    tm, tn = _tile(M, tm, 16), _tile(NW, tn, LANE)
    tk = K if tk is None else _tile(K, tk, LANE)
    nj, nk = NW // tn, K // tk
    tile = pl.BlockSpec((tm, tn), lambda j, i, k: (i, j))
    in_specs = [pl.BlockSpec((tm, tk), lambda j, i, k: (i, k)),
                pl.BlockSpec((None, tk, tn), lambda j, i, k: (j // nj, k, j % nj))]
    args = [x, w] + ([res] if res is not None else []) + list(extras)
    in_specs += [tile] * (len(args) - 2)
    out = jax.ShapeDtypeStruct((M, NB * NW), out_dtype)
    return _call(
        _mm_body((((1,), (0,)), ((), ())), nk, res is not None, len(extras), n_out, epi), args,
        name=name, grid=(NB * nj, M // tm, nk), in_specs=in_specs,
        out_specs=tile if n_out == 1 else [tile] * n_out, out_shape=out if n_out == 1 else [out] * n_out,
        scratch=[pltpu.VMEM((tm, tn), F32)] if nk > 1 else [],
        sem=("parallel", "parallel", "arbitrary"), comm=comm)


def _mm_nt(dy, w, *, name, out_dtype, res=None, tm=1024, tn=1024, tc=2816, comm=None, epi=None, extras=(), n_out=1):
    M, N = dy.shape
    NB, Kw, NW = w.shape
    assert N == NB * NW
    tm, tn, tc = _tile(M, tm, 16), _tile(Kw, tn, LANE), _tile(NW, tc, LANE)
    nc = NW // tc
    nk = NB * nc
    tile = pl.BlockSpec((tm, tn), lambda i, j, k: (i, j))
    in_specs = [pl.BlockSpec((tm, tc), lambda i, j, k: (i, k)),
                pl.BlockSpec((None, tn, tc), lambda i, j, k: (k // nc, j, k % nc))]
    args = [dy, w] + ([res] if res is not None else []) + list(extras)
    in_specs += [tile] * (len(args) - 2)
    out = jax.ShapeDtypeStruct((M, Kw), out_dtype)
    return _call(
        _mm_body(NT_DIMS, nk, res is not None, len(extras), n_out, epi), args,
        name=name, grid=(M // tm, Kw // tn, nk), in_specs=in_specs,
        out_specs=tile if n_out == 1 else [tile] * n_out, out_shape=out if n_out == 1 else [out] * n_out,
        scratch=[pltpu.VMEM((tm, tn), F32)] if nk > 1 else [],
        sem=("parallel", "parallel", "arbitrary"), comm=comm)


def _mm_tn(x, dy, nb, *, name, tm=512, tn=1024, comm=None):
    M, Kw = x.shape
    NW = dy.shape[1] // nb
    tm, tn = _tile(Kw, tm, LANE), _tile(NW, tn, LANE)
    nj = NW // tn

    def body(x_ref, d_ref, o_ref):
        o_ref[...] = lax.dot_general(x_ref[...], d_ref[...], TN_DIMS, preferred_element_type=F32).astype(o_ref.dtype)

    return _call(
        body, [x, dy], name=name, grid=(nb * nj, Kw // tm),
        in_specs=[pl.BlockSpec((M, tm), lambda j, i: (0, i)),
                  pl.BlockSpec((M, tn), lambda j, i: (0, j))],
        out_specs=pl.BlockSpec((None, tm, tn), lambda j, i: (j // nj, i, j % nj)),
        out_shape=jax.ShapeDtypeStruct((nb, Kw, NW), BF16),
        sem=("parallel", "parallel"), comm=comm)


def _rms_fwd(x, g, *, name, tr=256):
    T, D = x.shape
    tr = _tile(T, tr, 16)

    def body(x_ref, g_ref, h_ref):
        xv = x_ref[...]
        r = lax.rsqrt(jnp.mean(xv * xv, axis=-1, keepdims=True) + RMS_EPS)
        h_ref[...] = ((xv * r) * g_ref[...]).astype(BF16)

    return pl.pallas_call(
        body, name=name, grid=(T // tr,),
        in_specs=[pl.BlockSpec((tr, D), lambda i: (i, 0)), pl.BlockSpec((1, D), lambda i: (0, 0))],
        out_specs=pl.BlockSpec((tr, D), lambda i: (i, 0)),
        out_shape=jax.ShapeDtypeStruct((T, D), BF16),
        compiler_params=_cp("parallel"))(x, g)


def _rms_bwd(dh, x, g, dres, *, name, tr=128, with_bf16=True):
    T, D = x.shape
    tr = _tile(T, tr, 16)

    def body(dh_ref, x_ref, g_ref, dr_ref, dx_ref, *rest):
        dg_ref = rest[-1]
        i = pl.program_id(0)
        xv = x_ref[...]
        dhv = dh_ref[...]
        r = lax.rsqrt(jnp.mean(xv * xv, axis=-1, keepdims=True) + RMS_EPS)
        a = dhv * g_ref[...]
        dot = jnp.mean(a * xv, axis=-1, keepdims=True)
        dx = dr_ref[...] + r * a - xv * (r * r * r * dot)
        dx_ref[...] = dx
        if with_bf16:
            rest[0][...] = dx.astype(BF16)
        part = jnp.sum(dhv * (xv * r), axis=0, keepdims=True)

        @pl.when(i == 0)
        def _():
            dg_ref[...] = part

        @pl.when(i > 0)
        def _():
            dg_ref[...] += part

    row = pl.BlockSpec((tr, D), lambda i: (i, 0))
    vec = pl.BlockSpec((1, D), lambda i: (0, 0))
    halves = [jax.ShapeDtypeStruct((T, D), BF16)] if with_bf16 else []
    return pl.pallas_call(
        body, name=name, grid=(T // tr,),
        in_specs=[row, row, vec, row], out_specs=[row] + [row] * len(halves) + [vec],
        out_shape=[jax.ShapeDtypeStruct((T, D), F32)] + halves + [jax.ShapeDtypeStruct((1, D), F32)],
        compiler_params=_cp("arbitrary"))(dh, x, g, dres)


def _final_loss(x2, tgt, g, *, name, tr=128):
    T, D = x2.shape
    tr = _tile(T, tr, 16)

    def body(x_ref, t_ref, g_ref, dx_ref, dxb_ref, dg_ref, loss_ref):
        i = pl.program_id(0)
        xv = x_ref[...]
        gv = g_ref[...]
        r = lax.rsqrt(jnp.mean(xv * xv, axis=-1, keepdims=True) + RMS_EPS)
        xh = xv * r
        diff = xh * gv - t_ref[...]
        lpart = 0.5 * jnp.sum(jnp.mean(diff * diff, axis=-1, keepdims=True))
        dy = diff * (1.0 / D)
        a = dy * gv
        dot = jnp.mean(a * xv, axis=-1, keepdims=True)
        dx = r * a - xv * (r * r * r * dot)
        dx_ref[...] = dx
        dxb_ref[...] = dx.astype(BF16)
        part = jnp.sum(dy * xh, axis=0, keepdims=True)
        lblk = jnp.full((8, LANE), lpart, F32)

        @pl.when(i == 0)
        def _():
            dg_ref[...] = part
            loss_ref[...] = lblk

        @pl.when(i > 0)
        def _():
            dg_ref[...] += part
            loss_ref[...] += lblk

    row = pl.BlockSpec((tr, D), lambda i: (i, 0))
    vec = pl.BlockSpec((1, D), lambda i: (0, 0))
    return pl.pallas_call(
        body, name=name, grid=(T // tr,),
        in_specs=[row, row, vec],
        out_specs=[row, row, vec, pl.BlockSpec((8, LANE), lambda i: (0, 0))],
        out_shape=[jax.ShapeDtypeStruct((T, D), F32), jax.ShapeDtypeStruct((T, D), BF16),
                   jax.ShapeDtypeStruct((1, D), F32), jax.ShapeDtypeStruct((8, LANE), F32)],
        compiler_params=_cp("arbitrary"))(x2, tgt, g)


def _merge_fwd(gates, b_gate, ua, ub, *, name, tr=256):
    T, D = ua.shape
    tr = _tile(T, tr, 16)

    def body(ga_ref, gb_ref, ba_ref, bb_ref, ua_ref, ub_ref, o_ref):
        sa = _sigmoid(ga_ref[...].astype(F32) + ba_ref[...])
        sb = _sigmoid(gb_ref[...].astype(F32) + bb_ref[...])
        o_ref[...] = (sa * ua_ref[...].astype(F32) + sb * ub_ref[...].astype(F32)).astype(BF16)

    row = pl.BlockSpec((tr, D), lambda i: (i, 0))
    return pl.pallas_call(
        body, name=name, grid=(T // tr,),
        in_specs=[row, pl.BlockSpec((tr, D), lambda i: (i, 1)),
                  pl.BlockSpec((1, D), lambda i: (0, 0)), pl.BlockSpec((1, D), lambda i: (0, 1)), row, row],
        out_specs=row, out_shape=jax.ShapeDtypeStruct((T, D), BF16),
        compiler_params=_cp("parallel"))(gates, gates, b_gate, b_gate, ua, ub)


def _merge_bwd(dm, gates, b_gate, ua, ub, *, name, tr=256):
    T, D = ua.shape
    tr = _tile(T, tr, 16)

    def body(dm_ref, ga_ref, gb_ref, ba_ref, bb_ref, ua_ref, ub_ref, dua_ref, dub_ref, dg_ref, db_ref):
        i = pl.program_id(0)
        dmv = dm_ref[...].astype(F32)
        sa = _sigmoid(ga_ref[...].astype(F32) + ba_ref[...])
        sb = _sigmoid(gb_ref[...].astype(F32) + bb_ref[...])
        dua_ref[...] = (dmv * sa).astype(BF16)
        dub_ref[...] = (dmv * sb).astype(BF16)
        dga = dmv * ua_ref[...].astype(F32) * (sa * (1.0 - sa))
        dgb = dmv * ub_ref[...].astype(F32) * (sb * (1.0 - sb))
        dg_ref[:, :D] = dga.astype(BF16)
        dg_ref[:, D:] = dgb.astype(BF16)
        pa = jnp.sum(dga, axis=0, keepdims=True)
        pb = jnp.sum(dgb, axis=0, keepdims=True)

        @pl.when(i == 0)
        def _():
            db_ref[:, :D] = pa
            db_ref[:, D:] = pb

        @pl.when(i > 0)
        def _():
            db_ref[:, :D] += pa
            db_ref[:, D:] += pb

    row = pl.BlockSpec((tr, D), lambda i: (i, 0))
    vec = pl.BlockSpec((1, D), lambda i: (0, 0))
    act = jax.ShapeDtypeStruct((T, D), BF16)
    return pl.pallas_call(
        body, name=name, grid=(T // tr,),
        in_specs=[row, row, pl.BlockSpec((tr, D), lambda i: (i, 1)),
                  vec, pl.BlockSpec((1, D), lambda i: (0, 1)), row, row],
        out_specs=[row, row, pl.BlockSpec((tr, 2 * D), lambda i: (i, 0)), pl.BlockSpec((1, 2 * D), lambda i: (0, 0))],
        out_shape=[act, act, jax.ShapeDtypeStruct((T, 2 * D), BF16), jax.ShapeDtypeStruct((1, 2 * D), F32)],
        compiler_params=_cp("arbitrary"))(dm, gates, gates, b_gate, b_gate, ua, ub)


def _swiglu_fwd_epi(up, gate):
    g = gate.astype(F32)
    up = up.astype(BF16).astype(F32)
    return up, g * _sigmoid(g) * up


def _swiglu_bwd_epi(d, gate, up):
    g = gate.astype(F32)
    s = _sigmoid(g)
    return d * up.astype(F32) * (s * (1.0 + g * (1.0 - s))), d * (g * s)


def _cast_into(buf, src, row_off, col_off, *, name, tr=256):
    T = buf.shape[0]
    W = src.shape[1]
    tr = _tile(T, tr, 16)
    cw = W
    while col_off % cw or W % cw:
        cw -= LANE
    assert row_off % tr == 0
    c0, r0 = col_off // cw, row_off // tr

    def body(s_ref, b_ref, o_ref):
        o_ref[...] = s_ref[...].astype(o_ref.dtype)

    return pl.pallas_call(
        body, name=name, grid=(T // tr, W // cw),
        in_specs=[pl.BlockSpec((tr, cw), lambda i, j: (i + r0, j)), pl.BlockSpec(memory_space=pl.ANY)],
        out_specs=pl.BlockSpec((tr, cw), lambda i, j: (i, c0 + j)),
        out_shape=jax.ShapeDtypeStruct(buf.shape, buf.dtype), input_output_aliases={1: 0},
        compiler_params=_cp("parallel", "parallel"))(src, buf)


def _bias_table(rel, qb, kw):
    assert qb - 1 <= MAX_REL
    ha = rel.shape[0]
    n_clipped = kw - MAX_REL
    e = jnp.concatenate([jnp.broadcast_to(rel[:, 2 * MAX_REL:], (ha, n_clipped)),
                         rel[:, 2 * MAX_REL - 1:MAX_REL - qb:-1]], axis=1)
    period = qb + kw
    e = jnp.pad(e, ((0, 0), (0, period - e.shape[1])))
    z = jnp.broadcast_to(e[:, None, :], (ha, qb, period)).reshape(ha, qb * period)
    z = z[:, :qb * (period - 1)].reshape(ha, qb, period - 1)
    i = jnp.arange(qb)[:, None]
    j = jnp.arange(kw)[None, :]
    band = (j // CHUNK >= i // CHUNK) & (j // CHUNK <= i // CHUNK + LEFT_CHUNKS)
    return jnp.where(band[None], z[:, :, qb - 1:qb - 1 + kw], NEG_INF)


def _heads_per_step(n_heads):
    return 2 if n_heads % 2 == 0 else 1


def _attn_a_fwd(q, kp, vp, btab, *, name):
    T, WA = q.shape[0], kp.shape[1]
    HA = WA // HEAD_DIM
    _, QB, KW = btab.shape
    padk = KW - QB
    G = _heads_per_step(HA)
    GW = G * HEAD_DIM

    def body(q_ref, k_ref, v_ref, b_ref, o_ref, lse_ref):
        i = pl.program_id(1)
        start = pl.multiple_of(i * QB, QB)
        kpos = start - padk + lax.broadcasted_iota(jnp.int32, (QB, KW), 1)
        for g in range(G):
            lanes = pl.ds(g * HEAD_DIM, HEAD_DIM)
            k = k_ref[pl.ds(start, KW), lanes]
            v = v_ref[pl.ds(start, KW), lanes]
            s = lax.dot_general(q_ref[:, lanes], k, NT_DIMS, preferred_element_type=F32) * ATTN_SCALE + b_ref[g]
            s = jnp.where(kpos >= 0, s, NEG_INF)
            m = jnp.max(s, axis=-1, keepdims=True)
            p = jnp.exp(s - m)
            l = jnp.sum(p, axis=-1, keepdims=True)
            o = jnp.dot(p.astype(BF16), v, preferred_element_type=F32) / l
            o_ref[:, lanes] = o.astype(BF16)
            lse_ref[:, lanes] = jnp.broadcast_to(m + jnp.log(l), (QB, HEAD_DIM))

    qspec = pl.BlockSpec((QB, GW), lambda h, i: (i, h))
    kspec = pl.BlockSpec((T + padk, GW), lambda h, i: (0, h))
    return pl.pallas_call(
        body, name=name, grid=(HA // G, T // QB),
        in_specs=[qspec, kspec, kspec, pl.BlockSpec((G, QB, KW), lambda h, i: (h, 0, 0))],
        out_specs=[qspec, qspec],
        out_shape=[jax.ShapeDtypeStruct((T, WA), BF16), jax.ShapeDtypeStruct((T, WA), F32)],
        compiler_params=_cp("parallel", "arbitrary"))(q, kp, vp, btab)


def _attn_a_bwd(q, kp, vp, btab, o, do, lse, *, name, comm=None, dq_width=None):
    T, WA = q.shape[0], kp.shape[1]
    HA = WA // HEAD_DIM
    _, QB, KW = btab.shape
    padk = KW - QB
    G = _heads_per_step(HA)
    GW = G * HEAD_DIM

    def body(q_ref, k_ref, v_ref, b_ref, o_ref, do_ref, lse_ref, dq_ref, dk_ref, dv_ref, db_ref):
        i = pl.program_id(1)

        @pl.when(i == 0)
        def _():
            dk_ref[...] = jnp.zeros_like(dk_ref)
            dv_ref[...] = jnp.zeros_like(dv_ref)
            db_ref[...] = jnp.zeros_like(db_ref)

        start = pl.multiple_of(i * QB, QB)
        kpos = start - padk + lax.broadcasted_iota(jnp.int32, (QB, KW), 1)
        for g in range(G):
            lanes = pl.ds(g * HEAD_DIM, HEAD_DIM)
            qv = q_ref[:, lanes]
            dov = do_ref[:, lanes]
            k = k_ref[pl.ds(start, KW), lanes]
            v = v_ref[pl.ds(start, KW), lanes]
            s = lax.dot_general(qv, k, NT_DIMS, preferred_element_type=F32) * ATTN_SCALE + b_ref[g]
            s = jnp.where(kpos >= 0, s, NEG_INF)
            p = jnp.exp(s - lse_ref[:, pl.ds(g * HEAD_DIM, 1)])
            dp = lax.dot_general(dov, v, NT_DIMS, preferred_element_type=F32)
            delta = jnp.sum(dov.astype(F32) * o_ref[:, lanes].astype(F32), axis=-1, keepdims=True)
            ds = p * (dp - delta)
            db_ref[g] += ds
            dsb = ds.astype(BF16)
            dq_ref[:, lanes] = (jnp.dot(dsb, k, preferred_element_type=F32) * ATTN_SCALE).astype(BF16)
            dk_ref[pl.ds(start, KW), lanes] += (
                lax.dot_general(dsb, qv, TN_DIMS, preferred_element_type=F32) * ATTN_SCALE)
            dv_ref[pl.ds(start, KW), lanes] += lax.dot_general(
                p.astype(BF16), dov, TN_DIMS, preferred_element_type=F32)

    qspec = pl.BlockSpec((QB, GW), lambda h, i: (i, h))
    kspec = pl.BlockSpec((T + padk, GW), lambda h, i: (0, h))
    bspec = pl.BlockSpec((G, QB, KW), lambda h, i: (h, 0, 0))
    return _call(
        body, [q, kp, vp, btab, o, do, lse], name=name, grid=(HA // G, T // QB),
        in_specs=[qspec, kspec, kspec, bspec, qspec, qspec, qspec],
        out_specs=[qspec, kspec, kspec, bspec],
        out_shape=[jax.ShapeDtypeStruct((T, dq_width or WA), BF16), jax.ShapeDtypeStruct((T + padk, WA), F32),
                   jax.ShapeDtypeStruct((T + padk, WA), F32), jax.ShapeDtypeStruct((HA, QB, KW), F32)],
        sem=("parallel", "arbitrary"), comm=comm)


def _relbias_fold(skew, dtab, *, name):
    HA, qb, kw = dtab.shape
    W = kw + 1

    def body(s_ref, t_ref, col_ref, tail_ref):
        row = lax.broadcasted_iota(jnp.int32, (qb, W), 0)
        col = lax.broadcasted_iota(jnp.int32, (qb, W), 1)
        col_ref[...] = jnp.sum(jnp.where(row + col < kw, s_ref[...], 0.0), axis=0, keepdims=True)
        i = lax.broadcasted_iota(jnp.int32, (qb, kw), 0)
        j = lax.broadcasted_iota(jnp.int32, (qb, kw), 1)
        tail = jnp.sum(jnp.where((kw - qb) + i - j >= MAX_REL, t_ref[...], 0.0))
        tail_ref[...] = jnp.full((1, LANE), tail, F32)

    return pl.pallas_call(
        body, name=name, grid=(HA,),
        in_specs=[pl.BlockSpec((None, qb, W), lambda h: (h, 0, 0)), pl.BlockSpec((None, qb, kw), lambda h: (h, 0, 0))],
        out_specs=[pl.BlockSpec((None, 1, W), lambda h: (h, 0, 0)), pl.BlockSpec((None, 1, LANE), lambda h: (h, 0, 0))],
        out_shape=[jax.ShapeDtypeStruct((HA, 1, W), F32), jax.ShapeDtypeStruct((HA, 1, LANE), F32)],
        compiler_params=_cp("parallel"))(skew, dtab)


CUM_BLOCK = 128


def _fgate_fwd(f, bf, *, name):
    T = f.shape[0]
    nb = T // CUM_BLOCK

    def body(f_ref, b_ref, cum_ref):
        r = lax.broadcasted_iota(jnp.int32, (CUM_BLOCK, CUM_BLOCK), 0)
        c = lax.broadcasted_iota(jnp.int32, (CUM_BLOCK, CUM_BLOCK), 1)
        tri = (c <= r).astype(F32)

        def step(b, carry):
            r0 = pl.multiple_of(b * CUM_BLOCK, CUM_BLOCK)
            z = f_ref[pl.ds(r0, CUM_BLOCK), :] + b_ref[...]
            lf = jnp.minimum(z, 0.0) - jnp.log(1.0 + jnp.exp(-jnp.abs(z)))
            cs = jnp.dot(tri, lf, precision=lax.Precision.HIGHEST, preferred_element_type=F32) + carry
            cum_ref[pl.ds(r0, CUM_BLOCK), :] = cs
            return carry + jnp.sum(lf, axis=0, keepdims=True)

        lax.fori_loop(0, nb, step, jnp.zeros((1, LANE), F32))

    return pl.pallas_call(
        body, name=name, out_shape=jax.ShapeDtypeStruct((T, LANE), F32),
        compiler_params=_cp())(f, bf)


def _fgate_bwd(dcq, dck, f, bf, *, name):
    T = f.shape[0]
    nb = T // CUM_BLOCK

    def body(dq_ref, dk_ref, f_ref, b_ref, df_ref, dbf_ref):
        r = lax.broadcasted_iota(jnp.int32, (CUM_BLOCK, CUM_BLOCK), 0)
        c = lax.broadcasted_iota(jnp.int32, (CUM_BLOCK, CUM_BLOCK), 1)
        tri = (c >= r).astype(F32)

        def step(n, carry):
            tail, dbf = carry
            r0 = pl.multiple_of((nb - 1 - n) * CUM_BLOCK, CUM_BLOCK)
            dc = dq_ref[pl.ds(r0, CUM_BLOCK), :] + dk_ref[pl.ds(r0, CUM_BLOCK), :]
            ss = jnp.dot(tri, dc, precision=lax.Precision.HIGHEST, preferred_element_type=F32) + tail
            z = f_ref[pl.ds(r0, CUM_BLOCK), :] + b_ref[...]
            df = ss * _sigmoid(-z)
            df_ref[pl.ds(r0, CUM_BLOCK), :] = df
            return tail + jnp.sum(dc, axis=0, keepdims=True), dbf + jnp.sum(df, axis=0, keepdims=True)

        zero = jnp.zeros((1, LANE), F32)
        _, dbf = lax.fori_loop(0, nb, step, (zero, zero))
        dbf_ref[...] = dbf

    return pl.pallas_call(
        body, name=name,
        out_shape=[jax.ShapeDtypeStruct((T, LANE), F32), jax.ShapeDtypeStruct((1, LANE), F32)],
        compiler_params=_cp())(dcq, dck, f, bf)


def _fox_logits(qv, kk, cq, ck, diagonal):
    s = lax.dot_general(qv, kk, NT_DIMS, preferred_element_type=F32) * ATTN_SCALE + cq - ck
    if diagonal:
        row = lax.broadcasted_iota(jnp.int32, s.shape, 0)
        col = lax.broadcasted_iota(jnp.int32, s.shape, 1)
        s = jnp.where(col <= row, s, NEG_INF)
    return s


def _fox_fwd(qkv, cols, cum, cum_t, *, name, comm=None):
    T = qkv.shape[0]
    HB, _, KB = cum_t.shape
    WB = HB * HEAD_DIM
    QB = KB
    qc, kc, vc = cols

    def body(q_ref, k_ref, v_ref, c_ref, ct_ref, o_ref, lse_ref):
        h = pl.program_id(0)
        i = pl.program_id(1)
        qv = q_ref[...]
        lane = lax.broadcasted_iota(jnp.int32, (QB, LANE), 1)
        cq = jnp.sum(jnp.where(lane == h, c_ref[...], 0.0), axis=-1, keepdims=True)

        def block(kb, carry, diagonal):
            m, l, acc = carry
            k0 = pl.multiple_of(kb * KB, KB)
            s = _fox_logits(qv, k_ref[pl.ds(k0, KB), :], cq, ct_ref[pl.ds(kb, 1), :], diagonal)
            m_new = jnp.maximum(m, jnp.max(s, axis=-1, keepdims=True))
            alpha = jnp.exp(m - m_new)
            p = jnp.exp(s - m_new)
            l = alpha * l + jnp.sum(p, axis=-1, keepdims=True)
            acc = alpha * acc + jnp.dot(p.astype(BF16), v_ref[pl.ds(k0, KB), :], preferred_element_type=F32)
            return m_new, l, acc

        past = lax.fori_loop(
            0, i, lambda kb, carry: block(kb, carry, False),
            (jnp.full((QB, 1), NEG_INF, F32), jnp.zeros((QB, 1), F32), jnp.zeros((QB, HEAD_DIM), F32)))
        m, l, acc = block(i, past, True)
        o_ref[...] = (acc / l).astype(BF16)
        lse_ref[...] = jnp.broadcast_to(m + jnp.log(l), (QB, HEAD_DIM))

    ospec = pl.BlockSpec((QB, HEAD_DIM), lambda h, i: (i, h))
    return _call(
        body, [qkv, qkv, qkv, cum, cum_t], name=name, grid=(HB, T // QB),
        in_specs=[pl.BlockSpec((QB, HEAD_DIM), lambda h, i: (i, qc + h)),
                  pl.BlockSpec((T, HEAD_DIM), lambda h, i: (0, kc + h)),
                  pl.BlockSpec((T, HEAD_DIM), lambda h, i: (0, vc + h)),
                  pl.BlockSpec((QB, LANE), lambda h, i: (i, 0)),
                  pl.BlockSpec((None, T // KB, KB), lambda h, i: (h, 0, 0))],
        out_specs=[ospec, ospec],
        out_shape=[jax.ShapeDtypeStruct((T, WB), BF16), jax.ShapeDtypeStruct((T, WB), F32)],
        sem=("parallel", "arbitrary"), comm=comm)


def _fox_bwd(qkv, cols, cum, cum_t, o, do, lse, *, name, comm=None, dq_into=None):
    T = qkv.shape[0]
    HB, _, KB = cum_t.shape
    WB = HB * HEAD_DIM
    QB = KB
    qc, kc, vc = cols

    def body(*refs):
        q_ref, k_ref, v_ref, c_ref, ct_ref, o_ref, do_ref, lse_ref = refs[:8]
        dq_ref, dk_ref, dv_ref, dcq_ref, dck_ref = refs[-5:]
        h = pl.program_id(0)
        i = pl.program_id(1)

        @pl.when(i == 0)
        def _():
            dk_ref[...] = jnp.zeros_like(dk_ref)
            dv_ref[...] = jnp.zeros_like(dv_ref)
            dck_ref[...] = jnp.zeros_like(dck_ref)

        qv = q_ref[...]
        dov = do_ref[...]
        lse_v = lse_ref[:, :1]
        delta = jnp.sum(dov.astype(F32) * o_ref[...].astype(F32), axis=-1, keepdims=True)
        lane = lax.broadcasted_iota(jnp.int32, (QB, LANE), 1)
        cq = jnp.sum(jnp.where(lane == h, c_ref[...], 0.0), axis=-1, keepdims=True)

        def block(kb, carry, diagonal):
            dq, dcq = carry
            k0 = pl.multiple_of(kb * KB, KB)
            kk = k_ref[pl.ds(k0, KB), :]
            s = _fox_logits(qv, kk, cq, ct_ref[pl.ds(kb, 1), :], diagonal)
            p = jnp.exp(s - lse_v)
            dp = lax.dot_general(dov, v_ref[pl.ds(k0, KB), :], NT_DIMS, preferred_element_type=F32)
            ds = p * (dp - delta)
            dsb = ds.astype(BF16)
            dk_ref[pl.ds(k0, KB), :] += lax.dot_general(dsb, qv, TN_DIMS, preferred_element_type=F32) * ATTN_SCALE
            dv_ref[pl.ds(k0, KB), :] += lax.dot_general(p.astype(BF16), dov, TN_DIMS, preferred_element_type=F32)
            dck_ref[pl.ds(kb, 1), :] -= jnp.sum(ds, axis=0, keepdims=True)
            return (dq + jnp.dot(dsb, kk, preferred_element_type=F32),
                    dcq + jnp.sum(ds, axis=-1, keepdims=True))

        past = lax.fori_loop(0, i, lambda kb, carry: block(kb, carry, False),
                             (jnp.zeros((QB, HEAD_DIM), F32), jnp.zeros((QB, 1), F32)))
        dq, dcq = block(i, past, True)
        dq_ref[...] = (dq * ATTN_SCALE).astype(BF16)
        dcq_ref[...] = jnp.broadcast_to(dcq, (QB, HEAD_DIM))

    qspec = pl.BlockSpec((QB, HEAD_DIM), lambda h, i: (i, h))
    kspec = pl.BlockSpec((T, HEAD_DIM), lambda h, i: (0, h))
    cspec = pl.BlockSpec((QB, LANE), lambda h, i: (i, 0))
    tspec = pl.BlockSpec((None, T // KB, KB), lambda h, i: (h, 0, 0))
    wide = jax.ShapeDtypeStruct((T, WB), F32)
    args = [qkv, qkv, qkv, cum, cum_t, o, do, lse]
    in_specs = [pl.BlockSpec((QB, HEAD_DIM), lambda h, i: (i, qc + h)),
                pl.BlockSpec((T, HEAD_DIM), lambda h, i: (0, kc + h)),
                pl.BlockSpec((T, HEAD_DIM), lambda h, i: (0, vc + h)),
                cspec, tspec, qspec, qspec, qspec]
    dq_spec, dq_shape, aliases = qspec, jax.ShapeDtypeStruct((T, WB), BF16), None
    if dq_into is not None:
        buf, dq_col = dq_into
        args.append(buf)
        in_specs.append(pl.BlockSpec(memory_space=pl.ANY))
        dq_spec = pl.BlockSpec((QB, HEAD_DIM), lambda h, i: (i, dq_col + h))
        dq_shape, aliases = jax.ShapeDtypeStruct(buf.shape, buf.dtype), {8: 0}
    return _call(
        body, args, name=name, grid=(HB, T // QB), in_specs=in_specs,
        out_specs=[dq_spec, kspec, kspec, qspec, tspec],
        out_shape=[dq_shape, wide, wide, wide, jax.ShapeDtypeStruct((HB, T // KB, KB), F32)],
        sem=("parallel", "arbitrary"), comm=comm, aliases=aliases)


def _place():
    return lax.axis_index("x"), lax.axis_index("y"), lax.axis_index("c")


def _other_chips(x, y):
    return [(1 - x, y), (x, 1 - y), (1 - x, 1 - y)]


def _gather_comm(shards):
    n = len(shards)

    def copies(srcs, outs, scratch, want):
        send_sems, recv_sems, local_sems = scratch
        x, y, c = _place()
        me, sib = (x, y, c), (x, y, 1 - c)
        chip_a = (jnp.bitwise_xor(x, c), jnp.bitwise_xor(y, 1 - c))
        chip_b = (jnp.bitwise_xor(x, 1 - c), jnp.bitwise_xor(y, c))
        chip_d = (1 - x, 1 - y)

        def copy(t, k, block, to, src=None):
            rows = outs[t].at[4 * block[0] + 2 * block[1] + block[2]]
            return pltpu.make_async_remote_copy(
                src_ref=rows if src is None else src, dst_ref=rows,
                send_sem=send_sems.at[7 * t + k], recv_sem=recv_sems.at[7 * t + k],
                device_id=to, device_id_type=MESH)

        got = {}
        if "mine" in want:
            got["mine"] = [pltpu.make_async_copy(srcs[t], outs[t].at[4 * x + 2 * y + c], local_sems.at[t])
                           for t in range(n)]
        if "first" in want:
            got["first"] = [cp for t in range(n) for cp in
                            (copy(t, 0, me, sib, src=srcs[t]), copy(t, 1, me, (*chip_a, c), src=srcs[t]),
                             copy(t, 2, me, (*chip_b, c), src=srcs[t]))]
        if "from_a" in want:
            got["from_a"] = [copy(t, 1, (*chip_a, c), me) for t in range(n)]
        if "hop" in want:
            got["hop"] = [copy(t, 3, (*chip_a, c), (*chip_b, c)) for t in range(n)]
        if "from_b" in want:
            got["from_b"] = [copy(t, 2, (*chip_b, c), me) for t in range(n)]
        if "from_d" in want:
            got["from_d"] = [copy(t, 3, (*chip_d, c), me) for t in range(n)]
        for key, k, chip in (("pass_a", 4, chip_a), ("pass_b", 5, chip_b), ("pass_d", 6, chip_d)):
            if key in want:
                got[key] = [copy(t, k, (*chip, c), sib) for t in range(n)]
        if "late" in want:
            got["late"] = [cp for t in range(n) for cp in
                           (copy(t, 0, sib, me), copy(t, 4, (*chip_b, 1 - c), me),
                            copy(t, 5, (*chip_a, 1 - c), me), copy(t, 6, (*chip_d, 1 - c), me))]
        return got

    def begin(srcs, outs, scratch):
        cps = copies(srcs, outs, scratch, ("mine", "first"))
        for cp in cps["mine"] + cps["first"]:
            cp.start()

    def hop(srcs, outs, scratch):
        cps = copies(srcs, outs, scratch, ("from_a", "hop", "pass_a"))
        for landed, fwd, sibling in zip(cps["from_a"], cps["hop"], cps["pass_a"]):
            landed.wait_recv()
            fwd.start()
            sibling.start()

    def pass_on(srcs, outs, scratch):
        cps = copies(srcs, outs, scratch, ("from_b", "pass_b", "from_d", "pass_d"))
        for landed, sibling in list(zip(cps["from_b"], cps["pass_b"])) + list(zip(cps["from_d"], cps["pass_d"])):
            landed.wait_recv()
            sibling.start()

    def end(srcs, outs, scratch):
        cps = copies(srcs, outs, scratch, ("mine", "first", "hop", "pass_a", "pass_b", "pass_d", "late"))
        for cp in cps["late"]:
            cp.wait_recv()
        for cp in cps["first"] + cps["hop"] + cps["pass_a"] + cps["pass_b"] + cps["pass_d"]:
            cp.wait_send()
        for cp in cps["mine"]:
            cp.wait()

    return _Comm(
        shards, [jax.ShapeDtypeStruct((N_DEV,) + s.shape, s.dtype) for s in shards],
        [pltpu.SemaphoreType.DMA((7 * n,)), pltpu.SemaphoreType.DMA((7 * n,)), pltpu.SemaphoreType.DMA((n,))],
        begin, end, [(0.67, hop), (0.9, pass_on)])


def _sibling_comm(grads):
    n = len(grads)

    def copies(srcs, outs, scratch):
        send_sems, recv_sems = scratch
        x, y, c = _place()
        return [pltpu.make_async_remote_copy(
            src_ref=srcs[t].at[2 * p + (1 - c)], dst_ref=outs[t].at[p],
            send_sem=send_sems.at[4 * t + p], recv_sem=recv_sems.at[4 * t + p],
            device_id=(x, y, 1 - c), device_id_type=MESH) for t in range(n) for p in range(4)]

    def begin(srcs, outs, scratch):
        for cp in copies(srcs, outs, scratch):
            cp.start()

    def end(srcs, outs, scratch):
        cps = copies(srcs, outs, scratch)
        for cp in cps:
            cp.wait_recv()
        for cp in cps:
            cp.wait_send()

    return _Comm(
        grads, [jax.ShapeDtypeStruct((4,) + g.shape[1:], g.dtype) for g in grads],
        [pltpu.SemaphoreType.DMA((4 * n,)), pltpu.SemaphoreType.DMA((4 * n,))], begin, end)


def _chips_comm(sums, relations):
    n, nr = len(sums), len(relations)

    def copies(srcs, outs, scratch):
        send_sems, recv_sems = scratch
        x, y, c = _place()
        chips = _other_chips(x, y)
        return [pltpu.make_async_remote_copy(
            src_ref=srcs[t].at[2 * chips[j][0] + chips[j][1]], dst_ref=outs[t].at[s],
            send_sem=send_sems.at[nr * t + s], recv_sem=recv_sems.at[nr * t + s],
            device_id=(*chips[j], c), device_id_type=MESH) for t in range(n) for s, j in enumerate(relations)]

    def begin(srcs, outs, scratch):
        for cp in copies(srcs, outs, scratch):
            cp.start()

    def end(srcs, outs, scratch):
        cps = copies(srcs, outs, scratch)
        for cp in cps:
            cp.wait_recv()
        for cp in cps:
            cp.wait_send()

    return _Comm(
        sums, [jax.ShapeDtypeStruct((nr,) + s.shape[1:], s.dtype) for s in sums],
        [pltpu.SemaphoreType.DMA((nr * n,)), pltpu.SemaphoreType.DMA((nr * n,))], begin, end)


def _run_comm(comm, *, name):
    ci, co = len(comm.ins), len(comm.outs)

    def body(*refs):
        cin, cout, csc = refs[:ci], refs[ci:ci + co], refs[ci + co:]
        comm.begin(cin, cout, csc)
        for _, relay in comm.relays:
            relay(cin, cout, csc)
        comm.end(cin, cout, csc)

    return pl.pallas_call(
        body, name=name, in_specs=_any_specs(ci), out_specs=_any_specs(co), out_shape=comm.outs,
        scratch_shapes=comm.scratch)(*comm.ins)


def _chip_sum(g, r1, core, *, name, tr=256):
    _, R, C = g.shape
    tr = _tile(R, tr, 16)

    def body(c_ref, g_ref, r_ref, o_ref):
        o_ref[...] = (g_ref[...].astype(F32) + r_ref[...].astype(F32)).astype(BF16)

    return pl.pallas_call(
        body, name=name,
        grid_spec=pltpu.PrefetchScalarGridSpec(
            num_scalar_prefetch=1, grid=(4, R // tr),
            in_specs=[pl.BlockSpec((None, tr, C), lambda p, i, c: (2 * p + c[0], i, 0)),
                      pl.BlockSpec((None, tr, C), lambda p, i, c: (p, i, 0))],
            out_specs=pl.BlockSpec((None, tr, C), lambda p, i, c: (p, i, 0))),
        out_shape=jax.ShapeDtypeStruct((4, R, C), BF16),
        compiler_params=_cp("parallel", "parallel"))(core, g, r1)


def _adamw_math(w, g, m, v):
    m = ADAM_B1 * m + (1.0 - ADAM_B1) * g
    v = ADAM_B2 * v + (1.0 - ADAM_B2) * (g * g)
    m_hat = m / (1.0 - ADAM_B1 ** ADAM_STEP)
    v_hat = v / (1.0 - ADAM_B2 ** ADAM_STEP)
    delta = -ADAM_LR * (m_hat / (jnp.sqrt(v_hat) + ADAM_EPS) + ADAM_WD * w)
    return delta, m, v


def _adamw_shard(s1, others, chip, w, m, v, *, name, tr=64):
    R0, C0 = w.shape
    C = s1.shape[2]
    tr = _tile(R0, tr, 8)
    (a0, k0), (a1, k1), (a2, k2) = others

    def body(p_ref, s_ref, r0_ref, r1_ref, r2_ref, w_ref, m_ref, v_ref, g_out, d_out, m_out, v_out):
        g = s_ref[:, :C0].astype(F32)
        for r_ref in (r0_ref, r1_ref, r2_ref):
            g = g + r_ref[:, :C0].astype(F32)
        delta, mn, vn = _adamw_math(w_ref[...], g, m_ref[...], v_ref[...])
        g_out[...] = g
        d_out[...] = delta
        m_out[...] = mn
        v_out[...] = vn

    full = pl.BlockSpec((tr, C0), lambda i, p: (i, 0))
    out = jax.ShapeDtypeStruct((R0, C0), F32)
    return pl.pallas_call(
        body, name=name,
        grid_spec=pltpu.PrefetchScalarGridSpec(
            num_scalar_prefetch=1, grid=(R0 // tr,),
            in_specs=[pl.BlockSpec((None, tr, C), lambda i, p: (p[0], i, 0)),
                      pl.BlockSpec((None, tr, C), lambda i, p: (k0, i, 0)),
                      pl.BlockSpec((None, tr, C), lambda i, p: (k1, i, 0)),
                      pl.BlockSpec((None, tr, C), lambda i, p: (k2, i, 0)),
                      full, full, full],
            out_specs=[full, full, full, full]),
        out_shape=[out, out, out, out],
        compiler_params=_cp("parallel"))(chip, s1, a0, a1, a2, w, m, v)


def _all_reduce_small(part, *, name):
    SR = part.shape[0]

    def body(p_ref, o_ref, buf, send_sems, recv_sems):
        x, y, c = _place()
        me = 4 * x + 2 * y + c
        buf[0] = p_ref[...]
        copies = []
        for k in range(1, N_DEV):
            peer = (1 - x if k & 4 else x, 1 - y if k & 2 else y, 1 - c if k & 1 else c)
            copies.append(pltpu.make_async_remote_copy(
                src_ref=p_ref, dst_ref=buf.at[k], send_sem=send_sems.at[k - 1], recv_sem=recv_sems.at[k - 1],
                device_id=peer, device_id_type=MESH))
        for cp in copies:
            cp.start()
        for cp in copies:
            cp.wait_recv()
        for cp in copies:
            cp.wait_send()
        acc = buf[me]
        for d in range(1, N_DEV):
            acc = acc + buf[jnp.bitwise_xor(me, d)]
        o_ref[...] = acc

    return pl.pallas_call(
        body, name=name,
        in_specs=[pl.BlockSpec(memory_space=pltpu.VMEM)], out_specs=pl.BlockSpec(memory_space=pltpu.VMEM),
        out_shape=jax.ShapeDtypeStruct(part.shape, F32),
        scratch_shapes=[pltpu.VMEM((N_DEV, SR, LANE), F32), pltpu.SemaphoreType.DMA((N_DEV - 1,)),
                        pltpu.SemaphoreType.DMA((N_DEV - 1,))])(part)


def _adamw_small(w, g, m, v, *, name):
    def body(w_ref, g_ref, m_ref, v_ref, d_out, m_out, v_out):
        delta, mn, vn = _adamw_math(w_ref[...], g_ref[...], m_ref[...], v_ref[...])
        d_out[...] = delta
        m_out[...] = mn
        v_out[...] = vn

    out = jax.ShapeDtypeStruct(w.shape, F32)
    return pl.pallas_call(body, name=name, out_shape=[out, out, out], compiler_params=_cp())(w, g, m, v)


def _in_layout(din8, fq):
    lay = []
    for b in range(N_DEV):
        g0 = din8 * b
        nq = min(max(fq - g0, 0), din8)
        oq = g0 % LANE if nq else 0
        nqt = -(-(oq + nq) // LANE) if nq else 0
        tq0 = (g0 - oq) // LANE
        pg, tg0, ngt = oq + nq, 0, 0
        if nq < din8:
            u0 = g0 + nq - fq
            pg = nqt * LANE + u0 % LANE
            tg0, ngt = u0 // LANE, -(-(u0 % LANE + din8 - nq) // LANE)
        lay.append(dict(nq=nq, oq=oq, tq0=tq0, nqt=nqt, pg=pg, tg0=tg0, ngt=ngt, width=pg + din8 - nq))
    return lay


def _place_block(shard, lay, pin, b):
    rows, din8 = shard.shape

    def place(s, l):
        nq = l["nq"]
        parts, at = [], 0
        for start, cols in ((l["oq"], s[:, :nq]), (l["pg"], s[:, nq:])):
            if cols.shape[1]:
                parts += [jnp.zeros((rows, start - at), s.dtype), cols]
                at = start + cols.shape[1]
        parts.append(jnp.zeros((rows, pin - at), s.dtype))
        return jnp.concatenate([p for p in parts if p.shape[1]], axis=1)

    return lax.switch(b, [functools.partial(place, l=l) for l in lay], shard)


def _unplace_block(g, lay, din8, b):
    def unplace(a, l):
        nq = l["nq"]
        parts = [a[:, l["oq"]:l["oq"] + nq], a[:, l["pg"]:l["pg"] + din8 - nq]]
        return jnp.concatenate([p for p in parts if p.shape[1]], axis=1)

    return lax.switch(b, [functools.partial(unplace, l=l) for l in lay], g)


def _tile_runs(sources):
    runs = []
    for src in sources:
        if len(src) == 1 and runs and not isinstance(runs[-1], list) \
                and runs[-1][0] == src[0][0] and runs[-1][1] + runs[-1][2] == src[0][1]:
            runs[-1] = (runs[-1][0], runs[-1][1], runs[-1][2] + 1)
        elif len(src) == 1:
            runs.append((src[0][0], src[0][1], 1))
        else:
            runs.append(list(src))
    return runs


def _gather_tiles(arrays, sources):
    rows, dtype = arrays[0].shape[0], arrays[0].dtype
    pieces = []
    for run in _tile_runs(sources):
        if isinstance(run, list):
            tile = jnp.zeros((rows, LANE), dtype)
            for a, t in run:
                tile = tile + arrays[a][:, t * LANE:(t + 1) * LANE]
            pieces.append(tile)
        else:
            a, t0, n = run
            pieces.append(arrays[a][:, t0 * LANE:(t0 + n) * LANE])
    return jnp.concatenate(pieces, axis=1)


def _assemble_w_in(blocks, lay, n_qkv_tiles, n_f_tiles, n_g_tiles):
    q_src = [[] for _ in range(n_qkv_tiles + n_f_tiles)]
    g_src = [[] for _ in range(n_g_tiles)]
    for b, l in enumerate(lay):
        for j in range(l["nqt"]):
            q_src[l["tq0"] + j].append((b, j))
        for j in range(l["ngt"]):
            g_src[l["tg0"] + j].append((b, l["pg"] // LANE + j))
    arrays = [blocks[b] for b in range(N_DEV)]
    return (_gather_tiles(arrays, q_src[:n_qkv_tiles]), _gather_tiles(arrays, q_src[n_qkv_tiles:]),
            _gather_tiles(arrays, g_src))


def _scatter_g_in(g_qkv, g_f, g_gates, lay, pin):
    n_qkv_tiles = g_qkv.shape[1] // LANE
    out = []
    for l in lay:
        src = [[] for _ in range(pin // LANE)]
        for j in range(l["nqt"]):
            t = l["tq0"] + j
            src[j] = [(0, t)] if t < n_qkv_tiles else [(1, t - n_qkv_tiles)]
        for j in range(l["ngt"]):
            src[l["pg"] // LANE + j] = [(2, l["tg0"] + j)]
        out.append(_gather_tiles([g_qkv, g_f, g_gates], src))
    return jnp.stack(out)


def _pack(parts, rows):
    flat = jnp.concatenate([p.reshape(-1).astype(F32) for p in parts])
    return jnp.pad(flat, (0, rows * LANE - flat.shape[0])).reshape(rows, LANE)


def _unpack(packed, shapes):
    flat = packed.reshape(-1)
    out, at = [], 0
    for s in shapes:
        n = 1
        for d in s:
            n *= d
        out.append(flat[at:at + n].reshape(s))
        at += n
    return out


def kernel(x, g_mix, w_in, b_f, b_gate, rel_bias, w_branch_a, w_branch_b, w_out, g_ffn, w_gate_ffn, w_up_ffn, w_down_ffn, g_final, loss_target, m_g_mix, m_w_in, m_b_f, m_b_gate, m_rel_bias, m_w_branch_a, m_w_branch_b, m_w_out, m_g_ffn, m_w_gate_ffn, m_w_up_ffn, m_w_down_ffn, m_g_final, v_g_mix, v_w_in, v_b_f, v_b_gate, v_rel_bias, v_w_branch_a, v_w_branch_b, v_w_out, v_g_ffn, v_w_gate_ffn, v_w_up_ffn, v_w_down_ffn, v_g_final):
    _, T, D = x.shape
    HA, n_rel = rel_bias.shape[1], rel_bias.shape[2]
    HB = b_f.shape[1]
    WA, WB = HA * HEAD_DIM, HB * HEAD_DIM
    din8, f8, d8 = w_in.shape[2], w_gate_ffn.shape[2], D // N_DEV
    DIN = N_DEV * din8
    PF = _pad_to(f8, LANE)
    fcol = 3 * WA + 3 * WB
    lay = _in_layout(din8, fcol + HB)
    PIN = _pad_to(max(l["width"] for l in lay), LANE)
    padk = LEFT_CHUNKS * CHUNK
    QA = min(256, T)
    KW = QA + padk
    KB = min(512, T)
    x2d, tgt = x[0], loss_target[0]
    cx, cy, cc = _place()
    blk = 4 * cx + 2 * cy + cc
    core = jnp.reshape(cc, (1,)).astype(jnp.int32)
    chip = jnp.reshape(2 * cx + cy, (1,)).astype(jnp.int32)
    every = [0, 1, 2]

    s_in = _place_block(w_in[0].astype(BF16), lay, PIN, blk)
    s_a, s_b, s_out = w_branch_a[0].astype(BF16), w_branch_b[0].astype(BF16), w_out[0].astype(BF16)
    s_g = jnp.pad(w_gate_ffn[0].T.astype(BF16), ((0, PF - f8), (0, 0)))
    s_u = jnp.pad(w_up_ffn[0].T.astype(BF16), ((0, PF - f8), (0, 0)))
    s_d = jnp.pad(w_down_ffn[0].astype(BF16), ((0, PF - f8), (0, 0)))
    Win, = _run_comm(_gather_comm([s_in]), name="gather_w_in")
    Wqkv, Wf, Wgt = _assemble_w_in(Win, lay, fcol // LANE, 1, 2 * D // LANE)
    Wqkv, Wf, Wgt = Wqkv.reshape(1, D, fcol), Wf.reshape(1, D, LANE), Wgt.reshape(1, D, 2 * D)
    bf_pad = jnp.pad(b_f, ((0, 0), (0, LANE - HB)))
    fox_cols = (3 * HA, 3 * HA + HB, 3 * HA + 2 * HB)

    h = _rms_fwd(x2d, g_mix, name="rms_mix")
    qkv, (Wg,) = _mm_nn(h, Wqkv, name="proj_qkv", out_dtype=BF16, comm=_gather_comm([s_g]))
    Wg = Wg.reshape(1, N_DEV * PF, D)
    gates, (Wa, Wb, Wout) = _mm_nn(h, Wgt, name="proj_gates", out_dtype=BF16, comm=_gather_comm([s_a, s_b, s_out]))
    Wout = Wout.reshape(1, D, D)
    flog = _mm_nn(h, Wf, name="proj_forget", out_dtype=F32)

    kap = jnp.pad(qkv[:, WA:2 * WA], ((padk, 0), (0, 0)))
    vap = jnp.pad(qkv[:, 2 * WA:3 * WA], ((padk, 0), (0, 0)))
    btab = _bias_table(rel_bias[0], QA, KW)
    o_a, lse_a = _attn_a_fwd(qkv, kap, vap, btab, name="attn_a_fwd")

    cum = _fgate_fwd(flog, bf_pad, name="forget_cumsum")
    cum_t = cum[:, :HB].T.reshape(HB, T // KB, KB)
    (o_b, lse_b), (Wu,) = _fox_fwd(qkv, fox_cols, cum, cum_t, name="fox_fwd", comm=_gather_comm([s_u]))
    Wu = Wu.reshape(1, N_DEV * PF, D)

    u_a = _mm_nn(o_a, Wa, name="branch_a", out_dtype=BF16)
    u_b = _mm_nn(o_b, Wb, name="branch_b", out_dtype=BF16)
    merged = _merge_fwd(gates, b_gate, u_a, u_b, name="merge_fwd")
    x1 = _mm_nn(merged, Wout, name="out_proj", out_dtype=F32, res=x2d)
    h2 = _rms_fwd(x1, g_ffn, name="rms_ffn")
    gt, (Wd,) = _mm_nt(h2, Wg, name="ffn_gate", out_dtype=BF16, tc=D, comm=_gather_comm([s_d]))
    Wd = Wd.reshape(1, N_DEV * PF, D)
    up, hid = _mm_nt(h2, Wu, name="ffn_up", out_dtype=BF16, epi=_swiglu_fwd_epi, extras=[gt], n_out=2)
    x2 = _mm_nn(hid, Wd, name="ffn_down", out_dtype=F32, res=x1, tk=2 * PF)
    dx2, dx2b, dg_final, loss_part = _final_loss(x2, tgt, g_final.reshape(1, D), name="final_loss")

    dgt, dup = _mm_nt(dx2b, Wd, name="d_hidden", out_dtype=BF16, epi=_swiglu_bwd_epi, extras=[gt, up], n_out=2)
    gWd = _mm_tn(hid, dx2b, 1, name="gw_down").reshape(N_DEV, PF, D)
    dh2, (sWd,) = _mm_nn(dgt, Wg, name="d_h2_gate", out_dtype=F32, tk=2 * PF, comm=_sibling_comm([gWd]))
    cWd = _chip_sum(gWd, sWd, core, name="chip_sum_w_down_ffn")
    dh2, (rWd,) = _mm_nn(dup, Wu, name="d_h2_up", out_dtype=F32, res=dh2, tk=2 * PF,
                         comm=_chips_comm([cWd], every))
    gWg = _mm_tn(dgt, h2, 1, name="gw_gate").reshape(N_DEV, PF, D)
    gWu = _mm_tn(dup, h2, 1, name="gw_up").reshape(N_DEV, PF, D)
    dx1, dx1b, dg_ffn = _rms_bwd(dh2, x1, g_ffn, dx2, name="rms_ffn_bwd")
    dmer, (sWg, sWu) = _mm_nt(dx1b, Wout, name="d_merged", out_dtype=BF16, comm=_sibling_comm([gWg, gWu]))
    cWg = _chip_sum(gWg, sWg, core, name="chip_sum_w_gate_ffn")
    cWu = _chip_sum(gWu, sWu, core, name="chip_sum_w_up_ffn")
    gWout = _mm_tn(merged, dx1b, 1, name="gw_out").reshape(N_DEV, d8, D)
    dua, dub, dgates, dbg = _merge_bwd(dmer, gates, b_gate, u_a, u_b, name="merge_bwd")
    g_gates = _mm_tn(h, dgates, 1, name="gw_gates")[0]
    doa = _mm_nt(dua, Wa, name="d_o_a", out_dtype=BF16)
    dob = _mm_nt(dub, Wb, name="d_o_b", out_dtype=BF16)
    gWa = _mm_tn(o_a, dua, N_DEV, name="gw_branch_a")
    gWb = _mm_tn(o_b, dub, N_DEV, name="gw_branch_b")

    (dqkv, dkap, dvap, dtab), (sWout, sWa, sWb) = _attn_a_bwd(
        qkv, kap, vap, btab, o_a, doa, lse_a, name="attn_a_bwd", comm=_sibling_comm([gWout, gWa, gWb]),
        dq_width=fcol)
    cWout = _chip_sum(gWout, sWout, core, name="chip_sum_w_out")
    cWa = _chip_sum(gWa, sWa, core, name="chip_sum_w_branch_a")
    cWb = _chip_sum(gWb, sWb, core, name="chip_sum_w_branch_b")
    skew = jnp.pad(dtab.reshape(HA, QA * KW), ((0, 0), (0, QA))).reshape(HA, QA, KW + 1)
    colsum, tail = _relbias_fold(skew, dtab, name="relbias_fold")
    rev = colsum[:, 0, ::-1]
    d_rel = jnp.concatenate([rev[:, :2 * MAX_REL], tail[:, 0, :1]], axis=1)

    (dqkv, dkb, dvb, dcq, dck), (rWg, rWout, rWa, rWb) = _fox_bwd(
        qkv, fox_cols, cum, cum_t, o_b, dob, lse_b, name="fox_bwd", comm=_chips_comm([cWg, cWout, cWa, cWb], every),
        dq_into=(dqkv, fox_cols[0]))
    dcq_p = jnp.pad(dcq.reshape(T, HB, HEAD_DIM)[:, :, 0], ((0, 0), (0, LANE - HB)))
    dck_p = jnp.pad(dck.reshape(HB, T).T, ((0, 0), (0, LANE - HB)))
    dflog, dbf = _fgate_bwd(dcq_p, dck_p, flog, bf_pad, name="forget_bwd")

    dqkv = _cast_into(dqkv, dkap, padk, WA, name="dqkv_k_a")
    dqkv = _cast_into(dqkv, dvap, padk, 2 * WA, name="dqkv_v_a")
    dqkv = _cast_into(dqkv, dkb, 0, 3 * WA + WB, name="dqkv_k_b")
    dqkv = _cast_into(dqkv, dvb, 0, 3 * WA + 2 * WB, name="dqkv_v_b")
    dfb = dflog.astype(BF16)
    g_qkv, (rWu,) = _mm_tn(h, dqkv, 1, name="gw_qkv", comm=_chips_comm([cWu], every))
    gWin = _scatter_g_in(g_qkv[0], _mm_tn(h, dfb, 1, name="gw_forget")[0], g_gates, lay, PIN)
    dh, (sWin,) = _mm_nt(dfb, Wf, name="d_h_forget", out_dtype=F32, comm=_sibling_comm([gWin]))
    cWin = _chip_sum(gWin, sWin, core, name="chip_sum_w_in")
    dh, (rWin01,) = _mm_nt(dqkv, Wqkv, name="d_h_qkv", out_dtype=F32, res=dh, comm=_chips_comm([cWin], [0, 1]))
    dh, (rWin2,) = _mm_nt(dgates, Wgt, name="d_h_gates", out_dtype=F32, res=dh, comm=_chips_comm([cWin], [2]))
    dx, dg_mix = _rms_bwd(dh, x2d, g_mix, dx1, name="rms_mix_bwd", with_bf16=False)

    def slots(r):
        return [(r, 0), (r, 1), (r, 2)]

    in_parts = [_unplace_block(p, lay, din8, blk)[None] for p in
                (lax.dynamic_index_in_dim(cWin, 2 * cx + cy, keepdims=False), rWin01[0], rWin01[1], rWin2[0])]
    first = jnp.zeros((1,), jnp.int32)
    big = {}
    for nm, s1, others, which, w, m, v in [
            ("w_down_ffn", cWd, slots(rWd), chip, w_down_ffn, m_w_down_ffn, v_w_down_ffn),
            ("w_gate_ffn", cWg, slots(rWg), chip, w_gate_ffn, m_w_gate_ffn, v_w_gate_ffn),
            ("w_out", cWout, slots(rWout), chip, w_out, m_w_out, v_w_out),
            ("w_branch_a", cWa, slots(rWa), chip, w_branch_a, m_w_branch_a, v_w_branch_a),
            ("w_branch_b", cWb, slots(rWb), chip, w_branch_b, m_w_branch_b, v_w_branch_b),
            ("w_up_ffn", cWu, slots(rWu), chip, w_up_ffn, m_w_up_ffn, v_w_up_ffn),
            ("w_in", in_parts[0], [(p, 0) for p in in_parts[1:]], first, w_in, m_w_in, v_w_in)]:
        flipped = nm in ("w_gate_ffn", "w_up_ffn")
        w2, m2, v2 = ((a[0].T if flipped else a[0]) for a in (w, m, v))
        outs = _adamw_shard(s1, others, which, w2, m2, v2, name="adamw_" + nm)
        big[nm] = [(o.T if flipped else o)[None] for o in outs]

    small_names = ["g_mix", "b_f", "b_gate", "rel_bias", "g_ffn", "g_final"]
    small_w = [g_mix, b_f, b_gate, rel_bias, g_ffn, g_final]
    small_m = [m_g_mix, m_b_f, m_b_gate, m_rel_bias, m_g_ffn, m_g_final]
    small_v = [v_g_mix, v_b_f, v_b_gate, v_rel_bias, v_g_ffn, v_g_final]
    small_g = [dg_mix, dbf[:, :HB], dbg, d_rel[None], dg_ffn, dg_final[0]]
    shapes = [w.shape for w in small_w]
    n_small = sum(w.size for w in small_w) + 1
    rows = _pad_to(-(-n_small // LANE), 8)
    total = _all_reduce_small(_pack(small_g + [loss_part[:1, :1]], rows), name="all_reduce_small")
    d_s, m_s, v_s = _adamw_small(_pack(small_w, rows), total, _pack(small_m, rows), _pack(small_v, rows),
                                 name="adamw_small")
    g_small = dict(zip(small_names, _unpack(total, shapes)))
    d_small = dict(zip(small_names, _unpack(d_s, shapes)))
    m_small = dict(zip(small_names, _unpack(m_s, shapes)))
    v_small = dict(zip(small_names, _unpack(v_s, shapes)))
    loss = total.reshape(-1)[n_small - 1]

    order = ["g_mix", "w_in", "b_f", "b_gate", "rel_bias", "w_branch_a", "w_branch_b", "w_out", "g_ffn",
             "w_gate_ffn", "w_up_ffn", "w_down_ffn", "g_final"]

    def pick(idx, small):
        return [big[nm][idx] if nm in big else small[nm] for nm in order]

    return (loss, dx[None], *pick(0, g_small), *pick(1, d_small), *pick(2, m_small), *pick(3, v_small))
```

```python
import functools

import jax
import jax.numpy as jnp
from jax import lax
from jax.experimental import pallas as pl
from jax.experimental.pallas import tpu as pltpu

F32 = jnp.float32
BF16 = jnp.bfloat16
MESH = pl.DeviceIdType.MESH

N_DEV = 8
LANE = 128
HEAD_DIM = 128
CHUNK = 64
LEFT_CHUNKS = 8
MAX_REL = 256
RMS_EPS = 1e-6
NEG_INF = -1e30
ATTN_SCALE = HEAD_DIM ** -0.5
VMEM_LIMIT_BYTES = 56 * 1024 * 1024

ADAM_LR = 0.001
ADAM_B1 = 0.9
ADAM_B2 = 0.999
ADAM_EPS = 1e-08
ADAM_WD = 0.01
ADAM_STEP = 10

NT_DIMS = (((1,), (1,)), ((), ()))
TN_DIMS = (((0,), (0,)), ((), ()))


def _cp(*sem):
    return pltpu.CompilerParams(dimension_semantics=sem if sem else None, vmem_limit_bytes=VMEM_LIMIT_BYTES)


def _pad_to(n, mult):
    return -(-n // mult) * mult


def _tile(dim, pref, align):
    t = (min(pref, dim) // align) * align
    while t >= align:
        if dim % t == 0:
            return t
        t -= align
    return dim


def _sigmoid(z):
    return 1.0 / (1.0 + jnp.exp(-z))


class _Comm:
    def __init__(self, ins, outs, scratch, begin, end, relays=()):
        self.ins, self.outs, self.scratch = list(ins), list(outs), list(scratch)
        self.begin, self.end, self.relays = begin, end, list(relays)


def _call(body, args, *, name, grid, in_specs, out_specs, out_shape, scratch=(), sem=(), comm=None, aliases=None):
    single = not isinstance(out_shape, (list, tuple))
    out_shape = [out_shape] if single else list(out_shape)
    out_specs = [out_specs] if single else list(out_specs)
    aliases = dict(aliases or {})
    if comm is None:
        res = pl.pallas_call(
            body, name=name, grid=grid, in_specs=list(in_specs), out_specs=out_specs, out_shape=out_shape,
            scratch_shapes=list(scratch), input_output_aliases=aliases, compiler_params=_cp(*sem))(*args)
        return res[0] if single else res
    ni, no, ns = len(args), len(out_shape), len(scratch)
    ci, co = len(comm.ins), len(comm.outs)
    total = 1
    for g in grid:
        total *= g

    def hosted(*refs):
        a, cin = refs[:ni], refs[ni:ni + ci]
        o, cout = refs[ni + ci:ni + ci + no], refs[ni + ci + no:ni + ci + no + co]
        s, csc = refs[ni + ci + no + co:ni + ci + no + co + ns], refs[ni + ci + no + co + ns:]
        step = pl.program_id(0)
        for d in range(1, len(grid)):
            step = step * grid[d] + pl.program_id(d)

        @pl.when(step == 0)
        def _():
            comm.begin(cin, cout, csc)

        for fraction, relay in comm.relays:
            @pl.when(step == min(int(total * fraction), total - 1))
            def _(relay=relay):
                relay(cin, cout, csc)

        body(*a, *o, *s)

        @pl.when(step == total - 1)
        def _():
            comm.end(cin, cout, csc)

    res = pl.pallas_call(
        hosted, name=name, grid=grid, in_specs=list(in_specs) + _any_specs(ci),
        out_specs=out_specs + _any_specs(co), out_shape=out_shape + comm.outs,
        scratch_shapes=list(scratch) + comm.scratch, input_output_aliases=aliases,
        compiler_params=_cp(*(("arbitrary",) * len(grid))))(*args, *comm.ins)
    return (res[0] if single else res[:no]), res[no:]


def _any_specs(n):
    return [pl.BlockSpec(memory_space=pl.ANY)] * n


def _mm_body(dims, nk, has_res, n_ext, n_out, epi):
    def body(*refs):
        a_ref, w_ref = refs[:2]
        at = 2 + has_res
        r_ref = refs[2] if has_res else None
        e_refs, o_refs = refs[at:at + n_ext], refs[at + n_ext:at + n_ext + n_out]

        def emit(v):
            if r_ref is not None:
                v = v + r_ref[...]
            vals = (v,) if epi is None else epi(v, *[e[...] for e in e_refs])
            for o_ref, val in zip(o_refs, vals):
                o_ref[...] = val.astype(o_ref.dtype)

        if nk == 1:
            emit(lax.dot_general(a_ref[...], w_ref[...], dims, preferred_element_type=F32))
            return
        acc = refs[-1]
        k = pl.program_id(2)

        @pl.when(k == 0)
        def _():
            acc[...] = jnp.zeros_like(acc)

        acc[...] += lax.dot_general(a_ref[...], w_ref[...], dims, preferred_element_type=F32)

        @pl.when(k == nk - 1)
        def _():
            emit(acc[...])

    return body


def _mm_nn(x, w, *, name, out_dtype, res=None, tm=1024, tn=1024, tk=None, comm=None, epi=None, extras=(), n_out=1):
    M, K = x.shape
    NB, _, NW = w.shape
    tm, tn = _tile(M, tm, 16), _tile(NW, tn, LANE)
    tk = K if tk is None else _tile(K, tk, LANE)
    nj, nk = NW // tn, K // tk
    tile = pl.BlockSpec((tm, tn), lambda j, i, k: (i, j))
    in_specs = [pl.BlockSpec((tm, tk), lambda j, i, k: (i, k)),
                pl.BlockSpec((None, tk, tn), lambda j, i, k: (j // nj, k, j % nj))]
    args = [x, w] + ([res] if res is not None else []) + list(extras)
    in_specs += [tile] * (len(args) - 2)
    out = jax.ShapeDtypeStruct((M, NB * NW), out_dtype)
    return _call(
        _mm_body((((1,), (0,)), ((), ())), nk, res is not None, len(extras), n_out, epi), args,
        name=name, grid=(NB * nj, M // tm, nk), in_specs=in_specs,
        out_specs=tile if n_out == 1 else [tile] * n_out, out_shape=out if n_out == 1 else [out] * n_out,
        scratch=[pltpu.VMEM((tm, tn), F32)] if nk > 1 else [],
        sem=("parallel", "parallel", "arbitrary"), comm=comm)


def _mm_nt(dy, w, *, name, out_dtype, res=None, tm=1024, tn=1024, tc=2816, comm=None, epi=None, extras=(), n_out=1):
    M, N = dy.shape
    NB, Kw, NW = w.shape
    assert N == NB * NW
    tm, tn, tc = _tile(M, tm, 16), _tile(Kw, tn, LANE), _tile(NW, tc, LANE)
    nc = NW // tc
    nk = NB * nc
    tile = pl.BlockSpec((tm, tn), lambda i, j, k: (i, j))
    in_specs = [pl.BlockSpec((tm, tc), lambda i, j, k: (i, k)),
                pl.BlockSpec((None, tn, tc), lambda i, j, k: (k // nc, j, k % nc))]
    args = [dy, w] + ([res] if res is not None else []) + list(extras)
    in_specs += [tile] * (len(args) - 2)
    out = jax.ShapeDtypeStruct((M, Kw), out_dtype)
    return _call(
        _mm_body(NT_DIMS, nk, res is not None, len(extras), n_out, epi), args,
        name=name, grid=(M // tm, Kw // tn, nk), in_specs=in_specs,
        out_specs=tile if n_out == 1 else [tile] * n_out, out_shape=out if n_out == 1 else [out] * n_out,
        scratch=[pltpu.VMEM((tm, tn), F32)] if nk > 1 else [],
        sem=("parallel", "parallel", "arbitrary"), comm=comm)


def _mm_tn(x, dy, nb, *, name, tm=512, tn=1024, comm=None):
    M, Kw = x.shape
    NW = dy.shape[1] // nb
    tm, tn = _tile(Kw, tm, LANE), _tile(NW, tn, LANE)
    nj = NW // tn

    def body(x_ref, d_ref, o_ref):
        o_ref[...] = lax.dot_general(x_ref[...], d_ref[...], TN_DIMS, preferred_element_type=F32).astype(o_ref.dtype)

    return _call(
        body, [x, dy], name=name, grid=(nb * nj, Kw // tm),
        in_specs=[pl.BlockSpec((M, tm), lambda j, i: (0, i)),
                  pl.BlockSpec((M, tn), lambda j, i: (0, j))],
        out_specs=pl.BlockSpec((None, tm, tn), lambda j, i: (j // nj, i, j % nj)),
        out_shape=jax.ShapeDtypeStruct((nb, Kw, NW), BF16),
        sem=("parallel", "parallel"), comm=comm)


def _rms_fwd(x, g, *, name, tr=256):
    T, D = x.shape
    tr = _tile(T, tr, 16)

    def body(x_ref, g_ref, h_ref):
        xv = x_ref[...]
        r = lax.rsqrt(jnp.mean(xv * xv, axis=-1, keepdims=True) + RMS_EPS)
        h_ref[...] = ((xv * r) * g_ref[...]).astype(BF16)

    return pl.pallas_call(
        body, name=name, grid=(T // tr,),
        in_specs=[pl.BlockSpec((tr, D), lambda i: (i, 0)), pl.BlockSpec((1, D), lambda i: (0, 0))],
        out_specs=pl.BlockSpec((tr, D), lambda i: (i, 0)),
        out_shape=jax.ShapeDtypeStruct((T, D), BF16),
        compiler_params=_cp("parallel"))(x, g)


def _rms_bwd(dh, x, g, dres, *, name, tr=128, with_bf16=True):
    T, D = x.shape
    tr = _tile(T, tr, 16)

    def body(dh_ref, x_ref, g_ref, dr_ref, dx_ref, *rest):
        dg_ref = rest[-1]
        i = pl.program_id(0)
        xv = x_ref[...]
        dhv = dh_ref[...]
        r = lax.rsqrt(jnp.mean(xv * xv, axis=-1, keepdims=True) + RMS_EPS)
        a = dhv * g_ref[...]
        dot = jnp.mean(a * xv, axis=-1, keepdims=True)
        dx = dr_ref[...] + r * a - xv * (r * r * r * dot)
        dx_ref[...] = dx
        if with_bf16:
            rest[0][...] = dx.astype(BF16)
        part = jnp.sum(dhv * (xv * r), axis=0, keepdims=True)

        @pl.when(i == 0)
        def _():
            dg_ref[...] = part

        @pl.when(i > 0)
        def _():
            dg_ref[...] += part

    row = pl.BlockSpec((tr, D), lambda i: (i, 0))
    vec = pl.BlockSpec((1, D), lambda i: (0, 0))
    halves = [jax.ShapeDtypeStruct((T, D), BF16)] if with_bf16 else []
    return pl.pallas_call(
        body, name=name, grid=(T // tr,),
        in_specs=[row, row, vec, row], out_specs=[row] + [row] * len(halves) + [vec],
        out_shape=[jax.ShapeDtypeStruct((T, D), F32)] + halves + [jax.ShapeDtypeStruct((1, D), F32)],
        compiler_params=_cp("arbitrary"))(dh, x, g, dres)


def _final_loss(x2, tgt, g, *, name, tr=128):
    T, D = x2.shape
    tr = _tile(T, tr, 16)

    def body(x_ref, t_ref, g_ref, dx_ref, dxb_ref, dg_ref, loss_ref):
        i = pl.program_id(0)
        xv = x_ref[...]
        gv = g_ref[...]
        r = lax.rsqrt(jnp.mean(xv * xv, axis=-1, keepdims=True) + RMS_EPS)
        xh = xv * r
        diff = xh * gv - t_ref[...]
        lpart = 0.5 * jnp.sum(jnp.mean(diff * diff, axis=-1, keepdims=True))
        dy = diff * (1.0 / D)
        a = dy * gv
        dot = jnp.mean(a * xv, axis=-1, keepdims=True)
        dx = r * a - xv * (r * r * r * dot)
        dx_ref[...] = dx
        dxb_ref[...] = dx.astype(BF16)
        part = jnp.sum(dy * xh, axis=0, keepdims=True)
        lblk = jnp.full((8, LANE), lpart, F32)

        @pl.when(i == 0)
        def _():
            dg_ref[...] = part
            loss_ref[...] = lblk

        @pl.when(i > 0)
        def _():
            dg_ref[...] += part
            loss_ref[...] += lblk

    row = pl.BlockSpec((tr, D), lambda i: (i, 0))
    vec = pl.BlockSpec((1, D), lambda i: (0, 0))
    return pl.pallas_call(
        body, name=name, grid=(T // tr,),
        in_specs=[row, row, vec],
        out_specs=[row, row, vec, pl.BlockSpec((8, LANE), lambda i: (0, 0))],
        out_shape=[jax.ShapeDtypeStruct((T, D), F32), jax.ShapeDtypeStruct((T, D), BF16),
                   jax.ShapeDtypeStruct((1, D), F32), jax.ShapeDtypeStruct((8, LANE), F32)],
        compiler_params=_cp("arbitrary"))(x2, tgt, g)


def _merge_fwd(gates, b_gate, ua, ub, *, name, tr=256):
    T, D = ua.shape
    tr = _tile(T, tr, 16)

    def body(ga_ref, gb_ref, ba_ref, bb_ref, ua_ref, ub_ref, o_ref):
        sa = _sigmoid(ga_ref[...].astype(F32) + ba_ref[...])
        sb = _sigmoid(gb_ref[...].astype(F32) + bb_ref[...])
        o_ref[...] = (sa * ua_ref[...].astype(F32) + sb * ub_ref[...].astype(F32)).astype(BF16)

    row = pl.BlockSpec((tr, D), lambda i: (i, 0))
    return pl.pallas_call(
        body, name=name, grid=(T // tr,),
        in_specs=[row, pl.BlockSpec((tr, D), lambda i: (i, 1)),
                  pl.BlockSpec((1, D), lambda i: (0, 0)), pl.BlockSpec((1, D), lambda i: (0, 1)), row, row],
        out_specs=row, out_shape=jax.ShapeDtypeStruct((T, D), BF16),
        compiler_params=_cp("parallel"))(gates, gates, b_gate, b_gate, ua, ub)


def _merge_bwd(dm, gates, b_gate, ua, ub, *, name, tr=256):
    T, D = ua.shape
    tr = _tile(T, tr, 16)

    def body(dm_ref, ga_ref, gb_ref, ba_ref, bb_ref, ua_ref, ub_ref, dua_ref, dub_ref, dg_ref, db_ref):
        i = pl.program_id(0)
        dmv = dm_ref[...].astype(F32)
        sa = _sigmoid(ga_ref[...].astype(F32) + ba_ref[...])
        sb = _sigmoid(gb_ref[...].astype(F32) + bb_ref[...])
        dua_ref[...] = (dmv * sa).astype(BF16)
        dub_ref[...] = (dmv * sb).astype(BF16)
        dga = dmv * ua_ref[...].astype(F32) * (sa * (1.0 - sa))
        dgb = dmv * ub_ref[...].astype(F32) * (sb * (1.0 - sb))
        dg_ref[:, :D] = dga.astype(BF16)
        dg_ref[:, D:] = dgb.astype(BF16)
        pa = jnp.sum(dga, axis=0, keepdims=True)
        pb = jnp.sum(dgb, axis=0, keepdims=True)

        @pl.when(i == 0)
        def _():
            db_ref[:, :D] = pa
            db_ref[:, D:] = pb

        @pl.when(i > 0)
        def _():
            db_ref[:, :D] += pa
            db_ref[:, D:] += pb

    row = pl.BlockSpec((tr, D), lambda i: (i, 0))
    vec = pl.BlockSpec((1, D), lambda i: (0, 0))
    act = jax.ShapeDtypeStruct((T, D), BF16)
    return pl.pallas_call(
        body, name=name, grid=(T // tr,),
        in_specs=[row, row, pl.BlockSpec((tr, D), lambda i: (i, 1)),
                  vec, pl.BlockSpec((1, D), lambda i: (0, 1)), row, row],
        out_specs=[row, row, pl.BlockSpec((tr, 2 * D), lambda i: (i, 0)), pl.BlockSpec((1, 2 * D), lambda i: (0, 0))],
        out_shape=[act, act, jax.ShapeDtypeStruct((T, 2 * D), BF16), jax.ShapeDtypeStruct((1, 2 * D), F32)],
        compiler_params=_cp("arbitrary"))(dm, gates, gates, b_gate, b_gate, ua, ub)


def _swiglu_fwd_epi(up, gate):
    g = gate.astype(F32)
    up = up.astype(BF16).astype(F32)
    return up, g * _sigmoid(g) * up


def _swiglu_bwd_epi(d, gate, up):
    g = gate.astype(F32)
    s = _sigmoid(g)
    return d * up.astype(F32) * (s * (1.0 + g * (1.0 - s))), d * (g * s)


def _cast_into(buf, src, row_off, col_off, *, name, tr=256):
    T = buf.shape[0]
    W = src.shape[1]
    tr = _tile(T, tr, 16)
    cw = W
    while col_off % cw or W % cw:
        cw -= LANE
    assert row_off % tr == 0
    c0, r0 = col_off // cw, row_off // tr

    def body(s_ref, b_ref, o_ref):
        o_ref[...] = s_ref[...].astype(o_ref.dtype)

    return pl.pallas_call(
        body, name=name, grid=(T // tr, W // cw),
        in_specs=[pl.BlockSpec((tr, cw), lambda i, j: (i + r0, j)), pl.BlockSpec(memory_space=pl.ANY)],
        out_specs=pl.BlockSpec((tr, cw), lambda i, j: (i, c0 + j)),
        out_shape=jax.ShapeDtypeStruct(buf.shape, buf.dtype), input_output_aliases={1: 0},
        compiler_params=_cp("parallel", "parallel"))(src, buf)


def _bias_table(rel, qb, kw):
    assert qb - 1 <= MAX_REL
    ha = rel.shape[0]
    n_clipped = kw - MAX_REL
    e = jnp.concatenate([jnp.broadcast_to(rel[:, 2 * MAX_REL:], (ha, n_clipped)),
                         rel[:, 2 * MAX_REL - 1:MAX_REL - qb:-1]], axis=1)
    period = qb + kw
    e = jnp.pad(e, ((0, 0), (0, period - e.shape[1])))
    z = jnp.broadcast_to(e[:, None, :], (ha, qb, period)).reshape(ha, qb * period)
    z = z[:, :qb * (period - 1)].reshape(ha, qb, period - 1)
    i = jnp.arange(qb)[:, None]
    j = jnp.arange(kw)[None, :]
    band = (j // CHUNK >= i // CHUNK) & (j // CHUNK <= i // CHUNK + LEFT_CHUNKS)
    return jnp.where(band[None], z[:, :, qb - 1:qb - 1 + kw], NEG_INF)


def _heads_per_step(n_heads):
    return 2 if n_heads % 2 == 0 else 1


def _attn_a_fwd(q, kp, vp, btab, *, name):
    T, WA = q.shape[0], kp.shape[1]
    HA = WA // HEAD_DIM
    _, QB, KW = btab.shape
    padk = KW - QB
    G = _heads_per_step(HA)
    GW = G * HEAD_DIM

    def body(q_ref, k_ref, v_ref, b_ref, o_ref, lse_ref):
        i = pl.program_id(1)
        start = pl.multiple_of(i * QB, QB)
        kpos = start - padk + lax.broadcasted_iota(jnp.int32, (QB, KW), 1)
        for g in range(G):
            lanes = pl.ds(g * HEAD_DIM, HEAD_DIM)
            k = k_ref[pl.ds(start, KW), lanes]
            v = v_ref[pl.ds(start, KW), lanes]
            s = lax.dot_general(q_ref[:, lanes], k, NT_DIMS, preferred_element_type=F32) * ATTN_SCALE + b_ref[g]
            s = jnp.where(kpos >= 0, s, NEG_INF)
            m = jnp.max(s, axis=-1, keepdims=True)
            p = jnp.exp(s - m)
            l = jnp.sum(p, axis=-1, keepdims=True)
            o = jnp.dot(p.astype(BF16), v, preferred_element_type=F32) / l
            o_ref[:, lanes] = o.astype(BF16)
            lse_ref[:, lanes] = jnp.broadcast_to(m + jnp.log(l), (QB, HEAD_DIM))

    qspec = pl.BlockSpec((QB, GW), lambda h, i: (i, h))
    kspec = pl.BlockSpec((T + padk, GW), lambda h, i: (0, h))
    return pl.pallas_call(
        body, name=name, grid=(HA // G, T // QB),
        in_specs=[qspec, kspec, kspec, pl.BlockSpec((G, QB, KW), lambda h, i: (h, 0, 0))],
        out_specs=[qspec, qspec],
        out_shape=[jax.ShapeDtypeStruct((T, WA), BF16), jax.ShapeDtypeStruct((T, WA), F32)],
        compiler_params=_cp("parallel", "arbitrary"))(q, kp, vp, btab)


def _attn_a_bwd(q, kp, vp, btab, o, do, lse, *, name, comm=None, dq_width=None):
    T, WA = q.shape[0], kp.shape[1]
    HA = WA // HEAD_DIM
    _, QB, KW = btab.shape
    padk = KW - QB
    G = _heads_per_step(HA)
    GW = G * HEAD_DIM

    def body(q_ref, k_ref, v_ref, b_ref, o_ref, do_ref, lse_ref, dq_ref, dk_ref, dv_ref, db_ref):
        i = pl.program_id(1)

        @pl.when(i == 0)
        def _():
            dk_ref[...] = jnp.zeros_like(dk_ref)
            dv_ref[...] = jnp.zeros_like(dv_ref)
            db_ref[...] = jnp.zeros_like(db_ref)

        start = pl.multiple_of(i * QB, QB)
        kpos = start - padk + lax.broadcasted_iota(jnp.int32, (QB, KW), 1)
        for g in range(G):
            lanes = pl.ds(g * HEAD_DIM, HEAD_DIM)
            qv = q_ref[:, lanes]
            dov = do_ref[:, lanes]
            k = k_ref[pl.ds(start, KW), lanes]
            v = v_ref[pl.ds(start, KW), lanes]
            s = lax.dot_general(qv, k, NT_DIMS, preferred_element_type=F32) * ATTN_SCALE + b_ref[g]
            s = jnp.where(kpos >= 0, s, NEG_INF)
            p = jnp.exp(s - lse_ref[:, pl.ds(g * HEAD_DIM, 1)])
            dp = lax.dot_general(dov, v, NT_DIMS, preferred_element_type=F32)
            delta = jnp.sum(dov.astype(F32) * o_ref[:, lanes].astype(F32), axis=-1, keepdims=True)
            ds = p * (dp - delta)
            db_ref[g] += ds
            dsb = ds.astype(BF16)
            dq_ref[:, lanes] = (jnp.dot(dsb, k, preferred_element_type=F32) * ATTN_SCALE).astype(BF16)
            dk_ref[pl.ds(start, KW), lanes] += (
                lax.dot_general(dsb, qv, TN_DIMS, preferred_element_type=F32) * ATTN_SCALE)
            dv_ref[pl.ds(start, KW), lanes] += lax.dot_general(
                p.astype(BF16), dov, TN_DIMS, preferred_element_type=F32)

    qspec = pl.BlockSpec((QB, GW), lambda h, i: (i, h))
    kspec = pl.BlockSpec((T + padk, GW), lambda h, i: (0, h))
    bspec = pl.BlockSpec((G, QB, KW), lambda h, i: (h, 0, 0))
    return _call(
        body, [q, kp, vp, btab, o, do, lse], name=name, grid=(HA // G, T // QB),
        in_specs=[qspec, kspec, kspec, bspec, qspec, qspec, qspec],
        out_specs=[qspec, kspec, kspec, bspec],
        out_shape=[jax.ShapeDtypeStruct((T, dq_width or WA), BF16), jax.ShapeDtypeStruct((T + padk, WA), F32),
                   jax.ShapeDtypeStruct((T + padk, WA), F32), jax.ShapeDtypeStruct((HA, QB, KW), F32)],
        sem=("parallel", "arbitrary"), comm=comm)


def _relbias_fold(skew, dtab, *, name):
    HA, qb, kw = dtab.shape
    W = kw + 1

    def body(s_ref, t_ref, col_ref, tail_ref):
        row = lax.broadcasted_iota(jnp.int32, (qb, W), 0)
        col = lax.broadcasted_iota(jnp.int32, (qb, W), 1)
        col_ref[...] = jnp.sum(jnp.where(row + col < kw, s_ref[...], 0.0), axis=0, keepdims=True)
        i = lax.broadcasted_iota(jnp.int32, (qb, kw), 0)
        j = lax.broadcasted_iota(jnp.int32, (qb, kw), 1)
        tail = jnp.sum(jnp.where((kw - qb) + i - j >= MAX_REL, t_ref[...], 0.0))
        tail_ref[...] = jnp.full((1, LANE), tail, F32)

    return pl.pallas_call(
        body, name=name, grid=(HA,),
        in_specs=[pl.BlockSpec((None, qb, W), lambda h: (h, 0, 0)), pl.BlockSpec((None, qb, kw), lambda h: (h, 0, 0))],
        out_specs=[pl.BlockSpec((None, 1, W), lambda h: (h, 0, 0)), pl.BlockSpec((None, 1, LANE), lambda h: (h, 0, 0))],
        out_shape=[jax.ShapeDtypeStruct((HA, 1, W), F32), jax.ShapeDtypeStruct((HA, 1, LANE), F32)],
        compiler_params=_cp("parallel"))(skew, dtab)


CUM_BLOCK = 128


def _fgate_fwd(f, bf, *, name):
    T = f.shape[0]
    nb = T // CUM_BLOCK

    def body(f_ref, b_ref, cum_ref):
        r = lax.broadcasted_iota(jnp.int32, (CUM_BLOCK, CUM_BLOCK), 0)
        c = lax.broadcasted_iota(jnp.int32, (CUM_BLOCK, CUM_BLOCK), 1)
        tri = (c <= r).astype(F32)

        def step(b, carry):
            r0 = pl.multiple_of(b * CUM_BLOCK, CUM_BLOCK)
            z = f_ref[pl.ds(r0, CUM_BLOCK), :] + b_ref[...]
            lf = jnp.minimum(z, 0.0) - jnp.log(1.0 + jnp.exp(-jnp.abs(z)))
            cs = jnp.dot(tri, lf, precision=lax.Precision.HIGHEST, preferred_element_type=F32) + carry
            cum_ref[pl.ds(r0, CUM_BLOCK), :] = cs
            return carry + jnp.sum(lf, axis=0, keepdims=True)

        lax.fori_loop(0, nb, step, jnp.zeros((1, LANE), F32))

    return pl.pallas_call(
        body, name=name, out_shape=jax.ShapeDtypeStruct((T, LANE), F32),
        compiler_params=_cp())(f, bf)


def _fgate_bwd(dcq, dck, f, bf, *, name):
    T = f.shape[0]
    nb = T // CUM_BLOCK

    def body(dq_ref, dk_ref, f_ref, b_ref, df_ref, dbf_ref):
        r = lax.broadcasted_iota(jnp.int32, (CUM_BLOCK, CUM_BLOCK), 0)
        c = lax.broadcasted_iota(jnp.int32, (CUM_BLOCK, CUM_BLOCK), 1)
        tri = (c >= r).astype(F32)

        def step(n, carry):
            tail, dbf = carry
            r0 = pl.multiple_of((nb - 1 - n) * CUM_BLOCK, CUM_BLOCK)
            dc = dq_ref[pl.ds(r0, CUM_BLOCK), :] + dk_ref[pl.ds(r0, CUM_BLOCK), :]
            ss = jnp.dot(tri, dc, precision=lax.Precision.HIGHEST, preferred_element_type=F32) + tail
            z = f_ref[pl.ds(r0, CUM_BLOCK), :] + b_ref[...]
            df = ss * _sigmoid(-z)
            df_ref[pl.ds(r0, CUM_BLOCK), :] = df
            return tail + jnp.sum(dc, axis=0, keepdims=True), dbf + jnp.sum(df, axis=0, keepdims=True)

        zero = jnp.zeros((1, LANE), F32)
        _, dbf = lax.fori_loop(0, nb, step, (zero, zero))
        dbf_ref[...] = dbf

    return pl.pallas_call(
        body, name=name,
        out_shape=[jax.ShapeDtypeStruct((T, LANE), F32), jax.ShapeDtypeStruct((1, LANE), F32)],
        compiler_params=_cp())(dcq, dck, f, bf)


def _fox_logits(qv, kk, cq, ck, diagonal):
    s = lax.dot_general(qv, kk, NT_DIMS, preferred_element_type=F32) * ATTN_SCALE + cq - ck
    if diagonal:
        row = lax.broadcasted_iota(jnp.int32, s.shape, 0)
        col = lax.broadcasted_iota(jnp.int32, s.shape, 1)
        s = jnp.where(col <= row, s, NEG_INF)
    return s


def _fox_heads_per_step(n_heads, cols):
    return 2 if n_heads % 2 == 0 and all(c % 2 == 0 for c in cols) else 1


def _fox_fwd(qkv, cols, cum, cum_t, *, name, comm=None):
    T = qkv.shape[0]
    HB, _, KB = cum_t.shape
    WB = HB * HEAD_DIM
    QB = KB
    G = _fox_heads_per_step(HB, cols)
    GW = G * HEAD_DIM
    qc, kc, vc = (c // G for c in cols)

    def body(q_ref, k_ref, v_ref, c_ref, ct_ref, o_ref, lse_ref):
        h = pl.program_id(0)
        i = pl.program_id(1)
        lane = lax.broadcasted_iota(jnp.int32, (QB, LANE), 1)
        heads = [pl.ds(g * HEAD_DIM, HEAD_DIM) for g in range(G)]
        qs = [q_ref[:, lanes] for lanes in heads]
        cqs = [jnp.sum(jnp.where(lane == h * G + g, c_ref[...], 0.0), axis=-1, keepdims=True) for g in range(G)]

        def block(kb, carry, diagonal):
            k0 = pl.multiple_of(kb * KB, KB)
            out = []
            for g, (m, l, acc) in enumerate(carry):
                s = _fox_logits(qs[g], k_ref[pl.ds(k0, KB), heads[g]], cqs[g], ct_ref[g, pl.ds(kb, 1), :], diagonal)
                m_new = jnp.maximum(m, jnp.max(s, axis=-1, keepdims=True))
                alpha = jnp.exp(m - m_new)
                p = jnp.exp(s - m_new)
                l = alpha * l + jnp.sum(p, axis=-1, keepdims=True)
                acc = alpha * acc + jnp.dot(p.astype(BF16), v_ref[pl.ds(k0, KB), heads[g]],
                                            preferred_element_type=F32)
                out.append((m_new, l, acc))
            return tuple(out)

        start = tuple((jnp.full((QB, 1), NEG_INF, F32), jnp.zeros((QB, 1), F32), jnp.zeros((QB, HEAD_DIM), F32))
                      for _ in range(G))
        past = lax.fori_loop(0, i, lambda kb, carry: block(kb, carry, False), start)
        for g, (m, l, acc) in enumerate(block(i, past, True)):
            o_ref[:, heads[g]] = (acc / l).astype(BF16)
            lse_ref[:, heads[g]] = jnp.broadcast_to(m + jnp.log(l), (QB, HEAD_DIM))

    ospec = pl.BlockSpec((QB, GW), lambda h, i: (i, h))
    return _call(
        body, [qkv, qkv, qkv, cum, cum_t], name=name, grid=(HB // G, T // QB),
        in_specs=[pl.BlockSpec((QB, GW), lambda h, i: (i, qc + h)),
                  pl.BlockSpec((T, GW), lambda h, i: (0, kc + h)),
                  pl.BlockSpec((T, GW), lambda h, i: (0, vc + h)),
                  pl.BlockSpec((QB, LANE), lambda h, i: (i, 0)),
                  pl.BlockSpec((G, T // KB, KB), lambda h, i: (h, 0, 0))],
        out_specs=[ospec, ospec],
        out_shape=[jax.ShapeDtypeStruct((T, WB), BF16), jax.ShapeDtypeStruct((T, WB), F32)],
        sem=("parallel", "arbitrary"), comm=comm)


def _fox_bwd(qkv, cols, cum, cum_t, o, do, lse, *, name, comm=None, dq_into=None):
    T = qkv.shape[0]
    HB, _, KB = cum_t.shape
    WB = HB * HEAD_DIM
    QB = KB
    G = _fox_heads_per_step(HB, cols + ((dq_into[1],) if dq_into is not None else ()))
    GW = G * HEAD_DIM
    qc, kc, vc = (c // G for c in cols)

    def body(*refs):
        q_ref, k_ref, v_ref, c_ref, ct_ref, o_ref, do_ref, lse_ref = refs[:8]
        dq_ref, dk_ref, dv_ref, dcq_ref, dck_ref = refs[-5:]
        h = pl.program_id(0)
        i = pl.program_id(1)

        @pl.when(i == 0)
        def _():
            dk_ref[...] = jnp.zeros_like(dk_ref)
            dv_ref[...] = jnp.zeros_like(dv_ref)
            dck_ref[...] = jnp.zeros_like(dck_ref)

        lane = lax.broadcasted_iota(jnp.int32, (QB, LANE), 1)
        heads = [pl.ds(g * HEAD_DIM, HEAD_DIM) for g in range(G)]
        qs = [q_ref[:, lanes] for lanes in heads]
        dos = [do_ref[:, lanes] for lanes in heads]
        lses = [lse_ref[:, pl.ds(g * HEAD_DIM, 1)] for g in range(G)]
        deltas = [jnp.sum(dos[g].astype(F32) * o_ref[:, heads[g]].astype(F32), axis=-1, keepdims=True)
                  for g in range(G)]
        cqs = [jnp.sum(jnp.where(lane == h * G + g, c_ref[...], 0.0), axis=-1, keepdims=True) for g in range(G)]

        def block(kb, carry, diagonal):
            k0 = pl.multiple_of(kb * KB, KB)
            out = []
            for g, (dq, dcq) in enumerate(carry):
                kk = k_ref[pl.ds(k0, KB), heads[g]]
                s = _fox_logits(qs[g], kk, cqs[g], ct_ref[g, pl.ds(kb, 1), :], diagonal)
                p = jnp.exp(s - lses[g])
                dp = lax.dot_general(dos[g], v_ref[pl.ds(k0, KB), heads[g]], NT_DIMS, preferred_element_type=F32)
                ds = p * (dp - deltas[g])
                dsb = ds.astype(BF16)
                dk_ref[pl.ds(k0, KB), heads[g]] += (
                    lax.dot_general(dsb, qs[g], TN_DIMS, preferred_element_type=F32) * ATTN_SCALE)
                dv_ref[pl.ds(k0, KB), heads[g]] += lax.dot_general(
                    p.astype(BF16), dos[g], TN_DIMS, preferred_element_type=F32)
                dck_ref[g, pl.ds(kb, 1), :] -= jnp.sum(ds, axis=0, keepdims=True)
                out.append((dq + jnp.dot(dsb, kk, preferred_element_type=F32),
                            dcq + jnp.sum(ds, axis=-1, keepdims=True)))
            return tuple(out)

        start = tuple((jnp.zeros((QB, HEAD_DIM), F32), jnp.zeros((QB, 1), F32)) for _ in range(G))
        past = lax.fori_loop(0, i, lambda kb, carry: block(kb, carry, False), start)
        for g, (dq, dcq) in enumerate(block(i, past, True)):
            dq_ref[:, heads[g]] = (dq * ATTN_SCALE).astype(BF16)
            dcq_ref[:, heads[g]] = jnp.broadcast_to(dcq, (QB, HEAD_DIM))

    qspec = pl.BlockSpec((QB, GW), lambda h, i: (i, h))
    kspec = pl.BlockSpec((T, GW), lambda h, i: (0, h))
    cspec = pl.BlockSpec((QB, LANE), lambda h, i: (i, 0))
    tspec = pl.BlockSpec((G, T // KB, KB), lambda h, i: (h, 0, 0))
    wide = jax.ShapeDtypeStruct((T, WB), F32)
    args = [qkv, qkv, qkv, cum, cum_t, o, do, lse]
    in_specs = [pl.BlockSpec((QB, GW), lambda h, i: (i, qc + h)),
                pl.BlockSpec((T, GW), lambda h, i: (0, kc + h)),
                pl.BlockSpec((T, GW), lambda h, i: (0, vc + h)),
                cspec, tspec, qspec, qspec, qspec]
    dq_spec, dq_shape, aliases = qspec, jax.ShapeDtypeStruct((T, WB), BF16), None
    if dq_into is not None:
        buf, dq_col = dq_into
        args.append(buf)
        in_specs.append(pl.BlockSpec(memory_space=pl.ANY))
        dq_spec = pl.BlockSpec((QB, GW), lambda h, i: (i, dq_col // G + h))
        dq_shape, aliases = jax.ShapeDtypeStruct(buf.shape, buf.dtype), {8: 0}
    return _call(
        body, args, name=name, grid=(HB // G, T // QB), in_specs=in_specs,
        out_specs=[dq_spec, kspec, kspec, qspec, tspec],
        out_shape=[dq_shape, wide, wide, wide, jax.ShapeDtypeStruct((HB, T // KB, KB), F32)],
        sem=("parallel", "arbitrary"), comm=comm, aliases=aliases)


def _place():
    return lax.axis_index("x"), lax.axis_index("y"), lax.axis_index("c")


def _other_chips(x, y):
    return [(1 - x, y), (x, 1 - y), (1 - x, 1 - y)]


def _gather_comm(shards):
    n = len(shards)

    def copies(srcs, outs, scratch, want):
        send_sems, recv_sems, local_sems = scratch
        x, y, c = _place()
        me, sib = (x, y, c), (x, y, 1 - c)
        chip_a = (jnp.bitwise_xor(x, c), jnp.bitwise_xor(y, 1 - c))
        chip_b = (jnp.bitwise_xor(x, 1 - c), jnp.bitwise_xor(y, c))
        chip_d = (1 - x, 1 - y)

        def copy(t, k, block, to, src=None):
            rows = outs[t].at[4 * block[0] + 2 * block[1] + block[2]]
            return pltpu.make_async_remote_copy(
                src_ref=rows if src is None else src, dst_ref=rows,
                send_sem=send_sems.at[7 * t + k], recv_sem=recv_sems.at[7 * t + k],
                device_id=to, device_id_type=MESH)

        got = {}
        if "mine" in want:
            got["mine"] = [pltpu.make_async_copy(srcs[t], outs[t].at[4 * x + 2 * y + c], local_sems.at[t])
                           for t in range(n)]
        if "first" in want:
            got["first"] = [cp for t in range(n) for cp in
                            (copy(t, 0, me, sib, src=srcs[t]), copy(t, 1, me, (*chip_a, c), src=srcs[t]),
                             copy(t, 2, me, (*chip_b, c), src=srcs[t]))]
        if "from_a" in want:
            got["from_a"] = [copy(t, 1, (*chip_a, c), me) for t in range(n)]
        if "hop" in want:
            got["hop"] = [copy(t, 3, (*chip_a, c), (*chip_b, c)) for t in range(n)]
        if "from_b" in want:
            got["from_b"] = [copy(t, 2, (*chip_b, c), me) for t in range(n)]
        if "from_d" in want:
            got["from_d"] = [copy(t, 3, (*chip_d, c), me) for t in range(n)]
        for key, k, chip in (("pass_a", 4, chip_a), ("pass_b", 5, chip_b), ("pass_d", 6, chip_d)):
            if key in want:
                got[key] = [copy(t, k, (*chip, c), sib) for t in range(n)]
        if "late" in want:
            got["late"] = [cp for t in range(n) for cp in
                           (copy(t, 0, sib, me), copy(t, 4, (*chip_b, 1 - c), me),
                            copy(t, 5, (*chip_a, 1 - c), me), copy(t, 6, (*chip_d, 1 - c), me))]
        return got

    def begin(srcs, outs, scratch):
        cps = copies(srcs, outs, scratch, ("mine", "first"))
        for cp in cps["mine"] + cps["first"]:
            cp.start()

    def hop(srcs, outs, scratch):
        cps = copies(srcs, outs, scratch, ("from_a", "hop", "pass_a"))
        for landed, fwd, sibling in zip(cps["from_a"], cps["hop"], cps["pass_a"]):
            landed.wait_recv()
            fwd.start()
            sibling.start()

    def pass_on(srcs, outs, scratch):
        cps = copies(srcs, outs, scratch, ("from_b", "pass_b", "from_d", "pass_d"))
        for landed, sibling in list(zip(cps["from_b"], cps["pass_b"])) + list(zip(cps["from_d"], cps["pass_d"])):
            landed.wait_recv()
            sibling.start()

    def end(srcs, outs, scratch):
        cps = copies(srcs, outs, scratch, ("mine", "first", "hop", "pass_a", "pass_b", "pass_d", "late"))
        for cp in cps["late"]:
            cp.wait_recv()
        for cp in cps["first"] + cps["hop"] + cps["pass_a"] + cps["pass_b"] + cps["pass_d"]:
            cp.wait_send()
        for cp in cps["mine"]:
            cp.wait()

    return _Comm(
        shards, [jax.ShapeDtypeStruct((N_DEV,) + s.shape, s.dtype) for s in shards],
        [pltpu.SemaphoreType.DMA((7 * n,)), pltpu.SemaphoreType.DMA((7 * n,)), pltpu.SemaphoreType.DMA((n,))],
        begin, end, [(0.67, hop), (0.9, pass_on)])


def _sibling_comm(grads):
    n = len(grads)

    def copies(srcs, outs, scratch):
        send_sems, recv_sems = scratch
        x, y, c = _place()
        return [pltpu.make_async_remote_copy(
            src_ref=srcs[t].at[2 * p + (1 - c)], dst_ref=outs[t].at[p],
            send_sem=send_sems.at[4 * t + p], recv_sem=recv_sems.at[4 * t + p],
            device_id=(x, y, 1 - c), device_id_type=MESH) for t in range(n) for p in range(4)]

    def begin(srcs, outs, scratch):
        for cp in copies(srcs, outs, scratch):
            cp.start()

    def end(srcs, outs, scratch):
        cps = copies(srcs, outs, scratch)
        for cp in cps:
            cp.wait_recv()
        for cp in cps:
            cp.wait_send()

    return _Comm(
        grads, [jax.ShapeDtypeStruct((4,) + g.shape[1:], g.dtype) for g in grads],
        [pltpu.SemaphoreType.DMA((4 * n,)), pltpu.SemaphoreType.DMA((4 * n,))], begin, end)


def _chips_comm(sums, relations):
    n, nr = len(sums), len(relations)

    def copies(srcs, outs, scratch):
        send_sems, recv_sems = scratch
        x, y, c = _place()
        chips = _other_chips(x, y)
        return [pltpu.make_async_remote_copy(
            src_ref=srcs[t].at[2 * chips[j][0] + chips[j][1]], dst_ref=outs[t].at[s],
            send_sem=send_sems.at[nr * t + s], recv_sem=recv_sems.at[nr * t + s],
            device_id=(*chips[j], c), device_id_type=MESH) for t in range(n) for s, j in enumerate(relations)]

    def begin(srcs, outs, scratch):
        for cp in copies(srcs, outs, scratch):
            cp.start()

    def end(srcs, outs, scratch):
        cps = copies(srcs, outs, scratch)
        for cp in cps:
            cp.wait_recv()
        for cp in cps:
            cp.wait_send()

    return _Comm(
        sums, [jax.ShapeDtypeStruct((nr,) + s.shape[1:], s.dtype) for s in sums],
        [pltpu.SemaphoreType.DMA((nr * n,)), pltpu.SemaphoreType.DMA((nr * n,))], begin, end)


def _run_comm(comm, *, name):
    ci, co = len(comm.ins), len(comm.outs)

    def body(*refs):
        cin, cout, csc = refs[:ci], refs[ci:ci + co], refs[ci + co:]
        comm.begin(cin, cout, csc)
        for _, relay in comm.relays:
            relay(cin, cout, csc)
        comm.end(cin, cout, csc)

    return pl.pallas_call(
        body, name=name, in_specs=_any_specs(ci), out_specs=_any_specs(co), out_shape=comm.outs,
        scratch_shapes=comm.scratch)(*comm.ins)


def _chip_sum(g, r1, core, *, name, tr=256):
    _, R, C = g.shape
    tr = _tile(R, tr, 16)

    def body(c_ref, g_ref, r_ref, o_ref):
        o_ref[...] = (g_ref[...].astype(F32) + r_ref[...].astype(F32)).astype(BF16)

    return pl.pallas_call(
        body, name=name,
        grid_spec=pltpu.PrefetchScalarGridSpec(
            num_scalar_prefetch=1, grid=(4, R // tr),
            in_specs=[pl.BlockSpec((None, tr, C), lambda p, i, c: (2 * p + c[0], i, 0)),
                      pl.BlockSpec((None, tr, C), lambda p, i, c: (p, i, 0))],
            out_specs=pl.BlockSpec((None, tr, C), lambda p, i, c: (p, i, 0))),
        out_shape=jax.ShapeDtypeStruct((4, R, C), BF16),
        compiler_params=_cp("parallel", "parallel"))(core, g, r1)


def _adamw_math(w, g, m, v):
    m = ADAM_B1 * m + (1.0 - ADAM_B1) * g
    v = ADAM_B2 * v + (1.0 - ADAM_B2) * (g * g)
    m_hat = m / (1.0 - ADAM_B1 ** ADAM_STEP)
    v_hat = v / (1.0 - ADAM_B2 ** ADAM_STEP)
    delta = -ADAM_LR * (m_hat / (jnp.sqrt(v_hat) + ADAM_EPS) + ADAM_WD * w)
    return delta, m, v


def _adamw_shard(s1, others, chip, w, m, v, *, name, tr=64):
    R0, C0 = w.shape
    C = s1.shape[2]
    tr = _tile(R0, tr, 8)
    (a0, k0), (a1, k1), (a2, k2) = others

    def body(p_ref, s_ref, r0_ref, r1_ref, r2_ref, w_ref, m_ref, v_ref, g_out, d_out, m_out, v_out):
        g = s_ref[:, :C0].astype(F32)
        for r_ref in (r0_ref, r1_ref, r2_ref):
            g = g + r_ref[:, :C0].astype(F32)
        delta, mn, vn = _adamw_math(w_ref[...], g, m_ref[...], v_ref[...])
        g_out[...] = g
        d_out[...] = delta
        m_out[...] = mn
        v_out[...] = vn

    full = pl.BlockSpec((tr, C0), lambda i, p: (i, 0))
    out = jax.ShapeDtypeStruct((R0, C0), F32)
    return pl.pallas_call(
        body, name=name,
        grid_spec=pltpu.PrefetchScalarGridSpec(
            num_scalar_prefetch=1, grid=(R0 // tr,),
            in_specs=[pl.BlockSpec((None, tr, C), lambda i, p: (p[0], i, 0)),
                      pl.BlockSpec((None, tr, C), lambda i, p: (k0, i, 0)),
                      pl.BlockSpec((None, tr, C), lambda i, p: (k1, i, 0)),
                      pl.BlockSpec((None, tr, C), lambda i, p: (k2, i, 0)),
                      full, full, full],
            out_specs=[full, full, full, full]),
        out_shape=[out, out, out, out],
        compiler_params=_cp("parallel"))(chip, s1, a0, a1, a2, w, m, v)


def _all_reduce_small(part, *, name):
    SR = part.shape[0]

    def body(p_ref, o_ref, buf, send_sems, recv_sems):
        x, y, c = _place()
        me = 4 * x + 2 * y + c
        buf[0] = p_ref[...]
        copies = []
        for k in range(1, N_DEV):
            peer = (1 - x if k & 4 else x, 1 - y if k & 2 else y, 1 - c if k & 1 else c)
            copies.append(pltpu.make_async_remote_copy(
                src_ref=p_ref, dst_ref=buf.at[k], send_sem=send_sems.at[k - 1], recv_sem=recv_sems.at[k - 1],
                device_id=peer, device_id_type=MESH))
        for cp in copies:
            cp.start()
        for cp in copies:
            cp.wait_recv()
        for cp in copies:
            cp.wait_send()
        acc = buf[me]
        for d in range(1, N_DEV):
            acc = acc + buf[jnp.bitwise_xor(me, d)]
        o_ref[...] = acc

    return pl.pallas_call(
        body, name=name,
        in_specs=[pl.BlockSpec(memory_space=pltpu.VMEM)], out_specs=pl.BlockSpec(memory_space=pltpu.VMEM),
        out_shape=jax.ShapeDtypeStruct(part.shape, F32),
        scratch_shapes=[pltpu.VMEM((N_DEV, SR, LANE), F32), pltpu.SemaphoreType.DMA((N_DEV - 1,)),
                        pltpu.SemaphoreType.DMA((N_DEV - 1,))])(part)


def _adamw_small(w, g, m, v, *, name):
    def body(w_ref, g_ref, m_ref, v_ref, d_out, m_out, v_out):
        delta, mn, vn = _adamw_math(w_ref[...], g_ref[...], m_ref[...], v_ref[...])
        d_out[...] = delta
        m_out[...] = mn
        v_out[...] = vn

    out = jax.ShapeDtypeStruct(w.shape, F32)
    return pl.pallas_call(body, name=name, out_shape=[out, out, out], compiler_params=_cp())(w, g, m, v)


def _in_layout(din8, fq):
    lay = []
    for b in range(N_DEV):
        g0 = din8 * b
        nq = min(max(fq - g0, 0), din8)
        oq = g0 % LANE if nq else 0
        nqt = -(-(oq + nq) // LANE) if nq else 0
        tq0 = (g0 - oq) // LANE
        pg, tg0, ngt = oq + nq, 0, 0
        if nq < din8:
            u0 = g0 + nq - fq
            pg = nqt * LANE + u0 % LANE
            tg0, ngt = u0 // LANE, -(-(u0 % LANE + din8 - nq) // LANE)
        lay.append(dict(nq=nq, oq=oq, tq0=tq0, nqt=nqt, pg=pg, tg0=tg0, ngt=ngt, width=pg + din8 - nq))
    return lay


def _place_block(shard, lay, pin, b):
    rows, din8 = shard.shape

    def place(s, l):
        nq = l["nq"]
        parts, at = [], 0
        for start, cols in ((l["oq"], s[:, :nq]), (l["pg"], s[:, nq:])):
            if cols.shape[1]:
                parts += [jnp.zeros((rows, start - at), s.dtype), cols]
                at = start + cols.shape[1]
        parts.append(jnp.zeros((rows, pin - at), s.dtype))
        return jnp.concatenate([p for p in parts if p.shape[1]], axis=1)

    return lax.switch(b, [functools.partial(place, l=l) for l in lay], shard)


def _unplace_block(g, lay, din8, b):
    def unplace(a, l):
        nq = l["nq"]
        parts = [a[:, l["oq"]:l["oq"] + nq], a[:, l["pg"]:l["pg"] + din8 - nq]]
        return jnp.concatenate([p for p in parts if p.shape[1]], axis=1)

    return lax.switch(b, [functools.partial(unplace, l=l) for l in lay], g)


def _tile_runs(sources):
    runs = []
    for src in sources:
        if len(src) == 1 and runs and not isinstance(runs[-1], list) \
                and runs[-1][0] == src[0][0] and runs[-1][1] + runs[-1][2] == src[0][1]:
            runs[-1] = (runs[-1][0], runs[-1][1], runs[-1][2] + 1)
        elif len(src) == 1:
            runs.append((src[0][0], src[0][1], 1))
        else:
            runs.append(list(src))
    return runs


def _gather_tiles(arrays, sources):
    rows, dtype = arrays[0].shape[0], arrays[0].dtype
    pieces = []
    for run in _tile_runs(sources):
        if isinstance(run, list):
            tile = jnp.zeros((rows, LANE), dtype)
            for a, t in run:
                tile = tile + arrays[a][:, t * LANE:(t + 1) * LANE]
            pieces.append(tile)
        else:
            a, t0, n = run
            pieces.append(arrays[a][:, t0 * LANE:(t0 + n) * LANE])
    return jnp.concatenate(pieces, axis=1)


def _assemble_w_in(blocks, lay, n_qkv_tiles, n_f_tiles, n_g_tiles):
    q_src = [[] for _ in range(n_qkv_tiles + n_f_tiles)]
    g_src = [[] for _ in range(n_g_tiles)]
    for b, l in enumerate(lay):
        for j in range(l["nqt"]):
            q_src[l["tq0"] + j].append((b, j))
        for j in range(l["ngt"]):
            g_src[l["tg0"] + j].append((b, l["pg"] // LANE + j))
    arrays = [blocks[b] for b in range(N_DEV)]
    return (_gather_tiles(arrays, q_src[:n_qkv_tiles]), _gather_tiles(arrays, q_src[n_qkv_tiles:]),
            _gather_tiles(arrays, g_src))


def _scatter_g_in(g_qkv, g_f, g_gates, lay, pin):
    n_qkv_tiles = g_qkv.shape[1] // LANE
    out = []
    for l in lay:
        src = [[] for _ in range(pin // LANE)]
        for j in range(l["nqt"]):
            t = l["tq0"] + j
            src[j] = [(0, t)] if t < n_qkv_tiles else [(1, t - n_qkv_tiles)]
        for j in range(l["ngt"]):
            src[l["pg"] // LANE + j] = [(2, l["tg0"] + j)]
        out.append(_gather_tiles([g_qkv, g_f, g_gates], src))
    return jnp.stack(out)


def _pack(parts, rows):
    flat = jnp.concatenate([p.reshape(-1).astype(F32) for p in parts])
    return jnp.pad(flat, (0, rows * LANE - flat.shape[0])).reshape(rows, LANE)


def _unpack(packed, shapes):
    flat = packed.reshape(-1)
    out, at = [], 0
    for s in shapes:
        n = 1
        for d in s:
            n *= d
        out.append(flat[at:at + n].reshape(s))
        at += n
    return out


def kernel(x, g_mix, w_in, b_f, b_gate, rel_bias, w_branch_a, w_branch_b, w_out, g_ffn, w_gate_ffn, w_up_ffn, w_down_ffn, g_final, loss_target, m_g_mix, m_w_in, m_b_f, m_b_gate, m_rel_bias, m_w_branch_a, m_w_branch_b, m_w_out, m_g_ffn, m_w_gate_ffn, m_w_up_ffn, m_w_down_ffn, m_g_final, v_g_mix, v_w_in, v_b_f, v_b_gate, v_rel_bias, v_w_branch_a, v_w_branch_b, v_w_out, v_g_ffn, v_w_gate_ffn, v_w_up_ffn, v_w_down_ffn, v_g_final):
    _, T, D = x.shape
    HA, n_rel = rel_bias.shape[1], rel_bias.shape[2]
    HB = b_f.shape[1]
    WA, WB = HA * HEAD_DIM, HB * HEAD_DIM
    din8, f8, d8 = w_in.shape[2], w_gate_ffn.shape[2], D // N_DEV
    DIN = N_DEV * din8
    PF = _pad_to(f8, LANE)
    fcol = 3 * WA + 3 * WB
    lay = _in_layout(din8, fcol + HB)
    PIN = _pad_to(max(l["width"] for l in lay), LANE)
    padk = LEFT_CHUNKS * CHUNK
    QA = min(256, T)
    KW = QA + padk
    KB = min(512, T)
    x2d, tgt = x[0], loss_target[0]
    cx, cy, cc = _place()
    blk = 4 * cx + 2 * cy + cc
    core = jnp.reshape(cc, (1,)).astype(jnp.int32)
    chip = jnp.reshape(2 * cx + cy, (1,)).astype(jnp.int32)
    every = [0, 1, 2]

    s_in = _place_block(w_in[0].astype(BF16), lay, PIN, blk)
    s_a, s_b, s_out = w_branch_a[0].astype(BF16), w_branch_b[0].astype(BF16), w_out[0].astype(BF16)
    s_g = jnp.pad(w_gate_ffn[0].T.astype(BF16), ((0, PF - f8), (0, 0)))
    s_u = jnp.pad(w_up_ffn[0].T.astype(BF16), ((0, PF - f8), (0, 0)))
    s_d = jnp.pad(w_down_ffn[0].astype(BF16), ((0, PF - f8), (0, 0)))
    Win, = _run_comm(_gather_comm([s_in]), name="gather_w_in")
    Wqkv, Wf, Wgt = _assemble_w_in(Win, lay, fcol // LANE, 1, 2 * D // LANE)
    Wqkv, Wf, Wgt = Wqkv.reshape(1, D, fcol), Wf.reshape(1, D, LANE), Wgt.reshape(1, D, 2 * D)
    bf_pad = jnp.pad(b_f, ((0, 0), (0, LANE - HB)))
    fox_cols = (3 * HA, 3 * HA + HB, 3 * HA + 2 * HB)

    h = _rms_fwd(x2d, g_mix, name="rms_mix")
    qkv, (Wg,) = _mm_nn(h, Wqkv, name="proj_qkv", out_dtype=BF16, comm=_gather_comm([s_g]))
    Wg = Wg.reshape(1, N_DEV * PF, D)
    gates, (Wa, Wb, Wout) = _mm_nn(h, Wgt, name="proj_gates", out_dtype=BF16, comm=_gather_comm([s_a, s_b, s_out]))
    Wout = Wout.reshape(1, D, D)
    flog = _mm_nn(h, Wf, name="proj_forget", out_dtype=F32)

    kap = jnp.pad(qkv[:, WA:2 * WA], ((padk, 0), (0, 0)))
    vap = jnp.pad(qkv[:, 2 * WA:3 * WA], ((padk, 0), (0, 0)))
    btab = _bias_table(rel_bias[0], QA, KW)
    o_a, lse_a = _attn_a_fwd(qkv, kap, vap, btab, name="attn_a_fwd")

    cum = _fgate_fwd(flog, bf_pad, name="forget_cumsum")
    cum_t = cum[:, :HB].T.reshape(HB, T // KB, KB)
    (o_b, lse_b), (Wu,) = _fox_fwd(qkv, fox_cols, cum, cum_t, name="fox_fwd", comm=_gather_comm([s_u]))
    Wu = Wu.reshape(1, N_DEV * PF, D)

    u_a = _mm_nn(o_a, Wa, name="branch_a", out_dtype=BF16)
    u_b = _mm_nn(o_b, Wb, name="branch_b", out_dtype=BF16)
    merged = _merge_fwd(gates, b_gate, u_a, u_b, name="merge_fwd")
    x1 = _mm_nn(merged, Wout, name="out_proj", out_dtype=F32, res=x2d)
    h2 = _rms_fwd(x1, g_ffn, name="rms_ffn")
    gt, (Wd,) = _mm_nt(h2, Wg, name="ffn_gate", out_dtype=BF16, tc=D, comm=_gather_comm([s_d]))
    Wd = Wd.reshape(1, N_DEV * PF, D)
    up, hid = _mm_nt(h2, Wu, name="ffn_up", out_dtype=BF16, epi=_swiglu_fwd_epi, extras=[gt], n_out=2)
    x2 = _mm_nn(hid, Wd, name="ffn_down", out_dtype=F32, res=x1, tk=2 * PF)
    dx2, dx2b, dg_final, loss_part = _final_loss(x2, tgt, g_final.reshape(1, D), name="final_loss")

    dgt, dup = _mm_nt(dx2b, Wd, name="d_hidden", out_dtype=BF16, epi=_swiglu_bwd_epi, extras=[gt, up], n_out=2)
    gWd = _mm_tn(hid, dx2b, 1, name="gw_down").reshape(N_DEV, PF, D)
    dh2, (sWd,) = _mm_nn(dgt, Wg, name="d_h2_gate", out_dtype=F32, tk=2 * PF, comm=_sibling_comm([gWd]))
    cWd = _chip_sum(gWd, sWd, core, name="chip_sum_w_down_ffn")
    dh2, (rWd,) = _mm_nn(dup, Wu, name="d_h2_up", out_dtype=F32, res=dh2, tk=2 * PF,
                         comm=_chips_comm([cWd], every))
    gWg = _mm_tn(dgt, h2, 1, name="gw_gate").reshape(N_DEV, PF, D)
    gWu = _mm_tn(dup, h2, 1, name="gw_up").reshape(N_DEV, PF, D)
    dx1, dx1b, dg_ffn = _rms_bwd(dh2, x1, g_ffn, dx2, name="rms_ffn_bwd")
    dmer, (sWg, sWu) = _mm_nt(dx1b, Wout, name="d_merged", out_dtype=BF16, comm=_sibling_comm([gWg, gWu]))
    cWg = _chip_sum(gWg, sWg, core, name="chip_sum_w_gate_ffn")
    cWu = _chip_sum(gWu, sWu, core, name="chip_sum_w_up_ffn")
    gWout = _mm_tn(merged, dx1b, 1, name="gw_out").reshape(N_DEV, d8, D)
    dua, dub, dgates, dbg = _merge_bwd(dmer, gates, b_gate, u_a, u_b, name="merge_bwd")
    g_gates = _mm_tn(h, dgates, 1, name="gw_gates")[0]
    doa = _mm_nt(dua, Wa, name="d_o_a", out_dtype=BF16)
    dob = _mm_nt(dub, Wb, name="d_o_b", out_dtype=BF16)
    gWa = _mm_tn(o_a, dua, N_DEV, name="gw_branch_a")
    gWb = _mm_tn(o_b, dub, N_DEV, name="gw_branch_b")

    (dqkv, dkap, dvap, dtab), (sWout, sWa, sWb) = _attn_a_bwd(
        qkv, kap, vap, btab, o_a, doa, lse_a, name="attn_a_bwd", comm=_sibling_comm([gWout, gWa, gWb]),
        dq_width=fcol)
    cWout = _chip_sum(gWout, sWout, core, name="chip_sum_w_out")
    cWa = _chip_sum(gWa, sWa, core, name="chip_sum_w_branch_a")
    cWb = _chip_sum(gWb, sWb, core, name="chip_sum_w_branch_b")
    skew = jnp.pad(dtab.reshape(HA, QA * KW), ((0, 0), (0, QA))).reshape(HA, QA, KW + 1)
    colsum, tail = _relbias_fold(skew, dtab, name="relbias_fold")
    rev = colsum[:, 0, ::-1]
    d_rel = jnp.concatenate([rev[:, :2 * MAX_REL], tail[:, 0, :1]], axis=1)

    (dqkv, dkb, dvb, dcq, dck), (rWg, rWout, rWa, rWb) = _fox_bwd(
        qkv, fox_cols, cum, cum_t, o_b, dob, lse_b, name="fox_bwd", comm=_chips_comm([cWg, cWout, cWa, cWb], every),
        dq_into=(dqkv, fox_cols[0]))
    dcq_p = jnp.pad(dcq.reshape(T, HB, HEAD_DIM)[:, :, 0], ((0, 0), (0, LANE - HB)))
    dck_p = jnp.pad(dck.reshape(HB, T).T, ((0, 0), (0, LANE - HB)))
    dflog, dbf = _fgate_bwd(dcq_p, dck_p, flog, bf_pad, name="forget_bwd")

    dqkv = _cast_into(dqkv, dkap, padk, WA, name="dqkv_k_a")
    dqkv = _cast_into(dqkv, dvap, padk, 2 * WA, name="dqkv_v_a")
    dqkv = _cast_into(dqkv, dkb, 0, 3 * WA + WB, name="dqkv_k_b")
    dqkv = _cast_into(dqkv, dvb, 0, 3 * WA + 2 * WB, name="dqkv_v_b")
    dfb = dflog.astype(BF16)
    g_qkv, (rWu,) = _mm_tn(h, dqkv, 1, name="gw_qkv", comm=_chips_comm([cWu], every))
    gWin = _scatter_g_in(g_qkv[0], _mm_tn(h, dfb, 1, name="gw_forget")[0], g_gates, lay, PIN)
    dh, (sWin,) = _mm_nt(dfb, Wf, name="d_h_forget", out_dtype=F32, comm=_sibling_comm([gWin]))
    cWin = _chip_sum(gWin, sWin, core, name="chip_sum_w_in")
    dh, (rWin01,) = _mm_nt(dqkv, Wqkv, name="d_h_qkv", out_dtype=F32, res=dh, comm=_chips_comm([cWin], [0, 1]))
    dh, (rWin2,) = _mm_nt(dgates, Wgt, name="d_h_gates", out_dtype=F32, res=dh, comm=_chips_comm([cWin], [2]))
    dx, dg_mix = _rms_bwd(dh, x2d, g_mix, dx1, name="rms_mix_bwd", with_bf16=False)

    def slots(r):
        return [(r, 0), (r, 1), (r, 2)]

    in_parts = [_unplace_block(p, lay, din8, blk)[None] for p in
                (lax.dynamic_index_in_dim(cWin, 2 * cx + cy, keepdims=False), rWin01[0], rWin01[1], rWin2[0])]
    first = jnp.zeros((1,), jnp.int32)
    big = {}
    for nm, s1, others, which, w, m, v in [
            ("w_down_ffn", cWd, slots(rWd), chip, w_down_ffn, m_w_down_ffn, v_w_down_ffn),
            ("w_gate_ffn", cWg, slots(rWg), chip, w_gate_ffn, m_w_gate_ffn, v_w_gate_ffn),
            ("w_out", cWout, slots(rWout), chip, w_out, m_w_out, v_w_out),
            ("w_branch_a", cWa, slots(rWa), chip, w_branch_a, m_w_branch_a, v_w_branch_a),
            ("w_branch_b", cWb, slots(rWb), chip, w_branch_b, m_w_branch_b, v_w_branch_b),
            ("w_up_ffn", cWu, slots(rWu), chip, w_up_ffn, m_w_up_ffn, v_w_up_ffn),
            ("w_in", in_parts[0], [(p, 0) for p in in_parts[1:]], first, w_in, m_w_in, v_w_in)]:
        flipped = nm in ("w_gate_ffn", "w_up_ffn")
        w2, m2, v2 = ((a[0].T if flipped else a[0]) for a in (w, m, v))
        outs = _adamw_shard(s1, others, which, w2, m2, v2, name="adamw_" + nm)
        big[nm] = [(o.T if flipped else o)[None] for o in outs]

    small_names = ["g_mix", "b_f", "b_gate", "rel_bias", "g_ffn", "g_final"]
    small_w = [g_mix, b_f, b_gate, rel_bias, g_ffn, g_final]
    small_m = [m_g_mix, m_b_f, m_b_gate, m_rel_bias, m_g_ffn, m_g_final]
    small_v = [v_g_mix, v_b_f, v_b_gate, v_rel_bias, v_g_ffn, v_g_final]
    small_g = [dg_mix, dbf[:, :HB], dbg, d_rel[None], dg_ffn, dg_final[0]]
    shapes = [w.shape for w in small_w]
    n_small = sum(w.size for w in small_w) + 1
    rows = _pad_to(-(-n_small // LANE), 8)
    total = _all_reduce_small(_pack(small_g + [loss_part[:1, :1]], rows), name="all_reduce_small")
    d_s, m_s, v_s = _adamw_small(_pack(small_w, rows), total, _pack(small_m, rows), _pack(small_v, rows),
                                 name="adamw_small")
    g_small = dict(zip(small_names, _unpack(total, shapes)))
    d_small = dict(zip(small_names, _unpack(d_s, shapes)))
    m_small = dict(zip(small_names, _unpack(m_s, shapes)))
    v_small = dict(zip(small_names, _unpack(v_s, shapes)))
    loss = total.reshape(-1)[n_small - 1]

    order = ["g_mix", "w_in", "b_f", "b_gate", "rel_bias", "w_branch_a", "w_branch_b", "w_out", "g_ffn",
             "w_gate_ffn", "w_up_ffn", "w_down_ffn", "g_final"]

    def pick(idx, small):
        return [big[nm][idx] if nm in big else small[nm] for nm in order]

    return (loss, dx[None], *pick(0, g_small), *pick(1, d_small), *pick(2, m_small), *pick(3, v_small))
```

```python
import functools

import jax
import jax.numpy as jnp
from jax import lax
from jax.experimental import pallas as pl
from jax.experimental.pallas import tpu as pltpu

F32 = jnp.float32
BF16 = jnp.bfloat16
MESH = pl.DeviceIdType.MESH

N_DEV = 8
LANE = 128
HEAD_DIM = 128
CHUNK = 64
LEFT_CHUNKS = 8
MAX_REL = 256
RMS_EPS = 1e-6
NEG_INF = -1e30
ATTN_SCALE = HEAD_DIM ** -0.5
VMEM_LIMIT_BYTES = 56 * 1024 * 1024

ADAM_LR = 0.001
ADAM_B1 = 0.9
ADAM_B2 = 0.999
ADAM_EPS = 1e-08
ADAM_WD = 0.01
ADAM_STEP = 10

NT_DIMS = (((1,), (1,)), ((), ()))
TN_DIMS = (((0,), (0,)), ((), ()))


def _cp(*sem):
    return pltpu.CompilerParams(dimension_semantics=sem if sem else None, vmem_limit_bytes=VMEM_LIMIT_BYTES)


def _pad_to(n, mult):
    return -(-n // mult) * mult


def _tile(dim, pref, align):
    t = (min(pref, dim) // align) * align
    while t >= align:
        if dim % t == 0:
            return t
        t -= align
    return dim


def _sigmoid(z):
    return 1.0 / (1.0 + jnp.exp(-z))


class _Comm:
    def __init__(self, ins, outs, scratch, begin, end, relays=()):
        self.ins, self.outs, self.scratch = list(ins), list(outs), list(scratch)
        self.begin, self.end, self.relays = begin, end, list(relays)


def _call(body, args, *, name, grid, in_specs, out_specs, out_shape, scratch=(), sem=(), comm=None, aliases=None):
    single = not isinstance(out_shape, (list, tuple))
    out_shape = [out_shape] if single else list(out_shape)
    out_specs = [out_specs] if single else list(out_specs)
    aliases = dict(aliases or {})
    if comm is None:
        res = pl.pallas_call(
            body, name=name, grid=grid, in_specs=list(in_specs), out_specs=out_specs, out_shape=out_shape,
            scratch_shapes=list(scratch), input_output_aliases=aliases, compiler_params=_cp(*sem))(*args)
        return res[0] if single else res
    ni, no, ns = len(args), len(out_shape), len(scratch)
    ci, co = len(comm.ins), len(comm.outs)
    total = 1
    for g in grid:
        total *= g

    def hosted(*refs):
        a, cin = refs[:ni], refs[ni:ni + ci]
        o, cout = refs[ni + ci:ni + ci + no], refs[ni + ci + no:ni + ci + no + co]
        s, csc = refs[ni + ci + no + co:ni + ci + no + co + ns], refs[ni + ci + no + co + ns:]
        step = pl.program_id(0)
        for d in range(1, len(grid)):
            step = step * grid[d] + pl.program_id(d)

        @pl.when(step == 0)
        def _():
            comm.begin(cin, cout, csc)

        for fraction, relay in comm.relays:
            @pl.when(step == min(int(total * fraction), total - 1))
            def _(relay=relay):
                relay(cin, cout, csc)

        body(*a, *o, *s)

        @pl.when(step == total - 1)
        def _():
            comm.end(cin, cout, csc)

    res = pl.pallas_call(
        hosted, name=name, grid=grid, in_specs=list(in_specs) + _any_specs(ci),
        out_specs=out_specs + _any_specs(co), out_shape=out_shape + comm.outs,
        scratch_shapes=list(scratch) + comm.scratch, input_output_aliases=aliases,
        compiler_params=_cp(*(("arbitrary",) * len(grid))))(*args, *comm.ins)
    return (res[0] if single else res[:no]), res[no:]


def _any_specs(n):
    return [pl.BlockSpec(memory_space=pl.ANY)] * n


def _mm_body(dims, nk, has_res, n_ext, n_out, epi):
    def body(*refs):
        a_ref, w_ref = refs[:2]
        at = 2 + has_res
        r_ref = refs[2] if has_res else None
        e_refs, o_refs = refs[at:at + n_ext], refs[at + n_ext:at + n_ext + n_out]

        def emit(v):
            if r_ref is not None:
                v = v + r_ref[...]
            vals = (v,) if epi is None else epi(v, *[e[...] for e in e_refs])
            for o_ref, val in zip(o_refs, vals):
                o_ref[...] = val.astype(o_ref.dtype)

        if nk == 1:
            emit(lax.dot_general(a_ref[...], w_ref[...], dims, preferred_element_type=F32))
            return
        acc = refs[-1]
        k = pl.program_id(2)

        @pl.when(k == 0)
        def _():
            acc[...] = jnp.zeros_like(acc)

        acc[...] += lax.dot_general(a_ref[...], w_ref[...], dims, preferred_element_type=F32)

        @pl.when(k == nk - 1)
        def _():
            emit(acc[...])

    return body


def _mm_nn(x, w, *, name, out_dtype, res=None, tm=1024, tn=1024, tk=None, comm=None, epi=None, extras=(), n_out=1):
    M, K = x.shape
    NB, _, NW = w.shape
    tm, tn = _tile(M, tm, 16), _tile(NW, tn, LANE)
    tk = K if tk is None else _tile(K, tk, LANE)
    nj, nk = NW // tn, K // tk
    tile = pl.BlockSpec((tm, tn), lambda j, i, k: (i, j))
    in_specs = [pl.BlockSpec((tm, tk), lambda j, i, k: (i, k)),
                pl.BlockSpec((None, tk, tn), lambda j, i, k: (j // nj, k, j % nj))]
    args = [x, w] + ([res] if res is not None else []) + list(extras)
    in_specs += [tile] * (len(args) - 2)
    out = jax.ShapeDtypeStruct((M, NB * NW), out_dtype)
    return _call(
        _mm_body((((1,), (0,)), ((), ())), nk, res is not None, len(extras), n_out, epi), args,
        name=name, grid=(NB * nj, M // tm, nk), in_specs=in_specs,
        out_specs=tile if n_out == 1 else [tile] * n_out, out_shape=out if n_out == 1 else [out] * n_out,
        scratch=[pltpu.VMEM((tm, tn), F32)] if nk > 1 else [],
        sem=("parallel", "parallel", "arbitrary"), comm=comm)


def _mm_nt(dy, w, *, name, out_dtype, res=None, tm=1024, tn=1024, tc=2816, comm=None, epi=None, extras=(), n_out=1):
    M, N = dy.shape
    NB, Kw, NW = w.shape
    assert N == NB * NW
    tm, tn, tc = _tile(M, tm, 16), _tile(Kw, tn, LANE), _tile(NW, tc, LANE)
    nc = NW // tc
    nk = NB * nc
    tile = pl.BlockSpec((tm, tn), lambda i, j, k: (i, j))
    in_specs = [pl.BlockSpec((tm, tc), lambda i, j, k: (i, k)),
                pl.BlockSpec((None, tn, tc), lambda i, j, k: (k // nc, j, k % nc))]
    args = [dy, w] + ([res] if res is not None else []) + list(extras)
    in_specs += [tile] * (len(args) - 2)
    out = jax.ShapeDtypeStruct((M, Kw), out_dtype)
    return _call(
        _mm_body(NT_DIMS, nk, res is not None, len(extras), n_out, epi), args,
        name=name, grid=(M // tm, Kw // tn, nk), in_specs=in_specs,
        out_specs=tile if n_out == 1 else [tile] * n_out, out_shape=out if n_out == 1 else [out] * n_out,
        scratch=[pltpu.VMEM((tm, tn), F32)] if nk > 1 else [],
        sem=("parallel", "parallel", "arbitrary"), comm=comm)


def _mm_tn(x, dy, nb, *, name, tm=512, tn=1024, comm=None):
    M, Kw = x.shape
    NW = dy.shape[1] // nb
    tm, tn = _tile(Kw, tm, LANE), _tile(NW, tn, LANE)
    nj = NW // tn

    def body(x_ref, d_ref, o_ref):
        o_ref[...] = lax.dot_general(x_ref[...], d_ref[...], TN_DIMS, preferred_element_type=F32).astype(o_ref.dtype)

    return _call(
        body, [x, dy], name=name, grid=(nb * nj, Kw // tm),
        in_specs=[pl.BlockSpec((M, tm), lambda j, i: (0, i)),
                  pl.BlockSpec((M, tn), lambda j, i: (0, j))],
        out_specs=pl.BlockSpec((None, tm, tn), lambda j, i: (j // nj, i, j % nj)),
        out_shape=jax.ShapeDtypeStruct((nb, Kw, NW), BF16),
        sem=("parallel", "parallel"), comm=comm)


def _rms_fwd(x, g, *, name, tr=256):
    T, D = x.shape
    tr = _tile(T, tr, 16)

    def body(x_ref, g_ref, h_ref):
        xv = x_ref[...]
        r = lax.rsqrt(jnp.mean(xv * xv, axis=-1, keepdims=True) + RMS_EPS)
        h_ref[...] = ((xv * r) * g_ref[...]).astype(BF16)

    return pl.pallas_call(
        body, name=name, grid=(T // tr,),
        in_specs=[pl.BlockSpec((tr, D), lambda i: (i, 0)), pl.BlockSpec((1, D), lambda i: (0, 0))],
        out_specs=pl.BlockSpec((tr, D), lambda i: (i, 0)),
        out_shape=jax.ShapeDtypeStruct((T, D), BF16),
        compiler_params=_cp("parallel"))(x, g)


def _rms_bwd(dh, x, g, dres, *, name, tr=128, with_bf16=True):
    T, D = x.shape
    tr = _tile(T, tr, 16)

    def body(dh_ref, x_ref, g_ref, dr_ref, dx_ref, *rest):
        dg_ref = rest[-1]
        i = pl.program_id(0)
        xv = x_ref[...]
        dhv = dh_ref[...]
        r = lax.rsqrt(jnp.mean(xv * xv, axis=-1, keepdims=True) + RMS_EPS)
        a = dhv * g_ref[...]
        dot = jnp.mean(a * xv, axis=-1, keepdims=True)
        dx = dr_ref[...] + r * a - xv * (r * r * r * dot)
        dx_ref[...] = dx
        if with_bf16:
            rest[0][...] = dx.astype(BF16)
        part = jnp.sum(dhv * (xv * r), axis=0, keepdims=True)

        @pl.when(i == 0)
        def _():
            dg_ref[...] = part

        @pl.when(i > 0)
        def _():
            dg_ref[...] += part

    row = pl.BlockSpec((tr, D), lambda i: (i, 0))
    vec = pl.BlockSpec((1, D), lambda i: (0, 0))
    halves = [jax.ShapeDtypeStruct((T, D), BF16)] if with_bf16 else []
    return pl.pallas_call(
        body, name=name, grid=(T // tr,),
        in_specs=[row, row, vec, row], out_specs=[row] + [row] * len(halves) + [vec],
        out_shape=[jax.ShapeDtypeStruct((T, D), F32)] + halves + [jax.ShapeDtypeStruct((1, D), F32)],
        compiler_params=_cp("arbitrary"))(dh, x, g, dres)


def _final_loss(x2, tgt, g, *, name, tr=128):
    T, D = x2.shape
    tr = _tile(T, tr, 16)

    def body(x_ref, t_ref, g_ref, dx_ref, dxb_ref, dg_ref, loss_ref):
        i = pl.program_id(0)
        xv = x_ref[...]
        gv = g_ref[...]
        r = lax.rsqrt(jnp.mean(xv * xv, axis=-1, keepdims=True) + RMS_EPS)
        xh = xv * r
        diff = xh * gv - t_ref[...]
        lpart = 0.5 * jnp.sum(jnp.mean(diff * diff, axis=-1, keepdims=True))
        dy = diff * (1.0 / D)
        a = dy * gv
        dot = jnp.mean(a * xv, axis=-1, keepdims=True)
        dx = r * a - xv * (r * r * r * dot)
        dx_ref[...] = dx
        dxb_ref[...] = dx.astype(BF16)
        part = jnp.sum(dy * xh, axis=0, keepdims=True)
        lblk = jnp.full((8, LANE), lpart, F32)

        @pl.when(i == 0)
        def _():
            dg_ref[...] = part
            loss_ref[...] = lblk

        @pl.when(i > 0)
        def _():
            dg_ref[...] += part
            loss_ref[...] += lblk

    row = pl.BlockSpec((tr, D), lambda i: (i, 0))
    vec = pl.BlockSpec((1, D), lambda i: (0, 0))
    return pl.pallas_call(
        body, name=name, grid=(T // tr,),
        in_specs=[row, row, vec],
        out_specs=[row, row, vec, pl.BlockSpec((8, LANE), lambda i: (0, 0))],
        out_shape=[jax.ShapeDtypeStruct((T, D), F32), jax.ShapeDtypeStruct((T, D), BF16),
                   jax.ShapeDtypeStruct((1, D), F32), jax.ShapeDtypeStruct((8, LANE), F32)],
        compiler_params=_cp("arbitrary"))(x2, tgt, g)


def _merge_fwd(gates, b_gate, ua, ub, *, name, tr=256):
    T, D = ua.shape
    tr = _tile(T, tr, 16)

    def body(ga_ref, gb_ref, ba_ref, bb_ref, ua_ref, ub_ref, o_ref):
        sa = _sigmoid(ga_ref[...].astype(F32) + ba_ref[...])
        sb = _sigmoid(gb_ref[...].astype(F32) + bb_ref[...])
        o_ref[...] = (sa * ua_ref[...].astype(F32) + sb * ub_ref[...].astype(F32)).astype(BF16)

    row = pl.BlockSpec((tr, D), lambda i: (i, 0))
    return pl.pallas_call(
        body, name=name, grid=(T // tr,),
        in_specs=[row, pl.BlockSpec((tr, D), lambda i: (i, 1)),
                  pl.BlockSpec((1, D), lambda i: (0, 0)), pl.BlockSpec((1, D), lambda i: (0, 1)), row, row],
        out_specs=row, out_shape=jax.ShapeDtypeStruct((T, D), BF16),
        compiler_params=_cp("parallel"))(gates, gates, b_gate, b_gate, ua, ub)


def _merge_bwd(dm, gates, b_gate, ua, ub, *, name, tr=256):
    T, D = ua.shape
    tr = _tile(T, tr, 16)

    def body(dm_ref, ga_ref, gb_ref, ba_ref, bb_ref, ua_ref, ub_ref, dua_ref, dub_ref, dg_ref, db_ref):
        i = pl.program_id(0)
        dmv = dm_ref[...].astype(F32)
        sa = _sigmoid(ga_ref[...].astype(F32) + ba_ref[...])
        sb = _sigmoid(gb_ref[...].astype(F32) + bb_ref[...])
        dua_ref[...] = (dmv * sa).astype(BF16)
        dub_ref[...] = (dmv * sb).astype(BF16)
        dga = dmv * ua_ref[...].astype(F32) * (sa * (1.0 - sa))
        dgb = dmv * ub_ref[...].astype(F32) * (sb * (1.0 - sb))
        dg_ref[:, :D] = dga.astype(BF16)
        dg_ref[:, D:] = dgb.astype(BF16)
        pa = jnp.sum(dga, axis=0, keepdims=True)
        pb = jnp.sum(dgb, axis=0, keepdims=True)

        @pl.when(i == 0)
        def _():
            db_ref[:, :D] = pa
            db_ref[:, D:] = pb

        @pl.when(i > 0)
        def _():
            db_ref[:, :D] += pa
            db_ref[:, D:] += pb

    row = pl.BlockSpec((tr, D), lambda i: (i, 0))
    vec = pl.BlockSpec((1, D), lambda i: (0, 0))
    act = jax.ShapeDtypeStruct((T, D), BF16)
    return pl.pallas_call(
        body, name=name, grid=(T // tr,),
        in_specs=[row, row, pl.BlockSpec((tr, D), lambda i: (i, 1)),
                  vec, pl.BlockSpec((1, D), lambda i: (0, 1)), row, row],
        out_specs=[row, row, pl.BlockSpec((tr, 2 * D), lambda i: (i, 0)), pl.BlockSpec((1, 2 * D), lambda i: (0, 0))],
        out_shape=[act, act, jax.ShapeDtypeStruct((T, 2 * D), BF16), jax.ShapeDtypeStruct((1, 2 * D), F32)],
        compiler_params=_cp("arbitrary"))(dm, gates, gates, b_gate, b_gate, ua, ub)


def _swiglu_fwd_epi(up, gate):
    g = gate.astype(F32)
    up = up.astype(BF16).astype(F32)
    return up, g * _sigmoid(g) * up


def _swiglu_bwd_epi(d, gate, up):
    g = gate.astype(F32)
    s = _sigmoid(g)
    return d * up.astype(F32) * (s * (1.0 + g * (1.0 - s))), d * (g * s)


def _cast_into(buf, src, row_off, col_off, *, name, tr=256):
    T = buf.shape[0]
    W = src.shape[1]
    tr = _tile(T, tr, 16)
    cw = W
    while col_off % cw or W % cw:
        cw -= LANE
    assert row_off % tr == 0
    c0, r0 = col_off // cw, row_off // tr

    def body(s_ref, b_ref, o_ref):
        o_ref[...] = s_ref[...].astype(o_ref.dtype)

    return pl.pallas_call(
        body, name=name, grid=(T // tr, W // cw),
        in_specs=[pl.BlockSpec((tr, cw), lambda i, j: (i + r0, j)), pl.BlockSpec(memory_space=pl.ANY)],
        out_specs=pl.BlockSpec((tr, cw), lambda i, j: (i, c0 + j)),
        out_shape=jax.ShapeDtypeStruct(buf.shape, buf.dtype), input_output_aliases={1: 0},
        compiler_params=_cp("parallel", "parallel"))(src, buf)


def _bias_table(rel, qb, kw):
    assert qb - 1 <= MAX_REL
    ha = rel.shape[0]
    n_clipped = kw - MAX_REL
    e = jnp.concatenate([jnp.broadcast_to(rel[:, 2 * MAX_REL:], (ha, n_clipped)),
                         rel[:, 2 * MAX_REL - 1:MAX_REL - qb:-1]], axis=1)
    period = qb + kw
    e = jnp.pad(e, ((0, 0), (0, period - e.shape[1])))
    z = jnp.broadcast_to(e[:, None, :], (ha, qb, period)).reshape(ha, qb * period)
    z = z[:, :qb * (period - 1)].reshape(ha, qb, period - 1)
    i = jnp.arange(qb)[:, None]
    j = jnp.arange(kw)[None, :]
    band = (j // CHUNK >= i // CHUNK) & (j // CHUNK <= i // CHUNK + LEFT_CHUNKS)
    return jnp.where(band[None], z[:, :, qb - 1:qb - 1 + kw], NEG_INF)


def _heads_per_step(n_heads):
    return 2 if n_heads % 2 == 0 else 1


def _attn_a_fwd(q, kp, vp, btab, *, name):
    T, WA = q.shape[0], kp.shape[1]
    HA = WA // HEAD_DIM
    _, QB, KW = btab.shape
    padk = KW - QB
    G = _heads_per_step(HA)
    GW = G * HEAD_DIM

    def body(q_ref, k_ref, v_ref, b_ref, o_ref, lse_ref):
        i = pl.program_id(1)
        start = pl.multiple_of(i * QB, QB)
        kpos = start - padk + lax.broadcasted_iota(jnp.int32, (QB, KW), 1)
        for g in range(G):
            lanes = pl.ds(g * HEAD_DIM, HEAD_DIM)
            k = k_ref[pl.ds(start, KW), lanes]
            v = v_ref[pl.ds(start, KW), lanes]
            s = lax.dot_general(q_ref[:, lanes], k, NT_DIMS, preferred_element_type=F32) * ATTN_SCALE + b_ref[g]
            s = jnp.where(kpos >= 0, s, NEG_INF)
            m = jnp.max(s, axis=-1, keepdims=True)
            p = jnp.exp(s - m)
            l = jnp.sum(p, axis=-1, keepdims=True)
            o = jnp.dot(p.astype(BF16), v, preferred_element_type=F32) / l
            o_ref[:, lanes] = o.astype(BF16)
            lse_ref[:, lanes] = jnp.broadcast_to(m + jnp.log(l), (QB, HEAD_DIM))

    qspec = pl.BlockSpec((QB, GW), lambda h, i: (i, h))
    kspec = pl.BlockSpec((T + padk, GW), lambda h, i: (0, h))
    return pl.pallas_call(
        body, name=name, grid=(HA // G, T // QB),
        in_specs=[qspec, kspec, kspec, pl.BlockSpec((G, QB, KW), lambda h, i: (h, 0, 0))],
        out_specs=[qspec, qspec],
        out_shape=[jax.ShapeDtypeStruct((T, WA), BF16), jax.ShapeDtypeStruct((T, WA), F32)],
        compiler_params=_cp("parallel", "arbitrary"))(q, kp, vp, btab)


def _attn_a_bwd(q, kp, vp, btab, o, do, lse, *, name, comm=None, dq_width=None):
    T, WA = q.shape[0], kp.shape[1]
    HA = WA // HEAD_DIM
    _, QB, KW = btab.shape
    padk = KW - QB
    G = _heads_per_step(HA)
    GW = G * HEAD_DIM

    def body(q_ref, k_ref, v_ref, b_ref, o_ref, do_ref, lse_ref, dq_ref, dk_ref, dv_ref, db_ref):
        i = pl.program_id(1)

        @pl.when(i == 0)
        def _():
            dk_ref[...] = jnp.zeros_like(dk_ref)
            dv_ref[...] = jnp.zeros_like(dv_ref)
            db_ref[...] = jnp.zeros_like(db_ref)

        start = pl.multiple_of(i * QB, QB)
        kpos = start - padk + lax.broadcasted_iota(jnp.int32, (QB, KW), 1)
        for g in range(G):
            lanes = pl.ds(g * HEAD_DIM, HEAD_DIM)
            qv = q_ref[:, lanes]
            dov = do_ref[:, lanes]
            k = k_ref[pl.ds(start, KW), lanes]
            v = v_ref[pl.ds(start, KW), lanes]
            s = lax.dot_general(qv, k, NT_DIMS, preferred_element_type=F32) * ATTN_SCALE + b_ref[g]
            s = jnp.where(kpos >= 0, s, NEG_INF)
            p = jnp.exp(s - lse_ref[:, pl.ds(g * HEAD_DIM, 1)])
            dp = lax.dot_general(dov, v, NT_DIMS, preferred_element_type=F32)
            delta = jnp.sum(dov.astype(F32) * o_ref[:, lanes].astype(F32), axis=-1, keepdims=True)
            ds = p * (dp - delta)
            db_ref[g] += ds
            dsb = ds.astype(BF16)
            dq_ref[:, lanes] = (jnp.dot(dsb, k, preferred_element_type=F32) * ATTN_SCALE).astype(BF16)
            dk_ref[pl.ds(start, KW), lanes] += (
                lax.dot_general(dsb, qv, TN_DIMS, preferred_element_type=F32) * ATTN_SCALE)
            dv_ref[pl.ds(start, KW), lanes] += lax.dot_general(
                p.astype(BF16), dov, TN_DIMS, preferred_element_type=F32)

    qspec = pl.BlockSpec((QB, GW), lambda h, i: (i, h))
    kspec = pl.BlockSpec((T + padk, GW), lambda h, i: (0, h))
    bspec = pl.BlockSpec((G, QB, KW), lambda h, i: (h, 0, 0))
    return _call(
        body, [q, kp, vp, btab, o, do, lse], name=name, grid=(HA // G, T // QB),
        in_specs=[qspec, kspec, kspec, bspec, qspec, qspec, qspec],
        out_specs=[qspec, kspec, kspec, bspec],
        out_shape=[jax.ShapeDtypeStruct((T, dq_width or WA), BF16), jax.ShapeDtypeStruct((T + padk, WA), F32),
                   jax.ShapeDtypeStruct((T + padk, WA), F32), jax.ShapeDtypeStruct((HA, QB, KW), F32)],
        sem=("parallel", "arbitrary"), comm=comm)


def _relbias_fold(skew, dtab, *, name):
    HA, qb, kw = dtab.shape
    W = kw + 1

    def body(s_ref, t_ref, col_ref, tail_ref):
        row = lax.broadcasted_iota(jnp.int32, (qb, W), 0)
        col = lax.broadcasted_iota(jnp.int32, (qb, W), 1)
        col_ref[...] = jnp.sum(jnp.where(row + col < kw, s_ref[...], 0.0), axis=0, keepdims=True)
        i = lax.broadcasted_iota(jnp.int32, (qb, kw), 0)
        j = lax.broadcasted_iota(jnp.int32, (qb, kw), 1)
        tail = jnp.sum(jnp.where((kw - qb) + i - j >= MAX_REL, t_ref[...], 0.0))
        tail_ref[...] = jnp.full((1, LANE), tail, F32)

    return pl.pallas_call(
        body, name=name, grid=(HA,),
        in_specs=[pl.BlockSpec((None, qb, W), lambda h: (h, 0, 0)), pl.BlockSpec((None, qb, kw), lambda h: (h, 0, 0))],
        out_specs=[pl.BlockSpec((None, 1, W), lambda h: (h, 0, 0)), pl.BlockSpec((None, 1, LANE), lambda h: (h, 0, 0))],
        out_shape=[jax.ShapeDtypeStruct((HA, 1, W), F32), jax.ShapeDtypeStruct((HA, 1, LANE), F32)],
        compiler_params=_cp("parallel"))(skew, dtab)


CUM_BLOCK = 128


def _fgate_fwd(f, bf, *, name):
    T = f.shape[0]
    nb = T // CUM_BLOCK

    def body(f_ref, b_ref, cum_ref):
        r = lax.broadcasted_iota(jnp.int32, (CUM_BLOCK, CUM_BLOCK), 0)
        c = lax.broadcasted_iota(jnp.int32, (CUM_BLOCK, CUM_BLOCK), 1)
        tri = (c <= r).astype(F32)

        def step(b, carry):
            r0 = pl.multiple_of(b * CUM_BLOCK, CUM_BLOCK)
            z = f_ref[pl.ds(r0, CUM_BLOCK), :] + b_ref[...]
            lf = jnp.minimum(z, 0.0) - jnp.log(1.0 + jnp.exp(-jnp.abs(z)))
            cs = jnp.dot(tri, lf, precision=lax.Precision.HIGHEST, preferred_element_type=F32) + carry
            cum_ref[pl.ds(r0, CUM_BLOCK), :] = cs
            return carry + jnp.sum(lf, axis=0, keepdims=True)

        lax.fori_loop(0, nb, step, jnp.zeros((1, LANE), F32))

    return pl.pallas_call(
        body, name=name, out_shape=jax.ShapeDtypeStruct((T, LANE), F32),
        compiler_params=_cp())(f, bf)


def _fgate_bwd(dcq, dck, f, bf, *, name):
    T = f.shape[0]
    nb = T // CUM_BLOCK

    def body(dq_ref, dk_ref, f_ref, b_ref, df_ref, dbf_ref):
        r = lax.broadcasted_iota(jnp.int32, (CUM_BLOCK, CUM_BLOCK), 0)
        c = lax.broadcasted_iota(jnp.int32, (CUM_BLOCK, CUM_BLOCK), 1)
        tri = (c >= r).astype(F32)

        def step(n, carry):
            tail, dbf = carry
            r0 = pl.multiple_of((nb - 1 - n) * CUM_BLOCK, CUM_BLOCK)
            dc = dq_ref[pl.ds(r0, CUM_BLOCK), :] + dk_ref[pl.ds(r0, CUM_BLOCK), :]
            ss = jnp.dot(tri, dc, precision=lax.Precision.HIGHEST, preferred_element_type=F32) + tail
            z = f_ref[pl.ds(r0, CUM_BLOCK), :] + b_ref[...]
            df = ss * _sigmoid(-z)
            df_ref[pl.ds(r0, CUM_BLOCK), :] = df
            return tail + jnp.sum(dc, axis=0, keepdims=True), dbf + jnp.sum(df, axis=0, keepdims=True)

        zero = jnp.zeros((1, LANE), F32)
        _, dbf = lax.fori_loop(0, nb, step, (zero, zero))
        dbf_ref[...] = dbf

    return pl.pallas_call(
        body, name=name,
        out_shape=[jax.ShapeDtypeStruct((T, LANE), F32), jax.ShapeDtypeStruct((1, LANE), F32)],
        compiler_params=_cp())(dcq, dck, f, bf)


def _fox_logits(qv, kk, cq, ck, diagonal):
    s = lax.dot_general(qv, kk, NT_DIMS, preferred_element_type=F32) * ATTN_SCALE + cq - ck
    if diagonal:
        row = lax.broadcasted_iota(jnp.int32, s.shape, 0)
        col = lax.broadcasted_iota(jnp.int32, s.shape, 1)
        s = jnp.where(col <= row, s, NEG_INF)
    return s


def _fox_heads_per_step(n_heads, cols):
    return 2 if n_heads % 2 == 0 and all(c % 2 == 0 for c in cols) else 1


def _fox_fwd(qkv, cols, cum, cum_t, *, name, comm=None):
    T = qkv.shape[0]
    HB, _, KB = cum_t.shape
    WB = HB * HEAD_DIM
    QB = KB
    G = _fox_heads_per_step(HB, cols)
    GW = G * HEAD_DIM
    qc, kc, vc = (c // G for c in cols)

    def body(q_ref, k_ref, v_ref, c_ref, ct_ref, o_ref, lse_ref):
        h = pl.program_id(0)
        i = pl.program_id(1)
        lane = lax.broadcasted_iota(jnp.int32, (QB, LANE), 1)
        heads = [pl.ds(g * HEAD_DIM, HEAD_DIM) for g in range(G)]
        qs = [q_ref[:, lanes] for lanes in heads]
        cqs = [jnp.sum(jnp.where(lane == h * G + g, c_ref[...], 0.0), axis=-1, keepdims=True) for g in range(G)]

        def block(kb, carry, diagonal):
            k0 = pl.multiple_of(kb * KB, KB)
            out = []
            for g, (m, l, acc) in enumerate(carry):
                s = _fox_logits(qs[g], k_ref[pl.ds(k0, KB), heads[g]], cqs[g], ct_ref[g, pl.ds(kb, 1), :], diagonal)
                m_new = jnp.maximum(m, jnp.max(s, axis=-1, keepdims=True))
                alpha = jnp.exp(m - m_new)
                p = jnp.exp(s - m_new)
                l = alpha * l + jnp.sum(p, axis=-1, keepdims=True)
                acc = alpha * acc + jnp.dot(p.astype(BF16), v_ref[pl.ds(k0, KB), heads[g]],
                                            preferred_element_type=F32)
                out.append((m_new, l, acc))
            return tuple(out)

        start = tuple((jnp.full((QB, 1), NEG_INF, F32), jnp.zeros((QB, 1), F32), jnp.zeros((QB, HEAD_DIM), F32))
                      for _ in range(G))
        past = lax.fori_loop(0, i, lambda kb, carry: block(kb, carry, False), start)
        for g, (m, l, acc) in enumerate(block(i, past, True)):
            o_ref[:, heads[g]] = (acc / l).astype(BF16)
            lse_ref[:, heads[g]] = jnp.broadcast_to(m + jnp.log(l), (QB, HEAD_DIM))

    ospec = pl.BlockSpec((QB, GW), lambda h, i: (i, h))
    return _call(
        body, [qkv, qkv, qkv, cum, cum_t], name=name, grid=(HB // G, T // QB),
        in_specs=[pl.BlockSpec((QB, GW), lambda h, i: (i, qc + h)),
                  pl.BlockSpec((T, GW), lambda h, i: (0, kc + h)),
                  pl.BlockSpec((T, GW), lambda h, i: (0, vc + h)),
                  pl.BlockSpec((QB, LANE), lambda h, i: (i, 0)),
                  pl.BlockSpec((G, T // KB, KB), lambda h, i: (h, 0, 0))],
        out_specs=[ospec, ospec],
        out_shape=[jax.ShapeDtypeStruct((T, WB), BF16), jax.ShapeDtypeStruct((T, WB), F32)],
        sem=("parallel", "arbitrary"), comm=comm)


def _fox_bwd(qkv, cols, cum, cum_t, o, do, lse, *, name, comm=None, dq_into=None):
    T = qkv.shape[0]
    HB, _, KB = cum_t.shape
    WB = HB * HEAD_DIM
    QB = KB
    G = _fox_heads_per_step(HB, cols + ((dq_into[1],) if dq_into is not None else ()))
    GW = G * HEAD_DIM
    qc, kc, vc = (c // G for c in cols)

    def body(*refs):
        q_ref, k_ref, v_ref, c_ref, ct_ref, o_ref, do_ref, lse_ref = refs[:8]
        dq_ref, dk_ref, dv_ref, dcq_ref, dck_ref = refs[-5:]
        h = pl.program_id(0)
        i = pl.program_id(1)

        @pl.when(i == 0)
        def _():
            dk_ref[...] = jnp.zeros_like(dk_ref)
            dv_ref[...] = jnp.zeros_like(dv_ref)
            dck_ref[...] = jnp.zeros_like(dck_ref)

        lane = lax.broadcasted_iota(jnp.int32, (QB, LANE), 1)
        heads = [pl.ds(g * HEAD_DIM, HEAD_DIM) for g in range(G)]
        qs = [q_ref[:, lanes] for lanes in heads]
        dos = [do_ref[:, lanes] for lanes in heads]
        lses = [lse_ref[:, pl.ds(g * HEAD_DIM, 1)] for g in range(G)]
        deltas = [jnp.sum(dos[g].astype(F32) * o_ref[:, heads[g]].astype(F32), axis=-1, keepdims=True)
                  for g in range(G)]
        cqs = [jnp.sum(jnp.where(lane == h * G + g, c_ref[...], 0.0), axis=-1, keepdims=True) for g in range(G)]

        def block(kb, carry, diagonal):
            k0 = pl.multiple_of(kb * KB, KB)
            out = []
            for g, (dq, dcq) in enumerate(carry):
                kk = k_ref[pl.ds(k0, KB), heads[g]]
                s = _fox_logits(qs[g], kk, cqs[g], ct_ref[g, pl.ds(kb, 1), :], diagonal)
                p = jnp.exp(s - lses[g])
                dp = lax.dot_general(dos[g], v_ref[pl.ds(k0, KB), heads[g]], NT_DIMS, preferred_element_type=F32)
                ds = p * (dp - deltas[g])
                dsb = ds.astype(BF16)
                dk_ref[pl.ds(k0, KB), heads[g]] += (
                    lax.dot_general(dsb, qs[g], TN_DIMS, preferred_element_type=F32) * ATTN_SCALE)
                dv_ref[pl.ds(k0, KB), heads[g]] += lax.dot_general(
                    p.astype(BF16), dos[g], TN_DIMS, preferred_element_type=F32)
                dck_ref[g, pl.ds(kb, 1), :] -= jnp.sum(ds, axis=0, keepdims=True)
                out.append((dq + jnp.dot(dsb, kk, preferred_element_type=F32),
                            dcq + jnp.sum(ds, axis=-1, keepdims=True)))
            return tuple(out)

        start = tuple((jnp.zeros((QB, HEAD_DIM), F32), jnp.zeros((QB, 1), F32)) for _ in range(G))
        past = lax.fori_loop(0, i, lambda kb, carry: block(kb, carry, False), start)
        for g, (dq, dcq) in enumerate(block(i, past, True)):
            dq_ref[:, heads[g]] = (dq * ATTN_SCALE).astype(BF16)
            dcq_ref[:, heads[g]] = jnp.broadcast_to(dcq, (QB, HEAD_DIM))

    qspec = pl.BlockSpec((QB, GW), lambda h, i: (i, h))
    kspec = pl.BlockSpec((T, GW), lambda h, i: (0, h))
    cspec = pl.BlockSpec((QB, LANE), lambda h, i: (i, 0))
    tspec = pl.BlockSpec((G, T // KB, KB), lambda h, i: (h, 0, 0))
    wide = jax.ShapeDtypeStruct((T, WB), F32)
    args = [qkv, qkv, qkv, cum, cum_t, o, do, lse]
    in_specs = [pl.BlockSpec((QB, GW), lambda h, i: (i, qc + h)),
                pl.BlockSpec((T, GW), lambda h, i: (0, kc + h)),
                pl.BlockSpec((T, GW), lambda h, i: (0, vc + h)),
                cspec, tspec, qspec, qspec, qspec]
    dq_spec, dq_shape, aliases = qspec, jax.ShapeDtypeStruct((T, WB), BF16), None
    if dq_into is not None:
        buf, dq_col = dq_into
        args.append(buf)
        in_specs.append(pl.BlockSpec(memory_space=pl.ANY))
        dq_spec = pl.BlockSpec((QB, GW), lambda h, i: (i, dq_col // G + h))
        dq_shape, aliases = jax.ShapeDtypeStruct(buf.shape, buf.dtype), {8: 0}
    return _call(
        body, args, name=name, grid=(HB // G, T // QB), in_specs=in_specs,
        out_specs=[dq_spec, kspec, kspec, qspec, tspec],
        out_shape=[dq_shape, wide, wide, wide, jax.ShapeDtypeStruct((HB, T // KB, KB), F32)],
        sem=("parallel", "arbitrary"), comm=comm, aliases=aliases)


def _place():
    return lax.axis_index("x"), lax.axis_index("y"), lax.axis_index("c")


def _other_chips(x, y):
    return [(1 - x, y), (x, 1 - y), (1 - x, 1 - y)]


def _gather_comm(shards):
    n = len(shards)

    def copies(srcs, outs, scratch, want):
        send_sems, recv_sems, local_sems = scratch
        x, y, c = _place()
        me, sib = (x, y, c), (x, y, 1 - c)
        chip_a = (jnp.bitwise_xor(x, c), jnp.bitwise_xor(y, 1 - c))
        chip_b = (jnp.bitwise_xor(x, 1 - c), jnp.bitwise_xor(y, c))
        chip_d = (1 - x, 1 - y)

        def copy(t, k, block, to, src=None):
            rows = outs[t].at[4 * block[0] + 2 * block[1] + block[2]]
            return pltpu.make_async_remote_copy(
                src_ref=rows if src is None else src, dst_ref=rows,
                send_sem=send_sems.at[7 * t + k], recv_sem=recv_sems.at[7 * t + k],
                device_id=to, device_id_type=MESH)

        got = {}
        if "mine" in want:
            got["mine"] = [pltpu.make_async_copy(srcs[t], outs[t].at[4 * x + 2 * y + c], local_sems.at[t])
                           for t in range(n)]
        if "first" in want:
            got["first"] = [cp for t in range(n) for cp in
                            (copy(t, 0, me, sib, src=srcs[t]), copy(t, 1, me, (*chip_a, c), src=srcs[t]),
                             copy(t, 2, me, (*chip_b, c), src=srcs[t]))]
        if "from_a" in want:
            got["from_a"] = [copy(t, 1, (*chip_a, c), me) for t in range(n)]
        if "hop" in want:
            got["hop"] = [copy(t, 3, (*chip_a, c), (*chip_b, c)) for t in range(n)]
        if "from_b" in want:
            got["from_b"] = [copy(t, 2, (*chip_b, c), me) for t in range(n)]
        if "from_d" in want:
            got["from_d"] = [copy(t, 3, (*chip_d, c), me) for t in range(n)]
        for key, k, chip in (("pass_a", 4, chip_a), ("pass_b", 5, chip_b), ("pass_d", 6, chip_d)):
            if key in want:
                got[key] = [copy(t, k, (*chip, c), sib) for t in range(n)]
        if "late" in want:
            got["late"] = [cp for t in range(n) for cp in
                           (copy(t, 0, sib, me), copy(t, 4, (*chip_b, 1 - c), me),
                            copy(t, 5, (*chip_a, 1 - c), me), copy(t, 6, (*chip_d, 1 - c), me))]
        return got

    def begin(srcs, outs, scratch):
        cps = copies(srcs, outs, scratch, ("mine", "first"))
        for cp in cps["mine"] + cps["first"]:
            cp.start()

    def hop(srcs, outs, scratch):
        cps = copies(srcs, outs, scratch, ("from_a", "hop", "pass_a"))
        for landed, fwd, sibling in zip(cps["from_a"], cps["hop"], cps["pass_a"]):
            landed.wait_recv()
            fwd.start()
            sibling.start()

    def pass_on(srcs, outs, scratch):
        cps = copies(srcs, outs, scratch, ("from_b", "pass_b", "from_d", "pass_d"))
        for landed, sibling in list(zip(cps["from_b"], cps["pass_b"])) + list(zip(cps["from_d"], cps["pass_d"])):
            landed.wait_recv()
            sibling.start()

    def end(srcs, outs, scratch):
        cps = copies(srcs, outs, scratch, ("mine", "first", "hop", "pass_a", "pass_b", "pass_d", "late"))
        for cp in cps["late"]:
            cp.wait_recv()
        for cp in cps["first"] + cps["hop"] + cps["pass_a"] + cps["pass_b"] + cps["pass_d"]:
            cp.wait_send()
        for cp in cps["mine"]:
            cp.wait()

    return _Comm(
        shards, [jax.ShapeDtypeStruct((N_DEV,) + s.shape, s.dtype) for s in shards],
        [pltpu.SemaphoreType.DMA((7 * n,)), pltpu.SemaphoreType.DMA((7 * n,)), pltpu.SemaphoreType.DMA((n,))],
        begin, end, [(0.67, hop), (0.9, pass_on)])


def _sibling_comm(grads):
    n = len(grads)

    def copies(srcs, outs, scratch):
        send_sems, recv_sems = scratch
        x, y, c = _place()
        return [pltpu.make_async_remote_copy(
            src_ref=srcs[t].at[2 * p + (1 - c)], dst_ref=outs[t].at[p],
            send_sem=send_sems.at[4 * t + p], recv_sem=recv_sems.at[4 * t + p],
            device_id=(x, y, 1 - c), device_id_type=MESH) for t in range(n) for p in range(4)]

    def begin(srcs, outs, scratch):
        for cp in copies(srcs, outs, scratch):
            cp.start()

    def end(srcs, outs, scratch):
        cps = copies(srcs, outs, scratch)
        for cp in cps:
            cp.wait_recv()
        for cp in cps:
            cp.wait_send()

    return _Comm(
        grads, [jax.ShapeDtypeStruct((4,) + g.shape[1:], g.dtype) for g in grads],
        [pltpu.SemaphoreType.DMA((4 * n,)), pltpu.SemaphoreType.DMA((4 * n,))], begin, end)


def _chips_comm(sums, relations):
    n = len(sums)
    per = [list(r) for r in relations] if isinstance(relations[0], (list, tuple)) else [list(relations)] * n
    first = [sum(len(r) for r in per[:t]) for t in range(n)]
    n_copies = sum(len(r) for r in per)

    def copies(srcs, outs, scratch):
        send_sems, recv_sems = scratch
        x, y, c = _place()
        chips = _other_chips(x, y)
        return [pltpu.make_async_remote_copy(
            src_ref=srcs[t].at[2 * chips[j][0] + chips[j][1]], dst_ref=outs[t].at[s],
            send_sem=send_sems.at[first[t] + s], recv_sem=recv_sems.at[first[t] + s],
            device_id=(*chips[j], c), device_id_type=MESH) for t in range(n) for s, j in enumerate(per[t])]

    def begin(srcs, outs, scratch):
        for cp in copies(srcs, outs, scratch):
            cp.start()

    def end(srcs, outs, scratch):
        cps = copies(srcs, outs, scratch)
        for cp in cps:
            cp.wait_recv()
        for cp in cps:
            cp.wait_send()

    return _Comm(
        sums, [jax.ShapeDtypeStruct((len(r),) + s.shape[1:], s.dtype) for s, r in zip(sums, per)],
        [pltpu.SemaphoreType.DMA((n_copies,)), pltpu.SemaphoreType.DMA((n_copies,))], begin, end)


def _run_comm(comm, *, name):
    ci, co = len(comm.ins), len(comm.outs)

    def body(*refs):
        cin, cout, csc = refs[:ci], refs[ci:ci + co], refs[ci + co:]
        comm.begin(cin, cout, csc)
        for _, relay in comm.relays:
            relay(cin, cout, csc)
        comm.end(cin, cout, csc)

    return pl.pallas_call(
        body, name=name, in_specs=_any_specs(ci), out_specs=_any_specs(co), out_shape=comm.outs,
        scratch_shapes=comm.scratch)(*comm.ins)


def _chip_sum(g, r1, core, *, name, tr=256):
    _, R, C = g.shape
    tr = _tile(R, tr, 16)

    def body(c_ref, g_ref, r_ref, o_ref):
        o_ref[...] = (g_ref[...].astype(F32) + r_ref[...].astype(F32)).astype(BF16)

    return pl.pallas_call(
        body, name=name,
        grid_spec=pltpu.PrefetchScalarGridSpec(
            num_scalar_prefetch=1, grid=(4, R // tr),
            in_specs=[pl.BlockSpec((None, tr, C), lambda p, i, c: (2 * p + c[0], i, 0)),
                      pl.BlockSpec((None, tr, C), lambda p, i, c: (p, i, 0))],
            out_specs=pl.BlockSpec((None, tr, C), lambda p, i, c: (p, i, 0))),
        out_shape=jax.ShapeDtypeStruct((4, R, C), BF16),
        compiler_params=_cp("parallel", "parallel"))(core, g, r1)


def _adamw_math(w, g, m, v):
    m = ADAM_B1 * m + (1.0 - ADAM_B1) * g
    v = ADAM_B2 * v + (1.0 - ADAM_B2) * (g * g)
    m_hat = m / (1.0 - ADAM_B1 ** ADAM_STEP)
    v_hat = v / (1.0 - ADAM_B2 ** ADAM_STEP)
    delta = -ADAM_LR * (m_hat / (jnp.sqrt(v_hat) + ADAM_EPS) + ADAM_WD * w)
    return delta, m, v


def _adamw_shard(s1, others, chip, w, m, v, *, name, tr=64):
    R0, C0 = w.shape
    C = s1.shape[2]
    tr = _tile(R0, tr, 8)
    (a0, k0), (a1, k1), (a2, k2) = others

    def body(p_ref, s_ref, r0_ref, r1_ref, r2_ref, w_ref, m_ref, v_ref, g_out, d_out, m_out, v_out):
        g = s_ref[:, :C0].astype(F32)
        for r_ref in (r0_ref, r1_ref, r2_ref):
            g = g + r_ref[:, :C0].astype(F32)
        delta, mn, vn = _adamw_math(w_ref[...], g, m_ref[...], v_ref[...])
        g_out[...] = g
        d_out[...] = delta
        m_out[...] = mn
        v_out[...] = vn

    full = pl.BlockSpec((tr, C0), lambda i, p: (i, 0))
    out = jax.ShapeDtypeStruct((R0, C0), F32)
    return pl.pallas_call(
        body, name=name,
        grid_spec=pltpu.PrefetchScalarGridSpec(
            num_scalar_prefetch=1, grid=(R0 // tr,),
            in_specs=[pl.BlockSpec((None, tr, C), lambda i, p: (p[0], i, 0)),
                      pl.BlockSpec((None, tr, C), lambda i, p: (k0, i, 0)),
                      pl.BlockSpec((None, tr, C), lambda i, p: (k1, i, 0)),
                      pl.BlockSpec((None, tr, C), lambda i, p: (k2, i, 0)),
                      full, full, full],
            out_specs=[full, full, full, full]),
        out_shape=[out, out, out, out],
        compiler_params=_cp("parallel"))(chip, s1, a0, a1, a2, w, m, v)


def _all_reduce_small(part, *, name):
    SR = part.shape[0]

    def body(p_ref, o_ref, buf, send_sems, recv_sems):
        x, y, c = _place()
        me = 4 * x + 2 * y + c
        buf[0] = p_ref[...]
        copies = []
        for k in range(1, N_DEV):
            peer = (1 - x if k & 4 else x, 1 - y if k & 2 else y, 1 - c if k & 1 else c)
            copies.append(pltpu.make_async_remote_copy(
                src_ref=p_ref, dst_ref=buf.at[k], send_sem=send_sems.at[k - 1], recv_sem=recv_sems.at[k - 1],
                device_id=peer, device_id_type=MESH))
        for cp in copies:
            cp.start()
        for cp in copies:
            cp.wait_recv()
        for cp in copies:
            cp.wait_send()
        acc = buf[me]
        for d in range(1, N_DEV):
            acc = acc + buf[jnp.bitwise_xor(me, d)]
        o_ref[...] = acc

    return pl.pallas_call(
        body, name=name,
        in_specs=[pl.BlockSpec(memory_space=pltpu.VMEM)], out_specs=pl.BlockSpec(memory_space=pltpu.VMEM),
        out_shape=jax.ShapeDtypeStruct(part.shape, F32),
        scratch_shapes=[pltpu.VMEM((N_DEV, SR, LANE), F32), pltpu.SemaphoreType.DMA((N_DEV - 1,)),
                        pltpu.SemaphoreType.DMA((N_DEV - 1,))])(part)


def _adamw_small(w, g, m, v, *, name):
    def body(w_ref, g_ref, m_ref, v_ref, d_out, m_out, v_out):
        delta, mn, vn = _adamw_math(w_ref[...], g_ref[...], m_ref[...], v_ref[...])
        d_out[...] = delta
        m_out[...] = mn
        v_out[...] = vn

    out = jax.ShapeDtypeStruct(w.shape, F32)
    return pl.pallas_call(body, name=name, out_shape=[out, out, out], compiler_params=_cp())(w, g, m, v)


def _in_layout(din8, fq):
    lay = []
    for b in range(N_DEV):
        g0 = din8 * b
        nq = min(max(fq - g0, 0), din8)
        oq = g0 % LANE if nq else 0
        nqt = -(-(oq + nq) // LANE) if nq else 0
        tq0 = (g0 - oq) // LANE
        pg, tg0, ngt = oq + nq, 0, 0
        if nq < din8:
            u0 = g0 + nq - fq
            pg = nqt * LANE + u0 % LANE
            tg0, ngt = u0 // LANE, -(-(u0 % LANE + din8 - nq) // LANE)
        lay.append(dict(nq=nq, oq=oq, tq0=tq0, nqt=nqt, pg=pg, tg0=tg0, ngt=ngt, width=pg + din8 - nq))
    return lay


def _place_block(shard, lay, pin, b):
    rows, din8 = shard.shape

    def place(s, l):
        nq = l["nq"]
        parts, at = [], 0
        for start, cols in ((l["oq"], s[:, :nq]), (l["pg"], s[:, nq:])):
            if cols.shape[1]:
                parts += [jnp.zeros((rows, start - at), s.dtype), cols]
                at = start + cols.shape[1]
        parts.append(jnp.zeros((rows, pin - at), s.dtype))
        return jnp.concatenate([p for p in parts if p.shape[1]], axis=1)

    return lax.switch(b, [functools.partial(place, l=l) for l in lay], shard)


def _unplace_block(g, lay, din8, b):
    def unplace(a, l):
        nq = l["nq"]
        parts = [a[:, l["oq"]:l["oq"] + nq], a[:, l["pg"]:l["pg"] + din8 - nq]]
        return jnp.concatenate([p for p in parts if p.shape[1]], axis=1)

    return lax.switch(b, [functools.partial(unplace, l=l) for l in lay], g)


def _tile_runs(sources):
    runs = []
    for src in sources:
        if len(src) == 1 and runs and not isinstance(runs[-1], list) \
                and runs[-1][0] == src[0][0] and runs[-1][1] + runs[-1][2] == src[0][1]:
            runs[-1] = (runs[-1][0], runs[-1][1], runs[-1][2] + 1)
        elif len(src) == 1:
            runs.append((src[0][0], src[0][1], 1))
        else:
            runs.append(list(src))
    return runs


def _gather_tiles(arrays, sources):
    rows, dtype = arrays[0].shape[0], arrays[0].dtype
    pieces = []
    for run in _tile_runs(sources):
        if isinstance(run, list):
            tile = jnp.zeros((rows, LANE), dtype)
            for a, t in run:
                tile = tile + arrays[a][:, t * LANE:(t + 1) * LANE]
            pieces.append(tile)
        else:
            a, t0, n = run
            pieces.append(arrays[a][:, t0 * LANE:(t0 + n) * LANE])
    return jnp.concatenate(pieces, axis=1)


def _assemble_w_in(blocks, lay, n_qkv_tiles, n_f_tiles, n_g_tiles):
    q_src = [[] for _ in range(n_qkv_tiles + n_f_tiles)]
    g_src = [[] for _ in range(n_g_tiles)]
    for b, l in enumerate(lay):
        for j in range(l["nqt"]):
            q_src[l["tq0"] + j].append((b, j))
        for j in range(l["ngt"]):
            g_src[l["tg0"] + j].append((b, l["pg"] // LANE + j))
    arrays = [blocks[b] for b in range(N_DEV)]
    return (_gather_tiles(arrays, q_src[:n_qkv_tiles]), _gather_tiles(arrays, q_src[n_qkv_tiles:]),
            _gather_tiles(arrays, g_src))


def _scatter_g_in(g_qkv, g_f, g_gates, lay, pin):
    n_qkv_tiles = g_qkv.shape[1] // LANE
    out = []
    for l in lay:
        src = [[] for _ in range(pin // LANE)]
        for j in range(l["nqt"]):
            t = l["tq0"] + j
            src[j] = [(0, t)] if t < n_qkv_tiles else [(1, t - n_qkv_tiles)]
        for j in range(l["ngt"]):
            src[l["pg"] // LANE + j] = [(2, l["tg0"] + j)]
        out.append(_gather_tiles([g_qkv, g_f, g_gates], src))
    return jnp.stack(out)


def _pack(parts, rows):
    flat = jnp.concatenate([p.reshape(-1).astype(F32) for p in parts])
    return jnp.pad(flat, (0, rows * LANE - flat.shape[0])).reshape(rows, LANE)


def _unpack(packed, shapes):
    flat = packed.reshape(-1)
    out, at = [], 0
    for s in shapes:
        n = 1
        for d in s:
            n *= d
        out.append(flat[at:at + n].reshape(s))
        at += n
    return out


def kernel(x, g_mix, w_in, b_f, b_gate, rel_bias, w_branch_a, w_branch_b, w_out, g_ffn, w_gate_ffn, w_up_ffn, w_down_ffn, g_final, loss_target, m_g_mix, m_w_in, m_b_f, m_b_gate, m_rel_bias, m_w_branch_a, m_w_branch_b, m_w_out, m_g_ffn, m_w_gate_ffn, m_w_up_ffn, m_w_down_ffn, m_g_final, v_g_mix, v_w_in, v_b_f, v_b_gate, v_rel_bias, v_w_branch_a, v_w_branch_b, v_w_out, v_g_ffn, v_w_gate_ffn, v_w_up_ffn, v_w_down_ffn, v_g_final):
    _, T, D = x.shape
    HA, n_rel = rel_bias.shape[1], rel_bias.shape[2]
    HB = b_f.shape[1]
    WA, WB = HA * HEAD_DIM, HB * HEAD_DIM
    din8, f8, d8 = w_in.shape[2], w_gate_ffn.shape[2], D // N_DEV
    DIN = N_DEV * din8
    PF = _pad_to(f8, LANE)
    fcol = 3 * WA + 3 * WB
    lay = _in_layout(din8, fcol + HB)
    PIN = _pad_to(max(l["width"] for l in lay), LANE)
    padk = LEFT_CHUNKS * CHUNK
    QA = min(256, T)
    KW = QA + padk
    KB = min(512, T)
    x2d, tgt = x[0], loss_target[0]
    cx, cy, cc = _place()
    blk = 4 * cx + 2 * cy + cc
    core = jnp.reshape(cc, (1,)).astype(jnp.int32)
    chip = jnp.reshape(2 * cx + cy, (1,)).astype(jnp.int32)
    every = [0, 1, 2]

    s_in = _place_block(w_in[0].astype(BF16), lay, PIN, blk)
    s_a, s_b, s_out = w_branch_a[0].astype(BF16), w_branch_b[0].astype(BF16), w_out[0].astype(BF16)
    s_g = jnp.pad(w_gate_ffn[0].T.astype(BF16), ((0, PF - f8), (0, 0)))
    s_u = jnp.pad(w_up_ffn[0].T.astype(BF16), ((0, PF - f8), (0, 0)))
    s_d = jnp.pad(w_down_ffn[0].astype(BF16), ((0, PF - f8), (0, 0)))
    Win, = _run_comm(_gather_comm([s_in]), name="gather_w_in")
    Wqkv, Wf, Wgt = _assemble_w_in(Win, lay, fcol // LANE, 1, 2 * D // LANE)
    Wqkv, Wf, Wgt = Wqkv.reshape(1, D, fcol), Wf.reshape(1, D, LANE), Wgt.reshape(1, D, 2 * D)
    bf_pad = jnp.pad(b_f, ((0, 0), (0, LANE - HB)))
    fox_cols = (3 * HA, 3 * HA + HB, 3 * HA + 2 * HB)

    h = _rms_fwd(x2d, g_mix, name="rms_mix")
    qkv, (Wg,) = _mm_nn(h, Wqkv, name="proj_qkv", out_dtype=BF16, comm=_gather_comm([s_g]))
    Wg = Wg.reshape(1, N_DEV * PF, D)
    gates, (Wa, Wb, Wout) = _mm_nn(h, Wgt, name="proj_gates", out_dtype=BF16, comm=_gather_comm([s_a, s_b, s_out]))
    Wout = Wout.reshape(1, D, D)
    flog = _mm_nn(h, Wf, name="proj_forget", out_dtype=F32)

    kap = jnp.pad(qkv[:, WA:2 * WA], ((padk, 0), (0, 0)))
    vap = jnp.pad(qkv[:, 2 * WA:3 * WA], ((padk, 0), (0, 0)))
    btab = _bias_table(rel_bias[0], QA, KW)
    o_a, lse_a = _attn_a_fwd(qkv, kap, vap, btab, name="attn_a_fwd")

    cum = _fgate_fwd(flog, bf_pad, name="forget_cumsum")
    cum_t = cum[:, :HB].T.reshape(HB, T // KB, KB)
    (o_b, lse_b), (Wu,) = _fox_fwd(qkv, fox_cols, cum, cum_t, name="fox_fwd", comm=_gather_comm([s_u]))
    Wu = Wu.reshape(1, N_DEV * PF, D)

    u_a = _mm_nn(o_a, Wa, name="branch_a", out_dtype=BF16)
    u_b = _mm_nn(o_b, Wb, name="branch_b", out_dtype=BF16)
    merged = _merge_fwd(gates, b_gate, u_a, u_b, name="merge_fwd")
    x1 = _mm_nn(merged, Wout, name="out_proj", out_dtype=F32, res=x2d)
    h2 = _rms_fwd(x1, g_ffn, name="rms_ffn")
    gt, (Wd,) = _mm_nt(h2, Wg, name="ffn_gate", out_dtype=BF16, tc=D, comm=_gather_comm([s_d]))
    Wd = Wd.reshape(1, N_DEV * PF, D)
    up, hid = _mm_nt(h2, Wu, name="ffn_up", out_dtype=BF16, epi=_swiglu_fwd_epi, extras=[gt], n_out=2)
    x2 = _mm_nn(hid, Wd, name="ffn_down", out_dtype=F32, res=x1, tk=2 * PF)
    dx2, dx2b, dg_final, loss_part = _final_loss(x2, tgt, g_final.reshape(1, D), name="final_loss")

    dgt, dup = _mm_nt(dx2b, Wd, name="d_hidden", out_dtype=BF16, epi=_swiglu_bwd_epi, extras=[gt, up], n_out=2)
    gWd = _mm_tn(hid, dx2b, 1, name="gw_down").reshape(N_DEV, PF, D)
    dh2, (sWd,) = _mm_nn(dgt, Wg, name="d_h2_gate", out_dtype=F32, tk=2 * PF, comm=_sibling_comm([gWd]))
    cWd = _chip_sum(gWd, sWd, core, name="chip_sum_w_down_ffn")
    dh2, (rWd,) = _mm_nn(dup, Wu, name="d_h2_up", out_dtype=F32, res=dh2, tk=2 * PF,
                         comm=_chips_comm([cWd], every))
    gWg = _mm_tn(dgt, h2, 1, name="gw_gate").reshape(N_DEV, PF, D)
    gWu = _mm_tn(dup, h2, 1, name="gw_up").reshape(N_DEV, PF, D)
    dx1, dx1b, dg_ffn = _rms_bwd(dh2, x1, g_ffn, dx2, name="rms_ffn_bwd")
    dmer, (sWg, sWu) = _mm_nt(dx1b, Wout, name="d_merged", out_dtype=BF16, comm=_sibling_comm([gWg, gWu]))
    cWg = _chip_sum(gWg, sWg, core, name="chip_sum_w_gate_ffn")
    cWu = _chip_sum(gWu, sWu, core, name="chip_sum_w_up_ffn")
    gWout = _mm_tn(merged, dx1b, 1, name="gw_out").reshape(N_DEV, d8, D)
    dua, dub, dgates, dbg = _merge_bwd(dmer, gates, b_gate, u_a, u_b, name="merge_bwd")
    g_gates, (rWg01,) = _mm_tn(h, dgates, 1, name="gw_gates", comm=_chips_comm([cWg], [0, 1]))
    g_gates = g_gates[0]
    doa = _mm_nt(dua, Wa, name="d_o_a", out_dtype=BF16)
    dob = _mm_nt(dub, Wb, name="d_o_b", out_dtype=BF16)
    gWa = _mm_tn(o_a, dua, N_DEV, name="gw_branch_a")
    gWb = _mm_tn(o_b, dub, N_DEV, name="gw_branch_b")

    (dqkv, dkap, dvap, dtab), (sWout, sWa, sWb) = _attn_a_bwd(
        qkv, kap, vap, btab, o_a, doa, lse_a, name="attn_a_bwd", comm=_sibling_comm([gWout, gWa, gWb]),
        dq_width=fcol)
    cWout = _chip_sum(gWout, sWout, core, name="chip_sum_w_out")
    cWa = _chip_sum(gWa, sWa, core, name="chip_sum_w_branch_a")
    cWb = _chip_sum(gWb, sWb, core, name="chip_sum_w_branch_b")
    skew = jnp.pad(dtab.reshape(HA, QA * KW), ((0, 0), (0, QA))).reshape(HA, QA, KW + 1)
    colsum, tail = _relbias_fold(skew, dtab, name="relbias_fold")
    rev = colsum[:, 0, ::-1]
    d_rel = jnp.concatenate([rev[:, :2 * MAX_REL], tail[:, 0, :1]], axis=1)

    (dqkv, dkb, dvb, dcq, dck), (rWg2, rWout, rWa, rWb) = _fox_bwd(
        qkv, fox_cols, cum, cum_t, o_b, dob, lse_b, name="fox_bwd",
        comm=_chips_comm([cWg, cWout, cWa, cWb], [[2], every, every, every]), dq_into=(dqkv, fox_cols[0]))
    dcq_p = jnp.pad(dcq.reshape(T, HB, HEAD_DIM)[:, :, 0], ((0, 0), (0, LANE - HB)))
    dck_p = jnp.pad(dck.reshape(HB, T).T, ((0, 0), (0, LANE - HB)))
    dflog, dbf = _fgate_bwd(dcq_p, dck_p, flog, bf_pad, name="forget_bwd")

    dqkv = _cast_into(dqkv, dkap, padk, WA, name="dqkv_k_a")
    dqkv = _cast_into(dqkv, dvap, padk, 2 * WA, name="dqkv_v_a")
    dqkv = _cast_into(dqkv, dkb, 0, 3 * WA + WB, name="dqkv_k_b")
    dqkv = _cast_into(dqkv, dvb, 0, 3 * WA + 2 * WB, name="dqkv_v_b")
    dfb = dflog.astype(BF16)
    g_qkv, (rWu,) = _mm_tn(h, dqkv, 1, name="gw_qkv", comm=_chips_comm([cWu], every))
    gWin = _scatter_g_in(g_qkv[0], _mm_tn(h, dfb, 1, name="gw_forget")[0], g_gates, lay, PIN)
    dh, (sWin,) = _mm_nt(dfb, Wf, name="d_h_forget", out_dtype=F32, comm=_sibling_comm([gWin]))
    cWin = _chip_sum(gWin, sWin, core, name="chip_sum_w_in")
    dh, (rWin01,) = _mm_nt(dqkv, Wqkv, name="d_h_qkv", out_dtype=F32, res=dh, comm=_chips_comm([cWin], [0, 1]))
    dh, (rWin2,) = _mm_nt(dgates, Wgt, name="d_h_gates", out_dtype=F32, res=dh, comm=_chips_comm([cWin], [2]))
    dx, dg_mix = _rms_bwd(dh, x2d, g_mix, dx1, name="rms_mix_bwd", with_bf16=False)

    def slots(r):
        return [(r, 0), (r, 1), (r, 2)]

    in_parts = [_unplace_block(p, lay, din8, blk)[None] for p in
                (lax.dynamic_index_in_dim(cWin, 2 * cx + cy, keepdims=False), rWin01[0], rWin01[1], rWin2[0])]
    first = jnp.zeros((1,), jnp.int32)
    big = {}
    for nm, s1, others, which, w, m, v in [
            ("w_down_ffn", cWd, slots(rWd), chip, w_down_ffn, m_w_down_ffn, v_w_down_ffn),
            ("w_gate_ffn", cWg, [(rWg01, 0), (rWg01, 1), (rWg2, 0)], chip, w_gate_ffn, m_w_gate_ffn, v_w_gate_ffn),
            ("w_out", cWout, slots(rWout), chip, w_out, m_w_out, v_w_out),
            ("w_branch_a", cWa, slots(rWa), chip, w_branch_a, m_w_branch_a, v_w_branch_a),
            ("w_branch_b", cWb, slots(rWb), chip, w_branch_b, m_w_branch_b, v_w_branch_b),
            ("w_up_ffn", cWu, slots(rWu), chip, w_up_ffn, m_w_up_ffn, v_w_up_ffn),
            ("w_in", in_parts[0], [(p, 0) for p in in_parts[1:]], first, w_in, m_w_in, v_w_in)]:
        flipped = nm in ("w_gate_ffn", "w_up_ffn")
        w2, m2, v2 = ((a[0].T if flipped else a[0]) for a in (w, m, v))
        outs = _adamw_shard(s1, others, which, w2, m2, v2, name="adamw_" + nm)
        big[nm] = [(o.T if flipped else o)[None] for o in outs]

    small_names = ["g_mix", "b_f", "b_gate", "rel_bias", "g_ffn", "g_final"]
    small_w = [g_mix, b_f, b_gate, rel_bias, g_ffn, g_final]
    small_m = [m_g_mix, m_b_f, m_b_gate, m_rel_bias, m_g_ffn, m_g_final]
    small_v = [v_g_mix, v_b_f, v_b_gate, v_rel_bias, v_g_ffn, v_g_final]
    small_g = [dg_mix, dbf[:, :HB], dbg, d_rel[None], dg_ffn, dg_final[0]]
    shapes = [w.shape for w in small_w]
    n_small = sum(w.size for w in small_w) + 1
    rows = _pad_to(-(-n_small // LANE), 8)
    total = _all_reduce_small(_pack(small_g + [loss_part[:1, :1]], rows), name="all_reduce_small")
    d_s, m_s, v_s = _adamw_small(_pack(small_w, rows), total, _pack(small_m, rows), _pack(small_v, rows),
                                 name="adamw_small")
    g_small = dict(zip(small_names, _unpack(total, shapes)))
    d_small = dict(zip(small_names, _unpack(d_s, shapes)))
    m_small = dict(zip(small_names, _unpack(m_s, shapes)))
    v_small = dict(zip(small_names, _unpack(v_s, shapes)))
    loss = total.reshape(-1)[n_small - 1]

    order = ["g_mix", "w_in", "b_f", "b_gate", "rel_bias", "w_branch_a", "w_branch_b", "w_out", "g_ffn",
             "w_gate_ffn", "w_up_ffn", "w_down_ffn", "g_final"]

    def pick(idx, small):
        return [big[nm][idx] if nm in big else small[nm] for nm in order]

    return (loss, dx[None], *pick(0, g_small), *pick(1, d_small), *pick(2, m_small), *pick(3, v_small))
```

```python
import functools

import jax
import jax.numpy as jnp
from jax import lax
from jax.experimental import pallas as pl
from jax.experimental.pallas import tpu as pltpu

F32 = jnp.float32
BF16 = jnp.bfloat16
MESH = pl.DeviceIdType.MESH

N_DEV = 8
LANE = 128
HEAD_DIM = 128
CHUNK = 64
LEFT_CHUNKS = 8
MAX_REL = 256
RMS_EPS = 1e-6
NEG_INF = -1e30
ATTN_SCALE = HEAD_DIM ** -0.5
VMEM_LIMIT_BYTES = 56 * 1024 * 1024

ADAM_LR = 0.001
ADAM_B1 = 0.9
ADAM_B2 = 0.999
ADAM_EPS = 1e-08
ADAM_WD = 0.01
ADAM_STEP = 10

NT_DIMS = (((1,), (1,)), ((), ()))
TN_DIMS = (((0,), (0,)), ((), ()))


def _cp(*sem):
    return pltpu.CompilerParams(dimension_semantics=sem if sem else None, vmem_limit_bytes=VMEM_LIMIT_BYTES)


def _pad_to(n, mult):
    return -(-n // mult) * mult


def _tile(dim, pref, align):
    t = (min(pref, dim) // align) * align
    while t >= align:
        if dim % t == 0:
            return t
        t -= align
    return dim


def _sigmoid(z):
    return 1.0 / (1.0 + jnp.exp(-z))


class _Comm:
    def __init__(self, ins, outs, scratch, begin, end, relays=()):
        self.ins, self.outs, self.scratch = list(ins), list(outs), list(scratch)
        self.begin, self.end, self.relays = begin, end, list(relays)


def _call(body, args, *, name, grid, in_specs, out_specs, out_shape, scratch=(), sem=(), comm=None, aliases=None):
    single = not isinstance(out_shape, (list, tuple))
    out_shape = [out_shape] if single else list(out_shape)
    out_specs = [out_specs] if single else list(out_specs)
    aliases = dict(aliases or {})
    if comm is None:
        res = pl.pallas_call(
            body, name=name, grid=grid, in_specs=list(in_specs), out_specs=out_specs, out_shape=out_shape,
            scratch_shapes=list(scratch), input_output_aliases=aliases, compiler_params=_cp(*sem))(*args)
        return res[0] if single else res
    ni, no, ns = len(args), len(out_shape), len(scratch)
    ci, co = len(comm.ins), len(comm.outs)
    total = 1
    for g in grid:
        total *= g

    def hosted(*refs):
        a, cin = refs[:ni], refs[ni:ni + ci]
        o, cout = refs[ni + ci:ni + ci + no], refs[ni + ci + no:ni + ci + no + co]
        s, csc = refs[ni + ci + no + co:ni + ci + no + co + ns], refs[ni + ci + no + co + ns:]
        step = pl.program_id(0)
        for d in range(1, len(grid)):
            step = step * grid[d] + pl.program_id(d)

        @pl.when(step == 0)
        def _():
            comm.begin(cin, cout, csc)

        for fraction, relay in comm.relays:
            @pl.when(step == min(int(total * fraction), total - 1))
            def _(relay=relay):
                relay(cin, cout, csc)

        body(*a, *o, *s)

        @pl.when(step == total - 1)
        def _():
            comm.end(cin, cout, csc)

    res = pl.pallas_call(
        hosted, name=name, grid=grid, in_specs=list(in_specs) + _any_specs(ci),
        out_specs=out_specs + _any_specs(co), out_shape=out_shape + comm.outs,
        scratch_shapes=list(scratch) + comm.scratch, input_output_aliases=aliases,
        compiler_params=_cp(*(("arbitrary",) * len(grid))))(*args, *comm.ins)
    return (res[0] if single else res[:no]), res[no:]


def _any_specs(n):
    return [pl.BlockSpec(memory_space=pl.ANY)] * n


def _mm_body(dims, nk, has_res, n_ext, n_out, epi):
    def body(*refs):
        a_ref, w_ref = refs[:2]
        at = 2 + has_res
        r_ref = refs[2] if has_res else None
        e_refs, o_refs = refs[at:at + n_ext], refs[at + n_ext:at + n_ext + n_out]

        def emit(v):
            if r_ref is not None:
                v = v + r_ref[...]
            vals = (v,) if epi is None else epi(v, *[e[...] for e in e_refs])
            for o_ref, val in zip(o_refs, vals):
                o_ref[...] = val.astype(o_ref.dtype)

        if nk == 1:
            emit(lax.dot_general(a_ref[...], w_ref[...], dims, preferred_element_type=F32))
            return
        acc = refs[-1]
        k = pl.program_id(2)

        @pl.when(k == 0)
        def _():
            acc[...] = jnp.zeros_like(acc)

        acc[...] += lax.dot_general(a_ref[...], w_ref[...], dims, preferred_element_type=F32)

        @pl.when(k == nk - 1)
        def _():
            emit(acc[...])

    return body


def _mm_nn(x, w, *, name, out_dtype, res=None, tm=1024, tn=1024, tk=None, comm=None, epi=None, extras=(), n_out=1):
    M, K = x.shape
    NB, _, NW = w.shape
    tm, tn = _tile(M, tm, 16), _tile(NW, tn, LANE)
    tk = K if tk is None else _tile(K, tk, LANE)
    nj, nk = NW // tn, K // tk
    tile = pl.BlockSpec((tm, tn), lambda j, i, k: (i, j))
    in_specs = [pl.BlockSpec((tm, tk), lambda j, i, k: (i, k)),
                pl.BlockSpec((None, tk, tn), lambda j, i, k: (j // nj, k, j % nj))]
    args = [x, w] + ([res] if res is not None else []) + list(extras)
    in_specs += [tile] * (len(args) - 2)
    out = jax.ShapeDtypeStruct((M, NB * NW), out_dtype)
    return _call(
        _mm_body((((1,), (0,)), ((), ())), nk, res is not None, len(extras), n_out, epi), args,
        name=name, grid=(NB * nj, M // tm, nk), in_specs=in_specs,
        out_specs=tile if n_out == 1 else [tile] * n_out, out_shape=out if n_out == 1 else [out] * n_out,
        scratch=[pltpu.VMEM((tm, tn), F32)] if nk > 1 else [],
        sem=("parallel", "parallel", "arbitrary"), comm=comm)


def _mm_nt(dy, w, *, name, out_dtype, res=None, tm=1024, tn=1024, tc=2816, comm=None, epi=None, extras=(), n_out=1):
    M, N = dy.shape
    NB, Kw, NW = w.shape
    assert N == NB * NW
    tm, tn, tc = _tile(M, tm, 16), _tile(Kw, tn, LANE), _tile(NW, tc, LANE)
    nc = NW // tc
    nk = NB * nc
    tile = pl.BlockSpec((tm, tn), lambda i, j, k: (i, j))
    in_specs = [pl.BlockSpec((tm, tc), lambda i, j, k: (i, k)),
                pl.BlockSpec((None, tn, tc), lambda i, j, k: (k // nc, j, k % nc))]
    args = [dy, w] + ([res] if res is not None else []) + list(extras)
    in_specs += [tile] * (len(args) - 2)
    out = jax.ShapeDtypeStruct((M, Kw), out_dtype)
    return _call(
        _mm_body(NT_DIMS, nk, res is not None, len(extras), n_out, epi), args,
        name=name, grid=(M // tm, Kw // tn, nk), in_specs=in_specs,
        out_specs=tile if n_out == 1 else [tile] * n_out, out_shape=out if n_out == 1 else [out] * n_out,
        scratch=[pltpu.VMEM((tm, tn), F32)] if nk > 1 else [],
        sem=("parallel", "parallel", "arbitrary"), comm=comm)


def _mm_tn(x, dy, nb, *, name, tm=512, tn=1024, comm=None):
    M, Kw = x.shape
    NW = dy.shape[1] // nb
    tm, tn = _tile(Kw, tm, LANE), _tile(NW, tn, LANE)
    nj = NW // tn

    def body(x_ref, d_ref, o_ref):
        o_ref[...] = lax.dot_general(x_ref[...], d_ref[...], TN_DIMS, preferred_element_type=F32).astype(o_ref.dtype)

    return _call(
        body, [x, dy], name=name, grid=(nb * nj, Kw // tm),
        in_specs=[pl.BlockSpec((M, tm), lambda j, i: (0, i)),
                  pl.BlockSpec((M, tn), lambda j, i: (0, j))],
        out_specs=pl.BlockSpec((None, tm, tn), lambda j, i: (j // nj, i, j % nj)),
        out_shape=jax.ShapeDtypeStruct((nb, Kw, NW), BF16),
        sem=("parallel", "parallel"), comm=comm)


def _rms_fwd(x, g, *, name, tr=256):
    T, D = x.shape
    tr = _tile(T, tr, 16)

    def body(x_ref, g_ref, h_ref):
        xv = x_ref[...]
        r = lax.rsqrt(jnp.mean(xv * xv, axis=-1, keepdims=True) + RMS_EPS)
        h_ref[...] = ((xv * r) * g_ref[...]).astype(BF16)

    return pl.pallas_call(
        body, name=name, grid=(T // tr,),
        in_specs=[pl.BlockSpec((tr, D), lambda i: (i, 0)), pl.BlockSpec((1, D), lambda i: (0, 0))],
        out_specs=pl.BlockSpec((tr, D), lambda i: (i, 0)),
        out_shape=jax.ShapeDtypeStruct((T, D), BF16),
        compiler_params=_cp("parallel"))(x, g)


def _rms_bwd(dh, x, g, dres, *, name, tr=128, with_bf16=True):
    T, D = x.shape
    tr = _tile(T, tr, 16)

    def body(dh_ref, x_ref, g_ref, dr_ref, dx_ref, *rest):
        dg_ref = rest[-1]
        i = pl.program_id(0)
        xv = x_ref[...]
        dhv = dh_ref[...]
        r = lax.rsqrt(jnp.mean(xv * xv, axis=-1, keepdims=True) + RMS_EPS)
        a = dhv * g_ref[...]
        dot = jnp.mean(a * xv, axis=-1, keepdims=True)
        dx = dr_ref[...] + r * a - xv * (r * r * r * dot)
        dx_ref[...] = dx
        if with_bf16:
            rest[0][...] = dx.astype(BF16)
        part = jnp.sum(dhv * (xv * r), axis=0, keepdims=True)

        @pl.when(i == 0)
        def _():
            dg_ref[...] = part

        @pl.when(i > 0)
        def _():
            dg_ref[...] += part

    row = pl.BlockSpec((tr, D), lambda i: (i, 0))
    vec = pl.BlockSpec((1, D), lambda i: (0, 0))
    halves = [jax.ShapeDtypeStruct((T, D), BF16)] if with_bf16 else []
    return pl.pallas_call(
        body, name=name, grid=(T // tr,),
        in_specs=[row, row, vec, row], out_specs=[row] + [row] * len(halves) + [vec],
        out_shape=[jax.ShapeDtypeStruct((T, D), F32)] + halves + [jax.ShapeDtypeStruct((1, D), F32)],
        compiler_params=_cp("arbitrary"))(dh, x, g, dres)


def _final_loss(x2, tgt, g, *, name, tr=128):
    T, D = x2.shape
    tr = _tile(T, tr, 16)

    def body(x_ref, t_ref, g_ref, dx_ref, dxb_ref, dg_ref, loss_ref):
        i = pl.program_id(0)
        xv = x_ref[...]
        gv = g_ref[...]
        r = lax.rsqrt(jnp.mean(xv * xv, axis=-1, keepdims=True) + RMS_EPS)
        xh = xv * r
        diff = xh * gv - t_ref[...]
        lpart = 0.5 * jnp.sum(jnp.mean(diff * diff, axis=-1, keepdims=True))
        dy = diff * (1.0 / D)
        a = dy * gv
        dot = jnp.mean(a * xv, axis=-1, keepdims=True)
        dx = r * a - xv * (r * r * r * dot)
        dx_ref[...] = dx
        dxb_ref[...] = dx.astype(BF16)
        part = jnp.sum(dy * xh, axis=0, keepdims=True)
        lblk = jnp.full((8, LANE), lpart, F32)

        @pl.when(i == 0)
        def _():
            dg_ref[...] = part
            loss_ref[...] = lblk

        @pl.when(i > 0)
        def _():
            dg_ref[...] += part
            loss_ref[...] += lblk

    row = pl.BlockSpec((tr, D), lambda i: (i, 0))
    vec = pl.BlockSpec((1, D), lambda i: (0, 0))
    return pl.pallas_call(
        body, name=name, grid=(T // tr,),
        in_specs=[row, row, vec],
        out_specs=[row, row, vec, pl.BlockSpec((8, LANE), lambda i: (0, 0))],
        out_shape=[jax.ShapeDtypeStruct((T, D), F32), jax.ShapeDtypeStruct((T, D), BF16),
                   jax.ShapeDtypeStruct((1, D), F32), jax.ShapeDtypeStruct((8, LANE), F32)],
        compiler_params=_cp("arbitrary"))(x2, tgt, g)


def _merge_fwd(gates, b_gate, ua, ub, *, name, tr=256):
    T, D = ua.shape
    tr = _tile(T, tr, 16)

    def body(ga_ref, gb_ref, ba_ref, bb_ref, ua_ref, ub_ref, o_ref):
        sa = _sigmoid(ga_ref[...].astype(F32) + ba_ref[...])
        sb = _sigmoid(gb_ref[...].astype(F32) + bb_ref[...])
        o_ref[...] = (sa * ua_ref[...].astype(F32) + sb * ub_ref[...].astype(F32)).astype(BF16)

    row = pl.BlockSpec((tr, D), lambda i: (i, 0))
    return pl.pallas_call(
        body, name=name, grid=(T // tr,),
        in_specs=[row, pl.BlockSpec((tr, D), lambda i: (i, 1)),
                  pl.BlockSpec((1, D), lambda i: (0, 0)), pl.BlockSpec((1, D), lambda i: (0, 1)), row, row],
        out_specs=row, out_shape=jax.ShapeDtypeStruct((T, D), BF16),
        compiler_params=_cp("parallel"))(gates, gates, b_gate, b_gate, ua, ub)


def _merge_bwd(dm, gates, b_gate, ua, ub, *, name, tr=256):
    T, D = ua.shape
    tr = _tile(T, tr, 16)

    def body(dm_ref, ga_ref, gb_ref, ba_ref, bb_ref, ua_ref, ub_ref, dua_ref, dub_ref, dg_ref, db_ref):
        i = pl.program_id(0)
        dmv = dm_ref[...].astype(F32)
        sa = _sigmoid(ga_ref[...].astype(F32) + ba_ref[...])
        sb = _sigmoid(gb_ref[...].astype(F32) + bb_ref[...])
        dua_ref[...] = (dmv * sa).astype(BF16)
        dub_ref[...] = (dmv * sb).astype(BF16)
        dga = dmv * ua_ref[...].astype(F32) * (sa * (1.0 - sa))
        dgb = dmv * ub_ref[...].astype(F32) * (sb * (1.0 - sb))
        dg_ref[:, :D] = dga.astype(BF16)
        dg_ref[:, D:] = dgb.astype(BF16)
        pa = jnp.sum(dga, axis=0, keepdims=True)
        pb = jnp.sum(dgb, axis=0, keepdims=True)

        @pl.when(i == 0)
        def _():
            db_ref[:, :D] = pa
            db_ref[:, D:] = pb

        @pl.when(i > 0)
        def _():
            db_ref[:, :D] += pa
            db_ref[:, D:] += pb

    row = pl.BlockSpec((tr, D), lambda i: (i, 0))
    vec = pl.BlockSpec((1, D), lambda i: (0, 0))
    act = jax.ShapeDtypeStruct((T, D), BF16)
    return pl.pallas_call(
        body, name=name, grid=(T // tr,),
        in_specs=[row, row, pl.BlockSpec((tr, D), lambda i: (i, 1)),
                  vec, pl.BlockSpec((1, D), lambda i: (0, 1)), row, row],
        out_specs=[row, row, pl.BlockSpec((tr, 2 * D), lambda i: (i, 0)), pl.BlockSpec((1, 2 * D), lambda i: (0, 0))],
        out_shape=[act, act, jax.ShapeDtypeStruct((T, 2 * D), BF16), jax.ShapeDtypeStruct((1, 2 * D), F32)],
        compiler_params=_cp("arbitrary"))(dm, gates, gates, b_gate, b_gate, ua, ub)


def _swiglu_fwd_epi(up, gate):
    g = gate.astype(F32)
    up = up.astype(BF16).astype(F32)
    return up, g * _sigmoid(g) * up


def _swiglu_bwd_epi(d, gate, up):
    g = gate.astype(F32)
    s = _sigmoid(g)
    return d * up.astype(F32) * (s * (1.0 + g * (1.0 - s))), d * (g * s)


def _cast_into(buf, src, row_off, col_off, *, name, tr=256):
    T = buf.shape[0]
    W = src.shape[1]
    tr = _tile(T, tr, 16)
    cw = W
    while col_off % cw or W % cw:
        cw -= LANE
    assert row_off % tr == 0
    c0, r0 = col_off // cw, row_off // tr

    def body(s_ref, b_ref, o_ref):
        o_ref[...] = s_ref[...].astype(o_ref.dtype)

    return pl.pallas_call(
        body, name=name, grid=(T // tr, W // cw),
        in_specs=[pl.BlockSpec((tr, cw), lambda i, j: (i + r0, j)), pl.BlockSpec(memory_space=pl.ANY)],
        out_specs=pl.BlockSpec((tr, cw), lambda i, j: (i, c0 + j)),
        out_shape=jax.ShapeDtypeStruct(buf.shape, buf.dtype), input_output_aliases={1: 0},
        compiler_params=_cp("parallel", "parallel"))(src, buf)


def _bias_table(rel, qb, kw):
    assert qb - 1 <= MAX_REL
    ha = rel.shape[0]
    n_clipped = kw - MAX_REL
    e = jnp.concatenate([jnp.broadcast_to(rel[:, 2 * MAX_REL:], (ha, n_clipped)),
                         rel[:, 2 * MAX_REL - 1:MAX_REL - qb:-1]], axis=1)
    period = qb + kw
    e = jnp.pad(e, ((0, 0), (0, period - e.shape[1])))
    z = jnp.broadcast_to(e[:, None, :], (ha, qb, period)).reshape(ha, qb * period)
    z = z[:, :qb * (period - 1)].reshape(ha, qb, period - 1)
    i = jnp.arange(qb)[:, None]
    j = jnp.arange(kw)[None, :]
    band = (j // CHUNK >= i // CHUNK) & (j // CHUNK <= i // CHUNK + LEFT_CHUNKS)
    return jnp.where(band[None], z[:, :, qb - 1:qb - 1 + kw], NEG_INF)


def _heads_per_step(n_heads):
    return 2 if n_heads % 2 == 0 else 1


def _attn_a_fwd(q, kp, vp, btab, *, name):
    T, WA = q.shape[0], kp.shape[1]
    HA = WA // HEAD_DIM
    _, QB, KW = btab.shape
    padk = KW - QB
    G = _heads_per_step(HA)
    GW = G * HEAD_DIM

    def body(q_ref, k_ref, v_ref, b_ref, o_ref, lse_ref):
        i = pl.program_id(1)
        start = pl.multiple_of(i * QB, QB)
        kpos = start - padk + lax.broadcasted_iota(jnp.int32, (QB, KW), 1)
        for g in range(G):
            lanes = pl.ds(g * HEAD_DIM, HEAD_DIM)
            k = k_ref[pl.ds(start, KW), lanes]
            v = v_ref[pl.ds(start, KW), lanes]
            s = lax.dot_general(q_ref[:, lanes], k, NT_DIMS, preferred_element_type=F32) * ATTN_SCALE + b_ref[g]
            s = jnp.where(kpos >= 0, s, NEG_INF)
            m = jnp.max(s, axis=-1, keepdims=True)
            p = jnp.exp(s - m)
            l = jnp.sum(p, axis=-1, keepdims=True)
            o = jnp.dot(p.astype(BF16), v, preferred_element_type=F32) / l
            o_ref[:, lanes] = o.astype(BF16)
            lse_ref[:, lanes] = jnp.broadcast_to(m + jnp.log(l), (QB, HEAD_DIM))

    qspec = pl.BlockSpec((QB, GW), lambda h, i: (i, h))
    kspec = pl.BlockSpec((T + padk, GW), lambda h, i: (0, h))
    return pl.pallas_call(
        body, name=name, grid=(HA // G, T // QB),
        in_specs=[qspec, kspec, kspec, pl.BlockSpec((G, QB, KW), lambda h, i: (h, 0, 0))],
        out_specs=[qspec, qspec],
        out_shape=[jax.ShapeDtypeStruct((T, WA), BF16), jax.ShapeDtypeStruct((T, WA), F32)],
        compiler_params=_cp("parallel", "arbitrary"))(q, kp, vp, btab)


def _attn_a_bwd(q, kp, vp, btab, o, do, lse, *, name, comm=None, dq_width=None):
    T, WA = q.shape[0], kp.shape[1]
    HA = WA // HEAD_DIM
    _, QB, KW = btab.shape
    padk = KW - QB
    G = _heads_per_step(HA)
    GW = G * HEAD_DIM

    def body(q_ref, k_ref, v_ref, b_ref, o_ref, do_ref, lse_ref, dq_ref, dk_ref, dv_ref, db_ref):
        i = pl.program_id(1)

        @pl.when(i == 0)
        def _():
            dk_ref[...] = jnp.zeros_like(dk_ref)
            dv_ref[...] = jnp.zeros_like(dv_ref)
            db_ref[...] = jnp.zeros_like(db_ref)

        start = pl.multiple_of(i * QB, QB)
        kpos = start - padk + lax.broadcasted_iota(jnp.int32, (QB, KW), 1)
        for g in range(G):
            lanes = pl.ds(g * HEAD_DIM, HEAD_DIM)
            qv = q_ref[:, lanes]
            dov = do_ref[:, lanes]
            k = k_ref[pl.ds(start, KW), lanes]
            v = v_ref[pl.ds(start, KW), lanes]
            s = lax.dot_general(qv, k, NT_DIMS, preferred_element_type=F32) * ATTN_SCALE + b_ref[g]
            s = jnp.where(kpos >= 0, s, NEG_INF)
            p = jnp.exp(s - lse_ref[:, pl.ds(g * HEAD_DIM, 1)])
            dp = lax.dot_general(dov, v, NT_DIMS, preferred_element_type=F32)
            delta = jnp.sum(dov.astype(F32) * o_ref[:, lanes].astype(F32), axis=-1, keepdims=True)
            ds = p * (dp - delta)
            db_ref[g] += ds
            dsb = ds.astype(BF16)
            dq_ref[:, lanes] = (jnp.dot(dsb, k, preferred_element_type=F32) * ATTN_SCALE).astype(BF16)
            dk_ref[pl.ds(start, KW), lanes] += (
                lax.dot_general(dsb, qv, TN_DIMS, preferred_element_type=F32) * ATTN_SCALE)
            dv_ref[pl.ds(start, KW), lanes] += lax.dot_general(
                p.astype(BF16), dov, TN_DIMS, preferred_element_type=F32)

    qspec = pl.BlockSpec((QB, GW), lambda h, i: (i, h))
    kspec = pl.BlockSpec((T + padk, GW), lambda h, i: (0, h))
    bspec = pl.BlockSpec((G, QB, KW), lambda h, i: (h, 0, 0))
    return _call(
        body, [q, kp, vp, btab, o, do, lse], name=name, grid=(HA // G, T // QB),
        in_specs=[qspec, kspec, kspec, bspec, qspec, qspec, qspec],
        out_specs=[qspec, kspec, kspec, bspec],
        out_shape=[jax.ShapeDtypeStruct((T, dq_width or WA), BF16), jax.ShapeDtypeStruct((T + padk, WA), F32),
                   jax.ShapeDtypeStruct((T + padk, WA), F32), jax.ShapeDtypeStruct((HA, QB, KW), F32)],
        sem=("parallel", "arbitrary"), comm=comm)


def _relbias_fold(skew, dtab, *, name):
    HA, qb, kw = dtab.shape
    W = kw + 1

    def body(s_ref, t_ref, col_ref, tail_ref):
        row = lax.broadcasted_iota(jnp.int32, (qb, W), 0)
        col = lax.broadcasted_iota(jnp.int32, (qb, W), 1)
        col_ref[...] = jnp.sum(jnp.where(row + col < kw, s_ref[...], 0.0), axis=0, keepdims=True)
        i = lax.broadcasted_iota(jnp.int32, (qb, kw), 0)
        j = lax.broadcasted_iota(jnp.int32, (qb, kw), 1)
        tail = jnp.sum(jnp.where((kw - qb) + i - j >= MAX_REL, t_ref[...], 0.0))
        tail_ref[...] = jnp.full((1, LANE), tail, F32)

    return pl.pallas_call(
        body, name=name, grid=(HA,),
        in_specs=[pl.BlockSpec((None, qb, W), lambda h: (h, 0, 0)), pl.BlockSpec((None, qb, kw), lambda h: (h, 0, 0))],
        out_specs=[pl.BlockSpec((None, 1, W), lambda h: (h, 0, 0)), pl.BlockSpec((None, 1, LANE), lambda h: (h, 0, 0))],
        out_shape=[jax.ShapeDtypeStruct((HA, 1, W), F32), jax.ShapeDtypeStruct((HA, 1, LANE), F32)],
        compiler_params=_cp("parallel"))(skew, dtab)


CUM_BLOCK = 128


def _fgate_fwd(f, bf, *, name):
    T = f.shape[0]
    nb = T // CUM_BLOCK

    def body(f_ref, b_ref, cum_ref):
        r = lax.broadcasted_iota(jnp.int32, (CUM_BLOCK, CUM_BLOCK), 0)
        c = lax.broadcasted_iota(jnp.int32, (CUM_BLOCK, CUM_BLOCK), 1)
        tri = (c <= r).astype(F32)

        def step(b, carry):
            r0 = pl.multiple_of(b * CUM_BLOCK, CUM_BLOCK)
            z = f_ref[pl.ds(r0, CUM_BLOCK), :] + b_ref[...]
            lf = jnp.minimum(z, 0.0) - jnp.log(1.0 + jnp.exp(-jnp.abs(z)))
            cs = jnp.dot(tri, lf, precision=lax.Precision.HIGHEST, preferred_element_type=F32) + carry
            cum_ref[pl.ds(r0, CUM_BLOCK), :] = cs
            return carry + jnp.sum(lf, axis=0, keepdims=True)

        lax.fori_loop(0, nb, step, jnp.zeros((1, LANE), F32))

    return pl.pallas_call(
        body, name=name, out_shape=jax.ShapeDtypeStruct((T, LANE), F32),
        compiler_params=_cp())(f, bf)


def _fgate_bwd(dcq, dck, f, bf, *, name):
    T = f.shape[0]
    nb = T // CUM_BLOCK

    def body(dq_ref, dk_ref, f_ref, b_ref, df_ref, dbf_ref):
        r = lax.broadcasted_iota(jnp.int32, (CUM_BLOCK, CUM_BLOCK), 0)
        c = lax.broadcasted_iota(jnp.int32, (CUM_BLOCK, CUM_BLOCK), 1)
        tri = (c >= r).astype(F32)

        def step(n, carry):
            tail, dbf = carry
            r0 = pl.multiple_of((nb - 1 - n) * CUM_BLOCK, CUM_BLOCK)
            dc = dq_ref[pl.ds(r0, CUM_BLOCK), :] + dk_ref[pl.ds(r0, CUM_BLOCK), :]
            ss = jnp.dot(tri, dc, precision=lax.Precision.HIGHEST, preferred_element_type=F32) + tail
            z = f_ref[pl.ds(r0, CUM_BLOCK), :] + b_ref[...]
            df = ss * _sigmoid(-z)
            df_ref[pl.ds(r0, CUM_BLOCK), :] = df
            return tail + jnp.sum(dc, axis=0, keepdims=True), dbf + jnp.sum(df, axis=0, keepdims=True)

        zero = jnp.zeros((1, LANE), F32)
        _, dbf = lax.fori_loop(0, nb, step, (zero, zero))
        dbf_ref[...] = dbf

    return pl.pallas_call(
        body, name=name,
        out_shape=[jax.ShapeDtypeStruct((T, LANE), F32), jax.ShapeDtypeStruct((1, LANE), F32)],
        compiler_params=_cp())(dcq, dck, f, bf)


def _fox_logits(qv, kk, cq, ck, diagonal):
    s = lax.dot_general(qv, kk, NT_DIMS, preferred_element_type=F32) * ATTN_SCALE + cq - ck
    if diagonal:
        row = lax.broadcasted_iota(jnp.int32, s.shape, 0)
        col = lax.broadcasted_iota(jnp.int32, s.shape, 1)
        s = jnp.where(col <= row, s, NEG_INF)
    return s


def _fox_heads_per_step(n_heads, cols):
    return 2 if n_heads % 2 == 0 and all(c % 2 == 0 for c in cols) else 1


def _fox_fwd(qkv, cols, cum, cum_t, *, name, comm=None):
    T = qkv.shape[0]
    HB, _, KB = cum_t.shape
    WB = HB * HEAD_DIM
    QB = KB
    G = _fox_heads_per_step(HB, cols)
    GW = G * HEAD_DIM
    qc, kc, vc = (c // G for c in cols)

    def body(q_ref, k_ref, v_ref, c_ref, ct_ref, o_ref, lse_ref):
        h = pl.program_id(0)
        i = pl.program_id(1)
        lane = lax.broadcasted_iota(jnp.int32, (QB, LANE), 1)
        heads = [pl.ds(g * HEAD_DIM, HEAD_DIM) for g in range(G)]
        qs = [q_ref[:, lanes] for lanes in heads]
        cqs = [jnp.sum(jnp.where(lane == h * G + g, c_ref[...], 0.0), axis=-1, keepdims=True) for g in range(G)]

        def block(kb, carry, diagonal):
            k0 = pl.multiple_of(kb * KB, KB)
            out = []
            for g, (m, l, acc) in enumerate(carry):
                s = _fox_logits(qs[g], k_ref[pl.ds(k0, KB), heads[g]], cqs[g], ct_ref[g, pl.ds(kb, 1), :], diagonal)
                m_new = jnp.maximum(m, jnp.max(s, axis=-1, keepdims=True))
                alpha = jnp.exp(m - m_new)
                p = jnp.exp(s - m_new)
                l = alpha * l + jnp.sum(p, axis=-1, keepdims=True)
                acc = alpha * acc + jnp.dot(p.astype(BF16), v_ref[pl.ds(k0, KB), heads[g]],
                                            preferred_element_type=F32)
                out.append((m_new, l, acc))
            return tuple(out)

        start = tuple((jnp.full((QB, 1), NEG_INF, F32), jnp.zeros((QB, 1), F32), jnp.zeros((QB, HEAD_DIM), F32))
                      for _ in range(G))
        past = lax.fori_loop(0, i, lambda kb, carry: block(kb, carry, False), start)
        for g, (m, l, acc) in enumerate(block(i, past, True)):
            o_ref[:, heads[g]] = (acc / l).astype(BF16)
            lse_ref[:, heads[g]] = jnp.broadcast_to(m + jnp.log(l), (QB, HEAD_DIM))

    ospec = pl.BlockSpec((QB, GW), lambda h, i: (i, h))
    return _call(
        body, [qkv, qkv, qkv, cum, cum_t], name=name, grid=(HB // G, T // QB),
        in_specs=[pl.BlockSpec((QB, GW), lambda h, i: (i, qc + h)),
                  pl.BlockSpec((T, GW), lambda h, i: (0, kc + h)),
                  pl.BlockSpec((T, GW), lambda h, i: (0, vc + h)),
                  pl.BlockSpec((QB, LANE), lambda h, i: (i, 0)),
                  pl.BlockSpec((G, T // KB, KB), lambda h, i: (h, 0, 0))],
        out_specs=[ospec, ospec],
        out_shape=[jax.ShapeDtypeStruct((T, WB), BF16), jax.ShapeDtypeStruct((T, WB), F32)],
        sem=("parallel", "arbitrary"), comm=comm)


def _fox_bwd(qkv, cols, cum, cum_t, o, do, lse, *, name, comm=None, dq_into=None):
    T = qkv.shape[0]
    HB, _, KB = cum_t.shape
    WB = HB * HEAD_DIM
    QB = KB
    G = _fox_heads_per_step(HB, cols + ((dq_into[1],) if dq_into is not None else ()))
    GW = G * HEAD_DIM
    qc, kc, vc = (c // G for c in cols)

    def body(*refs):
        q_ref, k_ref, v_ref, c_ref, ct_ref, o_ref, do_ref, lse_ref = refs[:8]
        dq_ref, dk_ref, dv_ref, dcq_ref, dck_ref = refs[-5:]
        h = pl.program_id(0)
        i = pl.program_id(1)

        @pl.when(i == 0)
        def _():
            dk_ref[...] = jnp.zeros_like(dk_ref)
            dv_ref[...] = jnp.zeros_like(dv_ref)
            dck_ref[...] = jnp.zeros_like(dck_ref)

        lane = lax.broadcasted_iota(jnp.int32, (QB, LANE), 1)
        heads = [pl.ds(g * HEAD_DIM, HEAD_DIM) for g in range(G)]
        qs = [q_ref[:, lanes] for lanes in heads]
        dos = [do_ref[:, lanes] for lanes in heads]
        lses = [lse_ref[:, pl.ds(g * HEAD_DIM, 1)] for g in range(G)]
        deltas = [jnp.sum(dos[g].astype(F32) * o_ref[:, heads[g]].astype(F32), axis=-1, keepdims=True)
                  for g in range(G)]
        cqs = [jnp.sum(jnp.where(lane == h * G + g, c_ref[...], 0.0), axis=-1, keepdims=True) for g in range(G)]

        def block(kb, carry, diagonal):
            k0 = pl.multiple_of(kb * KB, KB)
            out = []
            for g, (dq, dcq) in enumerate(carry):
                kk = k_ref[pl.ds(k0, KB), heads[g]]
                s = _fox_logits(qs[g], kk, cqs[g], ct_ref[g, pl.ds(kb, 1), :], diagonal)
                p = jnp.exp(s - lses[g])
                dp = lax.dot_general(dos[g], v_ref[pl.ds(k0, KB), heads[g]], NT_DIMS, preferred_element_type=F32)
                ds = p * (dp - deltas[g])
                dsb = ds.astype(BF16)
                dk_ref[pl.ds(k0, KB), heads[g]] += (
                    lax.dot_general(dsb, qs[g], TN_DIMS, preferred_element_type=F32) * ATTN_SCALE)
                dv_ref[pl.ds(k0, KB), heads[g]] += lax.dot_general(
                    p.astype(BF16), dos[g], TN_DIMS, preferred_element_type=F32)
                dck_ref[g, pl.ds(kb, 1), :] -= jnp.sum(ds, axis=0, keepdims=True)
                out.append((dq + jnp.dot(dsb, kk, preferred_element_type=F32),
                            dcq + jnp.sum(ds, axis=-1, keepdims=True)))
            return tuple(out)

        start = tuple((jnp.zeros((QB, HEAD_DIM), F32), jnp.zeros((QB, 1), F32)) for _ in range(G))
        past = lax.fori_loop(0, i, lambda kb, carry: block(kb, carry, False), start)
        for g, (dq, dcq) in enumerate(block(i, past, True)):
            dq_ref[:, heads[g]] = (dq * ATTN_SCALE).astype(BF16)
            dcq_ref[:, heads[g]] = jnp.broadcast_to(dcq, (QB, HEAD_DIM))

    qspec = pl.BlockSpec((QB, GW), lambda h, i: (i, h))
    kspec = pl.BlockSpec((T, GW), lambda h, i: (0, h))
    cspec = pl.BlockSpec((QB, LANE), lambda h, i: (i, 0))
    tspec = pl.BlockSpec((G, T // KB, KB), lambda h, i: (h, 0, 0))
    wide = jax.ShapeDtypeStruct((T, WB), F32)
    args = [qkv, qkv, qkv, cum, cum_t, o, do, lse]
    in_specs = [pl.BlockSpec((QB, GW), lambda h, i: (i, qc + h)),
                pl.BlockSpec((T, GW), lambda h, i: (0, kc + h)),
                pl.BlockSpec((T, GW), lambda h, i: (0, vc + h)),
                cspec, tspec, qspec, qspec, qspec]
    dq_spec, dq_shape, aliases = qspec, jax.ShapeDtypeStruct((T, WB), BF16), None
    if dq_into is not None:
        buf, dq_col = dq_into
        args.append(buf)
        in_specs.append(pl.BlockSpec(memory_space=pl.ANY))
        dq_spec = pl.BlockSpec((QB, GW), lambda h, i: (i, dq_col // G + h))
        dq_shape, aliases = jax.ShapeDtypeStruct(buf.shape, buf.dtype), {8: 0}
    return _call(
        body, args, name=name, grid=(HB // G, T // QB), in_specs=in_specs,
        out_specs=[dq_spec, kspec, kspec, qspec, tspec],
        out_shape=[dq_shape, wide, wide, wide, jax.ShapeDtypeStruct((HB, T // KB, KB), F32)],
        sem=("parallel", "arbitrary"), comm=comm, aliases=aliases)


def _place():
    return lax.axis_index("x"), lax.axis_index("y"), lax.axis_index("c")


def _other_chips(x, y):
    return [(1 - x, y), (x, 1 - y), (1 - x, 1 - y)]


def _gather_comm(shards):
    n = len(shards)

    def copies(srcs, outs, scratch, want):
        send_sems, recv_sems, local_sems = scratch
        x, y, c = _place()
        me, sib = (x, y, c), (x, y, 1 - c)
        chip_a = (jnp.bitwise_xor(x, c), jnp.bitwise_xor(y, 1 - c))
        chip_b = (jnp.bitwise_xor(x, 1 - c), jnp.bitwise_xor(y, c))
        chip_d = (1 - x, 1 - y)

        def copy(t, k, block, to, src=None):
            rows = outs[t].at[4 * block[0] + 2 * block[1] + block[2]]
            return pltpu.make_async_remote_copy(
                src_ref=rows if src is None else src, dst_ref=rows,
                send_sem=send_sems.at[7 * t + k], recv_sem=recv_sems.at[7 * t + k],
                device_id=to, device_id_type=MESH)

        got = {}
        if "mine" in want:
            got["mine"] = [pltpu.make_async_copy(srcs[t], outs[t].at[4 * x + 2 * y + c], local_sems.at[t])
                           for t in range(n)]
        if "first" in want:
            got["first"] = [cp for t in range(n) for cp in
                            (copy(t, 0, me, sib, src=srcs[t]), copy(t, 1, me, (*chip_a, c), src=srcs[t]),
                             copy(t, 2, me, (*chip_b, c), src=srcs[t]))]
        if "from_a" in want:
            got["from_a"] = [copy(t, 1, (*chip_a, c), me) for t in range(n)]
        if "hop" in want:
            got["hop"] = [copy(t, 3, (*chip_a, c), (*chip_b, c)) for t in range(n)]
        if "from_b" in want:
            got["from_b"] = [copy(t, 2, (*chip_b, c), me) for t in range(n)]
        if "from_d" in want:
            got["from_d"] = [copy(t, 3, (*chip_d, c), me) for t in range(n)]
        for key, k, chip in (("pass_a", 4, chip_a), ("pass_b", 5, chip_b), ("pass_d", 6, chip_d)):
            if key in want:
                got[key] = [copy(t, k, (*chip, c), sib) for t in range(n)]
        if "late" in want:
            got["late"] = [cp for t in range(n) for cp in
                           (copy(t, 0, sib, me), copy(t, 4, (*chip_b, 1 - c), me),
                            copy(t, 5, (*chip_a, 1 - c), me), copy(t, 6, (*chip_d, 1 - c), me))]
        return got

    def begin(srcs, outs, scratch):
        cps = copies(srcs, outs, scratch, ("mine", "first"))
        for cp in cps["mine"] + cps["first"]:
            cp.start()

    def hop(srcs, outs, scratch):
        cps = copies(srcs, outs, scratch, ("from_a", "hop", "pass_a"))
        for landed, fwd, sibling in zip(cps["from_a"], cps["hop"], cps["pass_a"]):
            landed.wait_recv()
            fwd.start()
            sibling.start()

    def pass_on(srcs, outs, scratch):
        cps = copies(srcs, outs, scratch, ("from_b", "pass_b", "from_d", "pass_d"))
        for landed, sibling in list(zip(cps["from_b"], cps["pass_b"])) + list(zip(cps["from_d"], cps["pass_d"])):
            landed.wait_recv()
            sibling.start()

    def end(srcs, outs, scratch):
        cps = copies(srcs, outs, scratch, ("mine", "first", "hop", "pass_a", "pass_b", "pass_d", "late"))
        for cp in cps["late"]:
            cp.wait_recv()
        for cp in cps["first"] + cps["hop"] + cps["pass_a"] + cps["pass_b"] + cps["pass_d"]:
            cp.wait_send()
        for cp in cps["mine"]:
            cp.wait()

    return _Comm(
        shards, [jax.ShapeDtypeStruct((N_DEV,) + s.shape, s.dtype) for s in shards],
        [pltpu.SemaphoreType.DMA((7 * n,)), pltpu.SemaphoreType.DMA((7 * n,)), pltpu.SemaphoreType.DMA((n,))],
        begin, end, [(0.6, hop), (0.88, pass_on)])


def _sibling_comm(grads):
    n = len(grads)

    def copies(srcs, outs, scratch):
        send_sems, recv_sems = scratch
        x, y, c = _place()
        return [pltpu.make_async_remote_copy(
            src_ref=srcs[t].at[2 * p + (1 - c)], dst_ref=outs[t].at[p],
            send_sem=send_sems.at[4 * t + p], recv_sem=recv_sems.at[4 * t + p],
            device_id=(x, y, 1 - c), device_id_type=MESH) for t in range(n) for p in range(4)]

    def begin(srcs, outs, scratch):
        for cp in copies(srcs, outs, scratch):
            cp.start()

    def end(srcs, outs, scratch):
        cps = copies(srcs, outs, scratch)
        for cp in cps:
            cp.wait_recv()
        for cp in cps:
            cp.wait_send()

    return _Comm(
        grads, [jax.ShapeDtypeStruct((4,) + g.shape[1:], g.dtype) for g in grads],
        [pltpu.SemaphoreType.DMA((4 * n,)), pltpu.SemaphoreType.DMA((4 * n,))], begin, end)


def _chips_comm(sums, relations):
    n = len(sums)
    per = [list(r) for r in relations] if isinstance(relations[0], (list, tuple)) else [list(relations)] * n
    first = [sum(len(r) for r in per[:t]) for t in range(n)]
    n_copies = sum(len(r) for r in per)

    def copies(srcs, outs, scratch):
        send_sems, recv_sems = scratch
        x, y, c = _place()
        chips = _other_chips(x, y)
        return [pltpu.make_async_remote_copy(
            src_ref=srcs[t].at[2 * chips[j][0] + chips[j][1]], dst_ref=outs[t].at[s],
            send_sem=send_sems.at[first[t] + s], recv_sem=recv_sems.at[first[t] + s],
            device_id=(*chips[j], c), device_id_type=MESH) for t in range(n) for s, j in enumerate(per[t])]

    def begin(srcs, outs, scratch):
        for cp in copies(srcs, outs, scratch):
            cp.start()

    def end(srcs, outs, scratch):
        cps = copies(srcs, outs, scratch)
        for cp in cps:
            cp.wait_recv()
        for cp in cps:
            cp.wait_send()

    return _Comm(
        sums, [jax.ShapeDtypeStruct((len(r),) + s.shape[1:], s.dtype) for s, r in zip(sums, per)],
        [pltpu.SemaphoreType.DMA((n_copies,)), pltpu.SemaphoreType.DMA((n_copies,))], begin, end)


def _run_comm(comm, *, name):
    ci, co = len(comm.ins), len(comm.outs)

    def body(*refs):
        cin, cout, csc = refs[:ci], refs[ci:ci + co], refs[ci + co:]
        comm.begin(cin, cout, csc)
        for _, relay in comm.relays:
            relay(cin, cout, csc)
        comm.end(cin, cout, csc)

    return pl.pallas_call(
        body, name=name, in_specs=_any_specs(ci), out_specs=_any_specs(co), out_shape=comm.outs,
        scratch_shapes=comm.scratch)(*comm.ins)


def _chip_sum(g, r1, core, *, name, tr=256):
    _, R, C = g.shape
    tr = _tile(R, tr, 16)

    def body(c_ref, g_ref, r_ref, o_ref):
        o_ref[...] = (g_ref[...].astype(F32) + r_ref[...].astype(F32)).astype(BF16)

    return pl.pallas_call(
        body, name=name,
        grid_spec=pltpu.PrefetchScalarGridSpec(
            num_scalar_prefetch=1, grid=(4, R // tr),
            in_specs=[pl.BlockSpec((None, tr, C), lambda p, i, c: (2 * p + c[0], i, 0)),
                      pl.BlockSpec((None, tr, C), lambda p, i, c: (p, i, 0))],
            out_specs=pl.BlockSpec((None, tr, C), lambda p, i, c: (p, i, 0))),
        out_shape=jax.ShapeDtypeStruct((4, R, C), BF16),
        compiler_params=_cp("parallel", "parallel"))(core, g, r1)


def _adamw_math(w, g, m, v):
    m = ADAM_B1 * m + (1.0 - ADAM_B1) * g
    v = ADAM_B2 * v + (1.0 - ADAM_B2) * (g * g)
    m_hat = m / (1.0 - ADAM_B1 ** ADAM_STEP)
    v_hat = v / (1.0 - ADAM_B2 ** ADAM_STEP)
    delta = -ADAM_LR * (m_hat / (jnp.sqrt(v_hat) + ADAM_EPS) + ADAM_WD * w)
    return delta, m, v


def _adamw_shard(s1, others, chip, w, m, v, *, name, tr=64):
    R0, C0 = w.shape
    C = s1.shape[2]
    tr = _tile(R0, tr, 8)
    (a0, k0), (a1, k1), (a2, k2) = others

    def body(p_ref, s_ref, r0_ref, r1_ref, r2_ref, w_ref, m_ref, v_ref, g_out, d_out, m_out, v_out):
        g = s_ref[:, :C0].astype(F32)
        for r_ref in (r0_ref, r1_ref, r2_ref):
            g = g + r_ref[:, :C0].astype(F32)
        delta, mn, vn = _adamw_math(w_ref[...], g, m_ref[...], v_ref[...])
        g_out[...] = g
        d_out[...] = delta
        m_out[...] = mn
        v_out[...] = vn

    full = pl.BlockSpec((tr, C0), lambda i, p: (i, 0))
    out = jax.ShapeDtypeStruct((R0, C0), F32)
    return pl.pallas_call(
        body, name=name,
        grid_spec=pltpu.PrefetchScalarGridSpec(
            num_scalar_prefetch=1, grid=(R0 // tr,),
            in_specs=[pl.BlockSpec((None, tr, C), lambda i, p: (p[0], i, 0)),
                      pl.BlockSpec((None, tr, C), lambda i, p: (k0, i, 0)),
                      pl.BlockSpec((None, tr, C), lambda i, p: (k1, i, 0)),
                      pl.BlockSpec((None, tr, C), lambda i, p: (k2, i, 0)),
                      full, full, full],
            out_specs=[full, full, full, full]),
        out_shape=[out, out, out, out],
        compiler_params=_cp("parallel"))(chip, s1, a0, a1, a2, w, m, v)


def _all_reduce_small(part, *, name):
    SR = part.shape[0]

    def body(p_ref, o_ref, buf, send_sems, recv_sems):
        x, y, c = _place()
        me = 4 * x + 2 * y + c
        buf[0] = p_ref[...]
        copies = []
        for k in range(1, N_DEV):
            peer = (1 - x if k & 4 else x, 1 - y if k & 2 else y, 1 - c if k & 1 else c)
            copies.append(pltpu.make_async_remote_copy(
                src_ref=p_ref, dst_ref=buf.at[k], send_sem=send_sems.at[k - 1], recv_sem=recv_sems.at[k - 1],
                device_id=peer, device_id_type=MESH))
        for cp in copies:
            cp.start()
        for cp in copies:
            cp.wait_recv()
        for cp in copies:
            cp.wait_send()
        acc = buf[me]
        for d in range(1, N_DEV):
            acc = acc + buf[jnp.bitwise_xor(me, d)]
        o_ref[...] = acc

    return pl.pallas_call(
        body, name=name,
        in_specs=[pl.BlockSpec(memory_space=pltpu.VMEM)], out_specs=pl.BlockSpec(memory_space=pltpu.VMEM),
        out_shape=jax.ShapeDtypeStruct(part.shape, F32),
        scratch_shapes=[pltpu.VMEM((N_DEV, SR, LANE), F32), pltpu.SemaphoreType.DMA((N_DEV - 1,)),
                        pltpu.SemaphoreType.DMA((N_DEV - 1,))])(part)


def _adamw_small(w, g, m, v, *, name):
    def body(w_ref, g_ref, m_ref, v_ref, d_out, m_out, v_out):
        delta, mn, vn = _adamw_math(w_ref[...], g_ref[...], m_ref[...], v_ref[...])
        d_out[...] = delta
        m_out[...] = mn
        v_out[...] = vn

    out = jax.ShapeDtypeStruct(w.shape, F32)
    return pl.pallas_call(body, name=name, out_shape=[out, out, out], compiler_params=_cp())(w, g, m, v)


def _in_layout(din8, fq):
    lay = []
    for b in range(N_DEV):
        g0 = din8 * b
        nq = min(max(fq - g0, 0), din8)
        oq = g0 % LANE if nq else 0
        nqt = -(-(oq + nq) // LANE) if nq else 0
        tq0 = (g0 - oq) // LANE
        pg, tg0, ngt = oq + nq, 0, 0
        if nq < din8:
            u0 = g0 + nq - fq
            pg = nqt * LANE + u0 % LANE
            tg0, ngt = u0 // LANE, -(-(u0 % LANE + din8 - nq) // LANE)
        lay.append(dict(nq=nq, oq=oq, tq0=tq0, nqt=nqt, pg=pg, tg0=tg0, ngt=ngt, width=pg + din8 - nq))
    return lay


def _place_block(shard, lay, pin, b):
    rows, din8 = shard.shape

    def place(s, l):
        nq = l["nq"]
        parts, at = [], 0
        for start, cols in ((l["oq"], s[:, :nq]), (l["pg"], s[:, nq:])):
            if cols.shape[1]:
                parts += [jnp.zeros((rows, start - at), s.dtype), cols]
                at = start + cols.shape[1]
        parts.append(jnp.zeros((rows, pin - at), s.dtype))
        return jnp.concatenate([p for p in parts if p.shape[1]], axis=1)

    return lax.switch(b, [functools.partial(place, l=l) for l in lay], shard)


def _unplace_block(g, lay, din8, b):
    def unplace(a, l):
        nq = l["nq"]
        parts = [a[:, l["oq"]:l["oq"] + nq], a[:, l["pg"]:l["pg"] + din8 - nq]]
        return jnp.concatenate([p for p in parts if p.shape[1]], axis=1)

    return lax.switch(b, [functools.partial(unplace, l=l) for l in lay], g)


def _tile_runs(sources):
    runs = []
    for src in sources:
        if len(src) == 1 and runs and not isinstance(runs[-1], list) \
                and runs[-1][0] == src[0][0] and runs[-1][1] + runs[-1][2] == src[0][1]:
            runs[-1] = (runs[-1][0], runs[-1][1], runs[-1][2] + 1)
        elif len(src) == 1:
            runs.append((src[0][0], src[0][1], 1))
        else:
            runs.append(list(src))
    return runs


def _gather_tiles(arrays, sources):
    rows, dtype = arrays[0].shape[0], arrays[0].dtype
    pieces = []
    for run in _tile_runs(sources):
        if isinstance(run, list):
            tile = jnp.zeros((rows, LANE), dtype)
            for a, t in run:
                tile = tile + arrays[a][:, t * LANE:(t + 1) * LANE]
            pieces.append(tile)
        else:
            a, t0, n = run
            pieces.append(arrays[a][:, t0 * LANE:(t0 + n) * LANE])
    return jnp.concatenate(pieces, axis=1)


def _assemble_w_in(blocks, lay, n_qkv_tiles, n_f_tiles, n_g_tiles):
    q_src = [[] for _ in range(n_qkv_tiles + n_f_tiles)]
    g_src = [[] for _ in range(n_g_tiles)]
    for b, l in enumerate(lay):
        for j in range(l["nqt"]):
            q_src[l["tq0"] + j].append((b, j))
        for j in range(l["ngt"]):
            g_src[l["tg0"] + j].append((b, l["pg"] // LANE + j))
    arrays = [blocks[b] for b in range(N_DEV)]
    return (_gather_tiles(arrays, q_src[:n_qkv_tiles]), _gather_tiles(arrays, q_src[n_qkv_tiles:]),
            _gather_tiles(arrays, g_src))


def _scatter_g_in(g_qkv, g_f, g_gates, lay, pin):
    n_qkv_tiles = g_qkv.shape[1] // LANE
    out = []
    for l in lay:
        src = [[] for _ in range(pin // LANE)]
        for j in range(l["nqt"]):
            t = l["tq0"] + j
            src[j] = [(0, t)] if t < n_qkv_tiles else [(1, t - n_qkv_tiles)]
        for j in range(l["ngt"]):
            src[l["pg"] // LANE + j] = [(2, l["tg0"] + j)]
        out.append(_gather_tiles([g_qkv, g_f, g_gates], src))
    return jnp.stack(out)


def _pack(parts, rows):
    flat = jnp.concatenate([p.reshape(-1).astype(F32) for p in parts])
    return jnp.pad(flat, (0, rows * LANE - flat.shape[0])).reshape(rows, LANE)


def _unpack(packed, shapes):
    flat = packed.reshape(-1)
    out, at = [], 0
    for s in shapes:
        n = 1
        for d in s:
            n *= d
        out.append(flat[at:at + n].reshape(s))
        at += n
    return out


def kernel(x, g_mix, w_in, b_f, b_gate, rel_bias, w_branch_a, w_branch_b, w_out, g_ffn, w_gate_ffn, w_up_ffn, w_down_ffn, g_final, loss_target, m_g_mix, m_w_in, m_b_f, m_b_gate, m_rel_bias, m_w_branch_a, m_w_branch_b, m_w_out, m_g_ffn, m_w_gate_ffn, m_w_up_ffn, m_w_down_ffn, m_g_final, v_g_mix, v_w_in, v_b_f, v_b_gate, v_rel_bias, v_w_branch_a, v_w_branch_b, v_w_out, v_g_ffn, v_w_gate_ffn, v_w_up_ffn, v_w_down_ffn, v_g_final):
    _, T, D = x.shape
    HA, n_rel = rel_bias.shape[1], rel_bias.shape[2]
    HB = b_f.shape[1]
    WA, WB = HA * HEAD_DIM, HB * HEAD_DIM
    din8, f8, d8 = w_in.shape[2], w_gate_ffn.shape[2], D // N_DEV
    DIN = N_DEV * din8
    PF = _pad_to(f8, LANE)
    fcol = 3 * WA + 3 * WB
    lay = _in_layout(din8, fcol + HB)
    PIN = _pad_to(max(l["width"] for l in lay), LANE)
    padk = LEFT_CHUNKS * CHUNK
    QA = min(256, T)
    KW = QA + padk
    KB = min(512, T)
    x2d, tgt = x[0], loss_target[0]
    cx, cy, cc = _place()
    blk = 4 * cx + 2 * cy + cc
    core = jnp.reshape(cc, (1,)).astype(jnp.int32)
    chip = jnp.reshape(2 * cx + cy, (1,)).astype(jnp.int32)
    every = [0, 1, 2]

    s_in = _place_block(w_in[0].astype(BF16), lay, PIN, blk)
    s_a, s_b, s_out = w_branch_a[0].astype(BF16), w_branch_b[0].astype(BF16), w_out[0].astype(BF16)
    s_g = jnp.pad(w_gate_ffn[0].T.astype(BF16), ((0, PF - f8), (0, 0)))
    s_u = jnp.pad(w_up_ffn[0].T.astype(BF16), ((0, PF - f8), (0, 0)))
    s_d = jnp.pad(w_down_ffn[0].astype(BF16), ((0, PF - f8), (0, 0)))
    Win, = _run_comm(_gather_comm([s_in]), name="gather_w_in")
    Wqkv, Wf, Wgt = _assemble_w_in(Win, lay, fcol // LANE, 1, 2 * D // LANE)
    Wqkv, Wf, Wgt = Wqkv.reshape(1, D, fcol), Wf.reshape(1, D, LANE), Wgt.reshape(1, D, 2 * D)
    bf_pad = jnp.pad(b_f, ((0, 0), (0, LANE - HB)))
    fox_cols = (3 * HA, 3 * HA + HB, 3 * HA + 2 * HB)

    h = _rms_fwd(x2d, g_mix, name="rms_mix")
    qkv, (Wg,) = _mm_nn(h, Wqkv, name="proj_qkv", out_dtype=BF16, comm=_gather_comm([s_g]))
    Wg = Wg.reshape(1, N_DEV * PF, D)
    gates, (Wa, Wb, Wout) = _mm_nn(h, Wgt, name="proj_gates", out_dtype=BF16, comm=_gather_comm([s_a, s_b, s_out]))
    Wout = Wout.reshape(1, D, D)
    flog = _mm_nn(h, Wf, name="proj_forget", out_dtype=F32)

    kap = jnp.pad(qkv[:, WA:2 * WA], ((padk, 0), (0, 0)))
    vap = jnp.pad(qkv[:, 2 * WA:3 * WA], ((padk, 0), (0, 0)))
    btab = _bias_table(rel_bias[0], QA, KW)
    o_a, lse_a = _attn_a_fwd(qkv, kap, vap, btab, name="attn_a_fwd")

    cum = _fgate_fwd(flog, bf_pad, name="forget_cumsum")
    cum_t = cum[:, :HB].T.reshape(HB, T // KB, KB)
    (o_b, lse_b), (Wu,) = _fox_fwd(qkv, fox_cols, cum, cum_t, name="fox_fwd", comm=_gather_comm([s_u]))
    Wu = Wu.reshape(1, N_DEV * PF, D)

    u_a = _mm_nn(o_a, Wa, name="branch_a", out_dtype=BF16)
    u_b = _mm_nn(o_b, Wb, name="branch_b", out_dtype=BF16)
    merged = _merge_fwd(gates, b_gate, u_a, u_b, name="merge_fwd")
    x1 = _mm_nn(merged, Wout, name="out_proj", out_dtype=F32, res=x2d)
    h2 = _rms_fwd(x1, g_ffn, name="rms_ffn")
    gt, (Wd,) = _mm_nt(h2, Wg, name="ffn_gate", out_dtype=BF16, tc=D, comm=_gather_comm([s_d]))
    Wd = Wd.reshape(1, N_DEV * PF, D)
    up, hid = _mm_nt(h2, Wu, name="ffn_up", out_dtype=BF16, epi=_swiglu_fwd_epi, extras=[gt], n_out=2)
    x2 = _mm_nn(hid, Wd, name="ffn_down", out_dtype=F32, res=x1, tk=2 * PF)
    dx2, dx2b, dg_final, loss_part = _final_loss(x2, tgt, g_final.reshape(1, D), name="final_loss")

    dgt, dup = _mm_nt(dx2b, Wd, name="d_hidden", out_dtype=BF16, epi=_swiglu_bwd_epi, extras=[gt, up], n_out=2)
    gWd = _mm_tn(hid, dx2b, 1, name="gw_down").reshape(N_DEV, PF, D)
    dh2, (sWd,) = _mm_nn(dgt, Wg, name="d_h2_gate", out_dtype=F32, tk=2 * PF, comm=_sibling_comm([gWd]))
    cWd = _chip_sum(gWd, sWd, core, name="chip_sum_w_down_ffn")
    dh2, (rWd,) = _mm_nn(dup, Wu, name="d_h2_up", out_dtype=F32, res=dh2, tk=2 * PF,
                         comm=_chips_comm([cWd], every))
    gWg = _mm_tn(dgt, h2, 1, name="gw_gate").reshape(N_DEV, PF, D)
    gWu = _mm_tn(dup, h2, 1, name="gw_up").reshape(N_DEV, PF, D)
    dx1, dx1b, dg_ffn = _rms_bwd(dh2, x1, g_ffn, dx2, name="rms_ffn_bwd")
    dmer, (sWg, sWu) = _mm_nt(dx1b, Wout, name="d_merged", out_dtype=BF16, comm=_sibling_comm([gWg, gWu]))
    cWg = _chip_sum(gWg, sWg, core, name="chip_sum_w_gate_ffn")
    cWu = _chip_sum(gWu, sWu, core, name="chip_sum_w_up_ffn")
    gWout = _mm_tn(merged, dx1b, 1, name="gw_out").reshape(N_DEV, d8, D)
    dua, dub, dgates, dbg = _merge_bwd(dmer, gates, b_gate, u_a, u_b, name="merge_bwd")
    g_gates, (rWg01,) = _mm_tn(h, dgates, 1, name="gw_gates", comm=_chips_comm([cWg], [0, 1]))
    g_gates = g_gates[0]
    doa = _mm_nt(dua, Wa, name="d_o_a", out_dtype=BF16)
    dob = _mm_nt(dub, Wb, name="d_o_b", out_dtype=BF16)
    gWa = _mm_tn(o_a, dua, N_DEV, name="gw_branch_a")
    gWb = _mm_tn(o_b, dub, N_DEV, name="gw_branch_b")

    (dqkv, dkap, dvap, dtab), (sWout, sWa, sWb) = _attn_a_bwd(
        qkv, kap, vap, btab, o_a, doa, lse_a, name="attn_a_bwd", comm=_sibling_comm([gWout, gWa, gWb]),
        dq_width=fcol)
    cWout = _chip_sum(gWout, sWout, core, name="chip_sum_w_out")
    cWa = _chip_sum(gWa, sWa, core, name="chip_sum_w_branch_a")
    cWb = _chip_sum(gWb, sWb, core, name="chip_sum_w_branch_b")
    skew = jnp.pad(dtab.reshape(HA, QA * KW), ((0, 0), (0, QA))).reshape(HA, QA, KW + 1)
    colsum, tail = _relbias_fold(skew, dtab, name="relbias_fold")
    rev = colsum[:, 0, ::-1]
    d_rel = jnp.concatenate([rev[:, :2 * MAX_REL], tail[:, 0, :1]], axis=1)

    (dqkv, dkb, dvb, dcq, dck), (rWg2, rWout, rWa, rWb) = _fox_bwd(
        qkv, fox_cols, cum, cum_t, o_b, dob, lse_b, name="fox_bwd",
        comm=_chips_comm([cWg, cWout, cWa, cWb], [[2], every, every, every]), dq_into=(dqkv, fox_cols[0]))
    dcq_p = jnp.pad(dcq.reshape(T, HB, HEAD_DIM)[:, :, 0], ((0, 0), (0, LANE - HB)))
    dck_p = jnp.pad(dck.reshape(HB, T).T, ((0, 0), (0, LANE - HB)))
    dflog, dbf = _fgate_bwd(dcq_p, dck_p, flog, bf_pad, name="forget_bwd")

    dqkv = _cast_into(dqkv, dkap, padk, WA, name="dqkv_k_a")
    dqkv = _cast_into(dqkv, dvap, padk, 2 * WA, name="dqkv_v_a")
    dqkv = _cast_into(dqkv, dkb, 0, 3 * WA + WB, name="dqkv_k_b")
    dqkv = _cast_into(dqkv, dvb, 0, 3 * WA + 2 * WB, name="dqkv_v_b")
    dfb = dflog.astype(BF16)
    g_qkv, (rWu,) = _mm_tn(h, dqkv, 1, name="gw_qkv", comm=_chips_comm([cWu], every))
    gWin = _scatter_g_in(g_qkv[0], _mm_tn(h, dfb, 1, name="gw_forget")[0], g_gates, lay, PIN)
    dh, (sWin,) = _mm_nt(dfb, Wf, name="d_h_forget", out_dtype=F32, comm=_sibling_comm([gWin]))
    cWin = _chip_sum(gWin, sWin, core, name="chip_sum_w_in")
    dh, (rWin01,) = _mm_nt(dqkv, Wqkv, name="d_h_qkv", out_dtype=F32, res=dh, comm=_chips_comm([cWin], [0, 1]))
    dh, (rWin2,) = _mm_nt(dgates, Wgt, name="d_h_gates", out_dtype=F32, res=dh, comm=_chips_comm([cWin], [2]))
    dx, dg_mix = _rms_bwd(dh, x2d, g_mix, dx1, name="rms_mix_bwd", with_bf16=False)

    def slots(r):
        return [(r, 0), (r, 1), (r, 2)]

    in_parts = [_unplace_block(p, lay, din8, blk)[None] for p in
                (lax.dynamic_index_in_dim(cWin, 2 * cx + cy, keepdims=False), rWin01[0], rWin01[1], rWin2[0])]
    first = jnp.zeros((1,), jnp.int32)
    big = {}
    for nm, s1, others, which, w, m, v in [
            ("w_down_ffn", cWd, slots(rWd), chip, w_down_ffn, m_w_down_ffn, v_w_down_ffn),
            ("w_gate_ffn", cWg, [(rWg01, 0), (rWg01, 1), (rWg2, 0)], chip, w_gate_ffn, m_w_gate_ffn, v_w_gate_ffn),
            ("w_out", cWout, slots(rWout), chip, w_out, m_w_out, v_w_out),
            ("w_branch_a", cWa, slots(rWa), chip, w_branch_a, m_w_branch_a, v_w_branch_a),
            ("w_branch_b", cWb, slots(rWb), chip, w_branch_b, m_w_branch_b, v_w_branch_b),
            ("w_up_ffn", cWu, slots(rWu), chip, w_up_ffn, m_w_up_ffn, v_w_up_ffn),
            ("w_in", in_parts[0], [(p, 0) for p in in_parts[1:]], first, w_in, m_w_in, v_w_in)]:
        flipped = nm in ("w_gate_ffn", "w_up_ffn")
        w2, m2, v2 = ((a[0].T if flipped else a[0]) for a in (w, m, v))
        outs = _adamw_shard(s1, others, which, w2, m2, v2, name="adamw_" + nm)
        big[nm] = [(o.T if flipped else o)[None] for o in outs]

    small_names = ["g_mix", "b_f", "b_gate", "rel_bias", "g_ffn", "g_final"]
    small_w = [g_mix, b_f, b_gate, rel_bias, g_ffn, g_final]
    small_m = [m_g_mix, m_b_f, m_b_gate, m_rel_bias, m_g_ffn, m_g_final]
    small_v = [v_g_mix, v_b_f, v_b_gate, v_rel_bias, v_g_ffn, v_g_final]
    small_g = [dg_mix, dbf[:, :HB], dbg, d_rel[None], dg_ffn, dg_final[0]]
    shapes = [w.shape for w in small_w]
    n_small = sum(w.size for w in small_w) + 1
    rows = _pad_to(-(-n_small // LANE), 8)
    total = _all_reduce_small(_pack(small_g + [loss_part[:1, :1]], rows), name="all_reduce_small")
    d_s, m_s, v_s = _adamw_small(_pack(small_w, rows), total, _pack(small_m, rows), _pack(small_v, rows),
                                 name="adamw_small")
    g_small = dict(zip(small_names, _unpack(total, shapes)))
    d_small = dict(zip(small_names, _unpack(d_s, shapes)))
    m_small = dict(zip(small_names, _unpack(m_s, shapes)))
    v_small = dict(zip(small_names, _unpack(v_s, shapes)))
    loss = total.reshape(-1)[n_small - 1]

    order = ["g_mix", "w_in", "b_f", "b_gate", "rel_bias", "w_branch_a", "w_branch_b", "w_out", "g_ffn",
             "w_gate_ffn", "w_up_ffn", "w_down_ffn", "g_final"]

    def pick(idx, small):
        return [big[nm][idx] if nm in big else small[nm] for nm in order]

    return (loss, dx[None], *pick(0, g_small), *pick(1, d_small), *pick(2, m_small), *pick(3, v_small))
```

```python
import functools

import jax
import jax.numpy as jnp
from jax import lax
from jax.experimental import pallas as pl
from jax.experimental.pallas import tpu as pltpu

F32 = jnp.float32
BF16 = jnp.bfloat16
MESH = pl.DeviceIdType.MESH

N_DEV = 8
LANE = 128
HEAD_DIM = 128
CHUNK = 64
LEFT_CHUNKS = 8
MAX_REL = 256
RMS_EPS = 1e-6
NEG_INF = -1e30
ATTN_SCALE = HEAD_DIM ** -0.5
VMEM_LIMIT_BYTES = 56 * 1024 * 1024

ADAM_LR = 0.001
ADAM_B1 = 0.9
ADAM_B2 = 0.999
ADAM_EPS = 1e-08
ADAM_WD = 0.01
ADAM_STEP = 10

NT_DIMS = (((1,), (1,)), ((), ()))
TN_DIMS = (((0,), (0,)), ((), ()))


def _cp(*sem):
    return pltpu.CompilerParams(dimension_semantics=sem if sem else None, vmem_limit_bytes=VMEM_LIMIT_BYTES)


def _pad_to(n, mult):
    return -(-n // mult) * mult


def _tile(dim, pref, align):
    t = (min(pref, dim) // align) * align
    while t >= align:
        if dim % t == 0:
            return t
        t -= align
    return dim


def _sigmoid(z):
    return 1.0 / (1.0 + jnp.exp(-z))


class _Comm:
    def __init__(self, ins, outs, scratch, begin, end, relays=()):
        self.ins, self.outs, self.scratch = list(ins), list(outs), list(scratch)
        self.begin, self.end, self.relays = begin, end, list(relays)


def _call(body, args, *, name, grid, in_specs, out_specs, out_shape, scratch=(), sem=(), comm=None, aliases=None):
    single = not isinstance(out_shape, (list, tuple))
    out_shape = [out_shape] if single else list(out_shape)
    out_specs = [out_specs] if single else list(out_specs)
    aliases = dict(aliases or {})
    if comm is None:
        res = pl.pallas_call(
            body, name=name, grid=grid, in_specs=list(in_specs), out_specs=out_specs, out_shape=out_shape,
            scratch_shapes=list(scratch), input_output_aliases=aliases, compiler_params=_cp(*sem))(*args)
        return res[0] if single else res
    ni, no, ns = len(args), len(out_shape), len(scratch)
    ci, co = len(comm.ins), len(comm.outs)
    total = 1
    for g in grid:
        total *= g

    def hosted(*refs):
        a, cin = refs[:ni], refs[ni:ni + ci]
        o, cout = refs[ni + ci:ni + ci + no], refs[ni + ci + no:ni + ci + no + co]
        s, csc = refs[ni + ci + no + co:ni + ci + no + co + ns], refs[ni + ci + no + co + ns:]
        step = pl.program_id(0)
        for d in range(1, len(grid)):
            step = step * grid[d] + pl.program_id(d)

        @pl.when(step == 0)
        def _():
            comm.begin(cin, cout, csc)

        for fraction, relay in comm.relays:
            @pl.when(step == min(int(total * fraction), total - 1))
            def _(relay=relay):
                relay(cin, cout, csc)

        body(*a, *o, *s)

        @pl.when(step == total - 1)
        def _():
            comm.end(cin, cout, csc)

    res = pl.pallas_call(
        hosted, name=name, grid=grid, in_specs=list(in_specs) + _any_specs(ci),
        out_specs=out_specs + _any_specs(co), out_shape=out_shape + comm.outs,
        scratch_shapes=list(scratch) + comm.scratch, input_output_aliases=aliases,
        compiler_params=_cp(*(("arbitrary",) * len(grid))))(*args, *comm.ins)
    return (res[0] if single else res[:no]), res[no:]


def _any_specs(n):
    return [pl.BlockSpec(memory_space=pl.ANY)] * n


def _mm_body(dims, nk, has_res, n_ext, n_out, epi):
    def body(*refs):
        a_ref, w_ref = refs[:2]
        at = 2 + has_res
        r_ref = refs[2] if has_res else None
        e_refs, o_refs = refs[at:at + n_ext], refs[at + n_ext:at + n_ext + n_out]

        def emit(v):
            if r_ref is not None:
                v = v + r_ref[...]
            vals = (v,) if epi is None else epi(v, *[e[...] for e in e_refs])
            for o_ref, val in zip(o_refs, vals):
                o_ref[...] = val.astype(o_ref.dtype)

        if nk == 1:
            emit(lax.dot_general(a_ref[...], w_ref[...], dims, preferred_element_type=F32))
            return
        acc = refs[-1]
        k = pl.program_id(2)

        @pl.when(k == 0)
        def _():
            acc[...] = jnp.zeros_like(acc)

        acc[...] += lax.dot_general(a_ref[...], w_ref[...], dims, preferred_element_type=F32)

        @pl.when(k == nk - 1)
        def _():
            emit(acc[...])

    return body


def _mm_nn(x, w, *, name, out_dtype, res=None, tm=1024, tn=1024, tk=None, comm=None, epi=None, extras=(), n_out=1):
    M, K = x.shape
    NB, _, NW = w.shape
    tm, tn = _tile(M, tm, 16), _tile(NW, tn, LANE)
    tk = K if tk is None else _tile(K, tk, LANE)
    nj, nk = NW // tn, K // tk
    tile = pl.BlockSpec((tm, tn), lambda j, i, k: (i, j))
    in_specs = [pl.BlockSpec((tm, tk), lambda j, i, k: (i, k)),
                pl.BlockSpec((None, tk, tn), lambda j, i, k: (j // nj, k, j % nj))]
    args = [x, w] + ([res] if res is not None else []) + list(extras)
    in_specs += [tile] * (len(args) - 2)
    out = jax.ShapeDtypeStruct((M, NB * NW), out_dtype)
    return _call(
        _mm_body((((1,), (0,)), ((), ())), nk, res is not None, len(extras), n_out, epi), args,
        name=name, grid=(NB * nj, M // tm, nk), in_specs=in_specs,
        out_specs=tile if n_out == 1 else [tile] * n_out, out_shape=out if n_out == 1 else [out] * n_out,
        scratch=[pltpu.VMEM((tm, tn), F32)] if nk > 1 else [],
        sem=("parallel", "parallel", "arbitrary"), comm=comm)


def _mm_nt(dy, w, *, name, out_dtype, res=None, tm=1024, tn=1024, tc=2816, comm=None, epi=None, extras=(), n_out=1):
    M, N = dy.shape
    NB, Kw, NW = w.shape
    assert N == NB * NW
    tm, tn, tc = _tile(M, tm, 16), _tile(Kw, tn, LANE), _tile(NW, tc, LANE)
    nc = NW // tc
    nk = NB * nc
    tile = pl.BlockSpec((tm, tn), lambda i, j, k: (i, j))
    in_specs = [pl.BlockSpec((tm, tc), lambda i, j, k: (i, k)),
                pl.BlockSpec((None, tn, tc), lambda i, j, k: (k // nc, j, k % nc))]
    args = [dy, w] + ([res] if res is not None else []) + list(extras)
    in_specs += [tile] * (len(args) - 2)
    out = jax.ShapeDtypeStruct((M, Kw), out_dtype)
    return _call(
        _mm_body(NT_DIMS, nk, res is not None, len(extras), n_out, epi), args,
        name=name, grid=(M // tm, Kw // tn, nk), in_specs=in_specs,
        out_specs=tile if n_out == 1 else [tile] * n_out, out_shape=out if n_out == 1 else [out] * n_out,
        scratch=[pltpu.VMEM((tm, tn), F32)] if nk > 1 else [],
        sem=("parallel", "parallel", "arbitrary"), comm=comm)


def _mm_tn(x, dy, nb, *, name, tm=512, tn=1024, comm=None):
    M, Kw = x.shape
    NW = dy.shape[1] // nb
    tm, tn = _tile(Kw, tm, LANE), _tile(NW, tn, LANE)
    nj = NW // tn

    def body(x_ref, d_ref, o_ref):
        o_ref[...] = lax.dot_general(x_ref[...], d_ref[...], TN_DIMS, preferred_element_type=F32).astype(o_ref.dtype)

    return _call(
        body, [x, dy], name=name, grid=(nb * nj, Kw // tm),
        in_specs=[pl.BlockSpec((M, tm), lambda j, i: (0, i)),
                  pl.BlockSpec((M, tn), lambda j, i: (0, j))],
        out_specs=pl.BlockSpec((None, tm, tn), lambda j, i: (j // nj, i, j % nj)),
        out_shape=jax.ShapeDtypeStruct((nb, Kw, NW), BF16),
        sem=("parallel", "parallel"), comm=comm)


def _rms_fwd(x, g, *, name, tr=256):
    T, D = x.shape
    tr = _tile(T, tr, 16)

    def body(x_ref, g_ref, h_ref):
        xv = x_ref[...]
        r = lax.rsqrt(jnp.mean(xv * xv, axis=-1, keepdims=True) + RMS_EPS)
        h_ref[...] = ((xv * r) * g_ref[...]).astype(BF16)

    return pl.pallas_call(
        body, name=name, grid=(T // tr,),
        in_specs=[pl.BlockSpec((tr, D), lambda i: (i, 0)), pl.BlockSpec((1, D), lambda i: (0, 0))],
        out_specs=pl.BlockSpec((tr, D), lambda i: (i, 0)),
        out_shape=jax.ShapeDtypeStruct((T, D), BF16),
        compiler_params=_cp("parallel"))(x, g)


def _rms_bwd(dh, x, g, dres, *, name, tr=128, with_bf16=True):
    T, D = x.shape
    tr = _tile(T, tr, 16)

    def body(dh_ref, x_ref, g_ref, dr_ref, dx_ref, *rest):
        dg_ref = rest[-1]
        i = pl.program_id(0)
        xv = x_ref[...]
        dhv = dh_ref[...]
        r = lax.rsqrt(jnp.mean(xv * xv, axis=-1, keepdims=True) + RMS_EPS)
        a = dhv * g_ref[...]
        dot = jnp.mean(a * xv, axis=-1, keepdims=True)
        dx = dr_ref[...] + r * a - xv * (r * r * r * dot)
        dx_ref[...] = dx
        if with_bf16:
            rest[0][...] = dx.astype(BF16)
        part = jnp.sum(dhv * (xv * r), axis=0, keepdims=True)

        @pl.when(i == 0)
        def _():
            dg_ref[...] = part

        @pl.when(i > 0)
        def _():
            dg_ref[...] += part

    row = pl.BlockSpec((tr, D), lambda i: (i, 0))
    vec = pl.BlockSpec((1, D), lambda i: (0, 0))
    halves = [jax.ShapeDtypeStruct((T, D), BF16)] if with_bf16 else []
    return pl.pallas_call(
        body, name=name, grid=(T // tr,),
        in_specs=[row, row, vec, row], out_specs=[row] + [row] * len(halves) + [vec],
        out_shape=[jax.ShapeDtypeStruct((T, D), F32)] + halves + [jax.ShapeDtypeStruct((1, D), F32)],
        compiler_params=_cp("arbitrary"))(dh, x, g, dres)


def _final_loss(x2, tgt, g, *, name, tr=128):
    T, D = x2.shape
    tr = _tile(T, tr, 16)

    def body(x_ref, t_ref, g_ref, dx_ref, dxb_ref, dg_ref, loss_ref):
        i = pl.program_id(0)
        xv = x_ref[...]
        gv = g_ref[...]
        r = lax.rsqrt(jnp.mean(xv * xv, axis=-1, keepdims=True) + RMS_EPS)
        xh = xv * r
        diff = xh * gv - t_ref[...]
        lpart = 0.5 * jnp.sum(jnp.mean(diff * diff, axis=-1, keepdims=True))
        dy = diff * (1.0 / D)
        a = dy * gv
        dot = jnp.mean(a * xv, axis=-1, keepdims=True)
        dx = r * a - xv * (r * r * r * dot)
        dx_ref[...] = dx
        dxb_ref[...] = dx.astype(BF16)
        part = jnp.sum(dy * xh, axis=0, keepdims=True)
        lblk = jnp.full((8, LANE), lpart, F32)

        @pl.when(i == 0)
        def _():
            dg_ref[...] = part
            loss_ref[...] = lblk

        @pl.when(i > 0)
        def _():
            dg_ref[...] += part
            loss_ref[...] += lblk

    row = pl.BlockSpec((tr, D), lambda i: (i, 0))
    vec = pl.BlockSpec((1, D), lambda i: (0, 0))
    return pl.pallas_call(
        body, name=name, grid=(T // tr,),
        in_specs=[row, row, vec],
        out_specs=[row, row, vec, pl.BlockSpec((8, LANE), lambda i: (0, 0))],
        out_shape=[jax.ShapeDtypeStruct((T, D), F32), jax.ShapeDtypeStruct((T, D), BF16),
                   jax.ShapeDtypeStruct((1, D), F32), jax.ShapeDtypeStruct((8, LANE), F32)],
        compiler_params=_cp("arbitrary"))(x2, tgt, g)


def _merge_fwd(gates, b_gate, ua, ub, *, name, tr=256):
    T, D = ua.shape
    tr = _tile(T, tr, 16)

    def body(ga_ref, gb_ref, ba_ref, bb_ref, ua_ref, ub_ref, o_ref):
        sa = _sigmoid(ga_ref[...].astype(F32) + ba_ref[...])
        sb = _sigmoid(gb_ref[...].astype(F32) + bb_ref[...])
        o_ref[...] = (sa * ua_ref[...].astype(F32) + sb * ub_ref[...].astype(F32)).astype(BF16)

    row = pl.BlockSpec((tr, D), lambda i: (i, 0))
    return pl.pallas_call(
        body, name=name, grid=(T // tr,),
        in_specs=[row, pl.BlockSpec((tr, D), lambda i: (i, 1)),
                  pl.BlockSpec((1, D), lambda i: (0, 0)), pl.BlockSpec((1, D), lambda i: (0, 1)), row, row],
        out_specs=row, out_shape=jax.ShapeDtypeStruct((T, D), BF16),
        compiler_params=_cp("parallel"))(gates, gates, b_gate, b_gate, ua, ub)


def _merge_bwd(dm, gates, b_gate, ua, ub, *, name, tr=256):
    T, D = ua.shape
    tr = _tile(T, tr, 16)

    def body(dm_ref, ga_ref, gb_ref, ba_ref, bb_ref, ua_ref, ub_ref, dua_ref, dub_ref, dg_ref, db_ref):
        i = pl.program_id(0)
        dmv = dm_ref[...].astype(F32)
        sa = _sigmoid(ga_ref[...].astype(F32) + ba_ref[...])
        sb = _sigmoid(gb_ref[...].astype(F32) + bb_ref[...])
        dua_ref[...] = (dmv * sa).astype(BF16)
        dub_ref[...] = (dmv * sb).astype(BF16)
        dga = dmv * ua_ref[...].astype(F32) * (sa * (1.0 - sa))
        dgb = dmv * ub_ref[...].astype(F32) * (sb * (1.0 - sb))
        dg_ref[:, :D] = dga.astype(BF16)
        dg_ref[:, D:] = dgb.astype(BF16)
        pa = jnp.sum(dga, axis=0, keepdims=True)
        pb = jnp.sum(dgb, axis=0, keepdims=True)

        @pl.when(i == 0)
        def _():
            db_ref[:, :D] = pa
            db_ref[:, D:] = pb

        @pl.when(i > 0)
        def _():
            db_ref[:, :D] += pa
            db_ref[:, D:] += pb

    row = pl.BlockSpec((tr, D), lambda i: (i, 0))
    vec = pl.BlockSpec((1, D), lambda i: (0, 0))
    act = jax.ShapeDtypeStruct((T, D), BF16)
    return pl.pallas_call(
        body, name=name, grid=(T // tr,),
        in_specs=[row, row, pl.BlockSpec((tr, D), lambda i: (i, 1)),
                  vec, pl.BlockSpec((1, D), lambda i: (0, 1)), row, row],
        out_specs=[row, row, pl.BlockSpec((tr, 2 * D), lambda i: (i, 0)), pl.BlockSpec((1, 2 * D), lambda i: (0, 0))],
        out_shape=[act, act, jax.ShapeDtypeStruct((T, 2 * D), BF16), jax.ShapeDtypeStruct((1, 2 * D), F32)],
        compiler_params=_cp("arbitrary"))(dm, gates, gates, b_gate, b_gate, ua, ub)


def _swiglu_fwd_epi(up, gate):
    g = gate.astype(F32)
    up = up.astype(BF16).astype(F32)
    return up, g * _sigmoid(g) * up


def _swiglu_bwd_epi(d, gate, up):
    g = gate.astype(F32)
    s = _sigmoid(g)
    return d * up.astype(F32) * (s * (1.0 + g * (1.0 - s))), d * (g * s)


def _cast_into(buf, src, row_off, col_off, *, name, tr=256):
    T = buf.shape[0]
    W = src.shape[1]
    tr = _tile(T, tr, 16)
    cw = W
    while col_off % cw or W % cw:
        cw -= LANE
    assert row_off % tr == 0
    c0, r0 = col_off // cw, row_off // tr

    def body(s_ref, b_ref, o_ref):
        o_ref[...] = s_ref[...].astype(o_ref.dtype)

    return pl.pallas_call(
        body, name=name, grid=(T // tr, W // cw),
        in_specs=[pl.BlockSpec((tr, cw), lambda i, j: (i + r0, j)), pl.BlockSpec(memory_space=pl.ANY)],
        out_specs=pl.BlockSpec((tr, cw), lambda i, j: (i, c0 + j)),
        out_shape=jax.ShapeDtypeStruct(buf.shape, buf.dtype), input_output_aliases={1: 0},
        compiler_params=_cp("parallel", "parallel"))(src, buf)


def _bias_table(rel, qb, kw):
    assert qb - 1 <= MAX_REL
    ha = rel.shape[0]
    n_clipped = kw - MAX_REL
    e = jnp.concatenate([jnp.broadcast_to(rel[:, 2 * MAX_REL:], (ha, n_clipped)),
                         rel[:, 2 * MAX_REL - 1:MAX_REL - qb:-1]], axis=1)
    period = qb + kw
    e = jnp.pad(e, ((0, 0), (0, period - e.shape[1])))
    z = jnp.broadcast_to(e[:, None, :], (ha, qb, period)).reshape(ha, qb * period)
    z = z[:, :qb * (period - 1)].reshape(ha, qb, period - 1)
    i = jnp.arange(qb)[:, None]
    j = jnp.arange(kw)[None, :]
    band = (j // CHUNK >= i // CHUNK) & (j // CHUNK <= i // CHUNK + LEFT_CHUNKS)
    return jnp.where(band[None], z[:, :, qb - 1:qb - 1 + kw], NEG_INF)


def _heads_per_step(n_heads):
    return 2 if n_heads % 2 == 0 else 1


def _attn_a_fwd(q, kp, vp, btab, *, name):
    T, WA = q.shape[0], kp.shape[1]
    HA = WA // HEAD_DIM
    _, QB, KW = btab.shape
    padk = KW - QB
    G = _heads_per_step(HA)
    GW = G * HEAD_DIM

    def body(q_ref, k_ref, v_ref, b_ref, o_ref, lse_ref):
        i = pl.program_id(1)
        start = pl.multiple_of(i * QB, QB)
        kpos = start - padk + lax.broadcasted_iota(jnp.int32, (QB, KW), 1)
        for g in range(G):
            lanes = pl.ds(g * HEAD_DIM, HEAD_DIM)
            k = k_ref[pl.ds(start, KW), lanes]
            v = v_ref[pl.ds(start, KW), lanes]
            s = lax.dot_general(q_ref[:, lanes], k, NT_DIMS, preferred_element_type=F32) * ATTN_SCALE + b_ref[g]
            s = jnp.where(kpos >= 0, s, NEG_INF)
            m = jnp.max(s, axis=-1, keepdims=True)
            p = jnp.exp(s - m)
            l = jnp.sum(p, axis=-1, keepdims=True)
            o = jnp.dot(p.astype(BF16), v, preferred_element_type=F32) / l
            o_ref[:, lanes] = o.astype(BF16)
            lse_ref[:, lanes] = jnp.broadcast_to(m + jnp.log(l), (QB, HEAD_DIM))

    qspec = pl.BlockSpec((QB, GW), lambda h, i: (i, h))
    kspec = pl.BlockSpec((T + padk, GW), lambda h, i: (0, h))
    return pl.pallas_call(
        body, name=name, grid=(HA // G, T // QB),
        in_specs=[qspec, kspec, kspec, pl.BlockSpec((G, QB, KW), lambda h, i: (h, 0, 0))],
        out_specs=[qspec, qspec],
        out_shape=[jax.ShapeDtypeStruct((T, WA), BF16), jax.ShapeDtypeStruct((T, WA), F32)],
        compiler_params=_cp("parallel", "arbitrary"))(q, kp, vp, btab)


def _attn_a_bwd(q, kp, vp, btab, o, do, lse, *, name, comm=None, dq_width=None):
    T, WA = q.shape[0], kp.shape[1]
    HA = WA // HEAD_DIM
    _, QB, KW = btab.shape
    padk = KW - QB
    G = _heads_per_step(HA)
    GW = G * HEAD_DIM

    def body(q_ref, k_ref, v_ref, b_ref, o_ref, do_ref, lse_ref, dq_ref, dk_ref, dv_ref, db_ref):
        i = pl.program_id(1)

        @pl.when(i == 0)
        def _():
            dk_ref[...] = jnp.zeros_like(dk_ref)
            dv_ref[...] = jnp.zeros_like(dv_ref)
            db_ref[...] = jnp.zeros_like(db_ref)

        start = pl.multiple_of(i * QB, QB)
        kpos = start - padk + lax.broadcasted_iota(jnp.int32, (QB, KW), 1)
        for g in range(G):
            lanes = pl.ds(g * HEAD_DIM, HEAD_DIM)
            qv = q_ref[:, lanes]
            dov = do_ref[:, lanes]
            k = k_ref[pl.ds(start, KW), lanes]
            v = v_ref[pl.ds(start, KW), lanes]
            s = lax.dot_general(qv, k, NT_DIMS, preferred_element_type=F32) * ATTN_SCALE + b_ref[g]
            s = jnp.where(kpos >= 0, s, NEG_INF)
            p = jnp.exp(s - lse_ref[:, pl.ds(g * HEAD_DIM, 1)])
            dp = lax.dot_general(dov, v, NT_DIMS, preferred_element_type=F32)
            delta = jnp.sum(dov.astype(F32) * o_ref[:, lanes].astype(F32), axis=-1, keepdims=True)
            ds = p * (dp - delta)
            db_ref[g] += ds
            dsb = ds.astype(BF16)
            dq_ref[:, lanes] = (jnp.dot(dsb, k, preferred_element_type=F32) * ATTN_SCALE).astype(BF16)
            dk_ref[pl.ds(start, KW), lanes] += (
                lax.dot_general(dsb, qv, TN_DIMS, preferred_element_type=F32) * ATTN_SCALE)
            dv_ref[pl.ds(start, KW), lanes] += lax.dot_general(
                p.astype(BF16), dov, TN_DIMS, preferred_element_type=F32)

    qspec = pl.BlockSpec((QB, GW), lambda h, i: (i, h))
    kspec = pl.BlockSpec((T + padk, GW), lambda h, i: (0, h))
    bspec = pl.BlockSpec((G, QB, KW), lambda h, i: (h, 0, 0))
    return _call(
        body, [q, kp, vp, btab, o, do, lse], name=name, grid=(HA // G, T // QB),
        in_specs=[qspec, kspec, kspec, bspec, qspec, qspec, qspec],
        out_specs=[qspec, kspec, kspec, bspec],
        out_shape=[jax.ShapeDtypeStruct((T, dq_width or WA), BF16), jax.ShapeDtypeStruct((T + padk, WA), F32),
                   jax.ShapeDtypeStruct((T + padk, WA), F32), jax.ShapeDtypeStruct((HA, QB, KW), F32)],
        sem=("parallel", "arbitrary"), comm=comm)


def _relbias_fold(skew, dtab, *, name):
    HA, qb, kw = dtab.shape
    W = kw + 1

    def body(s_ref, t_ref, col_ref, tail_ref):
        row = lax.broadcasted_iota(jnp.int32, (qb, W), 0)
        col = lax.broadcasted_iota(jnp.int32, (qb, W), 1)
        col_ref[...] = jnp.sum(jnp.where(row + col < kw, s_ref[...], 0.0), axis=0, keepdims=True)
        i = lax.broadcasted_iota(jnp.int32, (qb, kw), 0)
        j = lax.broadcasted_iota(jnp.int32, (qb, kw), 1)
        tail = jnp.sum(jnp.where((kw - qb) + i - j >= MAX_REL, t_ref[...], 0.0))
        tail_ref[...] = jnp.full((1, LANE), tail, F32)

    return pl.pallas_call(
        body, name=name, grid=(HA,),
        in_specs=[pl.BlockSpec((None, qb, W), lambda h: (h, 0, 0)), pl.BlockSpec((None, qb, kw), lambda h: (h, 0, 0))],
        out_specs=[pl.BlockSpec((None, 1, W), lambda h: (h, 0, 0)), pl.BlockSpec((None, 1, LANE), lambda h: (h, 0, 0))],
        out_shape=[jax.ShapeDtypeStruct((HA, 1, W), F32), jax.ShapeDtypeStruct((HA, 1, LANE), F32)],
        compiler_params=_cp("parallel"))(skew, dtab)


CUM_BLOCK = 128


def _fgate_fwd(f, bf, *, name):
    T = f.shape[0]
    nb = T // CUM_BLOCK

    def body(f_ref, b_ref, cum_ref):
        r = lax.broadcasted_iota(jnp.int32, (CUM_BLOCK, CUM_BLOCK), 0)
        c = lax.broadcasted_iota(jnp.int32, (CUM_BLOCK, CUM_BLOCK), 1)
        tri = (c <= r).astype(F32)

        def step(b, carry):
            r0 = pl.multiple_of(b * CUM_BLOCK, CUM_BLOCK)
            z = f_ref[pl.ds(r0, CUM_BLOCK), :] + b_ref[...]
            lf = jnp.minimum(z, 0.0) - jnp.log(1.0 + jnp.exp(-jnp.abs(z)))
            cs = jnp.dot(tri, lf, precision=lax.Precision.HIGHEST, preferred_element_type=F32) + carry
            cum_ref[pl.ds(r0, CUM_BLOCK), :] = cs
            return carry + jnp.sum(lf, axis=0, keepdims=True)

        lax.fori_loop(0, nb, step, jnp.zeros((1, LANE), F32))

    return pl.pallas_call(
        body, name=name, out_shape=jax.ShapeDtypeStruct((T, LANE), F32),
        compiler_params=_cp())(f, bf)


def _fgate_bwd(dcq, dck, f, bf, *, name):
    T = f.shape[0]
    nb = T // CUM_BLOCK

    def body(dq_ref, dk_ref, f_ref, b_ref, df_ref, dbf_ref):
        r = lax.broadcasted_iota(jnp.int32, (CUM_BLOCK, CUM_BLOCK), 0)
        c = lax.broadcasted_iota(jnp.int32, (CUM_BLOCK, CUM_BLOCK), 1)
        tri = (c >= r).astype(F32)

        def step(n, carry):
            tail, dbf = carry
            r0 = pl.multiple_of((nb - 1 - n) * CUM_BLOCK, CUM_BLOCK)
            dc = dq_ref[pl.ds(r0, CUM_BLOCK), :] + dk_ref[pl.ds(r0, CUM_BLOCK), :]
            ss = jnp.dot(tri, dc, precision=lax.Precision.HIGHEST, preferred_element_type=F32) + tail
            z = f_ref[pl.ds(r0, CUM_BLOCK), :] + b_ref[...]
            df = ss * _sigmoid(-z)
            df_ref[pl.ds(r0, CUM_BLOCK), :] = df
            return tail + jnp.sum(dc, axis=0, keepdims=True), dbf + jnp.sum(df, axis=0, keepdims=True)

        zero = jnp.zeros((1, LANE), F32)
        _, dbf = lax.fori_loop(0, nb, step, (zero, zero))
        dbf_ref[...] = dbf

    return pl.pallas_call(
        body, name=name,
        out_shape=[jax.ShapeDtypeStruct((T, LANE), F32), jax.ShapeDtypeStruct((1, LANE), F32)],
        compiler_params=_cp())(dcq, dck, f, bf)


def _fox_logits(qv, kk, cq, ck, diagonal):
    s = lax.dot_general(qv, kk, NT_DIMS, preferred_element_type=F32) * ATTN_SCALE + cq - ck
    if diagonal:
        row = lax.broadcasted_iota(jnp.int32, s.shape, 0)
        col = lax.broadcasted_iota(jnp.int32, s.shape, 1)
        s = jnp.where(col <= row, s, NEG_INF)
    return s


def _fox_heads_per_step(n_heads, cols):
    return 2 if n_heads % 2 == 0 and all(c % 2 == 0 for c in cols) else 1


def _fox_fwd(qkv, cols, cum, cum_t, *, name, comm=None):
    T = qkv.shape[0]
    HB, _, KB = cum_t.shape
    WB = HB * HEAD_DIM
    QB = KB
    G = _fox_heads_per_step(HB, cols)
    GW = G * HEAD_DIM
    qc, kc, vc = (c // G for c in cols)

    def body(q_ref, k_ref, v_ref, c_ref, ct_ref, o_ref, lse_ref):
        h = pl.program_id(0)
        i = pl.program_id(1)
        lane = lax.broadcasted_iota(jnp.int32, (QB, LANE), 1)
        heads = [pl.ds(g * HEAD_DIM, HEAD_DIM) for g in range(G)]
        qs = [q_ref[:, lanes] for lanes in heads]
        cqs = [jnp.sum(jnp.where(lane == h * G + g, c_ref[...], 0.0), axis=-1, keepdims=True) for g in range(G)]

        def block(kb, carry, diagonal):
            k0 = pl.multiple_of(kb * KB, KB)
            out = []
            for g, (m, l, acc) in enumerate(carry):
                s = _fox_logits(qs[g], k_ref[pl.ds(k0, KB), heads[g]], cqs[g], ct_ref[g, pl.ds(kb, 1), :], diagonal)
                m_new = jnp.maximum(m, jnp.max(s, axis=-1, keepdims=True))
                alpha = jnp.exp(m - m_new)
                p = jnp.exp(s - m_new)
                l = alpha * l + jnp.sum(p, axis=-1, keepdims=True)
                acc = alpha * acc + jnp.dot(p.astype(BF16), v_ref[pl.ds(k0, KB), heads[g]],
                                            preferred_element_type=F32)
                out.append((m_new, l, acc))
            return tuple(out)

        start = tuple((jnp.full((QB, 1), NEG_INF, F32), jnp.zeros((QB, 1), F32), jnp.zeros((QB, HEAD_DIM), F32))
                      for _ in range(G))
        past = lax.fori_loop(0, i, lambda kb, carry: block(kb, carry, False), start)
        for g, (m, l, acc) in enumerate(block(i, past, True)):
            o_ref[:, heads[g]] = (acc / l).astype(BF16)
            lse_ref[:, heads[g]] = jnp.broadcast_to(m + jnp.log(l), (QB, HEAD_DIM))

    ospec = pl.BlockSpec((QB, GW), lambda h, i: (i, h))
    return _call(
        body, [qkv, qkv, qkv, cum, cum_t], name=name, grid=(HB // G, T // QB),
        in_specs=[pl.BlockSpec((QB, GW), lambda h, i: (i, qc + h)),
                  pl.BlockSpec((T, GW), lambda h, i: (0, kc + h)),
                  pl.BlockSpec((T, GW), lambda h, i: (0, vc + h)),
                  pl.BlockSpec((QB, LANE), lambda h, i: (i, 0)),
                  pl.BlockSpec((G, T // KB, KB), lambda h, i: (h, 0, 0))],
        out_specs=[ospec, ospec],
        out_shape=[jax.ShapeDtypeStruct((T, WB), BF16), jax.ShapeDtypeStruct((T, WB), F32)],
        sem=("parallel", "arbitrary"), comm=comm)


def _fox_bwd(qkv, cols, cum, cum_t, o, do, lse, *, name, comm=None, dq_into=None):
    T = qkv.shape[0]
    HB, _, KB = cum_t.shape
    WB = HB * HEAD_DIM
    QB = KB
    G = _fox_heads_per_step(HB, cols + ((dq_into[1],) if dq_into is not None else ()))
    GW = G * HEAD_DIM
    qc, kc, vc = (c // G for c in cols)

    def body(*refs):
        q_ref, k_ref, v_ref, c_ref, ct_ref, o_ref, do_ref, lse_ref = refs[:8]
        dq_ref, dk_ref, dv_ref, dcq_ref, dck_ref = refs[-5:]
        h = pl.program_id(0)
        i = pl.program_id(1)

        @pl.when(i == 0)
        def _():
            dk_ref[...] = jnp.zeros_like(dk_ref)
            dv_ref[...] = jnp.zeros_like(dv_ref)
            dck_ref[...] = jnp.zeros_like(dck_ref)

        lane = lax.broadcasted_iota(jnp.int32, (QB, LANE), 1)
        heads = [pl.ds(g * HEAD_DIM, HEAD_DIM) for g in range(G)]
        qs = [q_ref[:, lanes] for lanes in heads]
        dos = [do_ref[:, lanes] for lanes in heads]
        lses = [lse_ref[:, pl.ds(g * HEAD_DIM, 1)] for g in range(G)]
        deltas = [jnp.sum(dos[g].astype(F32) * o_ref[:, heads[g]].astype(F32), axis=-1, keepdims=True)
                  for g in range(G)]
        cqs = [jnp.sum(jnp.where(lane == h * G + g, c_ref[...], 0.0), axis=-1, keepdims=True) for g in range(G)]

        def block(kb, carry, diagonal):
            k0 = pl.multiple_of(kb * KB, KB)
            out = []
            for g, (dq, dcq) in enumerate(carry):
                kk = k_ref[pl.ds(k0, KB), heads[g]]
                s = _fox_logits(qs[g], kk, cqs[g], ct_ref[g, pl.ds(kb, 1), :], diagonal)
                p = jnp.exp(s - lses[g])
                dp = lax.dot_general(dos[g], v_ref[pl.ds(k0, KB), heads[g]], NT_DIMS, preferred_element_type=F32)
                ds = p * (dp - deltas[g])
                dsb = ds.astype(BF16)
                dk_ref[pl.ds(k0, KB), heads[g]] += (
                    lax.dot_general(dsb, qs[g], TN_DIMS, preferred_element_type=F32) * ATTN_SCALE)
                dv_ref[pl.ds(k0, KB), heads[g]] += lax.dot_general(
                    p.astype(BF16), dos[g], TN_DIMS, preferred_element_type=F32)
                dck_ref[g, pl.ds(kb, 1), :] -= jnp.sum(ds, axis=0, keepdims=True)
                out.append((dq + jnp.dot(dsb, kk, preferred_element_type=F32),
                            dcq + jnp.sum(ds, axis=-1, keepdims=True)))
            return tuple(out)

        start = tuple((jnp.zeros((QB, HEAD_DIM), F32), jnp.zeros((QB, 1), F32)) for _ in range(G))
        past = lax.fori_loop(0, i, lambda kb, carry: block(kb, carry, False), start)
        for g, (dq, dcq) in enumerate(block(i, past, True)):
            dq_ref[:, heads[g]] = (dq * ATTN_SCALE).astype(BF16)
            dcq_ref[:, heads[g]] = jnp.broadcast_to(dcq, (QB, HEAD_DIM))

    qspec = pl.BlockSpec((QB, GW), lambda h, i: (i, h))
    kspec = pl.BlockSpec((T, GW), lambda h, i: (0, h))
    cspec = pl.BlockSpec((QB, LANE), lambda h, i: (i, 0))
    tspec = pl.BlockSpec((G, T // KB, KB), lambda h, i: (h, 0, 0))
    wide = jax.ShapeDtypeStruct((T, WB), F32)
    args = [qkv, qkv, qkv, cum, cum_t, o, do, lse]
    in_specs = [pl.BlockSpec((QB, GW), lambda h, i: (i, qc + h)),
                pl.BlockSpec((T, GW), lambda h, i: (0, kc + h)),
                pl.BlockSpec((T, GW), lambda h, i: (0, vc + h)),
                cspec, tspec, qspec, qspec, qspec]
    dq_spec, dq_shape, aliases = qspec, jax.ShapeDtypeStruct((T, WB), BF16), None
    if dq_into is not None:
        buf, dq_col = dq_into
        args.append(buf)
        in_specs.append(pl.BlockSpec(memory_space=pl.ANY))
        dq_spec = pl.BlockSpec((QB, GW), lambda h, i: (i, dq_col // G + h))
        dq_shape, aliases = jax.ShapeDtypeStruct(buf.shape, buf.dtype), {8: 0}
    return _call(
        body, args, name=name, grid=(HB // G, T // QB), in_specs=in_specs,
        out_specs=[dq_spec, kspec, kspec, qspec, tspec],
        out_shape=[dq_shape, wide, wide, wide, jax.ShapeDtypeStruct((HB, T // KB, KB), F32)],
        sem=("parallel", "arbitrary"), comm=comm, aliases=aliases)


def _place():
    return lax.axis_index("x"), lax.axis_index("y"), lax.axis_index("c")


def _other_chips(x, y):
    return [(1 - x, y), (x, 1 - y), (1 - x, 1 - y)]


def _gather_comm(shards):
    n = len(shards)

    def copies(srcs, outs, scratch, want):
        send_sems, recv_sems, local_sems = scratch
        x, y, c = _place()
        me, sib = (x, y, c), (x, y, 1 - c)
        chip_a = (jnp.bitwise_xor(x, c), jnp.bitwise_xor(y, 1 - c))
        chip_b = (jnp.bitwise_xor(x, 1 - c), jnp.bitwise_xor(y, c))
        chip_d = (1 - x, 1 - y)

        def copy(t, k, block, to, src=None):
            rows = outs[t].at[4 * block[0] + 2 * block[1] + block[2]]
            return pltpu.make_async_remote_copy(
                src_ref=rows if src is None else src, dst_ref=rows,
                send_sem=send_sems.at[7 * t + k], recv_sem=recv_sems.at[7 * t + k],
                device_id=to, device_id_type=MESH)

        got = {}
        if "mine" in want:
            got["mine"] = [pltpu.make_async_copy(srcs[t], outs[t].at[4 * x + 2 * y + c], local_sems.at[t])
                           for t in range(n)]
        if "first" in want:
            got["first"] = [cp for t in range(n) for cp in
                            (copy(t, 0, me, sib, src=srcs[t]), copy(t, 1, me, (*chip_a, c), src=srcs[t]),
                             copy(t, 2, me, (*chip_b, c), src=srcs[t]))]
        if "from_a" in want:
            got["from_a"] = [copy(t, 1, (*chip_a, c), me) for t in range(n)]
        if "hop" in want:
            got["hop"] = [copy(t, 3, (*chip_a, c), (*chip_b, c)) for t in range(n)]
        if "from_b" in want:
            got["from_b"] = [copy(t, 2, (*chip_b, c), me) for t in range(n)]
        if "from_d" in want:
            got["from_d"] = [copy(t, 3, (*chip_d, c), me) for t in range(n)]
        for key, k, chip in (("pass_a", 4, chip_a), ("pass_b", 5, chip_b), ("pass_d", 6, chip_d)):
            if key in want:
                got[key] = [copy(t, k, (*chip, c), sib) for t in range(n)]
        if "late" in want:
            got["late"] = [cp for t in range(n) for cp in
                           (copy(t, 0, sib, me), copy(t, 4, (*chip_b, 1 - c), me),
                            copy(t, 5, (*chip_a, 1 - c), me), copy(t, 6, (*chip_d, 1 - c), me))]
        return got

    def begin(srcs, outs, scratch):
        cps = copies(srcs, outs, scratch, ("mine", "first"))
        for cp in cps["mine"] + cps["first"]:
            cp.start()

    def hop(srcs, outs, scratch):
        cps = copies(srcs, outs, scratch, ("from_a", "hop", "pass_a"))
        for landed, fwd, sibling in zip(cps["from_a"], cps["hop"], cps["pass_a"]):
            landed.wait_recv()
            fwd.start()
            sibling.start()

    def pass_on(srcs, outs, scratch):
        cps = copies(srcs, outs, scratch, ("from_b", "pass_b", "from_d", "pass_d"))
        for landed, sibling in list(zip(cps["from_b"], cps["pass_b"])) + list(zip(cps["from_d"], cps["pass_d"])):
            landed.wait_recv()
            sibling.start()

    def end(srcs, outs, scratch):
        cps = copies(srcs, outs, scratch, ("mine", "first", "hop", "pass_a", "pass_b", "pass_d", "late"))
        for cp in cps["late"]:
            cp.wait_recv()
        for cp in cps["first"] + cps["hop"] + cps["pass_a"] + cps["pass_b"] + cps["pass_d"]:
            cp.wait_send()
        for cp in cps["mine"]:
            cp.wait()

    return _Comm(
        shards, [jax.ShapeDtypeStruct((N_DEV,) + s.shape, s.dtype) for s in shards],
        [pltpu.SemaphoreType.DMA((7 * n,)), pltpu.SemaphoreType.DMA((7 * n,)), pltpu.SemaphoreType.DMA((n,))],
        begin, end, [(0.6, hop), (0.88, pass_on)])


def _sibling_comm(grads):
    n = len(grads)

    def copies(srcs, outs, scratch):
        send_sems, recv_sems = scratch
        x, y, c = _place()
        return [pltpu.make_async_remote_copy(
            src_ref=srcs[t].at[2 * p + (1 - c)], dst_ref=outs[t].at[p],
            send_sem=send_sems.at[4 * t + p], recv_sem=recv_sems.at[4 * t + p],
            device_id=(x, y, 1 - c), device_id_type=MESH) for t in range(n) for p in range(4)]

    def begin(srcs, outs, scratch):
        for cp in copies(srcs, outs, scratch):
            cp.start()

    def end(srcs, outs, scratch):
        cps = copies(srcs, outs, scratch)
        for cp in cps:
            cp.wait_recv()
        for cp in cps:
            cp.wait_send()

    return _Comm(
        grads, [jax.ShapeDtypeStruct((4,) + g.shape[1:], g.dtype) for g in grads],
        [pltpu.SemaphoreType.DMA((4 * n,)), pltpu.SemaphoreType.DMA((4 * n,))], begin, end)


def _chips_comm(sums, relations):
    n = len(sums)
    per = [list(r) for r in relations] if isinstance(relations[0], (list, tuple)) else [list(relations)] * n
    first = [sum(len(r) for r in per[:t]) for t in range(n)]
    n_copies = sum(len(r) for r in per)

    def copies(srcs, outs, scratch):
        send_sems, recv_sems = scratch
        x, y, c = _place()
        chips = _other_chips(x, y)
        return [pltpu.make_async_remote_copy(
            src_ref=srcs[t].at[2 * chips[j][0] + chips[j][1]], dst_ref=outs[t].at[s],
            send_sem=send_sems.at[first[t] + s], recv_sem=recv_sems.at[first[t] + s],
            device_id=(*chips[j], c), device_id_type=MESH) for t in range(n) for s, j in enumerate(per[t])]

    def begin(srcs, outs, scratch):
        for cp in copies(srcs, outs, scratch):
            cp.start()

    def end(srcs, outs, scratch):
        cps = copies(srcs, outs, scratch)
        for cp in cps:
            cp.wait_recv()
        for cp in cps:
            cp.wait_send()

    return _Comm(
        sums, [jax.ShapeDtypeStruct((len(r),) + s.shape[1:], s.dtype) for s, r in zip(sums, per)],
        [pltpu.SemaphoreType.DMA((n_copies,)), pltpu.SemaphoreType.DMA((n_copies,))], begin, end)


def _both(a, b):
    assert not a.relays and not b.relays
    ni, no, ns = len(a.ins), len(a.outs), len(a.scratch)

    def part(fa, fb):
        def run(ins, outs, scratch):
            fa(ins[:ni], outs[:no], scratch[:ns])
            fb(ins[ni:], outs[no:], scratch[ns:])
        return run

    return _Comm(a.ins + b.ins, a.outs + b.outs, a.scratch + b.scratch, part(a.begin, b.begin), part(a.end, b.end))


def _run_comm(comm, *, name):
    ci, co = len(comm.ins), len(comm.outs)

    def body(*refs):
        cin, cout, csc = refs[:ci], refs[ci:ci + co], refs[ci + co:]
        comm.begin(cin, cout, csc)
        for _, relay in comm.relays:
            relay(cin, cout, csc)
        comm.end(cin, cout, csc)

    return pl.pallas_call(
        body, name=name, in_specs=_any_specs(ci), out_specs=_any_specs(co), out_shape=comm.outs,
        scratch_shapes=comm.scratch)(*comm.ins)


def _chip_sum(g, r1, core, *, name, tr=256):
    _, R, C = g.shape
    tr = _tile(R, tr, 16)

    def body(c_ref, g_ref, r_ref, o_ref):
        o_ref[...] = (g_ref[...].astype(F32) + r_ref[...].astype(F32)).astype(BF16)

    return pl.pallas_call(
        body, name=name,
        grid_spec=pltpu.PrefetchScalarGridSpec(
            num_scalar_prefetch=1, grid=(4, R // tr),
            in_specs=[pl.BlockSpec((None, tr, C), lambda p, i, c: (2 * p + c[0], i, 0)),
                      pl.BlockSpec((None, tr, C), lambda p, i, c: (p, i, 0))],
            out_specs=pl.BlockSpec((None, tr, C), lambda p, i, c: (p, i, 0))),
        out_shape=jax.ShapeDtypeStruct((4, R, C), BF16),
        compiler_params=_cp("parallel", "parallel"))(core, g, r1)


def _adamw_math(w, g, m, v):
    m = ADAM_B1 * m + (1.0 - ADAM_B1) * g
    v = ADAM_B2 * v + (1.0 - ADAM_B2) * (g * g)
    m_hat = m / (1.0 - ADAM_B1 ** ADAM_STEP)
    v_hat = v / (1.0 - ADAM_B2 ** ADAM_STEP)
    delta = -ADAM_LR * (m_hat / (jnp.sqrt(v_hat) + ADAM_EPS) + ADAM_WD * w)
    return delta, m, v


def _adamw_shard(s1, others, chip, w, m, v, *, name, tr=64):
    R0, C0 = w.shape
    C = s1.shape[2]
    tr = _tile(R0, tr, 8)
    (a0, k0), (a1, k1), (a2, k2) = others

    def body(p_ref, s_ref, r0_ref, r1_ref, r2_ref, w_ref, m_ref, v_ref, g_out, d_out, m_out, v_out):
        g = s_ref[:, :C0].astype(F32)
        for r_ref in (r0_ref, r1_ref, r2_ref):
            g = g + r_ref[:, :C0].astype(F32)
        delta, mn, vn = _adamw_math(w_ref[...], g, m_ref[...], v_ref[...])
        g_out[...] = g
        d_out[...] = delta
        m_out[...] = mn
        v_out[...] = vn

    full = pl.BlockSpec((tr, C0), lambda i, p: (i, 0))
    out = jax.ShapeDtypeStruct((R0, C0), F32)
    return pl.pallas_call(
        body, name=name,
        grid_spec=pltpu.PrefetchScalarGridSpec(
            num_scalar_prefetch=1, grid=(R0 // tr,),
            in_specs=[pl.BlockSpec((None, tr, C), lambda i, p: (p[0], i, 0)),
                      pl.BlockSpec((None, tr, C), lambda i, p: (k0, i, 0)),
                      pl.BlockSpec((None, tr, C), lambda i, p: (k1, i, 0)),
                      pl.BlockSpec((None, tr, C), lambda i, p: (k2, i, 0)),
                      full, full, full],
            out_specs=[full, full, full, full]),
        out_shape=[out, out, out, out],
        compiler_params=_cp("parallel"))(chip, s1, a0, a1, a2, w, m, v)


def _all_reduce_small(part, *, name):
    SR = part.shape[0]

    def body(p_ref, o_ref, buf, send_sems, recv_sems):
        x, y, c = _place()
        me = 4 * x + 2 * y + c
        buf[0] = p_ref[...]
        copies = []
        for k in range(1, N_DEV):
            peer = (1 - x if k & 4 else x, 1 - y if k & 2 else y, 1 - c if k & 1 else c)
            copies.append(pltpu.make_async_remote_copy(
                src_ref=p_ref, dst_ref=buf.at[k], send_sem=send_sems.at[k - 1], recv_sem=recv_sems.at[k - 1],
                device_id=peer, device_id_type=MESH))
        for cp in copies:
            cp.start()
        for cp in copies:
            cp.wait_recv()
        for cp in copies:
            cp.wait_send()
        acc = buf[me]
        for d in range(1, N_DEV):
            acc = acc + buf[jnp.bitwise_xor(me, d)]
        o_ref[...] = acc

    return pl.pallas_call(
        body, name=name,
        in_specs=[pl.BlockSpec(memory_space=pltpu.VMEM)], out_specs=pl.BlockSpec(memory_space=pltpu.VMEM),
        out_shape=jax.ShapeDtypeStruct(part.shape, F32),
        scratch_shapes=[pltpu.VMEM((N_DEV, SR, LANE), F32), pltpu.SemaphoreType.DMA((N_DEV - 1,)),
                        pltpu.SemaphoreType.DMA((N_DEV - 1,))])(part)


def _adamw_small(w, g, m, v, *, name):
    def body(w_ref, g_ref, m_ref, v_ref, d_out, m_out, v_out):
        delta, mn, vn = _adamw_math(w_ref[...], g_ref[...], m_ref[...], v_ref[...])
        d_out[...] = delta
        m_out[...] = mn
        v_out[...] = vn

    out = jax.ShapeDtypeStruct(w.shape, F32)
    return pl.pallas_call(body, name=name, out_shape=[out, out, out], compiler_params=_cp())(w, g, m, v)


def _in_layout(din8, fq):
    lay = []
    for b in range(N_DEV):
        g0 = din8 * b
        nq = min(max(fq - g0, 0), din8)
        oq = g0 % LANE if nq else 0
        nqt = -(-(oq + nq) // LANE) if nq else 0
        tq0 = (g0 - oq) // LANE
        pg, tg0, ngt = oq + nq, 0, 0
        if nq < din8:
            u0 = g0 + nq - fq
            pg = nqt * LANE + u0 % LANE
            tg0, ngt = u0 // LANE, -(-(u0 % LANE + din8 - nq) // LANE)
        lay.append(dict(nq=nq, oq=oq, tq0=tq0, nqt=nqt, pg=pg, tg0=tg0, ngt=ngt, width=pg + din8 - nq))
    return lay


def _place_block(shard, lay, pin, b):
    rows, din8 = shard.shape

    def place(s, l):
        nq = l["nq"]
        parts, at = [], 0
        for start, cols in ((l["oq"], s[:, :nq]), (l["pg"], s[:, nq:])):
            if cols.shape[1]:
                parts += [jnp.zeros((rows, start - at), s.dtype), cols]
                at = start + cols.shape[1]
        parts.append(jnp.zeros((rows, pin - at), s.dtype))
        return jnp.concatenate([p for p in parts if p.shape[1]], axis=1)

    return lax.switch(b, [functools.partial(place, l=l) for l in lay], shard)


def _unplace_block(g, lay, din8, b):
    def unplace(a, l):
        nq = l["nq"]
        parts = [a[:, l["oq"]:l["oq"] + nq], a[:, l["pg"]:l["pg"] + din8 - nq]]
        return jnp.concatenate([p for p in parts if p.shape[1]], axis=1)

    return lax.switch(b, [functools.partial(unplace, l=l) for l in lay], g)


def _tile_runs(sources):
    runs = []
    for src in sources:
        if len(src) == 1 and runs and not isinstance(runs[-1], list) \
                and runs[-1][0] == src[0][0] and runs[-1][1] + runs[-1][2] == src[0][1]:
            runs[-1] = (runs[-1][0], runs[-1][1], runs[-1][2] + 1)
        elif len(src) == 1:
            runs.append((src[0][0], src[0][1], 1))
        else:
            runs.append(list(src))
    return runs


def _gather_tiles(arrays, sources):
    rows, dtype = arrays[0].shape[0], arrays[0].dtype
    pieces = []
    for run in _tile_runs(sources):
        if isinstance(run, list):
            tile = jnp.zeros((rows, LANE), dtype)
            for a, t in run:
                tile = tile + arrays[a][:, t * LANE:(t + 1) * LANE]
            pieces.append(tile)
        else:
            a, t0, n = run
            pieces.append(arrays[a][:, t0 * LANE:(t0 + n) * LANE])
    return jnp.concatenate(pieces, axis=1)


def _assemble_w_in(blocks, lay, n_qkv_tiles, n_f_tiles, n_g_tiles):
    q_src = [[] for _ in range(n_qkv_tiles + n_f_tiles)]
    g_src = [[] for _ in range(n_g_tiles)]
    for b, l in enumerate(lay):
        for j in range(l["nqt"]):
            q_src[l["tq0"] + j].append((b, j))
        for j in range(l["ngt"]):
            g_src[l["tg0"] + j].append((b, l["pg"] // LANE + j))
    arrays = [blocks[b] for b in range(N_DEV)]
    return (_gather_tiles(arrays, q_src[:n_qkv_tiles]), _gather_tiles(arrays, q_src[n_qkv_tiles:]),
            _gather_tiles(arrays, g_src))


def _scatter_g_in(g_qkv, g_f, g_gates, lay, pin):
    n_qkv_tiles = g_qkv.shape[1] // LANE
    out = []
    for l in lay:
        src = [[] for _ in range(pin // LANE)]
        for j in range(l["nqt"]):
            t = l["tq0"] + j
            src[j] = [(0, t)] if t < n_qkv_tiles else [(1, t - n_qkv_tiles)]
        for j in range(l["ngt"]):
            src[l["pg"] // LANE + j] = [(2, l["tg0"] + j)]
        out.append(_gather_tiles([g_qkv, g_f, g_gates], src))
    return jnp.stack(out)


def _pack(parts, rows):
    flat = jnp.concatenate([p.reshape(-1).astype(F32) for p in parts])
    return jnp.pad(flat, (0, rows * LANE - flat.shape[0])).reshape(rows, LANE)


def _unpack(packed, shapes):
    flat = packed.reshape(-1)
    out, at = [], 0
    for s in shapes:
        n = 1
        for d in s:
            n *= d
        out.append(flat[at:at + n].reshape(s))
        at += n
    return out


def kernel(x, g_mix, w_in, b_f, b_gate, rel_bias, w_branch_a, w_branch_b, w_out, g_ffn, w_gate_ffn, w_up_ffn, w_down_ffn, g_final, loss_target, m_g_mix, m_w_in, m_b_f, m_b_gate, m_rel_bias, m_w_branch_a, m_w_branch_b, m_w_out, m_g_ffn, m_w_gate_ffn, m_w_up_ffn, m_w_down_ffn, m_g_final, v_g_mix, v_w_in, v_b_f, v_b_gate, v_rel_bias, v_w_branch_a, v_w_branch_b, v_w_out, v_g_ffn, v_w_gate_ffn, v_w_up_ffn, v_w_down_ffn, v_g_final):
    _, T, D = x.shape
    HA, n_rel = rel_bias.shape[1], rel_bias.shape[2]
    HB = b_f.shape[1]
    WA, WB = HA * HEAD_DIM, HB * HEAD_DIM
    din8, f8, d8 = w_in.shape[2], w_gate_ffn.shape[2], D // N_DEV
    DIN = N_DEV * din8
    PF = _pad_to(f8, LANE)
    fcol = 3 * WA + 3 * WB
    lay = _in_layout(din8, fcol + HB)
    PIN = _pad_to(max(l["width"] for l in lay), LANE)
    padk = LEFT_CHUNKS * CHUNK
    QA = min(256, T)
    KW = QA + padk
    KB = min(512, T)
    x2d, tgt = x[0], loss_target[0]
    cx, cy, cc = _place()
    blk = 4 * cx + 2 * cy + cc
    core = jnp.reshape(cc, (1,)).astype(jnp.int32)
    chip = jnp.reshape(2 * cx + cy, (1,)).astype(jnp.int32)
    every = [0, 1, 2]

    s_in = _place_block(w_in[0].astype(BF16), lay, PIN, blk)
    s_a, s_b, s_out = w_branch_a[0].astype(BF16), w_branch_b[0].astype(BF16), w_out[0].astype(BF16)
    s_g = jnp.pad(w_gate_ffn[0].T.astype(BF16), ((0, PF - f8), (0, 0)))
    s_u = jnp.pad(w_up_ffn[0].T.astype(BF16), ((0, PF - f8), (0, 0)))
    s_d = jnp.pad(w_down_ffn[0].astype(BF16), ((0, PF - f8), (0, 0)))
    Win, = _run_comm(_gather_comm([s_in]), name="gather_w_in")
    Wqkv, Wf, Wgt = _assemble_w_in(Win, lay, fcol // LANE, 1, 2 * D // LANE)
    Wqkv, Wf, Wgt = Wqkv.reshape(1, D, fcol), Wf.reshape(1, D, LANE), Wgt.reshape(1, D, 2 * D)
    bf_pad = jnp.pad(b_f, ((0, 0), (0, LANE - HB)))
    fox_cols = (3 * HA, 3 * HA + HB, 3 * HA + 2 * HB)

    h = _rms_fwd(x2d, g_mix, name="rms_mix")
    qkv, (Wg,) = _mm_nn(h, Wqkv, name="proj_qkv", out_dtype=BF16, comm=_gather_comm([s_g]))
    Wg = Wg.reshape(1, N_DEV * PF, D)
    gates, (Wa, Wb, Wout) = _mm_nn(h, Wgt, name="proj_gates", out_dtype=BF16, comm=_gather_comm([s_a, s_b, s_out]))
    Wout = Wout.reshape(1, D, D)
    flog = _mm_nn(h, Wf, name="proj_forget", out_dtype=F32)

    kap = jnp.pad(qkv[:, WA:2 * WA], ((padk, 0), (0, 0)))
    vap = jnp.pad(qkv[:, 2 * WA:3 * WA], ((padk, 0), (0, 0)))
    btab = _bias_table(rel_bias[0], QA, KW)
    o_a, lse_a = _attn_a_fwd(qkv, kap, vap, btab, name="attn_a_fwd")

    cum = _fgate_fwd(flog, bf_pad, name="forget_cumsum")
    cum_t = cum[:, :HB].T.reshape(HB, T // KB, KB)
    (o_b, lse_b), (Wu,) = _fox_fwd(qkv, fox_cols, cum, cum_t, name="fox_fwd", comm=_gather_comm([s_u]))
    Wu = Wu.reshape(1, N_DEV * PF, D)

    u_a = _mm_nn(o_a, Wa, name="branch_a", out_dtype=BF16)
    u_b = _mm_nn(o_b, Wb, name="branch_b", out_dtype=BF16)
    merged = _merge_fwd(gates, b_gate, u_a, u_b, name="merge_fwd")
    x1 = _mm_nn(merged, Wout, name="out_proj", out_dtype=F32, res=x2d)
    h2 = _rms_fwd(x1, g_ffn, name="rms_ffn")
    gt, (Wd,) = _mm_nt(h2, Wg, name="ffn_gate", out_dtype=BF16, tc=D, comm=_gather_comm([s_d]))
    Wd = Wd.reshape(1, N_DEV * PF, D)
    up, hid = _mm_nt(h2, Wu, name="ffn_up", out_dtype=BF16, epi=_swiglu_fwd_epi, extras=[gt], n_out=2)
    x2 = _mm_nn(hid, Wd, name="ffn_down", out_dtype=F32, res=x1, tk=2 * PF)
    dx2, dx2b, dg_final, loss_part = _final_loss(x2, tgt, g_final.reshape(1, D), name="final_loss")

    dgt, dup = _mm_nt(dx2b, Wd, name="d_hidden", out_dtype=BF16, epi=_swiglu_bwd_epi, extras=[gt, up], n_out=2)
    gWd = _mm_tn(hid, dx2b, 1, name="gw_down").reshape(N_DEV, PF, D)
    dh2, (sWd,) = _mm_nn(dgt, Wg, name="d_h2_gate", out_dtype=F32, tk=2 * PF, comm=_sibling_comm([gWd]))
    cWd = _chip_sum(gWd, sWd, core, name="chip_sum_w_down_ffn")
    dh2, (rWd,) = _mm_nn(dup, Wu, name="d_h2_up", out_dtype=F32, res=dh2, tk=2 * PF,
                         comm=_chips_comm([cWd], every))
    gWg = _mm_tn(dgt, h2, 1, name="gw_gate").reshape(N_DEV, PF, D)
    gWu = _mm_tn(dup, h2, 1, name="gw_up").reshape(N_DEV, PF, D)
    dx1, dx1b, dg_ffn = _rms_bwd(dh2, x1, g_ffn, dx2, name="rms_ffn_bwd")
    dmer, (sWg, sWu) = _mm_nt(dx1b, Wout, name="d_merged", out_dtype=BF16, comm=_sibling_comm([gWg, gWu]))
    cWg = _chip_sum(gWg, sWg, core, name="chip_sum_w_gate_ffn")
    cWu = _chip_sum(gWu, sWu, core, name="chip_sum_w_up_ffn")
    gWout = _mm_tn(merged, dx1b, 1, name="gw_out").reshape(N_DEV, d8, D)
    dua, dub, dgates, dbg = _merge_bwd(dmer, gates, b_gate, u_a, u_b, name="merge_bwd")
    g_gates, (rWg01,) = _mm_tn(h, dgates, 1, name="gw_gates", comm=_chips_comm([cWg], [0, 1]))
    g_gates = g_gates[0]
    doa = _mm_nt(dua, Wa, name="d_o_a", out_dtype=BF16)
    dob = _mm_nt(dub, Wb, name="d_o_b", out_dtype=BF16)
    gWa = _mm_tn(o_a, dua, N_DEV, name="gw_branch_a")
    gWb = _mm_tn(o_b, dub, N_DEV, name="gw_branch_b")

    (dqkv, dkap, dvap, dtab), (sWout, sWa, sWb, rWu01) = _attn_a_bwd(
        qkv, kap, vap, btab, o_a, doa, lse_a, name="attn_a_bwd",
        comm=_both(_sibling_comm([gWout, gWa, gWb]), _chips_comm([cWu], [0, 1])),
        dq_width=fcol)
    cWout = _chip_sum(gWout, sWout, core, name="chip_sum_w_out")
    cWa = _chip_sum(gWa, sWa, core, name="chip_sum_w_branch_a")
    cWb = _chip_sum(gWb, sWb, core, name="chip_sum_w_branch_b")
    skew = jnp.pad(dtab.reshape(HA, QA * KW), ((0, 0), (0, QA))).reshape(HA, QA, KW + 1)
    colsum, tail = _relbias_fold(skew, dtab, name="relbias_fold")
    rev = colsum[:, 0, ::-1]
    d_rel = jnp.concatenate([rev[:, :2 * MAX_REL], tail[:, 0, :1]], axis=1)

    (dqkv, dkb, dvb, dcq, dck), (rWg2, rWout, rWa, rWb) = _fox_bwd(
        qkv, fox_cols, cum, cum_t, o_b, dob, lse_b, name="fox_bwd",
        comm=_chips_comm([cWg, cWout, cWa, cWb], [[2], every, every, every]), dq_into=(dqkv, fox_cols[0]))
    dcq_p = jnp.pad(dcq.reshape(T, HB, HEAD_DIM)[:, :, 0], ((0, 0), (0, LANE - HB)))
    dck_p = jnp.pad(dck.reshape(HB, T).T, ((0, 0), (0, LANE - HB)))
    dflog, dbf = _fgate_bwd(dcq_p, dck_p, flog, bf_pad, name="forget_bwd")

    dqkv = _cast_into(dqkv, dkap, padk, WA, name="dqkv_k_a")
    dqkv = _cast_into(dqkv, dvap, padk, 2 * WA, name="dqkv_v_a")
    dqkv = _cast_into(dqkv, dkb, 0, 3 * WA + WB, name="dqkv_k_b")
    dqkv = _cast_into(dqkv, dvb, 0, 3 * WA + 2 * WB, name="dqkv_v_b")
    dfb = dflog.astype(BF16)
    g_qkv, (rWu2,) = _mm_tn(h, dqkv, 1, name="gw_qkv", comm=_chips_comm([cWu], [2]))
    gWin = _scatter_g_in(g_qkv[0], _mm_tn(h, dfb, 1, name="gw_forget")[0], g_gates, lay, PIN)
    dh, (sWin,) = _mm_nt(dfb, Wf, name="d_h_forget", out_dtype=F32, comm=_sibling_comm([gWin]))
    cWin = _chip_sum(gWin, sWin, core, name="chip_sum_w_in")
    dh, (rWin01,) = _mm_nt(dqkv, Wqkv, name="d_h_qkv", out_dtype=F32, res=dh, comm=_chips_comm([cWin], [0, 1]))
    dh, (rWin2,) = _mm_nt(dgates, Wgt, name="d_h_gates", out_dtype=F32, res=dh, comm=_chips_comm([cWin], [2]))
    dx, dg_mix = _rms_bwd(dh, x2d, g_mix, dx1, name="rms_mix_bwd", with_bf16=False)

    def slots(r):
        return [(r, 0), (r, 1), (r, 2)]

    in_parts = [_unplace_block(p, lay, din8, blk)[None] for p in
                (lax.dynamic_index_in_dim(cWin, 2 * cx + cy, keepdims=False), rWin01[0], rWin01[1], rWin2[0])]
    first = jnp.zeros((1,), jnp.int32)
    big = {}
    for nm, s1, others, which, w, m, v in [
            ("w_down_ffn", cWd, slots(rWd), chip, w_down_ffn, m_w_down_ffn, v_w_down_ffn),
            ("w_gate_ffn", cWg, [(rWg01, 0), (rWg01, 1), (rWg2, 0)], chip, w_gate_ffn, m_w_gate_ffn, v_w_gate_ffn),
            ("w_out", cWout, slots(rWout), chip, w_out, m_w_out, v_w_out),
            ("w_branch_a", cWa, slots(rWa), chip, w_branch_a, m_w_branch_a, v_w_branch_a),
            ("w_branch_b", cWb, slots(rWb), chip, w_branch_b, m_w_branch_b, v_w_branch_b),
            ("w_up_ffn", cWu, [(rWu01, 0), (rWu01, 1), (rWu2, 0)], chip, w_up_ffn, m_w_up_ffn, v_w_up_ffn),
            ("w_in", in_parts[0], [(p, 0) for p in in_parts[1:]], first, w_in, m_w_in, v_w_in)]:
        flipped = nm in ("w_gate_ffn", "w_up_ffn")
        w2, m2, v2 = ((a[0].T if flipped else a[0]) for a in (w, m, v))
        outs = _adamw_shard(s1, others, which, w2, m2, v2, name="adamw_" + nm)
        big[nm] = [(o.T if flipped else o)[None] for o in outs]

    small_names = ["g_mix", "b_f", "b_gate", "rel_bias", "g_ffn", "g_final"]
    small_w = [g_mix, b_f, b_gate, rel_bias, g_ffn, g_final]
    small_m = [m_g_mix, m_b_f, m_b_gate, m_rel_bias, m_g_ffn, m_g_final]
    small_v = [v_g_mix, v_b_f, v_b_gate, v_rel_bias, v_g_ffn, v_g_final]
    small_g = [dg_mix, dbf[:, :HB], dbg, d_rel[None], dg_ffn, dg_final[0]]
    shapes = [w.shape for w in small_w]
    n_small = sum(w.size for w in small_w) + 1
    rows = _pad_to(-(-n_small // LANE), 8)
    total = _all_reduce_small(_pack(small_g + [loss_part[:1, :1]], rows), name="all_reduce_small")
    d_s, m_s, v_s = _adamw_small(_pack(small_w, rows), total, _pack(small_m, rows), _pack(small_v, rows),
                                 name="adamw_small")
    g_small = dict(zip(small_names, _unpack(total, shapes)))
    d_small = dict(zip(small_names, _unpack(d_s, shapes)))
    m_small = dict(zip(small_names, _unpack(m_s, shapes)))
    v_small = dict(zip(small_names, _unpack(v_s, shapes)))
    loss = total.reshape(-1)[n_small - 1]

    order = ["g_mix", "w_in", "b_f", "b_gate", "rel_bias", "w_branch_a", "w_branch_b", "w_out", "g_ffn",
             "w_gate_ffn", "w_up_ffn", "w_down_ffn", "g_final"]

    def pick(idx, small):
        return [big[nm][idx] if nm in big else small[nm] for nm in order]

    return (loss, dx[None], *pick(0, g_small), *pick(1, d_small), *pick(2, m_small), *pick(3, v_small))
```
